```python
import math
import jax, jax.numpy as jnp
from jax import lax
import numpy as np

D_MODEL = 1024
BATCH = 8
SEQ = 2048
DEPTH = 1
DEC_BATCH = 128
DEC_SEQ = 8
PAST_LEN = 16384
PAGE_SIZE = 128

D_MIX = D_MODEL
D_CONV = D_MIX // 2
D_HGRN = D_MIX - D_CONV
CONV_WIDTH = 3
CONV_GROUPS = 8
HGRN_EXPAND = 128
HGRN_HEADS = D_HGRN // HGRN_EXPAND
DK = HGRN_EXPAND
DV = D_HGRN // HGRN_HEADS
CHUNK = 64
N_EXPERTS = 32
TOP_K = 4
D_FF = D_MODEL
SWIGLU_LIMIT = 7.0
SWIGLU_ALPHA = 1.702
EPS = 1e-6
D_PROJ = 3 * D_CONV + 4 * D_HGRN

kernel_name = "hybrid_shortconv_hgrn2_moe_step"


def rmsnorm(x, w):
    xf = x.astype(jnp.float32)
    y = xf * lax.rsqrt(jnp.mean(xf * xf, axis=-1, keepdims=True) + EPS)
    return (y * w.astype(jnp.float32)).astype(x.dtype)


def short_conv(u, buf, conv_w, conv_b):
    L = u.shape[1]
    upad = jnp.concatenate([buf.astype(u.dtype), u], axis=1)
    out = (conv_w[0] * upad[:, 0:L] + conv_w[1] * upad[:, 1:L + 1]
           + conv_w[2] * upad[:, 2:L + 2] + conv_b)
    return out, upad[:, -(CONV_WIDTH - 1):]


def hgrn2_chunked(q, logf, k, v, s0):
    B, L, H, _ = q.shape
    c = math.gcd(CHUNK, L)
    n = L // c

    def blocks(a):
        return a.reshape(B, n, c, H, a.shape[-1]).transpose(1, 0, 2, 3, 4)

    mask = jnp.tril(jnp.ones((c, c), dtype=bool))[None, :, :, None, None]

    def step(S, inp):
        qc, lfc, kc, vc = inp
        A = jnp.cumsum(lfc, axis=1)
        inter = jnp.einsum('bthk,bhkv->bthv', qc * jnp.exp(A), S)
        diff = A[:, :, None] - A[:, None, :]
        decay = jnp.exp(jnp.where(mask, diff, -jnp.inf))
        scores = jnp.einsum('bthk,bshk,btshk->bths', qc, kc, decay)
        intra = jnp.einsum('bths,bshv->bthv', scores, vc)
        AL = A[:, -1]
        S_new = (jnp.exp(AL)[..., None] * S
                 + jnp.einsum('bshk,bshv->bhkv', kc * jnp.exp(AL[:, None] - A), vc))
        return S_new, inter + intra

    S_fin, o = lax.scan(step, s0, (blocks(q), blocks(logf), blocks(k), blocks(v)))
    o = o.transpose(1, 0, 2, 3, 4).reshape(B, L, H, DV)
    return o, S_fin


def moe(h, router_w, router_b, w_gate, b_gate, w_up, b_up, w_down, b_down):
    Bsz, L, D = h.shape
    t = h.reshape(Bsz * L, D)
    logits = (t @ router_w + router_b).astype(jnp.float32)
    vals, idx = lax.top_k(logits, TOP_K)
    gates = jax.nn.softmax(vals, axis=-1)
    combine = jnp.einsum('tk,tke->te', gates, jax.nn.one_hot(idx, N_EXPERTS, dtype=jnp.float32))
    y = jnp.zeros((Bsz * L, D), jnp.float32)
    for e in range(N_EXPERTS):
        a = jnp.minimum((t @ w_gate[e] + b_gate[e]).astype(jnp.float32), SWIGLU_LIMIT)
        u = jnp.clip((t @ w_up[e] + b_up[e]).astype(jnp.float32), -SWIGLU_LIMIT, SWIGLU_LIMIT)
        hid = ((u + 1.0) * a * jax.nn.sigmoid(SWIGLU_ALPHA * a)).astype(t.dtype)
        y = y + combine[:, e:e + 1] * (hid @ w_down[e] + b_down[e]).astype(jnp.float32)
    return y.astype(h.dtype).reshape(Bsz, L, D)


def layer(x, conv_state, hgrn_state, lb, norm_mix_w, w_in, conv_w, conv_b, gnorm_w, w_out,
          norm_ffn_w, router_w, router_b, w_gate, b_gate, w_up, b_up, w_down, b_down):
    B, L, _ = x.shape
    h = rmsnorm(x, norm_mix_w)
    proj = h @ w_in
    bg, cg, v_c, q, f, i, g = jnp.split(
        proj, np.cumsum([D_CONV, D_CONV, D_CONV, D_HGRN, D_HGRN, D_HGRN])[:].tolist(), axis=-1)
    conv_out, new_conv = short_conv(cg * v_c, conv_state, conv_w, conv_b)
    y_conv = bg * conv_out
    ff = f.astype(jnp.float32)
    lbf = lb.astype(jnp.float32)
    logf = jnp.log(lbf + (1.0 - lbf) * jax.nn.sigmoid(ff))
    k = (1.0 - lbf) * jax.nn.sigmoid(-ff)
    heads = lambda a: a.astype(jnp.float32).reshape(B, L, HGRN_HEADS, -1)
    o, new_hgrn = hgrn2_chunked(heads(q), heads(logf), heads(k), heads(i),
                                hgrn_state.astype(jnp.float32))
    o = o * lax.rsqrt(jnp.mean(o * o, axis=-1, keepdims=True) + EPS)
    o = o.reshape(B, L, D_HGRN) * gnorm_w.astype(jnp.float32) * jax.nn.silu(g.astype(jnp.float32))
    mix = jnp.concatenate([y_conv, o.astype(x.dtype)], axis=-1) @ w_out
    x = x + mix
    x = x + moe(rmsnorm(x, norm_ffn_w), router_w, router_b, w_gate, b_gate, w_up, b_up,
                w_down, b_down)
    return x, new_conv, new_hgrn


def setup_inputs(seed: int = 0) -> dict:
    key = jax.random.key(seed)
    ks = jax.random.split(key, 24)
    nrm = lambda k, s, sc: jax.random.normal(k, s, jnp.float32) * sc
    return {
        "x_prompt": nrm(ks[0], (BATCH, SEQ, D_MODEL), 1.0),
        "x_sample": nrm(ks[1], (DEC_BATCH, DEC_SEQ, D_MODEL), 1.0),
        "state_conv": nrm(ks[2], (DEPTH, DEC_BATCH, CONV_WIDTH - 1, D_CONV), 1.0),
        "state_hgrn": nrm(ks[3], (DEPTH, DEC_BATCH, HGRN_HEADS, DK, DV), 0.3),
        "lb_logits": nrm(ks[4], (DEPTH + 1, D_HGRN), 0.5),
        "norm_mix_w": 1.0 + nrm(ks[5], (DEPTH, D_MODEL), 0.02),
        "w_in": nrm(ks[6], (DEPTH, D_MODEL, D_PROJ), D_MODEL ** -0.5),
        "conv_w": nrm(ks[7], (DEPTH, CONV_WIDTH, D_CONV), CONV_WIDTH ** -0.5),
        "conv_b": nrm(ks[8], (DEPTH, D_CONV), 0.01),
        "gnorm_w": 1.0 + nrm(ks[9], (DEPTH, D_HGRN), 0.02),
        "w_out": nrm(ks[10], (DEPTH, D_MIX, D_MODEL), D_MIX ** -0.5),
        "norm_ffn_w": 1.0 + nrm(ks[11], (DEPTH, D_MODEL), 0.02),
        "router_w": nrm(ks[12], (DEPTH, D_MODEL, N_EXPERTS), D_MODEL ** -0.5),
        "router_b": nrm(ks[13], (DEPTH, N_EXPERTS), 0.01),
        "w_gate": nrm(ks[14], (DEPTH, N_EXPERTS, D_MODEL, D_FF), D_MODEL ** -0.5),
        "b_gate": nrm(ks[15], (DEPTH, N_EXPERTS, D_FF), 0.01),
        "w_up": nrm(ks[16], (DEPTH, N_EXPERTS, D_MODEL, D_FF), D_MODEL ** -0.5),
        "b_up": nrm(ks[17], (DEPTH, N_EXPERTS, D_FF), 0.01),
        "w_down": nrm(ks[18], (DEPTH, N_EXPERTS, D_FF, D_MODEL), D_FF ** -0.5),
        "b_down": nrm(ks[19], (DEPTH, N_EXPERTS, D_MODEL), 0.01),
        "final_norm_w": 1.0 + nrm(ks[20], (D_MODEL,), 0.02),
    }


def reference(x_prompt, x_sample, state_conv, state_hgrn, lb_logits, norm_mix_w, w_in, conv_w,
              conv_b, gnorm_w, w_out, norm_ffn_w, router_w, router_b, w_gate, b_gate, w_up,
              b_up, w_down, b_down, final_norm_w):
    lb_all = jnp.cumsum(jax.nn.softmax(lb_logits.astype(jnp.float32), axis=0), axis=0)
    xp, xs = x_prompt, x_sample
    conv_p, hgrn_p, conv_s, hgrn_s = [], [], [], []
    for l in range(DEPTH):
        w = (norm_mix_w[l], w_in[l], conv_w[l], conv_b[l], gnorm_w[l], w_out[l], norm_ffn_w[l],
             router_w[l], router_b[l], w_gate[l], b_gate[l], w_up[l], b_up[l], w_down[l],
             b_down[l])
        zc = jnp.zeros((BATCH, CONV_WIDTH - 1, D_CONV), x_prompt.dtype)
        zh = jnp.zeros((BATCH, HGRN_HEADS, DK, DV), jnp.float32)
        xp, cp, hp = layer(xp, zc, zh, lb_all[l], *w)
        xs, cs, hs = layer(xs, state_conv[l], state_hgrn[l], lb_all[l], *w)
        conv_p.append(cp)
        hgrn_p.append(hp.astype(x_prompt.dtype))
        conv_s.append(cs.astype(state_conv.dtype))
        hgrn_s.append(hs.astype(state_hgrn.dtype))
    y_prompt = rmsnorm(xp, final_norm_w)
    y_sample = rmsnorm(xs, final_norm_w)
    return (y_prompt, y_sample, jnp.stack(conv_p), jnp.stack(hgrn_p), jnp.stack(conv_s),
            jnp.stack(hgrn_s))
```

```python
import functools

import numpy as np
import jax
import jax.numpy as jnp
from jax import lax
from jax.experimental import pallas as pl
from jax.experimental.pallas import tpu as pltpu

F32 = jnp.float32
BF16 = jnp.bfloat16

D_MODEL = 1024
D_CONV = 512
D_HGRN = 512
N_HEADS = 4
D_HEAD = 128
D_PROJ = 3 * D_CONV + 4 * D_HGRN
N_EXPERTS = 32
TOP_K = 4
SWIGLU_LIMIT = 7.0
SWIGLU_ALPHA = 1.702
EPS = 1e-6
PROMPT_CHUNK = 64

LANES = 128
SUBLANES = 8
BF16_ROWS = 16
VMEM_LIMIT_BYTES = 56 * 1024 * 1024

MIX_SLAB = 256
MIX_ROWS = 512
MIX_SEQS = 16
TOKEN_TILE = 512
SEG_ALIGN = BF16_ROWS
ROW_TILE = 256
SEG_SIZES = tuple(SEG_ALIGN << i for i in reversed(range(6)))
TILE_CAP = TOP_K * TOKEN_TILE + N_EXPERTS * SEG_ALIGN

NT_DIMS = (((1,), (1,)), ((), ()))
TN_DIMS = (((0,), (0,)), ((), ()))


def _dot(a, b):
    return jnp.dot(a, b, preferred_element_type=F32)


def _dot_nt(a, b):
    return lax.dot_general(a, b, NT_DIMS, preferred_element_type=F32)


def _dot_tn(a, b):
    return lax.dot_general(a, b, TN_DIMS, preferred_element_type=F32)


def _split2(x):
    hi = x.astype(BF16)
    lo = (x - hi.astype(F32)).astype(BF16)
    return hi, lo


def _rmsnorm(x, w):
    return x * lax.rsqrt(jnp.mean(x * x, axis=-1, keepdims=True) + EPS) * w


def _level_exponent_matrix(n, chunk, h):
    x = np.zeros((n, n), np.float32)
    for t in range(n):
        base = t - t % (2 * h)
        m = base + h - 1
        if t % (2 * h) >= h:
            x[t, m + 1:t + 1] = 1.0
        else:
            x[t, t + 1:m + 1] = 1.0
    return x


def _mix_constants(n, chunk, mxu_levels, all_levels):
    t = np.arange(n)
    same_chunk = (t[:, None] // chunk) == (t[None, :] // chunk)
    tri = (same_chunk & (t[None, :] <= t[:, None])).astype(np.float32)
    suf = (same_chunk & (t[None, :] > t[:, None])).astype(np.float32)
    cmat = np.concatenate([tri, suf] + [_level_exponent_matrix(n, chunk, h) for h in mxu_levels], axis=0)
    masks = []
    for h in all_levels:
        blk = (t[:, None] // (2 * h)) == (t[None, :] // (2 * h))
        masks.append((blk & ((t[:, None] % (2 * h)) >= h) & ((t[None, :] % (2 * h)) < h)).astype(np.float32))
    masks.append(np.eye(n, dtype=np.float32))
    return jnp.asarray(cmat, BF16), jnp.asarray(np.stack(masks), F32)


def _vpu_level_exponent(a, h):
    n = a.shape[0]
    pieces = []
    for j in range(n // (2 * h)):
        b = j * 2 * h
        ref = a[b + h - 1:b + h, :]
        pieces.append(ref - a[b:b + h, :])
        pieces.append(a[b + h:b + 2 * h, :] - ref)
    return jnp.concatenate(pieces, axis=0)


def _forget_lower_bound(lbl):
    m = jnp.max(lbl, axis=0, keepdims=True)
    e = jnp.exp(lbl - m)
    return e[0:1, :] / jnp.sum(e, axis=0, keepdims=True)


def _projections(x, nw, w_in_ref):
    h = _rmsnorm(x, nw).astype(BF16)
    return _dot(h, w_in_ref[...])


def _gates(proj, lb):
    q = proj[:, 3 * D_CONV:3 * D_CONV + D_HGRN]
    ff = proj[:, 3 * D_CONV + D_HGRN:3 * D_CONV + 2 * D_HGRN]
    vi = proj[:, 3 * D_CONV + 2 * D_HGRN:3 * D_CONV + 3 * D_HGRN]
    g = proj[:, 3 * D_CONV + 3 * D_HGRN:]
    e = jnp.exp(-jnp.abs(ff))
    r = 1.0 / (1.0 + e)
    pos = ff >= 0
    sig = jnp.where(pos, r, e * r)
    sig_neg = jnp.where(pos, e * r, r)
    logf = jnp.log(lb + (1.0 - lb) * sig)
    kk = (1.0 - lb) * sig_neg
    return q, logf, kk, vi, g


def _intra_scores_times_v(q, kk, vi, exps, masks_ref):
    n = q.shape[0]
    sc = [jnp.zeros((n, n), F32) for _ in range(N_HEADS)]
    for l, ex in enumerate(exps):
        if ex is None:
            qh, kh = q.astype(BF16), kk.astype(BF16)
        else:
            w = jnp.exp(ex)
            qh, kh = (q * w).astype(BF16), (kk * w).astype(BF16)
        mask = masks_ref[l]
        for hd in range(N_HEADS):
            hs = slice(hd * D_HEAD, (hd + 1) * D_HEAD)
            sc[hd] = sc[hd] + mask * _dot_nt(qh[:, hs], kh[:, hs])
    vb = vi.astype(BF16)
    return [_dot(sc[hd].astype(BF16), vb[:, hd * D_HEAD:(hd + 1) * D_HEAD]) for hd in range(N_HEADS)]


def _head_out(o, g, gnw):
    parts = []
    for hd in range(N_HEADS):
        oh = o[:, hd * D_HEAD:(hd + 1) * D_HEAD]
        parts.append(oh * lax.rsqrt(jnp.mean(oh * oh, axis=-1, keepdims=True) + EPS))
    on = jnp.concatenate(parts, axis=1)
    return on * gnw * (g * jax.nn.sigmoid(g))


def _pad_rows_bf16(a):
    return jnp.concatenate([a, jnp.zeros_like(a)], axis=0).astype(BF16)


PROMPT_VPU_LEVELS = (32, 16, 8)
MXU_LEVELS = (4, 2, 1)


def _mix_prompt_kernel(x_ref, lbl_ref, nw_ref, win_ref, cw_ref, cb_ref, gnw_ref, wout_ref, cmat_ref,
                       masks_ref, x1_ref, nconv_ref, nstate_ref, st_ref, tail_ref, o_ref):
    r = pl.program_id(1)
    n_r = pl.num_programs(1)
    rows = x_ref.shape[0]

    @pl.when(r == 0)
    def _():
        st_ref[...] = jnp.zeros_like(st_ref)
        tail_ref[...] = jnp.zeros_like(tail_ref)

    x = x_ref[...]
    proj = _projections(x, nw_ref[...], win_ref)
    lb = _forget_lower_bound(lbl_ref[...])

    bg = proj[:, 0:D_CONV]
    u = proj[:, D_CONV:2 * D_CONV] * proj[:, 2 * D_CONV:3 * D_CONV]
    rid = lax.broadcasted_iota(jnp.int32, (rows, 1), 0)
    t0 = tail_ref[SUBLANES - 2:SUBLANES - 1, :]
    t1 = tail_ref[SUBLANES - 1:SUBLANES, :]
    u1 = jnp.where(rid == 0, t1, pltpu.roll(u, 1, 0))
    u2 = jnp.where(rid == 0, t0, jnp.where(rid == 1, t1, pltpu.roll(u, 2, 0)))
    cw = cw_ref[...]
    y_conv = bg * (cw[0:1, :] * u2 + cw[1:2, :] * u1 + cw[2:3, :] * u + cb_ref[...])
    tail_ref[...] = u[rows - SUBLANES:rows, :]

    q, logf, kk, vi, g = _gates(proj, lb)
    lf_hi, lf_lo = _split2(logf)
    n_chunks = MIX_SLAB // PROMPT_CHUNK
    for s in range(rows // MIX_SLAB):
        sl = slice(s * MIX_SLAB, (s + 1) * MIX_SLAB)
        qs, ks, vs = q[sl], kk[sl], vi[sl]
        ex = _dot(cmat_ref[...], lf_hi[sl]) + _dot(cmat_ref[...], lf_lo[sl])
        a_pre = ex[0:MIX_SLAB]
        a_suf = ex[MIX_SLAB:2 * MIX_SLAB]
        exps = [_vpu_level_exponent(a_pre, h) for h in PROMPT_VPU_LEVELS]
        exps += [ex[(2 + i) * MIX_SLAB:(3 + i) * MIX_SLAB] for i in range(len(MXU_LEVELS))]
        exps.append(None)
        intra = _intra_scores_times_v(qs, ks, vs, exps, masks_ref)
        ea = jnp.exp(a_pre)
        qa = (qs * ea).astype(BF16)
        kb = (ks * jnp.exp(a_suf)).astype(BF16)
        vb = vs.astype(BF16)
        for c in range(n_chunks):
            cr = slice(c * PROMPT_CHUNK, (c + 1) * PROMPT_CHUNK)
            last = (c + 1) * PROMPT_CHUNK - 1
            for hd in range(N_HEADS):
                hs = slice(hd * D_HEAD, (hd + 1) * D_HEAD)
                st = st_ref[hd]
                inter = _dot_nt(qa[cr, hs], st.astype(BF16))
                o_ref[s * MIX_SLAB + c * PROMPT_CHUNK:s * MIX_SLAB + (c + 1) * PROMPT_CHUNK, hs] = (
                    inter + intra[hd][cr, :])
                st_ref[hd] = st * ea[last:last + 1, hs] + _dot_tn(vb[cr, hs], kb[cr, hs])

    o = _head_out(o_ref[...], g, gnw_ref[...])
    mix_in = jnp.concatenate([y_conv, o], axis=1).astype(BF16)
    x1_ref[...] = x + _dot(mix_in, wout_ref[...])

    @pl.when(r == n_r - 1)
    def _():
        nconv_ref[...] = u[rows - SUBLANES:rows, :]
        for hd in range(N_HEADS):
            nstate_ref[hd] = st_ref[hd].T


def _mix_prompt(x, lbl, nw, w_in, cw, cb, gnw, w_out):
    b, l, _ = x.shape
    assert l % MIX_ROWS == 0 and MIX_ROWS % MIX_SLAB == 0 and l % PROMPT_CHUNK == 0
    cmat, masks = _mix_constants(MIX_SLAB, PROMPT_CHUNK, MXU_LEVELS, PROMPT_VPU_LEVELS + MXU_LEVELS)
    const = lambda shape: pl.BlockSpec(shape, lambda i, j: (0,) * len(shape))
    return pl.pallas_call(
        _mix_prompt_kernel,
        grid=(b, l // MIX_ROWS),
        in_specs=[
            pl.BlockSpec((None, MIX_ROWS, D_MODEL), lambda i, j: (i, j, 0)),
            const((2, D_HGRN)), const((1, D_MODEL)), const((D_MODEL, D_PROJ)), const((3, D_CONV)),
            const((1, D_CONV)), const((1, D_HGRN)), const((D_MODEL, D_MODEL)), const(cmat.shape),
            const(masks.shape),
        ],
        out_specs=[
            pl.BlockSpec((None, MIX_ROWS, D_MODEL), lambda i, j: (i, j, 0)),
            pl.BlockSpec((None, SUBLANES, D_CONV), lambda i, j: (i, 0, 0)),
            pl.BlockSpec((None, N_HEADS, D_HEAD, D_HEAD), lambda i, j: (i, 0, 0, 0)),
        ],
        out_shape=[
            jax.ShapeDtypeStruct((b, l, D_MODEL), F32),
            jax.ShapeDtypeStruct((b, SUBLANES, D_CONV), F32),
            jax.ShapeDtypeStruct((b, N_HEADS, D_HEAD, D_HEAD), F32),
        ],
        scratch_shapes=[
            pltpu.VMEM((N_HEADS, D_HEAD, D_HEAD), F32),
            pltpu.VMEM((SUBLANES, D_CONV), F32),
            pltpu.VMEM((MIX_ROWS, D_HGRN), F32),
        ],
        compiler_params=pltpu.CompilerParams(
            dimension_semantics=("arbitrary", "arbitrary"), vmem_limit_bytes=VMEM_LIMIT_BYTES),
        name="mix_prompt",
    )(x, lbl, nw, w_in, cw, cb, gnw, w_out, cmat, masks)


def _mix_sample_kernel(x_ref, cs_ref, hs_ref, lbl_ref, nw_ref, win_ref, cw_ref, cb_ref, gnw_ref, wout_ref,
                       cmat_ref, masks_ref, x1_ref, nconv_ref, nstate_ref, o_ref):
    nseq, length, _ = x_ref.shape
    rows = nseq * length
    x = x_ref[...].reshape(rows, D_MODEL)
    proj = _projections(x, nw_ref[...], win_ref)
    lb = _forget_lower_bound(lbl_ref[...])

    bg = proj[:, 0:D_CONV]
    u = proj[:, D_CONV:2 * D_CONV] * proj[:, 2 * D_CONV:3 * D_CONV]
    cs = cs_ref[...]
    expand = lambda a: jnp.broadcast_to(a, (nseq, length, D_CONV)).reshape(rows, D_CONV)
    t0 = expand(cs[:, 0:1, :])
    t1 = expand(cs[:, 1:2, :])
    pos = lax.broadcasted_iota(jnp.int32, (rows, 1), 0) % length
    u1 = jnp.where(pos == 0, t1, pltpu.roll(u, 1, 0))
    u2 = jnp.where(pos == 0, t0, jnp.where(pos == 1, t1, pltpu.roll(u, 2, 0)))
    cw = cw_ref[...]
    y_conv = bg * (cw[0:1, :] * u2 + cw[1:2, :] * u1 + cw[2:3, :] * u + cb_ref[...])
    nconv_ref[...] = u.reshape(nseq, length, D_CONV)

    q, logf, kk, vi, g = _gates(proj, lb)
    lf_hi, lf_lo = _split2(logf)
    ex = _dot(cmat_ref[...], lf_hi) + _dot(cmat_ref[...], lf_lo)
    a_pre = ex[0:rows]
    a_suf = ex[rows:2 * rows]
    exps = [ex[(2 + i) * rows:(3 + i) * rows] for i in range(len(MXU_LEVELS))] + [None]
    intra = _intra_scores_times_v(q, kk, vi, exps, masks_ref)
    ea = jnp.exp(a_pre)
    qa = q * ea
    kb = kk * jnp.exp(a_suf)
    for s in range(nseq):
        cr = slice(s * length, (s + 1) * length)
        last = (s + 1) * length - 1
        for hd in range(N_HEADS):
            hs = slice(hd * D_HEAD, (hd + 1) * D_HEAD)
            st = hs_ref[s, hd].T
            inter = _dot_nt(_pad_rows_bf16(qa[cr, hs]), st.astype(BF16))[0:length, :]
            o_ref[cr, hs] = inter + intra[hd][cr, :]
            st_new = st * ea[last:last + 1, hs] + _dot_tn(_pad_rows_bf16(vi[cr, hs]), _pad_rows_bf16(kb[cr, hs]))
            nstate_ref[s, hd] = st_new.T

    o = _head_out(o_ref[...], g, gnw_ref[...])
    mix_in = jnp.concatenate([y_conv, o], axis=1).astype(BF16)
    x1_ref[...] = (x + _dot(mix_in, wout_ref[...])).reshape(nseq, length, D_MODEL)


def _mix_sample(x, conv_state, hgrn_state, lbl, nw, w_in, cw, cb, gnw, w_out):
    nb, length, _ = x.shape
    assert nb % MIX_SEQS == 0 and length == SUBLANES
    rows = MIX_SEQS * length
    cmat, masks = _mix_constants(rows, length, MXU_LEVELS, MXU_LEVELS)
    const = lambda shape: pl.BlockSpec(shape, lambda i: (0,) * len(shape))
    return pl.pallas_call(
        _mix_sample_kernel,
        grid=(nb // MIX_SEQS,),
        in_specs=[
            pl.BlockSpec((MIX_SEQS, length, D_MODEL), lambda i: (i, 0, 0)),
            pl.BlockSpec((MIX_SEQS, 2, D_CONV), lambda i: (i, 0, 0)),
            pl.BlockSpec((MIX_SEQS, N_HEADS, D_HEAD, D_HEAD), lambda i: (i, 0, 0, 0)),
            const((2, D_HGRN)), const((1, D_MODEL)), const((D_MODEL, D_PROJ)), const((3, D_CONV)),
            const((1, D_CONV)), const((1, D_HGRN)), const((D_MODEL, D_MODEL)), const(cmat.shape),
            const(masks.shape),
        ],
        out_specs=[
            pl.BlockSpec((MIX_SEQS, length, D_MODEL), lambda i: (i, 0, 0)),
            pl.BlockSpec((MIX_SEQS, length, D_CONV), lambda i: (i, 0, 0)),
            pl.BlockSpec((MIX_SEQS, N_HEADS, D_HEAD, D_HEAD), lambda i: (i, 0, 0, 0)),
        ],
        out_shape=[
            jax.ShapeDtypeStruct((nb, length, D_MODEL), F32),
            jax.ShapeDtypeStruct((nb, length, D_CONV), F32),
            jax.ShapeDtypeStruct((nb, N_HEADS, D_HEAD, D_HEAD), F32),
        ],
        scratch_shapes=[pltpu.VMEM((rows, D_HGRN), F32)],
        compiler_params=pltpu.CompilerParams(
            dimension_semantics=("arbitrary",), vmem_limit_bytes=VMEM_LIMIT_BYTES),
        name="mix_sample",
    )(x, conv_state, hgrn_state, lbl, nw, w_in, cw, cb, gnw, w_out, cmat, masks)


def _router_kernel(x1_ref, nw_ref, rw_hi_ref, rw_lo_ref, rb_ref, ltri_ref, utri_ref, xn_ref, meta_ref, cnt_ref):
    n = x1_ref.shape[0]
    xn = _rmsnorm(x1_ref[...], nw_ref[...])
    xn_ref[...] = xn.astype(BF16)
    x_hi, x_lo = _split2(xn)
    logits = (_dot(x_hi, rw_hi_ref[...]) + _dot(x_lo, rw_hi_ref[...]) + _dot(x_hi, rw_lo_ref[...])
              + rb_ref[...])
    lane = lax.broadcasted_iota(jnp.int32, (n, LANES), 1).astype(F32)
    work = logits
    vals, ids = [], []
    for _ in range(TOP_K):
        m = jnp.max(work, axis=-1, keepdims=True)
        i = jnp.min(jnp.where(work == m, lane, float(LANES)), axis=-1, keepdims=True)
        vals.append(m)
        ids.append(i)
        work = jnp.where(lane == i, -jnp.inf, work)
    es = [jnp.exp(v - vals[0]) for v in vals]
    den = es[0] + es[1] + es[2] + es[3]
    gates = [e / den for e in es]

    onehots = [(lane == i) for i in ids]
    multi = jnp.zeros((n, LANES), F32)
    for oh in onehots:
        multi = multi + oh.astype(F32)
    counts = jnp.sum(multi, axis=0, keepdims=True)
    before = _dot(ltri_ref[...], multi.astype(BF16))
    seg = jnp.ceil(counts * (1.0 / SEG_ALIGN)) * SEG_ALIGN
    seg_rows = jnp.broadcast_to(seg, (BF16_ROWS, LANES)).astype(BF16)
    seg_off = _dot(seg_rows, utri_ref[...])[0:1, :]
    slot_of = seg_off + before
    meta = jnp.zeros((n, LANES), F32)
    for k in range(TOP_K):
        slot = jnp.sum(jnp.where(onehots[k], slot_of, 0.0), axis=-1, keepdims=True)
        meta = jnp.where(lane == k, ids[k], meta)
        meta = jnp.where(lane == TOP_K + k, gates[k], meta)
        meta = jnp.where(lane == 2 * TOP_K + k, slot, meta)
    meta_ref[...] = meta
    cnt_ref[...] = jnp.broadcast_to(counts, (SUBLANES, LANES))


def _router(x1, nw, router_w, router_b):
    t = x1.shape[0]
    n_tiles = t // TOKEN_TILE
    rw = jnp.zeros((D_MODEL, LANES), F32).at[:, :N_EXPERTS].set(router_w)
    rw_hi, rw_lo = _split2(rw)
    rb = jnp.full((1, LANES), -1e30, F32).at[0, :N_EXPERTS].set(router_b)
    idx = np.arange(TOKEN_TILE)
    ltri = jnp.asarray(idx[None, :] < idx[:, None], BF16)
    lid = np.arange(LANES)
    utri = jnp.asarray(lid[:, None] < lid[None, :], BF16)
    const = lambda shape: pl.BlockSpec(shape, lambda i: (0,) * len(shape))
    return pl.pallas_call(
        _router_kernel,
        grid=(n_tiles,),
        in_specs=[
            pl.BlockSpec((TOKEN_TILE, D_MODEL), lambda i: (i, 0)),
            const((1, D_MODEL)), const((D_MODEL, LANES)), const((D_MODEL, LANES)), const((1, LANES)),
            const((TOKEN_TILE, TOKEN_TILE)), const((LANES, LANES)),
        ],
        out_specs=[
            pl.BlockSpec((TOKEN_TILE, D_MODEL), lambda i: (i, 0)),
            pl.BlockSpec((TOKEN_TILE, LANES), lambda i: (i, 0)),
            pl.BlockSpec((None, SUBLANES, LANES), lambda i: (i, 0, 0)),
        ],
        out_shape=[
            jax.ShapeDtypeStruct((t, D_MODEL), BF16),
            jax.ShapeDtypeStruct((t, LANES), F32),
            jax.ShapeDtypeStruct((n_tiles, SUBLANES, LANES), F32),
        ],
        compiler_params=pltpu.CompilerParams(
            dimension_semantics=("arbitrary",), vmem_limit_bytes=VMEM_LIMIT_BYTES),
        name="router",
    )(x1, nw, rw_hi, rw_lo, rb, ltri, utri)


def _segment_tables(cnt, n_rows_total):
    seg = (cnt + SEG_ALIGN - 1) // SEG_ALIGN * SEG_ALIGN
    src = jnp.cumsum(seg, axis=1) - seg
    tot = jnp.sum(seg, axis=0)
    cap = (tot + ROW_TILE - 1) // ROW_TILE * ROW_TILE
    base = jnp.cumsum(cap) - cap
    dst = base[None, :] + jnp.cumsum(seg, axis=0) - seg
    n_used = jnp.sum(cap) // ROW_TILE
    n_row_tiles = n_rows_total // ROW_TILE
    tile_start = jnp.arange(n_row_tiles, dtype=jnp.int32) * ROW_TILE
    tile_expert = jnp.searchsorted(base + cap, tile_start, side="right").astype(jnp.int32)
    tile_expert = jnp.minimum(tile_expert, N_EXPERTS - 1)
    last_expert = tile_expert[jnp.maximum(n_used - 1, 0)]
    tile_expert = jnp.where(tile_start < n_used * ROW_TILE, tile_expert, last_expert)
    fill = base + tot
    i32 = lambda a: a.astype(jnp.int32)
    return (i32(seg).reshape(-1), i32(src).reshape(-1), i32(dst).reshape(-1), i32(fill), i32(cap - tot),
            i32(tile_expert), i32(n_used).reshape(1))


def _for_each_chunk(n, fn):
    for size in SEG_SIZES:
        done = n & ~(2 * size - 1)

        @pl.when((n & size) != 0)
        def _():
            fn(done, size)


def _segment_copies(seg_ref, src_ref, dst_ref, tile, make_copy, act):
    def per_expert(e, carry):
        j = tile * N_EXPERTS + e
        _for_each_chunk(seg_ref[j], lambda done, size: act(make_copy(
            pl.multiple_of(src_ref[j] + done, SEG_ALIGN), pl.multiple_of(dst_ref[j] + done, SEG_ALIGN), size)))
        return carry
    lax.fori_loop(0, N_EXPERTS, per_expert, 0)


def _dispatch_kernel(seg_ref, src_ref, dst_ref, fill_ref, gap_ref, nu_ref, slot_ref, xn_ref, xs_ref, sorted_ref,
                     zero_ref, sem):
    tile = pl.program_id(0)

    @pl.when(tile == 0)
    def _():
        zero_ref[...] = jnp.zeros_like(zero_ref)

        def fill_copy(row, size):
            return pltpu.make_async_copy(zero_ref.at[pl.ds(0, size)],
                                         xs_ref.at[pl.ds(pl.multiple_of(row, SEG_ALIGN), size)], sem)

        def gaps(act):
            def per_expert(e, carry):
                _for_each_chunk(gap_ref[e], lambda done, size: act(fill_copy(fill_ref[e] + done, size)))
                return carry
            lax.fori_loop(0, N_EXPERTS, per_expert, 0)

        def tail(act):
            n_tail = xs_ref.shape[0] // ROW_TILE - nu_ref[0]
            lax.fori_loop(0, n_tail, lambda i, c: (act(fill_copy((nu_ref[0] + i) * ROW_TILE, ROW_TILE)), c)[1], 0)

        gaps(lambda c: c.start())
        tail(lambda c: c.start())
        gaps(lambda c: c.wait())
        tail(lambda c: c.wait())

    slots = slot_ref[...]
    row = lax.broadcasted_iota(jnp.int32, (TILE_CAP, TOKEN_TILE), 0).astype(F32)
    hit = row == slots[0:1, :]
    for k in range(1, TOP_K):
        hit = hit | (row == slots[k:k + 1, :])
    onehot = jnp.where(hit, 1.0, 0.0).astype(BF16)
    sorted_ref[...] = _dot(onehot, xn_ref[...]).astype(BF16)

    def make_copy(src, dst, size):
        return pltpu.make_async_copy(sorted_ref.at[pl.ds(src, size)], xs_ref.at[pl.ds(dst, size)], sem)

    _segment_copies(seg_ref, src_ref, dst_ref, tile, make_copy, lambda c: c.start())
    _segment_copies(seg_ref, src_ref, dst_ref, tile, make_copy, lambda c: c.wait())


def _dispatch(tables, slot_t, xn, n_rows_total):
    t = xn.shape[0]
    n_tiles = t // TOKEN_TILE
    return pl.pallas_call(
        _dispatch_kernel,
        grid_spec=pltpu.PrefetchScalarGridSpec(
            num_scalar_prefetch=len(tables),
            grid=(n_tiles,),
            in_specs=[
                pl.BlockSpec((SUBLANES, TOKEN_TILE), lambda i, *_: (0, i)),
                pl.BlockSpec((TOKEN_TILE, D_MODEL), lambda i, *_: (i, 0)),
            ],
            out_specs=pl.BlockSpec(memory_space=pl.ANY),
            scratch_shapes=[
                pltpu.VMEM((TILE_CAP, D_MODEL), BF16),
                pltpu.VMEM((ROW_TILE, D_MODEL), BF16),
                pltpu.SemaphoreType.DMA(()),
            ],
        ),
        out_shape=jax.ShapeDtypeStruct((n_rows_total, D_MODEL), BF16),
        compiler_params=pltpu.CompilerParams(
            dimension_semantics=("arbitrary",), vmem_limit_bytes=VMEM_LIMIT_BYTES),
        name="dispatch",
    )(*tables, slot_t, xn)


def _experts_kernel(te_ref, nu_ref, xs_ref, wg_ref, bg_ref, wu_ref, bu_ref, wd_ref, bd_ref, zs_ref,
                    wg_bf, wu_bf, wd_bf):
    j = pl.program_id(0)
    prev = te_ref[jnp.maximum(j - 1, 0)]

    @pl.when(j < nu_ref[0])
    def _():
        @pl.when((j == 0) | (te_ref[j] != prev))
        def _():
            wg_bf[...] = wg_ref[...].astype(BF16)
            wu_bf[...] = wu_ref[...].astype(BF16)
            wd_bf[...] = wd_ref[...].astype(BF16)

        x = xs_ref[...]
        a = jnp.minimum(_dot(x, wg_bf[...]) + bg_ref[...], SWIGLU_LIMIT)
        u = jnp.clip(_dot(x, wu_bf[...]) + bu_ref[...], -SWIGLU_LIMIT, SWIGLU_LIMIT)
        hid = ((u + 1.0) * a * jax.nn.sigmoid(SWIGLU_ALPHA * a)).astype(BF16)
        zs_ref[...] = (_dot(hid, wd_bf[...]) + bd_ref[...]).astype(BF16)

    @pl.when(j >= nu_ref[0])
    def _():
        zs_ref[...] = jnp.zeros_like(zs_ref)


def _experts(tile_expert, n_used, xs, w_gate, b_gate, w_up, b_up, w_down, b_down, n_rows_total):
    n_row_tiles = n_rows_total // ROW_TILE
    row_map = lambda j, te, nu: (jnp.minimum(j, jnp.maximum(nu[0] - 1, 0)), 0)
    w_map = lambda j, te, nu: (te[j], 0, 0)
    w_spec = pl.BlockSpec((None, D_MODEL, D_MODEL), w_map)
    b_spec = pl.BlockSpec((None, 1, D_MODEL), w_map)
    return pl.pallas_call(
        _experts_kernel,
        grid_spec=pltpu.PrefetchScalarGridSpec(
            num_scalar_prefetch=2,
            grid=(n_row_tiles,),
            in_specs=[pl.BlockSpec((ROW_TILE, D_MODEL), row_map), w_spec, b_spec, w_spec, b_spec, w_spec, b_spec],
            out_specs=pl.BlockSpec((ROW_TILE, D_MODEL), lambda j, te, nu: (j, 0)),
            scratch_shapes=[pltpu.VMEM((D_MODEL, D_MODEL), BF16) for _ in range(3)],
        ),
        out_shape=jax.ShapeDtypeStruct((n_rows_total, D_MODEL), BF16),
        compiler_params=pltpu.CompilerParams(
            dimension_semantics=("arbitrary",), vmem_limit_bytes=VMEM_LIMIT_BYTES),
        name="experts",
    )(tile_expert, n_used, xs, w_gate, b_gate[:, None, :], w_up, b_up[:, None, :], w_down, b_down[:, None, :])


def _combine_kernel(seg_ref, src_ref, dst_ref, meta_ref, x1_ref, fw_ref, zs_ref, y_ref, sorted_ref, sem):
    tile = pl.program_id(0)
    sorted_ref[...] = jnp.zeros_like(sorted_ref)

    def make_copy(src, dst, size):
        return pltpu.make_async_copy(zs_ref.at[pl.ds(dst, size)], sorted_ref.at[pl.ds(src, size)], sem)

    _segment_copies(seg_ref, src_ref, dst_ref, tile, make_copy, lambda c: c.start())
    meta = meta_ref[...]
    col = lax.broadcasted_iota(jnp.int32, (TOKEN_TILE, TILE_CAP), 1).astype(F32)
    weights = jnp.zeros((TOKEN_TILE, TILE_CAP), F32)
    for k in range(TOP_K):
        gate = meta[:, TOP_K + k:TOP_K + k + 1]
        slot = meta[:, 2 * TOP_K + k:2 * TOP_K + k + 1]
        weights = jnp.where(col == slot, gate, weights)
    _segment_copies(seg_ref, src_ref, dst_ref, tile, make_copy, lambda c: c.wait())
    y = x1_ref[...] + _dot(weights.astype(BF16), sorted_ref[...])
    y_ref[...] = _rmsnorm(y, fw_ref[...])


def _combine(tables, meta, x1, final_w, zs):
    seg, src, dst = tables
    t = x1.shape[0]
    n_tiles = t // TOKEN_TILE
    return pl.pallas_call(
        _combine_kernel,
        grid_spec=pltpu.PrefetchScalarGridSpec(
            num_scalar_prefetch=3,
            grid=(n_tiles,),
            in_specs=[
                pl.BlockSpec((TOKEN_TILE, LANES), lambda i, *_: (i, 0)),
                pl.BlockSpec((TOKEN_TILE, D_MODEL), lambda i, *_: (i, 0)),
                pl.BlockSpec((1, D_MODEL), lambda i, *_: (0, 0)),
                pl.BlockSpec(memory_space=pl.ANY),
            ],
            out_specs=pl.BlockSpec((TOKEN_TILE, D_MODEL), lambda i, *_: (i, 0)),
            scratch_shapes=[pltpu.VMEM((TILE_CAP, D_MODEL), BF16), pltpu.SemaphoreType.DMA(())],
        ),
        out_shape=jax.ShapeDtypeStruct((t, D_MODEL), F32),
        compiler_params=pltpu.CompilerParams(
            dimension_semantics=("arbitrary",), vmem_limit_bytes=VMEM_LIMIT_BYTES),
        name="combine",
    )(seg, src, dst, meta, x1, final_w, zs)


def _moe_and_final_norm(x1, norm_ffn_w, router_w, router_b, w_gate, b_gate, w_up, b_up, w_down, b_down, final_w):
    t = x1.shape[0]
    assert t % TOKEN_TILE == 0
    n_tiles = t // TOKEN_TILE
    n_rows_total = TOP_K * t + n_tiles * N_EXPERTS * (SEG_ALIGN - 1) + N_EXPERTS * (ROW_TILE - 1)
    n_rows_total = (n_rows_total + ROW_TILE - 1) // ROW_TILE * ROW_TILE
    xn, meta, cnt = _router(x1, norm_ffn_w.reshape(1, D_MODEL), router_w, router_b)
    cnt = cnt[:, 0, :N_EXPERTS].astype(jnp.int32)
    seg, src, dst, fill, gap, tile_expert, n_used = _segment_tables(cnt, n_rows_total)
    slot_t = jnp.zeros((SUBLANES, t), F32).at[:TOP_K, :].set(meta[:, 2 * TOP_K:3 * TOP_K].T)
    xs = _dispatch((seg, src, dst, fill, gap, n_used), slot_t, xn, n_rows_total)
    zs = _experts(tile_expert, n_used, xs, w_gate, b_gate, w_up, b_up, w_down, b_down, n_rows_total)
    return _combine((seg, src, dst), meta, x1, final_w.reshape(1, D_MODEL), zs)


def kernel(x_prompt, x_sample, state_conv, state_hgrn, lb_logits, norm_mix_w, w_in, conv_w, conv_b, gnorm_w,
           w_out, norm_ffn_w, router_w, router_b, w_gate, b_gate, w_up, b_up, w_down, b_down, final_norm_w):
    assert norm_mix_w.shape[0] == 1 and lb_logits.shape[0] == 2, "single-layer step"
    b, l, _ = x_prompt.shape
    nb, ls, _ = x_sample.shape
    lbl = lb_logits.astype(F32)
    nw = norm_mix_w[0].reshape(1, D_MODEL)
    w_in_bf = w_in[0].astype(BF16)
    w_out_bf = w_out[0].astype(BF16)
    cw, cb = conv_w[0], conv_b[0].reshape(1, D_CONV)
    gnw = gnorm_w[0].reshape(1, D_HGRN)
    x1p, conv_p, hgrn_p = _mix_prompt(x_prompt, lbl, nw, w_in_bf, cw, cb, gnw, w_out_bf)
    x1s, conv_s, hgrn_s = _mix_sample(x_sample, state_conv[0], state_hgrn[0], lbl, nw, w_in_bf, cw, cb, gnw,
                                      w_out_bf)
    x1 = jnp.concatenate([x1p.reshape(b * l, D_MODEL), x1s.reshape(nb * ls, D_MODEL)], axis=0)
    y = _moe_and_final_norm(x1, norm_ffn_w[0], router_w[0], router_b[0], w_gate[0], b_gate[0], w_up[0], b_up[0],
                            w_down[0], b_down[0], final_norm_w)
    y_prompt = y[:b * l].reshape(b, l, D_MODEL)
    y_sample = y[b * l:].reshape(nb, ls, D_MODEL)
    conv_p = conv_p[:, SUBLANES - 2:, :]
    conv_s = conv_s[:, ls - 2:, :]
    return (y_prompt, y_sample, conv_p[None], hgrn_p[None], conv_s[None], hgrn_s[None])
```

```python
import functools

import numpy as np
import jax
import jax.numpy as jnp
from jax import lax
from jax.experimental import pallas as pl
from jax.experimental.pallas import tpu as pltpu

F32 = jnp.float32
BF16 = jnp.bfloat16

D_MODEL = 1024
D_CONV = 512
D_HGRN = 512
N_HEADS = 4
D_HEAD = 128
D_PROJ = 3 * D_CONV + 4 * D_HGRN
N_EXPERTS = 32
TOP_K = 4
SWIGLU_LIMIT = 7.0
SWIGLU_ALPHA = 1.702
EPS = 1e-6
PROMPT_CHUNK = 64

LANES = 128
SUBLANES = 8
BF16_ROWS = 16
VMEM_LIMIT_BYTES = 56 * 1024 * 1024

MIX_SLAB = 256
MIX_ROWS = 512
MIX_SEQS = 16
TOKEN_TILE = 512
SEG_ALIGN = BF16_ROWS
ROW_TILE = 256
SEG_SIZES = tuple(SEG_ALIGN << i for i in reversed(range(6)))
TILE_CAP = TOP_K * TOKEN_TILE + N_EXPERTS * SEG_ALIGN

NT_DIMS = (((1,), (1,)), ((), ()))
TN_DIMS = (((0,), (0,)), ((), ()))


def _dot(a, b):
    return jnp.dot(a, b, preferred_element_type=F32)


def _dot_nt(a, b):
    return lax.dot_general(a, b, NT_DIMS, preferred_element_type=F32)


def _dot_tn(a, b):
    return lax.dot_general(a, b, TN_DIMS, preferred_element_type=F32)


def _split2(x):
    hi = x.astype(BF16)
    lo = (x - hi.astype(F32)).astype(BF16)
    return hi, lo


def _rmsnorm(x, w):
    return x * lax.rsqrt(jnp.mean(x * x, axis=-1, keepdims=True) + EPS) * w


def _level_exponent_matrix(n, chunk, h):
    x = np.zeros((n, n), np.float32)
    for t in range(n):
        base = t - t % (2 * h)
        m = base + h - 1
        if t % (2 * h) >= h:
            x[t, m + 1:t + 1] = 1.0
        else:
            x[t, t + 1:m + 1] = 1.0
    return x


def _mix_constants(n, chunk, mxu_levels, all_levels):
    t = np.arange(n)
    same_chunk = (t[:, None] // chunk) == (t[None, :] // chunk)
    tri = (same_chunk & (t[None, :] <= t[:, None])).astype(np.float32)
    suf = (same_chunk & (t[None, :] > t[:, None])).astype(np.float32)
    cmat = np.concatenate([tri, suf] + [_level_exponent_matrix(n, chunk, h) for h in mxu_levels], axis=0)
    masks = []
    for h in all_levels:
        blk = (t[:, None] // (2 * h)) == (t[None, :] // (2 * h))
        masks.append((blk & ((t[:, None] % (2 * h)) >= h) & ((t[None, :] % (2 * h)) < h)).astype(np.float32))
    masks.append(np.eye(n, dtype=np.float32))
    return jnp.asarray(cmat, BF16), jnp.asarray(np.stack(masks), F32)


def _vpu_level_exponent(a, h):
    n = a.shape[0]
    pieces = []
    for j in range(n // (2 * h)):
        b = j * 2 * h
        ref = a[b + h - 1:b + h, :]
        pieces.append(ref - a[b:b + h, :])
        pieces.append(a[b + h:b + 2 * h, :] - ref)
    return jnp.concatenate(pieces, axis=0)


def _forget_lower_bound(lbl):
    m = jnp.max(lbl, axis=0, keepdims=True)
    e = jnp.exp(lbl - m)
    return e[0:1, :] / jnp.sum(e, axis=0, keepdims=True)


def _projections(x, nw, w_in_ref):
    h = _rmsnorm(x, nw).astype(BF16)
    return _dot(h, w_in_ref[...])


def _gates(proj, lb):
    q = proj[:, 3 * D_CONV:3 * D_CONV + D_HGRN]
    ff = proj[:, 3 * D_CONV + D_HGRN:3 * D_CONV + 2 * D_HGRN]
    vi = proj[:, 3 * D_CONV + 2 * D_HGRN:3 * D_CONV + 3 * D_HGRN]
    g = proj[:, 3 * D_CONV + 3 * D_HGRN:]
    e = jnp.exp(-jnp.abs(ff))
    r = 1.0 / (1.0 + e)
    pos = ff >= 0
    sig = jnp.where(pos, r, e * r)
    sig_neg = jnp.where(pos, e * r, r)
    logf = jnp.log(lb + (1.0 - lb) * sig)
    kk = (1.0 - lb) * sig_neg
    return q, logf, kk, vi, g


def _intra_scores_times_v(q, kk, vi, exps, masks_ref):
    n = q.shape[0]
    sc = [jnp.zeros((n, n), F32) for _ in range(N_HEADS)]
    for l, ex in enumerate(exps):
        if ex is None:
            qh, kh = q.astype(BF16), kk.astype(BF16)
        else:
            w = jnp.exp(ex)
            qh, kh = (q * w).astype(BF16), (kk * w).astype(BF16)
        mask = masks_ref[l]
        for hd in range(N_HEADS):
            hs = slice(hd * D_HEAD, (hd + 1) * D_HEAD)
            sc[hd] = sc[hd] + mask * _dot_nt(qh[:, hs], kh[:, hs])
    vb = vi.astype(BF16)
    return [_dot(sc[hd].astype(BF16), vb[:, hd * D_HEAD:(hd + 1) * D_HEAD]) for hd in range(N_HEADS)]


def _head_out(o, g, gnw):
    parts = []
    for hd in range(N_HEADS):
        oh = o[:, hd * D_HEAD:(hd + 1) * D_HEAD]
        parts.append(oh * lax.rsqrt(jnp.mean(oh * oh, axis=-1, keepdims=True) + EPS))
    on = jnp.concatenate(parts, axis=1)
    return on * gnw * (g * jax.nn.sigmoid(g))


def _pad_rows_bf16(a):
    return jnp.concatenate([a, jnp.zeros_like(a)], axis=0).astype(BF16)


PROMPT_VPU_LEVELS = (32, 16, 8)
MXU_LEVELS = (4, 2, 1)


def _mix_prompt_kernel(n_r, n_steps, *refs):
    step = pl.program_id(0)

    @pl.when(step < n_steps)
    def _():
        _mix_prompt_step(step % n_r, n_r, *refs)

    @pl.when(step >= n_steps)
    def _():
        x1_ref = refs[10]
        x1_ref[...] = jnp.zeros_like(x1_ref)


def _mix_prompt_step(r, n_r, x_ref, lbl_ref, nw_ref, win_ref, cw_ref, cb_ref, gnw_ref, wout_ref, cmat_ref,
                     masks_ref, x1_ref, nconv_ref, nstate_ref, st_ref, tail_ref, o_ref):
    rows = x_ref.shape[0]

    @pl.when(r == 0)
    def _():
        st_ref[...] = jnp.zeros_like(st_ref)
        tail_ref[...] = jnp.zeros_like(tail_ref)

    x = x_ref[...]
    proj = _projections(x, nw_ref[...], win_ref)
    lb = _forget_lower_bound(lbl_ref[...])

    bg = proj[:, 0:D_CONV]
    u = proj[:, D_CONV:2 * D_CONV] * proj[:, 2 * D_CONV:3 * D_CONV]
    rid = lax.broadcasted_iota(jnp.int32, (rows, 1), 0)
    t0 = tail_ref[SUBLANES - 2:SUBLANES - 1, :]
    t1 = tail_ref[SUBLANES - 1:SUBLANES, :]
    u1 = jnp.where(rid == 0, t1, pltpu.roll(u, 1, 0))
    u2 = jnp.where(rid == 0, t0, jnp.where(rid == 1, t1, pltpu.roll(u, 2, 0)))
    cw = cw_ref[...]
    y_conv = bg * (cw[0:1, :] * u2 + cw[1:2, :] * u1 + cw[2:3, :] * u + cb_ref[...])
    tail_ref[...] = u[rows - SUBLANES:rows, :]

    q, logf, kk, vi, g = _gates(proj, lb)
    lf_hi, lf_lo = _split2(logf)
    n_chunks = MIX_SLAB // PROMPT_CHUNK
    for s in range(rows // MIX_SLAB):
        sl = slice(s * MIX_SLAB, (s + 1) * MIX_SLAB)
        qs, ks, vs = q[sl], kk[sl], vi[sl]
        ex = _dot(cmat_ref[...], lf_hi[sl]) + _dot(cmat_ref[...], lf_lo[sl])
        a_pre = ex[0:MIX_SLAB]
        a_suf = ex[MIX_SLAB:2 * MIX_SLAB]
        exps = [_vpu_level_exponent(a_pre, h) for h in PROMPT_VPU_LEVELS]
        exps += [ex[(2 + i) * MIX_SLAB:(3 + i) * MIX_SLAB] for i in range(len(MXU_LEVELS))]
        exps.append(None)
        intra = _intra_scores_times_v(qs, ks, vs, exps, masks_ref)
        ea = jnp.exp(a_pre)
        qa = (qs * ea).astype(BF16)
        kb = (ks * jnp.exp(a_suf)).astype(BF16)
        vb = vs.astype(BF16)
        for c in range(n_chunks):
            cr = slice(c * PROMPT_CHUNK, (c + 1) * PROMPT_CHUNK)
            last = (c + 1) * PROMPT_CHUNK - 1
            for hd in range(N_HEADS):
                hs = slice(hd * D_HEAD, (hd + 1) * D_HEAD)
                st = st_ref[hd]
                inter = _dot_nt(qa[cr, hs], st.astype(BF16))
                o_ref[s * MIX_SLAB + c * PROMPT_CHUNK:s * MIX_SLAB + (c + 1) * PROMPT_CHUNK, hs] = (
                    inter + intra[hd][cr, :])
                st_ref[hd] = st * ea[last:last + 1, hs] + _dot_tn(vb[cr, hs], kb[cr, hs])

    o = _head_out(o_ref[...], g, gnw_ref[...])
    mix_in = jnp.concatenate([y_conv, o], axis=1).astype(BF16)
    x1_ref[...] = x + _dot(mix_in, wout_ref[...])

    @pl.when(r == n_r - 1)
    def _():
        nconv_ref[...] = u[rows - SUBLANES:rows, :]
        for hd in range(N_HEADS):
            nstate_ref[hd] = st_ref[hd].T


def _mix_prompt(x, lbl, nw, w_in, cw, cb, gnw, w_out, total_rows):
    b, l, _ = x.shape
    assert l % MIX_ROWS == 0 and MIX_ROWS % MIX_SLAB == 0 and l % PROMPT_CHUNK == 0
    assert total_rows % MIX_ROWS == 0
    n_r = l // MIX_ROWS
    n_steps = b * n_r
    seq = lambda s: jnp.minimum(s, n_steps - 1) // n_r
    cmat, masks = _mix_constants(MIX_SLAB, PROMPT_CHUNK, MXU_LEVELS, PROMPT_VPU_LEVELS + MXU_LEVELS)
    const = lambda shape: pl.BlockSpec(shape, lambda s: (0,) * len(shape))
    return pl.pallas_call(
        functools.partial(_mix_prompt_kernel, n_r, n_steps),
        grid=(total_rows // MIX_ROWS,),
        in_specs=[
            pl.BlockSpec((None, MIX_ROWS, D_MODEL), lambda s: (seq(s), jnp.minimum(s, n_steps - 1) % n_r, 0)),
            const((2, D_HGRN)), const((1, D_MODEL)), const((D_MODEL, D_PROJ)), const((3, D_CONV)),
            const((1, D_CONV)), const((1, D_HGRN)), const((D_MODEL, D_MODEL)), const(cmat.shape),
            const(masks.shape),
        ],
        out_specs=[
            pl.BlockSpec((MIX_ROWS, D_MODEL), lambda s: (s, 0)),
            pl.BlockSpec((None, SUBLANES, D_CONV), lambda s: (seq(s), 0, 0)),
            pl.BlockSpec((None, N_HEADS, D_HEAD, D_HEAD), lambda s: (seq(s), 0, 0, 0)),
        ],
        out_shape=[
            jax.ShapeDtypeStruct((total_rows, D_MODEL), F32),
            jax.ShapeDtypeStruct((b, SUBLANES, D_CONV), F32),
            jax.ShapeDtypeStruct((b, N_HEADS, D_HEAD, D_HEAD), F32),
        ],
        scratch_shapes=[
            pltpu.VMEM((N_HEADS, D_HEAD, D_HEAD), F32),
            pltpu.VMEM((SUBLANES, D_CONV), F32),
            pltpu.VMEM((MIX_ROWS, D_HGRN), F32),
        ],
        compiler_params=pltpu.CompilerParams(
            dimension_semantics=("arbitrary",), vmem_limit_bytes=VMEM_LIMIT_BYTES),
        name="mix_prompt",
    )(x, lbl, nw, w_in, cw, cb, gnw, w_out, cmat, masks)


def _mix_sample_kernel(x_ref, cs_ref, hs_ref, lbl_ref, nw_ref, win_ref, cw_ref, cb_ref, gnw_ref, wout_ref,
                       cmat_ref, masks_ref, x1_in_ref, x1_ref, nconv_ref, nstate_ref, o_ref):
    del x1_in_ref
    nseq, length, _ = nconv_ref.shape
    rows = nseq * length
    x = x_ref[...]
    proj = _projections(x, nw_ref[...], win_ref)
    lb = _forget_lower_bound(lbl_ref[...])

    bg = proj[:, 0:D_CONV]
    u = proj[:, D_CONV:2 * D_CONV] * proj[:, 2 * D_CONV:3 * D_CONV]
    cs = cs_ref[...]
    expand = lambda a: jnp.broadcast_to(a, (nseq, length, D_CONV)).reshape(rows, D_CONV)
    t0 = expand(cs[:, 0:1, :])
    t1 = expand(cs[:, 1:2, :])
    pos = lax.broadcasted_iota(jnp.int32, (rows, 1), 0) % length
    u1 = jnp.where(pos == 0, t1, pltpu.roll(u, 1, 0))
    u2 = jnp.where(pos == 0, t0, jnp.where(pos == 1, t1, pltpu.roll(u, 2, 0)))
    cw = cw_ref[...]
    y_conv = bg * (cw[0:1, :] * u2 + cw[1:2, :] * u1 + cw[2:3, :] * u + cb_ref[...])
    nconv_ref[...] = u.reshape(nseq, length, D_CONV)

    q, logf, kk, vi, g = _gates(proj, lb)
    lf_hi, lf_lo = _split2(logf)
    ex = _dot(cmat_ref[...], lf_hi) + _dot(cmat_ref[...], lf_lo)
    a_pre = ex[0:rows]
    a_suf = ex[rows:2 * rows]
    exps = [ex[(2 + i) * rows:(3 + i) * rows] for i in range(len(MXU_LEVELS))] + [None]
    intra = _intra_scores_times_v(q, kk, vi, exps, masks_ref)
    ea = jnp.exp(a_pre)
    qa = q * ea
    kb = kk * jnp.exp(a_suf)
    for s in range(nseq):
        cr = slice(s * length, (s + 1) * length)
        last = (s + 1) * length - 1
        for hd in range(N_HEADS):
            hs = slice(hd * D_HEAD, (hd + 1) * D_HEAD)
            st = hs_ref[s, hd].T
            inter = _dot_nt(_pad_rows_bf16(qa[cr, hs]), st.astype(BF16))[0:length, :]
            o_ref[cr, hs] = inter + intra[hd][cr, :]
            st_new = st * ea[last:last + 1, hs] + _dot_tn(_pad_rows_bf16(vi[cr, hs]), _pad_rows_bf16(kb[cr, hs]))
            nstate_ref[s, hd] = st_new.T

    o = _head_out(o_ref[...], g, gnw_ref[...])
    mix_in = jnp.concatenate([y_conv, o], axis=1).astype(BF16)
    x1_ref[...] = x + _dot(mix_in, wout_ref[...])


def _mix_sample(x, conv_state, hgrn_state, lbl, nw, w_in, cw, cb, gnw, w_out, x1_flat, row_offset):
    nb, length, _ = x.shape
    rows = MIX_SEQS * length
    assert nb % MIX_SEQS == 0 and length == SUBLANES and row_offset % rows == 0
    first_block = row_offset // rows
    cmat, masks = _mix_constants(rows, length, MXU_LEVELS, MXU_LEVELS)
    const = lambda shape: pl.BlockSpec(shape, lambda i: (0,) * len(shape))
    return pl.pallas_call(
        _mix_sample_kernel,
        grid=(nb // MIX_SEQS,),
        in_specs=[
            pl.BlockSpec((rows, D_MODEL), lambda i: (i, 0)),
            pl.BlockSpec((MIX_SEQS, 2, D_CONV), lambda i: (i, 0, 0)),
            pl.BlockSpec((MIX_SEQS, N_HEADS, D_HEAD, D_HEAD), lambda i: (i, 0, 0, 0)),
            const((2, D_HGRN)), const((1, D_MODEL)), const((D_MODEL, D_PROJ)), const((3, D_CONV)),
            const((1, D_CONV)), const((1, D_HGRN)), const((D_MODEL, D_MODEL)), const(cmat.shape),
            const(masks.shape), pl.BlockSpec(memory_space=pl.ANY),
        ],
        out_specs=[
            pl.BlockSpec((rows, D_MODEL), lambda i: (first_block + i, 0)),
            pl.BlockSpec((MIX_SEQS, length, D_CONV), lambda i: (i, 0, 0)),
            pl.BlockSpec((MIX_SEQS, N_HEADS, D_HEAD, D_HEAD), lambda i: (i, 0, 0, 0)),
        ],
        out_shape=[
            jax.ShapeDtypeStruct(x1_flat.shape, F32),
            jax.ShapeDtypeStruct((nb, length, D_CONV), F32),
            jax.ShapeDtypeStruct((nb, N_HEADS, D_HEAD, D_HEAD), F32),
        ],
        scratch_shapes=[pltpu.VMEM((rows, D_HGRN), F32)],
        input_output_aliases={12: 0},
        compiler_params=pltpu.CompilerParams(
            dimension_semantics=("arbitrary",), vmem_limit_bytes=VMEM_LIMIT_BYTES),
        name="mix_sample",
    )(x.reshape(nb * length, D_MODEL), conv_state, hgrn_state, lbl, nw, w_in, cw, cb, gnw, w_out, cmat, masks,
      x1_flat)


def _router_kernel(x1_ref, nw_ref, rw_hi_ref, rw_lo_ref, rb_ref, ltri_ref, utri_ref, xn_ref, meta_ref, cnt_ref):
    n = x1_ref.shape[0]
    xn = _rmsnorm(x1_ref[...], nw_ref[...])
    xn_ref[...] = xn.astype(BF16)
    x_hi, x_lo = _split2(xn)
    logits = (_dot(x_hi, rw_hi_ref[...]) + _dot(x_lo, rw_hi_ref[...]) + _dot(x_hi, rw_lo_ref[...])
              + rb_ref[...])
    lane = lax.broadcasted_iota(jnp.int32, (n, LANES), 1).astype(F32)
    work = logits
    vals, ids = [], []
    for _ in range(TOP_K):
        m = jnp.max(work, axis=-1, keepdims=True)
        i = jnp.min(jnp.where(work == m, lane, float(LANES)), axis=-1, keepdims=True)
        vals.append(m)
        ids.append(i)
        work = jnp.where(lane == i, -jnp.inf, work)
    es = [jnp.exp(v - vals[0]) for v in vals]
    den = es[0] + es[1] + es[2] + es[3]
    gates = [e / den for e in es]

    onehots = [(lane == i) for i in ids]
    multi = jnp.zeros((n, LANES), F32)
    for oh in onehots:
        multi = multi + oh.astype(F32)
    counts = jnp.sum(multi, axis=0, keepdims=True)
    before = _dot(ltri_ref[...], multi.astype(BF16))
    seg = jnp.ceil(counts * (1.0 / SEG_ALIGN)) * SEG_ALIGN
    seg_rows = jnp.broadcast_to(seg, (BF16_ROWS, LANES)).astype(BF16)
    seg_off = _dot(seg_rows, utri_ref[...])[0:1, :]
    slot_of = seg_off + before
    meta = jnp.zeros((n, LANES), F32)
    for k in range(TOP_K):
        slot = jnp.sum(jnp.where(onehots[k], slot_of, 0.0), axis=-1, keepdims=True)
        meta = jnp.where(lane == k, ids[k], meta)
        meta = jnp.where(lane == TOP_K + k, gates[k], meta)
        meta = jnp.where(lane == 2 * TOP_K + k, slot, meta)
    meta_ref[...] = meta
    cnt_ref[...] = jnp.broadcast_to(counts, (SUBLANES, LANES))


def _router(x1, nw, router_w, router_b):
    t = x1.shape[0]
    n_tiles = t // TOKEN_TILE
    rw = jnp.zeros((D_MODEL, LANES), F32).at[:, :N_EXPERTS].set(router_w)
    rw_hi, rw_lo = _split2(rw)
    rb = jnp.full((1, LANES), -1e30, F32).at[0, :N_EXPERTS].set(router_b)
    idx = np.arange(TOKEN_TILE)
    ltri = jnp.asarray(idx[None, :] < idx[:, None], BF16)
    lid = np.arange(LANES)
    utri = jnp.asarray(lid[:, None] < lid[None, :], BF16)
    const = lambda shape: pl.BlockSpec(shape, lambda i: (0,) * len(shape))
    return pl.pallas_call(
        _router_kernel,
        grid=(n_tiles,),
        in_specs=[
            pl.BlockSpec((TOKEN_TILE, D_MODEL), lambda i: (i, 0)),
            const((1, D_MODEL)), const((D_MODEL, LANES)), const((D_MODEL, LANES)), const((1, LANES)),
            const((TOKEN_TILE, TOKEN_TILE)), const((LANES, LANES)),
        ],
        out_specs=[
            pl.BlockSpec((TOKEN_TILE, D_MODEL), lambda i: (i, 0)),
            pl.BlockSpec((TOKEN_TILE, LANES), lambda i: (i, 0)),
            pl.BlockSpec((None, SUBLANES, LANES), lambda i: (i, 0, 0)),
        ],
        out_shape=[
            jax.ShapeDtypeStruct((t, D_MODEL), BF16),
            jax.ShapeDtypeStruct((t, LANES), F32),
            jax.ShapeDtypeStruct((n_tiles, SUBLANES, LANES), F32),
        ],
        compiler_params=pltpu.CompilerParams(
            dimension_semantics=("arbitrary",), vmem_limit_bytes=VMEM_LIMIT_BYTES),
        name="router",
    )(x1, nw, rw_hi, rw_lo, rb, ltri, utri)


def _segment_tables(cnt):
    seg = (cnt + SEG_ALIGN - 1) // SEG_ALIGN * SEG_ALIGN
    src = jnp.cumsum(seg, axis=1) - seg
    tot = jnp.sum(seg, axis=0)
    cap = (tot + ROW_TILE - 1) // ROW_TILE * ROW_TILE
    base = jnp.cumsum(cap) - cap
    dst = base[None, :] + jnp.cumsum(seg, axis=0) - seg
    n_used = jnp.sum(cap) // ROW_TILE
    fill = base + tot
    i32 = lambda a: a.astype(jnp.int32)
    return (i32(seg).reshape(-1), i32(src).reshape(-1), i32(dst).reshape(-1), i32(fill), i32(cap - tot),
            i32(base), i32(cap // ROW_TILE), i32(n_used).reshape(1))


def _for_each_chunk(n, fn):
    for size in SEG_SIZES:
        done = n & ~(2 * size - 1)

        @pl.when((n & size) != 0)
        def _():
            fn(done, size)


def _segment_copies(seg_ref, src_ref, dst_ref, tile, make_copy, act):
    def per_expert(e, carry):
        j = tile * N_EXPERTS + e
        _for_each_chunk(seg_ref[j], lambda done, size: act(make_copy(
            pl.multiple_of(src_ref[j] + done, SEG_ALIGN), pl.multiple_of(dst_ref[j] + done, SEG_ALIGN), size)))
        return carry
    lax.fori_loop(0, N_EXPERTS, per_expert, 0)


DISPATCH_ROWS = 32


def _dispatch_kernel(seg_ref, src_ref, dst_ref, fill_ref, gap_ref, nu_ref, slot_ref, xn_ref, xs_ref, sorted_ref,
                     onehot_ref, zero_ref, sem):
    tile = pl.program_id(0)

    @pl.when(tile == 0)
    def _():
        zero_ref[...] = jnp.zeros_like(zero_ref)

        def fill_copy(row, size):
            return pltpu.make_async_copy(zero_ref.at[pl.ds(0, size)],
                                         xs_ref.at[pl.ds(pl.multiple_of(row, SEG_ALIGN), size)], sem)

        def gaps(act):
            def per_expert(e, carry):
                _for_each_chunk(gap_ref[e], lambda done, size: act(fill_copy(fill_ref[e] + done, size)))
                return carry
            lax.fori_loop(0, N_EXPERTS, per_expert, 0)

        def tail(act):
            n_tail = xs_ref.shape[0] // ROW_TILE - nu_ref[0]
            lax.fori_loop(0, n_tail, lambda i, c: (act(fill_copy((nu_ref[0] + i) * ROW_TILE, ROW_TILE)), c)[1], 0)

        gaps(lambda c: c.start())
        tail(lambda c: c.start())
        gaps(lambda c: c.wait())
        tail(lambda c: c.wait())

    slots = slot_ref[...]
    slot_rows = [jnp.broadcast_to(slots[k:k + 1, :], (DISPATCH_ROWS, TOKEN_TILE)) for k in range(TOP_K)]

    def build(rb, carry):
        r0 = pl.multiple_of(rb * DISPATCH_ROWS, DISPATCH_ROWS)
        row = (lax.broadcasted_iota(jnp.int32, (DISPATCH_ROWS, TOKEN_TILE), 0) + r0).astype(F32)
        hot = jnp.zeros((DISPATCH_ROWS, TOKEN_TILE), F32)
        for k in range(TOP_K):
            hot = jnp.where(row == slot_rows[k], 1.0, hot)
        onehot_ref[pl.ds(r0, DISPATCH_ROWS), :] = hot.astype(BF16)
        return carry

    lax.fori_loop(0, TILE_CAP // DISPATCH_ROWS, build, 0)
    sorted_ref[...] = _dot(onehot_ref[...], xn_ref[...]).astype(BF16)

    def make_copy(src, dst, size):
        return pltpu.make_async_copy(sorted_ref.at[pl.ds(src, size)], xs_ref.at[pl.ds(dst, size)], sem)

    _segment_copies(seg_ref, src_ref, dst_ref, tile, make_copy, lambda c: c.start())
    _segment_copies(seg_ref, src_ref, dst_ref, tile, make_copy, lambda c: c.wait())


def _dispatch(tables, slot_t, xn, n_rows_total):
    t = xn.shape[0]
    n_tiles = t // TOKEN_TILE
    return pl.pallas_call(
        _dispatch_kernel,
        grid_spec=pltpu.PrefetchScalarGridSpec(
            num_scalar_prefetch=len(tables),
            grid=(n_tiles,),
            in_specs=[
                pl.BlockSpec((SUBLANES, TOKEN_TILE), lambda i, *_: (0, i)),
                pl.BlockSpec((TOKEN_TILE, D_MODEL), lambda i, *_: (i, 0)),
            ],
            out_specs=pl.BlockSpec(memory_space=pl.ANY),
            scratch_shapes=[
                pltpu.VMEM((TILE_CAP, D_MODEL), BF16),
                pltpu.VMEM((TILE_CAP, TOKEN_TILE), BF16),
                pltpu.VMEM((ROW_TILE, D_MODEL), BF16),
                pltpu.SemaphoreType.DMA(()),
            ],
        ),
        out_shape=jax.ShapeDtypeStruct((n_rows_total, D_MODEL), BF16),
        compiler_params=pltpu.CompilerParams(
            dimension_semantics=("arbitrary",), vmem_limit_bytes=VMEM_LIMIT_BYTES),
        name="dispatch",
    )(*tables, slot_t, xn)


def _experts_kernel(base_ref, nt_ref, nu_ref, wg_ref, bg_ref, wu_ref, bu_ref, wd_ref, bd_ref, xs_ref, zs_ref,
                    wg_bf, wu_bf, wd_bf, xbuf, zbuf, in_sem, out_sem):
    e = pl.program_id(0)
    n = nt_ref[e]
    base = base_ref[e]

    def x_copy(i, slot):
        rows = pl.ds(pl.multiple_of(base + i * ROW_TILE, ROW_TILE), ROW_TILE)
        return pltpu.make_async_copy(xs_ref.at[rows], xbuf.at[slot], in_sem.at[slot])

    def z_copy(row, slot):
        rows = pl.ds(pl.multiple_of(row, ROW_TILE), ROW_TILE)
        return pltpu.make_async_copy(zbuf.at[slot], zs_ref.at[rows], out_sem.at[slot])

    @pl.when(n > 0)
    def _():
        x_copy(0, 0).start()
        wg_bf[...] = wg_ref[...].astype(BF16)
        wu_bf[...] = wu_ref[...].astype(BF16)
        wd_bf[...] = wd_ref[...].astype(BF16)

        def tile(i, carry):
            slot = i % 2
            x_copy(i, slot).wait()

            @pl.when(i + 1 < n)
            def _():
                x_copy(i + 1, 1 - slot).start()

            @pl.when(i >= 2)
            def _():
                z_copy(base, slot).wait()

            x = xbuf[slot]
            a = jnp.minimum(_dot(x, wg_bf[...]) + bg_ref[...], SWIGLU_LIMIT)
            u = jnp.clip(_dot(x, wu_bf[...]) + bu_ref[...], -SWIGLU_LIMIT, SWIGLU_LIMIT)
            hid = ((u + 1.0) * a * jax.nn.sigmoid(SWIGLU_ALPHA * a)).astype(BF16)
            zbuf[slot] = (_dot(hid, wd_bf[...]) + bd_ref[...]).astype(BF16)
            z_copy(base + i * ROW_TILE, slot).start()
            return carry

        lax.fori_loop(0, n, tile, 0)

        @pl.when(n >= 2)
        def _():
            z_copy(base, n % 2).wait()
        z_copy(base, (n - 1) % 2).wait()

    @pl.when(e == pl.num_programs(0) - 1)
    def _():
        zbuf[0] = jnp.zeros((ROW_TILE, D_MODEL), BF16)
        n_tail = zs_ref.shape[0] // ROW_TILE - nu_ref[0]
        tail = lambda act: lax.fori_loop(
            0, n_tail, lambda i, c: (act(z_copy((nu_ref[0] + i) * ROW_TILE, 0)), c)[1], 0)
        tail(lambda c: c.start())
        tail(lambda c: c.wait())


def _experts(base, n_tiles, n_used, xs, w_gate, b_gate, w_up, b_up, w_down, b_down):
    w_spec = pl.BlockSpec((None, D_MODEL, D_MODEL), lambda e, *_: (e, 0, 0))
    b_spec = pl.BlockSpec((None, 1, D_MODEL), lambda e, *_: (e, 0, 0))
    any_spec = pl.BlockSpec(memory_space=pl.ANY)
    return pl.pallas_call(
        _experts_kernel,
        grid_spec=pltpu.PrefetchScalarGridSpec(
            num_scalar_prefetch=3,
            grid=(N_EXPERTS,),
            in_specs=[w_spec, b_spec, w_spec, b_spec, w_spec, b_spec, any_spec],
            out_specs=any_spec,
            scratch_shapes=[pltpu.VMEM((D_MODEL, D_MODEL), BF16) for _ in range(3)] + [
                pltpu.VMEM((2, ROW_TILE, D_MODEL), BF16), pltpu.VMEM((2, ROW_TILE, D_MODEL), BF16),
                pltpu.SemaphoreType.DMA((2,)), pltpu.SemaphoreType.DMA((2,))],
        ),
        out_shape=jax.ShapeDtypeStruct(xs.shape, BF16),
        compiler_params=pltpu.CompilerParams(
            dimension_semantics=("arbitrary",), vmem_limit_bytes=VMEM_LIMIT_BYTES),
        name="experts",
    )(base, n_tiles, n_used, w_gate, b_gate[:, None, :], w_up, b_up[:, None, :], w_down, b_down[:, None, :], xs)


COMBINE_ROWS = 32
COMBINE_COLS = 512


def _combine_kernel(n_first, seg_ref, src_ref, dst_ref, meta_ref, x1_ref, fw_ref, zs_ref, ya_ref, yb_ref,
                    sorted_ref, w_ref, sem):
    tile = pl.program_id(0)
    sorted_ref[...] = jnp.zeros_like(sorted_ref)

    def make_copy(src, dst, size):
        return pltpu.make_async_copy(zs_ref.at[pl.ds(dst, size)], sorted_ref.at[pl.ds(src, size)], sem)

    _segment_copies(seg_ref, src_ref, dst_ref, tile, make_copy, lambda c: c.start())

    def build(rb, carry):
        r0 = pl.multiple_of(rb * COMBINE_ROWS, COMBINE_ROWS)
        meta = meta_ref[pl.ds(r0, COMBINE_ROWS), :]
        gates = [jnp.broadcast_to(meta[:, TOP_K + k:TOP_K + k + 1], (COMBINE_ROWS, COMBINE_COLS))
                 for k in range(TOP_K)]
        slots = [jnp.broadcast_to(meta[:, 2 * TOP_K + k:2 * TOP_K + k + 1], (COMBINE_ROWS, COMBINE_COLS))
                 for k in range(TOP_K)]
        for cb in range(TILE_CAP // COMBINE_COLS):
            col = (lax.broadcasted_iota(jnp.int32, (COMBINE_ROWS, COMBINE_COLS), 1) + cb * COMBINE_COLS).astype(F32)
            w = jnp.zeros((COMBINE_ROWS, COMBINE_COLS), F32)
            for k in range(TOP_K):
                w = jnp.where(col == slots[k], gates[k], w)
            w_ref[pl.ds(r0, COMBINE_ROWS), cb * COMBINE_COLS:(cb + 1) * COMBINE_COLS] = w.astype(BF16)
        return carry

    lax.fori_loop(0, TOKEN_TILE // COMBINE_ROWS, build, 0)
    _segment_copies(seg_ref, src_ref, dst_ref, tile, make_copy, lambda c: c.wait())
    y = _rmsnorm(x1_ref[...] + _dot(w_ref[...], sorted_ref[...]), fw_ref[...])

    @pl.when(tile < n_first)
    def _():
        ya_ref[...] = y

    @pl.when(tile >= n_first)
    def _():
        yb_ref[...] = y


def _combine(tables, meta, x1, final_w, zs, rows_first):
    seg, src, dst = tables
    t = x1.shape[0]
    assert rows_first % TOKEN_TILE == 0 and 0 < rows_first < t and TILE_CAP % COMBINE_COLS == 0
    n_tiles = t // TOKEN_TILE
    n_first = rows_first // TOKEN_TILE
    return pl.pallas_call(
        functools.partial(_combine_kernel, n_first),
        grid_spec=pltpu.PrefetchScalarGridSpec(
            num_scalar_prefetch=3,
            grid=(n_tiles,),
            in_specs=[
                pl.BlockSpec((TOKEN_TILE, LANES), lambda i, *_: (i, 0)),
                pl.BlockSpec((TOKEN_TILE, D_MODEL), lambda i, *_: (i, 0)),
                pl.BlockSpec((1, D_MODEL), lambda i, *_: (0, 0)),
                pl.BlockSpec(memory_space=pl.ANY),
            ],
            out_specs=[
                pl.BlockSpec((TOKEN_TILE, D_MODEL), lambda i, *_: (jnp.minimum(i, n_first - 1), 0)),
                pl.BlockSpec((TOKEN_TILE, D_MODEL), lambda i, *_: (jnp.maximum(i - n_first, 0), 0)),
            ],
            scratch_shapes=[pltpu.VMEM((TILE_CAP, D_MODEL), BF16), pltpu.VMEM((TOKEN_TILE, TILE_CAP), BF16),
                            pltpu.SemaphoreType.DMA(())],
        ),
        out_shape=[jax.ShapeDtypeStruct((rows_first, D_MODEL), F32),
                   jax.ShapeDtypeStruct((t - rows_first, D_MODEL), F32)],
        compiler_params=pltpu.CompilerParams(
            dimension_semantics=("arbitrary",), vmem_limit_bytes=VMEM_LIMIT_BYTES),
        name="combine",
    )(seg, src, dst, meta, x1, final_w, zs)


def _moe_and_final_norm(x1, rows_first, norm_ffn_w, router_w, router_b, w_gate, b_gate, w_up, b_up, w_down, b_down,
                        final_w):
    t = x1.shape[0]
    assert t % TOKEN_TILE == 0
    n_tiles = t // TOKEN_TILE
    n_rows_total = TOP_K * t + n_tiles * N_EXPERTS * (SEG_ALIGN - 1) + N_EXPERTS * (ROW_TILE - 1)
    n_rows_total = (n_rows_total + ROW_TILE - 1) // ROW_TILE * ROW_TILE
    xn, meta, cnt = _router(x1, norm_ffn_w.reshape(1, D_MODEL), router_w, router_b)
    cnt = cnt[:, 0, :N_EXPERTS].astype(jnp.int32)
    seg, src, dst, fill, gap, base, n_row_tiles, n_used = _segment_tables(cnt)
    slot_t = jnp.zeros((SUBLANES, t), F32).at[:TOP_K, :].set(meta[:, 2 * TOP_K:3 * TOP_K].T)
    xs = _dispatch((seg, src, dst, fill, gap, n_used), slot_t, xn, n_rows_total)
    zs = _experts(base, n_row_tiles, n_used, xs, w_gate, b_gate, w_up, b_up, w_down, b_down)
    return _combine((seg, src, dst), meta, x1, final_w.reshape(1, D_MODEL), zs, rows_first)


def kernel(x_prompt, x_sample, state_conv, state_hgrn, lb_logits, norm_mix_w, w_in, conv_w, conv_b, gnorm_w,
           w_out, norm_ffn_w, router_w, router_b, w_gate, b_gate, w_up, b_up, w_down, b_down, final_norm_w):
    assert norm_mix_w.shape[0] == 1 and lb_logits.shape[0] == 2, "single-layer step"
    b, l, _ = x_prompt.shape
    nb, ls, _ = x_sample.shape
    lbl = lb_logits.astype(F32)
    nw = norm_mix_w[0].reshape(1, D_MODEL)
    w_in_bf = w_in[0].astype(BF16)
    w_out_bf = w_out[0].astype(BF16)
    cw, cb = conv_w[0], conv_b[0].reshape(1, D_CONV)
    gnw = gnorm_w[0].reshape(1, D_HGRN)
    rows_p, rows_s = b * l, nb * ls
    x1, conv_p, hgrn_p = _mix_prompt(x_prompt, lbl, nw, w_in_bf, cw, cb, gnw, w_out_bf, rows_p + rows_s)
    x1, conv_s, hgrn_s = _mix_sample(x_sample, state_conv[0], state_hgrn[0], lbl, nw, w_in_bf, cw, cb, gnw,
                                     w_out_bf, x1, rows_p)
    y_p, y_s = _moe_and_final_norm(x1, rows_p, norm_ffn_w[0], router_w[0], router_b[0], w_gate[0], b_gate[0],
                                   w_up[0], b_up[0], w_down[0], b_down[0], final_norm_w)
    conv_p = conv_p[:, SUBLANES - 2:, :]
    conv_s = conv_s[:, ls - 2:, :]
    return (y_p.reshape(b, l, D_MODEL), y_s.reshape(nb, ls, D_MODEL), conv_p[None], hgrn_p[None], conv_s[None],
            hgrn_s[None])
```

```python
import functools

import numpy as np
import jax
import jax.numpy as jnp
from jax import lax
from jax.experimental import pallas as pl
from jax.experimental.pallas import tpu as pltpu

F32 = jnp.float32
BF16 = jnp.bfloat16

D_MODEL = 1024
D_CONV = 512
D_HGRN = 512
N_HEADS = 4
D_HEAD = 128
D_PROJ = 3 * D_CONV + 4 * D_HGRN
N_EXPERTS = 32
TOP_K = 4
SWIGLU_LIMIT = 7.0
SWIGLU_ALPHA = 1.702
EPS = 1e-6
PROMPT_CHUNK = 64

LANES = 128
SUBLANES = 8
BF16_ROWS = 16
VMEM_LIMIT_BYTES = 56 * 1024 * 1024

MIX_SLAB = 256
MIX_ROWS = 512
MIX_SEQS = 16
TOKEN_TILE = 512
SEG_ALIGN = BF16_ROWS
ROW_TILE = 512
SEG_SIZES = tuple(SEG_ALIGN << i for i in reversed(range(6)))
TILE_CAP = TOP_K * TOKEN_TILE + N_EXPERTS * SEG_ALIGN

NT_DIMS = (((1,), (1,)), ((), ()))
TN_DIMS = (((0,), (0,)), ((), ()))


def _dot(a, b):
    return jnp.dot(a, b, preferred_element_type=F32)


def _dot_nt(a, b):
    return lax.dot_general(a, b, NT_DIMS, preferred_element_type=F32)


def _dot_tn(a, b):
    return lax.dot_general(a, b, TN_DIMS, preferred_element_type=F32)


def _split2(x):
    hi = x.astype(BF16)
    lo = (x - hi.astype(F32)).astype(BF16)
    return hi, lo


def _rmsnorm(x, w):
    return x * lax.rsqrt(jnp.mean(x * x, axis=-1, keepdims=True) + EPS) * w


def _level_exponent_matrix(n, chunk, h):
    x = np.zeros((n, n), np.float32)
    for t in range(n):
        base = t - t % (2 * h)
        m = base + h - 1
        if t % (2 * h) >= h:
            x[t, m + 1:t + 1] = 1.0
        else:
            x[t, t + 1:m + 1] = 1.0
    return x


def _mix_constants(n, chunk, mxu_levels, all_levels):
    t = np.arange(n)
    same_chunk = (t[:, None] // chunk) == (t[None, :] // chunk)
    tri = (same_chunk & (t[None, :] <= t[:, None])).astype(np.float32)
    suf = (same_chunk & (t[None, :] > t[:, None])).astype(np.float32)
    cmat = np.concatenate([tri, suf] + [_level_exponent_matrix(n, chunk, h) for h in mxu_levels], axis=0)
    masks = []
    for h in all_levels:
        blk = (t[:, None] // (2 * h)) == (t[None, :] // (2 * h))
        masks.append((blk & ((t[:, None] % (2 * h)) >= h) & ((t[None, :] % (2 * h)) < h)).astype(np.float32))
    masks.append(np.eye(n, dtype=np.float32))
    return jnp.asarray(cmat, BF16), jnp.asarray(np.stack(masks), F32)


def _vpu_level_exponent(a, h):
    n = a.shape[0]
    pieces = []
    for j in range(n // (2 * h)):
        b = j * 2 * h
        ref = a[b + h - 1:b + h, :]
        pieces.append(ref - a[b:b + h, :])
        pieces.append(a[b + h:b + 2 * h, :] - ref)
    return jnp.concatenate(pieces, axis=0)


def _forget_lower_bound(lbl):
    m = jnp.max(lbl, axis=0, keepdims=True)
    e = jnp.exp(lbl - m)
    return e[0:1, :] / jnp.sum(e, axis=0, keepdims=True)


def _projections(x, nw, w_in_ref):
    h = _rmsnorm(x, nw).astype(BF16)
    return _dot(h, w_in_ref[...])


def _gates(proj, lb):
    q = proj[:, 3 * D_CONV:3 * D_CONV + D_HGRN]
    ff = proj[:, 3 * D_CONV + D_HGRN:3 * D_CONV + 2 * D_HGRN]
    vi = proj[:, 3 * D_CONV + 2 * D_HGRN:3 * D_CONV + 3 * D_HGRN]
    g = proj[:, 3 * D_CONV + 3 * D_HGRN:]
    e = jnp.exp(-jnp.abs(ff))
    r = 1.0 / (1.0 + e)
    pos = ff >= 0
    sig = jnp.where(pos, r, e * r)
    sig_neg = jnp.where(pos, e * r, r)
    logf = jnp.log(lb + (1.0 - lb) * sig)
    kk = (1.0 - lb) * sig_neg
    return q, logf, kk, vi, g


def _intra_scores_times_v(q, kk, vi, exps, masks_ref):
    n = q.shape[0]
    sc = [jnp.zeros((n, n), F32) for _ in range(N_HEADS)]
    for l, ex in enumerate(exps):
        if ex is None:
            qh, kh = q.astype(BF16), kk.astype(BF16)
        else:
            w = jnp.exp(ex)
            qh, kh = (q * w).astype(BF16), (kk * w).astype(BF16)
        mask = masks_ref[l]
        for hd in range(N_HEADS):
            hs = slice(hd * D_HEAD, (hd + 1) * D_HEAD)
            sc[hd] = sc[hd] + mask * _dot_nt(qh[:, hs], kh[:, hs])
    vb = vi.astype(BF16)
    return [_dot(sc[hd].astype(BF16), vb[:, hd * D_HEAD:(hd + 1) * D_HEAD]) for hd in range(N_HEADS)]


def _head_out(o, g, gnw):
    parts = []
    for hd in range(N_HEADS):
        oh = o[:, hd * D_HEAD:(hd + 1) * D_HEAD]
        parts.append(oh * lax.rsqrt(jnp.mean(oh * oh, axis=-1, keepdims=True) + EPS))
    on = jnp.concatenate(parts, axis=1)
    return on * gnw * (g * jax.nn.sigmoid(g))


def _pad_rows_bf16(a):
    return jnp.concatenate([a, jnp.zeros_like(a)], axis=0).astype(BF16)


PROMPT_VPU_LEVELS = (32, 16, 8)
MXU_LEVELS = (4, 2, 1)


def _mix_prompt_kernel(n_r, n_steps, *refs):
    step = pl.program_id(0)

    @pl.when(step < n_steps)
    def _():
        _mix_prompt_step(step % n_r, n_r, *refs)

    @pl.when(step >= n_steps)
    def _():
        x1_ref = refs[10]
        x1_ref[...] = jnp.zeros_like(x1_ref)


def _mix_prompt_step(r, n_r, x_ref, lbl_ref, nw_ref, win_ref, cw_ref, cb_ref, gnw_ref, wout_ref, cmat_ref,
                     masks_ref, x1_ref, nconv_ref, nstate_ref, st_ref, tail_ref, o_ref):
    rows = x_ref.shape[0]

    @pl.when(r == 0)
    def _():
        st_ref[...] = jnp.zeros_like(st_ref)
        tail_ref[...] = jnp.zeros_like(tail_ref)

    x = x_ref[...]
    proj = _projections(x, nw_ref[...], win_ref)
    lb = _forget_lower_bound(lbl_ref[...])

    bg = proj[:, 0:D_CONV]
    u = proj[:, D_CONV:2 * D_CONV] * proj[:, 2 * D_CONV:3 * D_CONV]
    rid = lax.broadcasted_iota(jnp.int32, (rows, 1), 0)
    t0 = tail_ref[SUBLANES - 2:SUBLANES - 1, :]
    t1 = tail_ref[SUBLANES - 1:SUBLANES, :]
    u1 = jnp.where(rid == 0, t1, pltpu.roll(u, 1, 0))
    u2 = jnp.where(rid == 0, t0, jnp.where(rid == 1, t1, pltpu.roll(u, 2, 0)))
    cw = cw_ref[...]
    y_conv = bg * (cw[0:1, :] * u2 + cw[1:2, :] * u1 + cw[2:3, :] * u + cb_ref[...])
    tail_ref[...] = u[rows - SUBLANES:rows, :]

    q, logf, kk, vi, g = _gates(proj, lb)
    lf_hi, lf_lo = _split2(logf)
    n_chunks = MIX_SLAB // PROMPT_CHUNK
    for s in range(rows // MIX_SLAB):
        sl = slice(s * MIX_SLAB, (s + 1) * MIX_SLAB)
        qs, ks, vs = q[sl], kk[sl], vi[sl]
        ex = _dot(cmat_ref[...], lf_hi[sl]) + _dot(cmat_ref[...], lf_lo[sl])
        a_pre = ex[0:MIX_SLAB]
        a_suf = ex[MIX_SLAB:2 * MIX_SLAB]
        exps = [_vpu_level_exponent(a_pre, h) for h in PROMPT_VPU_LEVELS]
        exps += [ex[(2 + i) * MIX_SLAB:(3 + i) * MIX_SLAB] for i in range(len(MXU_LEVELS))]
        exps.append(None)
        intra = _intra_scores_times_v(qs, ks, vs, exps, masks_ref)
        ea = jnp.exp(a_pre)
        qa = (qs * ea).astype(BF16)
        kb = (ks * jnp.exp(a_suf)).astype(BF16)
        vb = vs.astype(BF16)
        for c in range(n_chunks):
            cr = slice(c * PROMPT_CHUNK, (c + 1) * PROMPT_CHUNK)
            last = (c + 1) * PROMPT_CHUNK - 1
            for hd in range(N_HEADS):
                hs = slice(hd * D_HEAD, (hd + 1) * D_HEAD)
                st = st_ref[hd]
                inter = _dot_nt(qa[cr, hs], st.astype(BF16))
                o_ref[s * MIX_SLAB + c * PROMPT_CHUNK:s * MIX_SLAB + (c + 1) * PROMPT_CHUNK, hs] = (
                    inter + intra[hd][cr, :])
                st_ref[hd] = st * ea[last:last + 1, hs] + _dot_tn(vb[cr, hs], kb[cr, hs])

    o = _head_out(o_ref[...], g, gnw_ref[...])
    mix_in = jnp.concatenate([y_conv, o], axis=1).astype(BF16)
    x1_ref[...] = x + _dot(mix_in, wout_ref[...])

    @pl.when(r == n_r - 1)
    def _():
        nconv_ref[...] = u[rows - SUBLANES:rows, :]
        for hd in range(N_HEADS):
            nstate_ref[hd] = st_ref[hd].T


def _mix_prompt(x, lbl, nw, w_in, cw, cb, gnw, w_out, total_rows):
    b, l, _ = x.shape
    assert l % MIX_ROWS == 0 and MIX_ROWS % MIX_SLAB == 0 and l % PROMPT_CHUNK == 0
    assert total_rows % MIX_ROWS == 0
    n_r = l // MIX_ROWS
    n_steps = b * n_r
    seq = lambda s: jnp.minimum(s, n_steps - 1) // n_r
    cmat, masks = _mix_constants(MIX_SLAB, PROMPT_CHUNK, MXU_LEVELS, PROMPT_VPU_LEVELS + MXU_LEVELS)
    const = lambda shape: pl.BlockSpec(shape, lambda s: (0,) * len(shape))
    return pl.pallas_call(
        functools.partial(_mix_prompt_kernel, n_r, n_steps),
        grid=(total_rows // MIX_ROWS,),
        in_specs=[
            pl.BlockSpec((None, MIX_ROWS, D_MODEL), lambda s: (seq(s), jnp.minimum(s, n_steps - 1) % n_r, 0)),
            const((2, D_HGRN)), const((1, D_MODEL)), const((D_MODEL, D_PROJ)), const((3, D_CONV)),
            const((1, D_CONV)), const((1, D_HGRN)), const((D_MODEL, D_MODEL)), const(cmat.shape),
            const(masks.shape),
        ],
        out_specs=[
            pl.BlockSpec((MIX_ROWS, D_MODEL), lambda s: (s, 0)),
            pl.BlockSpec((None, SUBLANES, D_CONV), lambda s: (seq(s), 0, 0)),
            pl.BlockSpec((None, N_HEADS, D_HEAD, D_HEAD), lambda s: (seq(s), 0, 0, 0)),
        ],
        out_shape=[
            jax.ShapeDtypeStruct((total_rows, D_MODEL), F32),
            jax.ShapeDtypeStruct((b, SUBLANES, D_CONV), F32),
            jax.ShapeDtypeStruct((b, N_HEADS, D_HEAD, D_HEAD), F32),
        ],
        scratch_shapes=[
            pltpu.VMEM((N_HEADS, D_HEAD, D_HEAD), F32),
            pltpu.VMEM((SUBLANES, D_CONV), F32),
            pltpu.VMEM((MIX_ROWS, D_HGRN), F32),
        ],
        compiler_params=pltpu.CompilerParams(
            dimension_semantics=("arbitrary",), vmem_limit_bytes=VMEM_LIMIT_BYTES),
        name="mix_prompt",
    )(x, lbl, nw, w_in, cw, cb, gnw, w_out, cmat, masks)


def _mix_sample_kernel(x_ref, cs_ref, hs_ref, lbl_ref, nw_ref, win_ref, cw_ref, cb_ref, gnw_ref, wout_ref,
                       cmat_ref, masks_ref, x1_in_ref, x1_ref, nconv_ref, nstate_ref, o_ref):
    del x1_in_ref
    nseq, length, _ = nconv_ref.shape
    rows = nseq * length
    x = x_ref[...]
    proj = _projections(x, nw_ref[...], win_ref)
    lb = _forget_lower_bound(lbl_ref[...])

    bg = proj[:, 0:D_CONV]
    u = proj[:, D_CONV:2 * D_CONV] * proj[:, 2 * D_CONV:3 * D_CONV]
    cs = cs_ref[...]
    expand = lambda a: jnp.broadcast_to(a, (nseq, length, D_CONV)).reshape(rows, D_CONV)
    t0 = expand(cs[:, 0:1, :])
    t1 = expand(cs[:, 1:2, :])
    pos = lax.broadcasted_iota(jnp.int32, (rows, 1), 0) % length
    u1 = jnp.where(pos == 0, t1, pltpu.roll(u, 1, 0))
    u2 = jnp.where(pos == 0, t0, jnp.where(pos == 1, t1, pltpu.roll(u, 2, 0)))
    cw = cw_ref[...]
    y_conv = bg * (cw[0:1, :] * u2 + cw[1:2, :] * u1 + cw[2:3, :] * u + cb_ref[...])
    nconv_ref[...] = u.reshape(nseq, length, D_CONV)

    q, logf, kk, vi, g = _gates(proj, lb)
    lf_hi, lf_lo = _split2(logf)
    ex = _dot(cmat_ref[...], lf_hi) + _dot(cmat_ref[...], lf_lo)
    a_pre = ex[0:rows]
    a_suf = ex[rows:2 * rows]
    exps = [ex[(2 + i) * rows:(3 + i) * rows] for i in range(len(MXU_LEVELS))] + [None]
    intra = _intra_scores_times_v(q, kk, vi, exps, masks_ref)
    ea = jnp.exp(a_pre)
    qa = q * ea
    kb = kk * jnp.exp(a_suf)
    for s in range(nseq):
        cr = slice(s * length, (s + 1) * length)
        last = (s + 1) * length - 1
        for hd in range(N_HEADS):
            hs = slice(hd * D_HEAD, (hd + 1) * D_HEAD)
            st = hs_ref[s, hd].T
            inter = _dot_nt(_pad_rows_bf16(qa[cr, hs]), st.astype(BF16))[0:length, :]
            o_ref[cr, hs] = inter + intra[hd][cr, :]
            st_new = st * ea[last:last + 1, hs] + _dot_tn(_pad_rows_bf16(vi[cr, hs]), _pad_rows_bf16(kb[cr, hs]))
            nstate_ref[s, hd] = st_new.T

    o = _head_out(o_ref[...], g, gnw_ref[...])
    mix_in = jnp.concatenate([y_conv, o], axis=1).astype(BF16)
    x1_ref[...] = x + _dot(mix_in, wout_ref[...])


def _mix_sample(x, conv_state, hgrn_state, lbl, nw, w_in, cw, cb, gnw, w_out, x1_flat, row_offset):
    nb, length, _ = x.shape
    rows = MIX_SEQS * length
    assert nb % MIX_SEQS == 0 and length == SUBLANES and row_offset % rows == 0
    first_block = row_offset // rows
    cmat, masks = _mix_constants(rows, length, MXU_LEVELS, MXU_LEVELS)
    const = lambda shape: pl.BlockSpec(shape, lambda i: (0,) * len(shape))
    return pl.pallas_call(
        _mix_sample_kernel,
        grid=(nb // MIX_SEQS,),
        in_specs=[
            pl.BlockSpec((rows, D_MODEL), lambda i: (i, 0)),
            pl.BlockSpec((MIX_SEQS, 2, D_CONV), lambda i: (i, 0, 0)),
            pl.BlockSpec((MIX_SEQS, N_HEADS, D_HEAD, D_HEAD), lambda i: (i, 0, 0, 0)),
            const((2, D_HGRN)), const((1, D_MODEL)), const((D_MODEL, D_PROJ)), const((3, D_CONV)),
            const((1, D_CONV)), const((1, D_HGRN)), const((D_MODEL, D_MODEL)), const(cmat.shape),
            const(masks.shape), pl.BlockSpec(memory_space=pl.ANY),
        ],
        out_specs=[
            pl.BlockSpec((rows, D_MODEL), lambda i: (first_block + i, 0)),
            pl.BlockSpec((MIX_SEQS, length, D_CONV), lambda i: (i, 0, 0)),
            pl.BlockSpec((MIX_SEQS, N_HEADS, D_HEAD, D_HEAD), lambda i: (i, 0, 0, 0)),
        ],
        out_shape=[
            jax.ShapeDtypeStruct(x1_flat.shape, F32),
            jax.ShapeDtypeStruct((nb, length, D_CONV), F32),
            jax.ShapeDtypeStruct((nb, N_HEADS, D_HEAD, D_HEAD), F32),
        ],
        scratch_shapes=[pltpu.VMEM((rows, D_HGRN), F32)],
        input_output_aliases={12: 0},
        compiler_params=pltpu.CompilerParams(
            dimension_semantics=("arbitrary",), vmem_limit_bytes=VMEM_LIMIT_BYTES),
        name="mix_sample",
    )(x.reshape(nb * length, D_MODEL), conv_state, hgrn_state, lbl, nw, w_in, cw, cb, gnw, w_out, cmat, masks,
      x1_flat)


def _router_kernel(x1_ref, nw_ref, rw_hi_ref, rw_lo_ref, rb_ref, ltri_ref, utri_ref, xn_ref, meta_ref, cnt_ref):
    n = x1_ref.shape[0]
    xn = _rmsnorm(x1_ref[...], nw_ref[...])
    xn_ref[...] = xn.astype(BF16)
    x_hi, x_lo = _split2(xn)
    logits = (_dot(x_hi, rw_hi_ref[...]) + _dot(x_lo, rw_hi_ref[...]) + _dot(x_hi, rw_lo_ref[...])
              + rb_ref[...])
    lane = lax.broadcasted_iota(jnp.int32, (n, LANES), 1).astype(F32)
    work = logits
    vals, ids = [], []
    for _ in range(TOP_K):
        m = jnp.max(work, axis=-1, keepdims=True)
        i = jnp.min(jnp.where(work == m, lane, float(LANES)), axis=-1, keepdims=True)
        vals.append(m)
        ids.append(i)
        work = jnp.where(lane == i, -jnp.inf, work)
    es = [jnp.exp(v - vals[0]) for v in vals]
    den = es[0] + es[1] + es[2] + es[3]
    gates = [e / den for e in es]

    onehots = [(lane == i) for i in ids]
    multi = jnp.zeros((n, LANES), F32)
    for oh in onehots:
        multi = multi + oh.astype(F32)
    counts = jnp.sum(multi, axis=0, keepdims=True)
    before = _dot(ltri_ref[...], multi.astype(BF16))
    seg = jnp.ceil(counts * (1.0 / SEG_ALIGN)) * SEG_ALIGN
    seg_rows = jnp.broadcast_to(seg, (BF16_ROWS, LANES)).astype(BF16)
    seg_off = _dot(seg_rows, utri_ref[...])[0:1, :]
    slot_of = seg_off + before
    meta = jnp.zeros((n, LANES), F32)
    for k in range(TOP_K):
        slot = jnp.sum(jnp.where(onehots[k], slot_of, 0.0), axis=-1, keepdims=True)
        meta = jnp.where(lane == k, ids[k], meta)
        meta = jnp.where(lane == TOP_K + k, gates[k], meta)
        meta = jnp.where(lane == 2 * TOP_K + k, slot, meta)
    meta_ref[...] = meta
    cnt_ref[...] = jnp.broadcast_to(counts, (SUBLANES, LANES))


def _router(x1, nw, router_w, router_b):
    t = x1.shape[0]
    n_tiles = t // TOKEN_TILE
    rw = jnp.zeros((D_MODEL, LANES), F32).at[:, :N_EXPERTS].set(router_w)
    rw_hi, rw_lo = _split2(rw)
    rb = jnp.full((1, LANES), -1e30, F32).at[0, :N_EXPERTS].set(router_b)
    idx = np.arange(TOKEN_TILE)
    ltri = jnp.asarray(idx[None, :] < idx[:, None], BF16)
    lid = np.arange(LANES)
    utri = jnp.asarray(lid[:, None] < lid[None, :], BF16)
    const = lambda shape: pl.BlockSpec(shape, lambda i: (0,) * len(shape))
    return pl.pallas_call(
        _router_kernel,
        grid=(n_tiles,),
        in_specs=[
            pl.BlockSpec((TOKEN_TILE, D_MODEL), lambda i: (i, 0)),
            const((1, D_MODEL)), const((D_MODEL, LANES)), const((D_MODEL, LANES)), const((1, LANES)),
            const((TOKEN_TILE, TOKEN_TILE)), const((LANES, LANES)),
        ],
        out_specs=[
            pl.BlockSpec((TOKEN_TILE, D_MODEL), lambda i: (i, 0)),
            pl.BlockSpec((TOKEN_TILE, LANES), lambda i: (i, 0)),
            pl.BlockSpec((None, SUBLANES, LANES), lambda i: (i, 0, 0)),
        ],
        out_shape=[
            jax.ShapeDtypeStruct((t, D_MODEL), BF16),
            jax.ShapeDtypeStruct((t, LANES), F32),
            jax.ShapeDtypeStruct((n_tiles, SUBLANES, LANES), F32),
        ],
        compiler_params=pltpu.CompilerParams(
            dimension_semantics=("arbitrary",), vmem_limit_bytes=VMEM_LIMIT_BYTES),
        name="router",
    )(x1, nw, rw_hi, rw_lo, rb, ltri, utri)


def _segment_tables(cnt):
    seg = (cnt + SEG_ALIGN - 1) // SEG_ALIGN * SEG_ALIGN
    src = jnp.cumsum(seg, axis=1) - seg
    tot = jnp.sum(seg, axis=0)
    cap = (tot + ROW_TILE - 1) // ROW_TILE * ROW_TILE
    base = jnp.cumsum(cap) - cap
    dst = base[None, :] + jnp.cumsum(seg, axis=0) - seg
    n_used = jnp.sum(cap) // ROW_TILE
    fill = base + tot
    lane = jnp.arange(N_EXPERTS)
    counts, srcs, dsts = [], [], []
    for size in SEG_SIZES:
        has = (seg & size) != 0
        done = seg & ~(2 * size - 1)
        place = has[:, :, None] & ((jnp.cumsum(has, axis=1) - 1)[:, :, None] == lane[None, None, :])
        counts.append(jnp.sum(has, axis=1))
        srcs.append(jnp.sum(jnp.where(place, (src + done)[:, :, None], 0), axis=1))
        dsts.append(jnp.sum(jnp.where(place, (dst + done)[:, :, None], 0), axis=1))
    i32 = lambda a: a.astype(jnp.int32).reshape(-1)
    chunks = (i32(jnp.stack(counts, axis=1)), i32(jnp.stack(srcs, axis=1)), i32(jnp.stack(dsts, axis=1)))
    return chunks, i32(fill), i32(cap - tot), i32(base), i32(cap // ROW_TILE), i32(n_used)


def _for_each_chunk(n, fn):
    for size in SEG_SIZES:
        done = n & ~(2 * size - 1)

        @pl.when((n & size) != 0)
        def _():
            fn(done, size)


def _segment_copies(cnt_ref, src_ref, dst_ref, tile, make_copy, act):
    for si, size in enumerate(SEG_SIZES):
        first = (tile * len(SEG_SIZES) + si) * N_EXPERTS

        def one(p, carry, size=size, first=first):
            act(make_copy(pl.multiple_of(src_ref[first + p], SEG_ALIGN),
                          pl.multiple_of(dst_ref[first + p], SEG_ALIGN), size))
            return carry
        lax.fori_loop(0, cnt_ref[tile * len(SEG_SIZES) + si], one, 0)


DISPATCH_ROWS = 32


def _dispatch_kernel(seg_ref, src_ref, dst_ref, fill_ref, gap_ref, nu_ref, slot_ref, xn_ref, xs_ref, sorted_ref,
                     onehot_ref, zero_ref, sem):
    tile = pl.program_id(0)

    @pl.when(tile == 0)
    def _():
        zero_ref[...] = jnp.zeros_like(zero_ref)

        def fill_copy(row, size):
            return pltpu.make_async_copy(zero_ref.at[pl.ds(0, size)],
                                         xs_ref.at[pl.ds(pl.multiple_of(row, SEG_ALIGN), size)], sem)

        def gaps(act):
            def per_expert(e, carry):
                _for_each_chunk(gap_ref[e], lambda done, size: act(fill_copy(fill_ref[e] + done, size)))
                return carry
            lax.fori_loop(0, N_EXPERTS, per_expert, 0)

        def tail(act):
            n_tail = xs_ref.shape[0] // ROW_TILE - nu_ref[0]
            lax.fori_loop(0, n_tail, lambda i, c: (act(fill_copy((nu_ref[0] + i) * ROW_TILE, ROW_TILE)), c)[1], 0)

        gaps(lambda c: c.start())
        tail(lambda c: c.start())
        gaps(lambda c: c.wait())
        tail(lambda c: c.wait())

    slots = slot_ref[...]
    slot_rows = [jnp.broadcast_to(slots[k:k + 1, :], (DISPATCH_ROWS, TOKEN_TILE)) for k in range(TOP_K)]

    def build(rb, carry):
        r0 = pl.multiple_of(rb * DISPATCH_ROWS, DISPATCH_ROWS)
        row = (lax.broadcasted_iota(jnp.int32, (DISPATCH_ROWS, TOKEN_TILE), 0) + r0).astype(F32)
        hot = jnp.zeros((DISPATCH_ROWS, TOKEN_TILE), F32)
        for k in range(TOP_K):
            hot = jnp.where(row == slot_rows[k], 1.0, hot)
        onehot_ref[pl.ds(r0, DISPATCH_ROWS), :] = hot.astype(BF16)
        return carry

    lax.fori_loop(0, TILE_CAP // DISPATCH_ROWS, build, 0)
    sorted_ref[...] = _dot(onehot_ref[...], xn_ref[...]).astype(BF16)

    def make_copy(src, dst, size):
        return pltpu.make_async_copy(sorted_ref.at[pl.ds(src, size)], xs_ref.at[pl.ds(dst, size)], sem)

    _segment_copies(seg_ref, src_ref, dst_ref, tile, make_copy, lambda c: c.start())
    _segment_copies(seg_ref, src_ref, dst_ref, tile, make_copy, lambda c: c.wait())


def _dispatch(tables, slot_t, xn, n_rows_total):
    t = xn.shape[0]
    n_tiles = t // TOKEN_TILE
    return pl.pallas_call(
        _dispatch_kernel,
        grid_spec=pltpu.PrefetchScalarGridSpec(
            num_scalar_prefetch=len(tables),
            grid=(n_tiles,),
            in_specs=[
                pl.BlockSpec((SUBLANES, TOKEN_TILE), lambda i, *_: (0, i)),
                pl.BlockSpec((TOKEN_TILE, D_MODEL), lambda i, *_: (i, 0)),
            ],
            out_specs=pl.BlockSpec(memory_space=pl.ANY),
            scratch_shapes=[
                pltpu.VMEM((TILE_CAP, D_MODEL), BF16),
                pltpu.VMEM((TILE_CAP, TOKEN_TILE), BF16),
                pltpu.VMEM((ROW_TILE, D_MODEL), BF16),
                pltpu.SemaphoreType.DMA(()),
            ],
        ),
        out_shape=jax.ShapeDtypeStruct((n_rows_total, D_MODEL), BF16),
        compiler_params=pltpu.CompilerParams(
            dimension_semantics=("arbitrary",), vmem_limit_bytes=VMEM_LIMIT_BYTES),
        name="dispatch",
    )(*tables, slot_t, xn)


def _experts_kernel(base_ref, nt_ref, nu_ref, wg_ref, bg_ref, wu_ref, bu_ref, wd_ref, bd_ref, xs_ref, zs_ref,
                    wg_bf, wu_bf, wd_bf, xbuf, hbuf, zbuf, in_sem, out_sem):
    e = pl.program_id(0)
    n = nt_ref[e]
    base = base_ref[e]

    def x_copy(i, slot):
        rows = pl.ds(pl.multiple_of(base + i * ROW_TILE, ROW_TILE), ROW_TILE)
        return pltpu.make_async_copy(xs_ref.at[rows], xbuf.at[slot], in_sem.at[slot])

    def z_copy(row, slot):
        rows = pl.ds(pl.multiple_of(row, ROW_TILE), ROW_TILE)
        return pltpu.make_async_copy(zbuf.at[slot], zs_ref.at[rows], out_sem.at[slot])

    def hidden(slot):
        x = xbuf[slot]
        a = jnp.minimum(_dot(x, wg_bf[...]) + bg_ref[...], SWIGLU_LIMIT)
        u = jnp.clip(_dot(x, wu_bf[...]) + bu_ref[...], -SWIGLU_LIMIT, SWIGLU_LIMIT)
        hbuf[slot] = ((u + 1.0) * a * jax.nn.sigmoid(SWIGLU_ALPHA * a)).astype(BF16)

    def project_down(i, slot):
        @pl.when(i >= 2)
        def _():
            z_copy(base, slot).wait()
        zbuf[slot] = (_dot(hbuf[slot], wd_bf[...]) + bd_ref[...]).astype(BF16)

    @pl.when(n > 0)
    def _():
        x_copy(0, 0).start()

        @pl.when(n > 1)
        def _():
            x_copy(1, 1).start()

        wg_bf[...] = wg_ref[...].astype(BF16)
        wu_bf[...] = wu_ref[...].astype(BF16)
        wd_bf[...] = wd_ref[...].astype(BF16)
        x_copy(0, 0).wait()
        hidden(0)

        def step(i, carry):
            slot = i % 2
            x_copy(i + 1, 1 - slot).wait()

            @pl.when(i + 2 < n)
            def _():
                x_copy(i + 2, slot).start()

            project_down(i, slot)
            hidden(1 - slot)
            z_copy(base + i * ROW_TILE, slot).start()
            return carry

        lax.fori_loop(0, n - 1, step, 0)
        project_down(n - 1, (n - 1) % 2)
        z_copy(base + (n - 1) * ROW_TILE, (n - 1) % 2).start()

        @pl.when(n >= 2)
        def _():
            z_copy(base, n % 2).wait()
        z_copy(base, (n - 1) % 2).wait()

    @pl.when(e == pl.num_programs(0) - 1)
    def _():
        zbuf[0] = jnp.zeros((ROW_TILE, D_MODEL), BF16)
        n_tail = zs_ref.shape[0] // ROW_TILE - nu_ref[0]
        tail = lambda act: lax.fori_loop(
            0, n_tail, lambda i, c: (act(z_copy((nu_ref[0] + i) * ROW_TILE, 0)), c)[1], 0)
        tail(lambda c: c.start())
        tail(lambda c: c.wait())


def _experts(base, n_tiles, n_used, xs, w_gate, b_gate, w_up, b_up, w_down, b_down):
    w_spec = pl.BlockSpec((None, D_MODEL, D_MODEL), lambda e, *_: (e, 0, 0))
    b_spec = pl.BlockSpec((None, 1, D_MODEL), lambda e, *_: (e, 0, 0))
    any_spec = pl.BlockSpec(memory_space=pl.ANY)
    return pl.pallas_call(
        _experts_kernel,
        grid_spec=pltpu.PrefetchScalarGridSpec(
            num_scalar_prefetch=3,
            grid=(N_EXPERTS,),
            in_specs=[w_spec, b_spec, w_spec, b_spec, w_spec, b_spec, any_spec],
            out_specs=any_spec,
            scratch_shapes=[pltpu.VMEM((D_MODEL, D_MODEL), BF16) for _ in range(3)] + [
                pltpu.VMEM((2, ROW_TILE, D_MODEL), BF16) for _ in range(3)] + [
                pltpu.SemaphoreType.DMA((2,)), pltpu.SemaphoreType.DMA((2,))],
        ),
        out_shape=jax.ShapeDtypeStruct(xs.shape, BF16),
        compiler_params=pltpu.CompilerParams(
            dimension_semantics=("arbitrary",), vmem_limit_bytes=VMEM_LIMIT_BYTES),
        name="experts",
    )(base, n_tiles, n_used, w_gate, b_gate[:, None, :], w_up, b_up[:, None, :], w_down, b_down[:, None, :], xs)


COMBINE_ROWS = 32
COMBINE_COLS = 512


def _combine_kernel(n_first, seg_ref, src_ref, dst_ref, meta_ref, x1_ref, fw_ref, zs_ref, ya_ref, yb_ref,
                    sorted_ref, w_ref, sem):
    tile = pl.program_id(0)
    sorted_ref[...] = jnp.zeros_like(sorted_ref)

    def make_copy(src, dst, size):
        return pltpu.make_async_copy(zs_ref.at[pl.ds(dst, size)], sorted_ref.at[pl.ds(src, size)], sem)

    _segment_copies(seg_ref, src_ref, dst_ref, tile, make_copy, lambda c: c.start())

    def build(rb, carry):
        r0 = pl.multiple_of(rb * COMBINE_ROWS, COMBINE_ROWS)
        meta = meta_ref[pl.ds(r0, COMBINE_ROWS), :]
        gates = [jnp.broadcast_to(meta[:, TOP_K + k:TOP_K + k + 1], (COMBINE_ROWS, COMBINE_COLS))
                 for k in range(TOP_K)]
        slots = [jnp.broadcast_to(meta[:, 2 * TOP_K + k:2 * TOP_K + k + 1], (COMBINE_ROWS, COMBINE_COLS))
                 for k in range(TOP_K)]
        for cb in range(TILE_CAP // COMBINE_COLS):
            col = (lax.broadcasted_iota(jnp.int32, (COMBINE_ROWS, COMBINE_COLS), 1) + cb * COMBINE_COLS).astype(F32)
            w = jnp.zeros((COMBINE_ROWS, COMBINE_COLS), F32)
            for k in range(TOP_K):
                w = jnp.where(col == slots[k], gates[k], w)
            w_ref[pl.ds(r0, COMBINE_ROWS), cb * COMBINE_COLS:(cb + 1) * COMBINE_COLS] = w.astype(BF16)
        return carry

    lax.fori_loop(0, TOKEN_TILE // COMBINE_ROWS, build, 0)
    _segment_copies(seg_ref, src_ref, dst_ref, tile, make_copy, lambda c: c.wait())
    y = _rmsnorm(x1_ref[...] + _dot(w_ref[...], sorted_ref[...]), fw_ref[...])

    @pl.when(tile < n_first)
    def _():
        ya_ref[...] = y

    @pl.when(tile >= n_first)
    def _():
        yb_ref[...] = y


def _combine(tables, meta, x1, final_w, zs, rows_first):
    seg, src, dst = tables
    t = x1.shape[0]
    assert rows_first % TOKEN_TILE == 0 and 0 < rows_first < t and TILE_CAP % COMBINE_COLS == 0
    n_tiles = t // TOKEN_TILE
    n_first = rows_first // TOKEN_TILE
    return pl.pallas_call(
        functools.partial(_combine_kernel, n_first),
        grid_spec=pltpu.PrefetchScalarGridSpec(
            num_scalar_prefetch=3,
            grid=(n_tiles,),
            in_specs=[
                pl.BlockSpec((TOKEN_TILE, LANES), lambda i, *_: (i, 0)),
                pl.BlockSpec((TOKEN_TILE, D_MODEL), lambda i, *_: (i, 0)),
                pl.BlockSpec((1, D_MODEL), lambda i, *_: (0, 0)),
                pl.BlockSpec(memory_space=pl.ANY),
            ],
            out_specs=[
                pl.BlockSpec((TOKEN_TILE, D_MODEL), lambda i, *_: (jnp.minimum(i, n_first - 1), 0)),
                pl.BlockSpec((TOKEN_TILE, D_MODEL), lambda i, *_: (jnp.maximum(i - n_first, 0), 0)),
            ],
            scratch_shapes=[pltpu.VMEM((TILE_CAP, D_MODEL), BF16), pltpu.VMEM((TOKEN_TILE, TILE_CAP), BF16),
                            pltpu.SemaphoreType.DMA(())],
        ),
        out_shape=[jax.ShapeDtypeStruct((rows_first, D_MODEL), F32),
                   jax.ShapeDtypeStruct((t - rows_first, D_MODEL), F32)],
        compiler_params=pltpu.CompilerParams(
            dimension_semantics=("arbitrary",), vmem_limit_bytes=VMEM_LIMIT_BYTES),
        name="combine",
    )(seg, src, dst, meta, x1, final_w, zs)


def _moe_and_final_norm(x1, rows_first, norm_ffn_w, router_w, router_b, w_gate, b_gate, w_up, b_up, w_down, b_down,
                        final_w):
    t = x1.shape[0]
    assert t % TOKEN_TILE == 0
    n_tiles = t // TOKEN_TILE
    n_rows_total = TOP_K * t + n_tiles * N_EXPERTS * (SEG_ALIGN - 1) + N_EXPERTS * (ROW_TILE - 1)
    n_rows_total = (n_rows_total + ROW_TILE - 1) // ROW_TILE * ROW_TILE
    xn, meta, cnt = _router(x1, norm_ffn_w.reshape(1, D_MODEL), router_w, router_b)
    cnt = cnt[:, 0, :N_EXPERTS].astype(jnp.int32)
    chunks, fill, gap, base, n_row_tiles, n_used = _segment_tables(cnt)
    slot_t = jnp.zeros((SUBLANES, t), F32).at[:TOP_K, :].set(meta[:, 2 * TOP_K:3 * TOP_K].T)
    xs = _dispatch((*chunks, fill, gap, n_used), slot_t, xn, n_rows_total)
    zs = _experts(base, n_row_tiles, n_used, xs, w_gate, b_gate, w_up, b_up, w_down, b_down)
    return _combine(chunks, meta, x1, final_w.reshape(1, D_MODEL), zs, rows_first)


def kernel(x_prompt, x_sample, state_conv, state_hgrn, lb_logits, norm_mix_w, w_in, conv_w, conv_b, gnorm_w,
           w_out, norm_ffn_w, router_w, router_b, w_gate, b_gate, w_up, b_up, w_down, b_down, final_norm_w):
    assert norm_mix_w.shape[0] == 1 and lb_logits.shape[0] == 2, "single-layer step"
    b, l, _ = x_prompt.shape
    nb, ls, _ = x_sample.shape
    lbl = lb_logits.astype(F32)
    nw = norm_mix_w[0].reshape(1, D_MODEL)
    w_in_bf = w_in[0].astype(BF16)
    w_out_bf = w_out[0].astype(BF16)
    cw, cb = conv_w[0], conv_b[0].reshape(1, D_CONV)
    gnw = gnorm_w[0].reshape(1, D_HGRN)
    rows_p, rows_s = b * l, nb * ls
    x1, conv_p, hgrn_p = _mix_prompt(x_prompt, lbl, nw, w_in_bf, cw, cb, gnw, w_out_bf, rows_p + rows_s)
    x1, conv_s, hgrn_s = _mix_sample(x_sample, state_conv[0], state_hgrn[0], lbl, nw, w_in_bf, cw, cb, gnw,
                                     w_out_bf, x1, rows_p)
    y_p, y_s = _moe_and_final_norm(x1, rows_p, norm_ffn_w[0], router_w[0], router_b[0], w_gate[0], b_gate[0],
                                   w_up[0], b_up[0], w_down[0], b_down[0], final_norm_w)
    conv_p = conv_p[:, SUBLANES - 2:, :]
    conv_s = conv_s[:, ls - 2:, :]
    return (y_p.reshape(b, l, D_MODEL), y_s.reshape(nb, ls, D_MODEL), conv_p[None], hgrn_p[None], conv_s[None],
            hgrn_s[None])
```

```python
import functools

import numpy as np
import jax
import jax.numpy as jnp
from jax import lax
from jax.experimental import pallas as pl
from jax.experimental.pallas import tpu as pltpu

F32 = jnp.float32
BF16 = jnp.bfloat16

D_MODEL = 1024
D_CONV = 512
D_HGRN = 512
N_HEADS = 4
D_HEAD = 128
D_PROJ = 3 * D_CONV + 4 * D_HGRN
N_EXPERTS = 32
TOP_K = 4
SWIGLU_LIMIT = 7.0
SWIGLU_ALPHA = 1.702
EPS = 1e-6
PROMPT_CHUNK = 64

LANES = 128
SUBLANES = 8
BF16_ROWS = 16
VMEM_LIMIT_BYTES = 56 * 1024 * 1024

MIX_SLAB = 256
MIX_ROWS = 512
MIX_SEQS = 16
TOKEN_TILE = 512
SEG_ALIGN = BF16_ROWS
ROW_TILE = 512
SEG_SIZES = tuple(SEG_ALIGN << i for i in reversed(range(6)))
TILE_CAP = TOP_K * TOKEN_TILE + N_EXPERTS * SEG_ALIGN

NT_DIMS = (((1,), (1,)), ((), ()))
TN_DIMS = (((0,), (0,)), ((), ()))


def _dot(a, b):
    return jnp.dot(a, b, preferred_element_type=F32)


def _dot_nt(a, b):
    return lax.dot_general(a, b, NT_DIMS, preferred_element_type=F32)


def _dot_tn(a, b):
    return lax.dot_general(a, b, TN_DIMS, preferred_element_type=F32)


def _split2(x):
    hi = x.astype(BF16)
    lo = (x - hi.astype(F32)).astype(BF16)
    return hi, lo


def _rmsnorm(x, w):
    return x * lax.rsqrt(jnp.mean(x * x, axis=-1, keepdims=True) + EPS) * w


def _level_exponent_matrix(n, chunk, h):
    x = np.zeros((n, n), np.float32)
    for t in range(n):
        base = t - t % (2 * h)
        m = base + h - 1
        if t % (2 * h) >= h:
            x[t, m + 1:t + 1] = 1.0
        else:
            x[t, t + 1:m + 1] = 1.0
    return x


def _mix_constants(n, chunk, mxu_levels, all_levels):
    t = np.arange(n)
    same_chunk = (t[:, None] // chunk) == (t[None, :] // chunk)
    tri = (same_chunk & (t[None, :] <= t[:, None])).astype(np.float32)
    suf = (same_chunk & (t[None, :] > t[:, None])).astype(np.float32)
    cmat = np.concatenate([tri, suf] + [_level_exponent_matrix(n, chunk, h) for h in mxu_levels], axis=0)
    masks = []
    for h in all_levels:
        blk = (t[:, None] // (2 * h)) == (t[None, :] // (2 * h))
        masks.append((blk & ((t[:, None] % (2 * h)) >= h) & ((t[None, :] % (2 * h)) < h)).astype(np.float32))
    masks.append(np.eye(n, dtype=np.float32))
    return jnp.asarray(cmat, BF16), jnp.asarray(np.stack(masks), F32)


def _vpu_level_exponent(a, h):
    n = a.shape[0]
    pieces = []
    for j in range(n // (2 * h)):
        b = j * 2 * h
        ref = a[b + h - 1:b + h, :]
        pieces.append(ref - a[b:b + h, :])
        pieces.append(a[b + h:b + 2 * h, :] - ref)
    return jnp.concatenate(pieces, axis=0)


def _forget_lower_bound(lbl):
    m = jnp.max(lbl, axis=0, keepdims=True)
    e = jnp.exp(lbl - m)
    return e[0:1, :] / jnp.sum(e, axis=0, keepdims=True)


def _projections(x, nw, w_in_ref):
    h = _rmsnorm(x, nw).astype(BF16)
    return _dot(h, w_in_ref[...])


def _gates(proj, lb):
    q = proj[:, 3 * D_CONV:3 * D_CONV + D_HGRN]
    ff = proj[:, 3 * D_CONV + D_HGRN:3 * D_CONV + 2 * D_HGRN]
    vi = proj[:, 3 * D_CONV + 2 * D_HGRN:3 * D_CONV + 3 * D_HGRN]
    g = proj[:, 3 * D_CONV + 3 * D_HGRN:]
    e = jnp.exp(-jnp.abs(ff))
    r = 1.0 / (1.0 + e)
    pos = ff >= 0
    sig = jnp.where(pos, r, e * r)
    sig_neg = jnp.where(pos, e * r, r)
    logf = jnp.log(lb + (1.0 - lb) * sig)
    kk = (1.0 - lb) * sig_neg
    return q, logf, kk, vi, g


def _intra_scores_times_v(q, kk, vi, exps, masks_ref):
    n = q.shape[0]
    sc = [jnp.zeros((n, n), F32) for _ in range(N_HEADS)]
    for l, ex in enumerate(exps):
        if ex is None:
            qh, kh = q.astype(BF16), kk.astype(BF16)
        else:
            w = jnp.exp(ex)
            qh, kh = (q * w).astype(BF16), (kk * w).astype(BF16)
        mask = masks_ref[l]
        for hd in range(N_HEADS):
            hs = slice(hd * D_HEAD, (hd + 1) * D_HEAD)
            sc[hd] = sc[hd] + mask * _dot_nt(qh[:, hs], kh[:, hs])
    vb = vi.astype(BF16)
    return [_dot(sc[hd].astype(BF16), vb[:, hd * D_HEAD:(hd + 1) * D_HEAD]) for hd in range(N_HEADS)]


def _head_out(o, g, gnw):
    parts = []
    for hd in range(N_HEADS):
        oh = o[:, hd * D_HEAD:(hd + 1) * D_HEAD]
        parts.append(oh * lax.rsqrt(jnp.mean(oh * oh, axis=-1, keepdims=True) + EPS))
    on = jnp.concatenate(parts, axis=1)
    return on * gnw * (g * jax.nn.sigmoid(g))


def _pad_rows_bf16(a):
    return jnp.concatenate([a, jnp.zeros_like(a)], axis=0).astype(BF16)


PROMPT_VPU_LEVELS = (32, 16, 8)
MXU_LEVELS = (4, 2, 1)


def _mix_prompt_kernel(n_r, n_steps, *refs):
    step = pl.program_id(0)

    @pl.when(step < n_steps)
    def _():
        _mix_prompt_step(step % n_r, n_r, *refs)

    @pl.when(step >= n_steps)
    def _():
        x1_ref = refs[10]
        x1_ref[...] = jnp.zeros_like(x1_ref)


def _mix_prompt_step(r, n_r, x_ref, lbl_ref, nw_ref, win_ref, cw_ref, cb_ref, gnw_ref, wout_ref, cmat_ref,
                     masks_ref, x1_ref, nconv_ref, nstate_ref, st_ref, tail_ref, o_ref):
    rows = x_ref.shape[0]

    @pl.when(r == 0)
    def _():
        st_ref[...] = jnp.zeros_like(st_ref)
        tail_ref[...] = jnp.zeros_like(tail_ref)

    x = x_ref[...]
    proj = _projections(x, nw_ref[...], win_ref)
    lb = _forget_lower_bound(lbl_ref[...])

    bg = proj[:, 0:D_CONV]
    u = proj[:, D_CONV:2 * D_CONV] * proj[:, 2 * D_CONV:3 * D_CONV]
    rid = lax.broadcasted_iota(jnp.int32, (rows, 1), 0)
    t0 = tail_ref[SUBLANES - 2:SUBLANES - 1, :]
    t1 = tail_ref[SUBLANES - 1:SUBLANES, :]
    u1 = jnp.where(rid == 0, t1, pltpu.roll(u, 1, 0))
    u2 = jnp.where(rid == 0, t0, jnp.where(rid == 1, t1, pltpu.roll(u, 2, 0)))
    cw = cw_ref[...]
    y_conv = bg * (cw[0:1, :] * u2 + cw[1:2, :] * u1 + cw[2:3, :] * u + cb_ref[...])
    tail_ref[...] = u[rows - SUBLANES:rows, :]

    q, logf, kk, vi, g = _gates(proj, lb)
    lf_hi, lf_lo = _split2(logf)
    n_chunks = MIX_SLAB // PROMPT_CHUNK
    for s in range(rows // MIX_SLAB):
        sl = slice(s * MIX_SLAB, (s + 1) * MIX_SLAB)
        qs, ks, vs = q[sl], kk[sl], vi[sl]
        ex = _dot(cmat_ref[...], lf_hi[sl]) + _dot(cmat_ref[...], lf_lo[sl])
        a_pre = ex[0:MIX_SLAB]
        a_suf = ex[MIX_SLAB:2 * MIX_SLAB]
        exps = [_vpu_level_exponent(a_pre, h) for h in PROMPT_VPU_LEVELS]
        exps += [ex[(2 + i) * MIX_SLAB:(3 + i) * MIX_SLAB] for i in range(len(MXU_LEVELS))]
        exps.append(None)
        intra = _intra_scores_times_v(qs, ks, vs, exps, masks_ref)
        ea = jnp.exp(a_pre)
        qa = (qs * ea).astype(BF16)
        kb = (ks * jnp.exp(a_suf)).astype(BF16)
        vb = vs.astype(BF16)
        for c in range(n_chunks):
            cr = slice(c * PROMPT_CHUNK, (c + 1) * PROMPT_CHUNK)
            last = (c + 1) * PROMPT_CHUNK - 1
            for hd in range(N_HEADS):
                hs = slice(hd * D_HEAD, (hd + 1) * D_HEAD)
                st = st_ref[hd]
                inter = _dot_nt(qa[cr, hs], st.astype(BF16))
                o_ref[s * MIX_SLAB + c * PROMPT_CHUNK:s * MIX_SLAB + (c + 1) * PROMPT_CHUNK, hs] = (
                    inter + intra[hd][cr, :])
                st_ref[hd] = st * ea[last:last + 1, hs] + _dot_tn(vb[cr, hs], kb[cr, hs])

    o = _head_out(o_ref[...], g, gnw_ref[...])
    mix_in = jnp.concatenate([y_conv, o], axis=1).astype(BF16)
    x1_ref[...] = x + _dot(mix_in, wout_ref[...])

    @pl.when(r == n_r - 1)
    def _():
        nconv_ref[...] = u[rows - SUBLANES:rows, :]
        for hd in range(N_HEADS):
            nstate_ref[hd] = st_ref[hd].T


def _mix_prompt(x, lbl, nw, w_in, cw, cb, gnw, w_out, total_rows):
    b, l, _ = x.shape
    assert l % MIX_ROWS == 0 and MIX_ROWS % MIX_SLAB == 0 and l % PROMPT_CHUNK == 0
    assert total_rows % MIX_ROWS == 0
    n_r = l // MIX_ROWS
    n_steps = b * n_r
    seq = lambda s: jnp.minimum(s, n_steps - 1) // n_r
    cmat, masks = _mix_constants(MIX_SLAB, PROMPT_CHUNK, MXU_LEVELS, PROMPT_VPU_LEVELS + MXU_LEVELS)
    const = lambda shape: pl.BlockSpec(shape, lambda s: (0,) * len(shape))
    return pl.pallas_call(
        functools.partial(_mix_prompt_kernel, n_r, n_steps),
        grid=(total_rows // MIX_ROWS,),
        in_specs=[
            pl.BlockSpec((None, MIX_ROWS, D_MODEL), lambda s: (seq(s), jnp.minimum(s, n_steps - 1) % n_r, 0)),
            const((2, D_HGRN)), const((1, D_MODEL)), const((D_MODEL, D_PROJ)), const((3, D_CONV)),
            const((1, D_CONV)), const((1, D_HGRN)), const((D_MODEL, D_MODEL)), const(cmat.shape),
            const(masks.shape),
        ],
        out_specs=[
            pl.BlockSpec((MIX_ROWS, D_MODEL), lambda s: (s, 0)),
            pl.BlockSpec((None, SUBLANES, D_CONV), lambda s: (seq(s), 0, 0)),
            pl.BlockSpec((None, N_HEADS, D_HEAD, D_HEAD), lambda s: (seq(s), 0, 0, 0)),
        ],
        out_shape=[
            jax.ShapeDtypeStruct((total_rows, D_MODEL), F32),
            jax.ShapeDtypeStruct((b, SUBLANES, D_CONV), F32),
            jax.ShapeDtypeStruct((b, N_HEADS, D_HEAD, D_HEAD), F32),
        ],
        scratch_shapes=[
            pltpu.VMEM((N_HEADS, D_HEAD, D_HEAD), F32),
            pltpu.VMEM((SUBLANES, D_CONV), F32),
            pltpu.VMEM((MIX_ROWS, D_HGRN), F32),
        ],
        compiler_params=pltpu.CompilerParams(
            dimension_semantics=("arbitrary",), vmem_limit_bytes=VMEM_LIMIT_BYTES),
        name="mix_prompt",
    )(x, lbl, nw, w_in, cw, cb, gnw, w_out, cmat, masks)


def _mix_sample_kernel(x_ref, cs_ref, hs_ref, lbl_ref, nw_ref, win_ref, cw_ref, cb_ref, gnw_ref, wout_ref,
                       cmat_ref, masks_ref, x1_in_ref, x1_ref, nconv_ref, nstate_ref, o_ref):
    del x1_in_ref
    nseq, length, _ = nconv_ref.shape
    rows = nseq * length
    x = x_ref[...]
    proj = _projections(x, nw_ref[...], win_ref)
    lb = _forget_lower_bound(lbl_ref[...])

    bg = proj[:, 0:D_CONV]
    u = proj[:, D_CONV:2 * D_CONV] * proj[:, 2 * D_CONV:3 * D_CONV]
    cs = cs_ref[...]
    expand = lambda a: jnp.broadcast_to(a, (nseq, length, D_CONV)).reshape(rows, D_CONV)
    t0 = expand(cs[:, 0:1, :])
    t1 = expand(cs[:, 1:2, :])
    pos = lax.broadcasted_iota(jnp.int32, (rows, 1), 0) % length
    u1 = jnp.where(pos == 0, t1, pltpu.roll(u, 1, 0))
    u2 = jnp.where(pos == 0, t0, jnp.where(pos == 1, t1, pltpu.roll(u, 2, 0)))
    cw = cw_ref[...]
    y_conv = bg * (cw[0:1, :] * u2 + cw[1:2, :] * u1 + cw[2:3, :] * u + cb_ref[...])
    nconv_ref[...] = u.reshape(nseq, length, D_CONV)

    q, logf, kk, vi, g = _gates(proj, lb)
    lf_hi, lf_lo = _split2(logf)
    ex = _dot(cmat_ref[...], lf_hi) + _dot(cmat_ref[...], lf_lo)
    a_pre = ex[0:rows]
    a_suf = ex[rows:2 * rows]
    exps = [ex[(2 + i) * rows:(3 + i) * rows] for i in range(len(MXU_LEVELS))] + [None]
    intra = _intra_scores_times_v(q, kk, vi, exps, masks_ref)
    ea = jnp.exp(a_pre)
    qa = q * ea
    kb = kk * jnp.exp(a_suf)
    for s in range(nseq):
        cr = slice(s * length, (s + 1) * length)
        last = (s + 1) * length - 1
        for hd in range(N_HEADS):
            hs = slice(hd * D_HEAD, (hd + 1) * D_HEAD)
            st = hs_ref[s, hd].T
            inter = _dot_nt(_pad_rows_bf16(qa[cr, hs]), st.astype(BF16))[0:length, :]
            o_ref[cr, hs] = inter + intra[hd][cr, :]
            st_new = st * ea[last:last + 1, hs] + _dot_tn(_pad_rows_bf16(vi[cr, hs]), _pad_rows_bf16(kb[cr, hs]))
            nstate_ref[s, hd] = st_new.T

    o = _head_out(o_ref[...], g, gnw_ref[...])
    mix_in = jnp.concatenate([y_conv, o], axis=1).astype(BF16)
    x1_ref[...] = x + _dot(mix_in, wout_ref[...])


def _mix_sample(x, conv_state, hgrn_state, lbl, nw, w_in, cw, cb, gnw, w_out, x1_flat, row_offset):
    nb, length, _ = x.shape
    rows = MIX_SEQS * length
    assert nb % MIX_SEQS == 0 and length == SUBLANES and row_offset % rows == 0
    first_block = row_offset // rows
    cmat, masks = _mix_constants(rows, length, MXU_LEVELS, MXU_LEVELS)
    const = lambda shape: pl.BlockSpec(shape, lambda i: (0,) * len(shape))
    return pl.pallas_call(
        _mix_sample_kernel,
        grid=(nb // MIX_SEQS,),
        in_specs=[
            pl.BlockSpec((rows, D_MODEL), lambda i: (i, 0)),
            pl.BlockSpec((MIX_SEQS, 2, D_CONV), lambda i: (i, 0, 0)),
            pl.BlockSpec((MIX_SEQS, N_HEADS, D_HEAD, D_HEAD), lambda i: (i, 0, 0, 0)),
            const((2, D_HGRN)), const((1, D_MODEL)), const((D_MODEL, D_PROJ)), const((3, D_CONV)),
            const((1, D_CONV)), const((1, D_HGRN)), const((D_MODEL, D_MODEL)), const(cmat.shape),
            const(masks.shape), pl.BlockSpec(memory_space=pl.ANY),
        ],
        out_specs=[
            pl.BlockSpec((rows, D_MODEL), lambda i: (first_block + i, 0)),
            pl.BlockSpec((MIX_SEQS, length, D_CONV), lambda i: (i, 0, 0)),
            pl.BlockSpec((MIX_SEQS, N_HEADS, D_HEAD, D_HEAD), lambda i: (i, 0, 0, 0)),
        ],
        out_shape=[
            jax.ShapeDtypeStruct(x1_flat.shape, F32),
            jax.ShapeDtypeStruct((nb, length, D_CONV), F32),
            jax.ShapeDtypeStruct((nb, N_HEADS, D_HEAD, D_HEAD), F32),
        ],
        scratch_shapes=[pltpu.VMEM((rows, D_HGRN), F32)],
        input_output_aliases={12: 0},
        compiler_params=pltpu.CompilerParams(
            dimension_semantics=("arbitrary",), vmem_limit_bytes=VMEM_LIMIT_BYTES),
        name="mix_sample",
    )(x.reshape(nb * length, D_MODEL), conv_state, hgrn_state, lbl, nw, w_in, cw, cb, gnw, w_out, cmat, masks,
      x1_flat)


def _router_kernel(x1_ref, nw_ref, rw_hi_ref, rw_lo_ref, rb_ref, ltri_ref, utri_ref, xn_ref, meta_ref, cnt_ref):
    n = x1_ref.shape[0]
    xn = _rmsnorm(x1_ref[...], nw_ref[...])
    xn_ref[...] = xn.astype(BF16)
    x_hi, x_lo = _split2(xn)
    logits = (_dot(x_hi, rw_hi_ref[...]) + _dot(x_lo, rw_hi_ref[...]) + _dot(x_hi, rw_lo_ref[...])
              + rb_ref[...])
    lane = lax.broadcasted_iota(jnp.int32, (n, LANES), 1).astype(F32)
    work = logits
    vals, ids = [], []
    for _ in range(TOP_K):
        m = jnp.max(work, axis=-1, keepdims=True)
        i = jnp.min(jnp.where(work == m, lane, float(LANES)), axis=-1, keepdims=True)
        vals.append(m)
        ids.append(i)
        work = jnp.where(lane == i, -jnp.inf, work)
    es = [jnp.exp(v - vals[0]) for v in vals]
    den = es[0] + es[1] + es[2] + es[3]
    gates = [e / den for e in es]

    onehots = [(lane == i) for i in ids]
    multi = jnp.zeros((n, LANES), F32)
    for oh in onehots:
        multi = multi + oh.astype(F32)
    counts = jnp.sum(multi, axis=0, keepdims=True)
    before = _dot(ltri_ref[...], multi.astype(BF16))
    seg = jnp.ceil(counts * (1.0 / SEG_ALIGN)) * SEG_ALIGN
    seg_rows = jnp.broadcast_to(seg, (BF16_ROWS, LANES)).astype(BF16)
    seg_off = _dot(seg_rows, utri_ref[...])[0:1, :]
    slot_of = seg_off + before
    meta = jnp.zeros((n, LANES), F32)
    for k in range(TOP_K):
        slot = jnp.sum(jnp.where(onehots[k], slot_of, 0.0), axis=-1, keepdims=True)
        meta = jnp.where(lane == k, ids[k], meta)
        meta = jnp.where(lane == TOP_K + k, gates[k], meta)
        meta = jnp.where(lane == 2 * TOP_K + k, slot, meta)
    meta_ref[...] = meta
    cnt_ref[...] = jnp.broadcast_to(counts, (SUBLANES, LANES))


def _router(x1, nw, router_w, router_b):
    t = x1.shape[0]
    n_tiles = t // TOKEN_TILE
    rw = jnp.zeros((D_MODEL, LANES), F32).at[:, :N_EXPERTS].set(router_w)
    rw_hi, rw_lo = _split2(rw)
    rb = jnp.full((1, LANES), -1e30, F32).at[0, :N_EXPERTS].set(router_b)
    idx = np.arange(TOKEN_TILE)
    ltri = jnp.asarray(idx[None, :] < idx[:, None], BF16)
    lid = np.arange(LANES)
    utri = jnp.asarray(lid[:, None] < lid[None, :], BF16)
    const = lambda shape: pl.BlockSpec(shape, lambda i: (0,) * len(shape))
    return pl.pallas_call(
        _router_kernel,
        grid=(n_tiles,),
        in_specs=[
            pl.BlockSpec((TOKEN_TILE, D_MODEL), lambda i: (i, 0)),
            const((1, D_MODEL)), const((D_MODEL, LANES)), const((D_MODEL, LANES)), const((1, LANES)),
            const((TOKEN_TILE, TOKEN_TILE)), const((LANES, LANES)),
        ],
        out_specs=[
            pl.BlockSpec((TOKEN_TILE, D_MODEL), lambda i: (i, 0)),
            pl.BlockSpec((TOKEN_TILE, LANES), lambda i: (i, 0)),
            pl.BlockSpec((None, SUBLANES, LANES), lambda i: (i, 0, 0)),
        ],
        out_shape=[
            jax.ShapeDtypeStruct((t, D_MODEL), BF16),
            jax.ShapeDtypeStruct((t, LANES), F32),
            jax.ShapeDtypeStruct((n_tiles, SUBLANES, LANES), F32),
        ],
        compiler_params=pltpu.CompilerParams(
            dimension_semantics=("arbitrary",), vmem_limit_bytes=VMEM_LIMIT_BYTES),
        name="router",
    )(x1, nw, rw_hi, rw_lo, rb, ltri, utri)


def _segment_tables(cnt):
    seg = (cnt + SEG_ALIGN - 1) // SEG_ALIGN * SEG_ALIGN
    src = jnp.cumsum(seg, axis=1) - seg
    tot = jnp.sum(seg, axis=0)
    cap = (tot + ROW_TILE - 1) // ROW_TILE * ROW_TILE
    base = jnp.cumsum(cap) - cap
    dst = base[None, :] + jnp.cumsum(seg, axis=0) - seg
    n_used = jnp.sum(cap) // ROW_TILE
    fill = base + tot
    lane = jnp.arange(N_EXPERTS)
    counts, srcs, dsts = [], [], []
    for size in SEG_SIZES:
        has = (seg & size) != 0
        done = seg & ~(2 * size - 1)
        place = has[:, :, None] & ((jnp.cumsum(has, axis=1) - 1)[:, :, None] == lane[None, None, :])
        counts.append(jnp.sum(has, axis=1))
        srcs.append(jnp.sum(jnp.where(place, (src + done)[:, :, None], 0), axis=1))
        dsts.append(jnp.sum(jnp.where(place, (dst + done)[:, :, None], 0), axis=1))
    i32 = lambda a: a.astype(jnp.int32).reshape(-1)
    chunks = (i32(jnp.stack(counts, axis=1)), i32(jnp.stack(srcs, axis=1)), i32(jnp.stack(dsts, axis=1)))
    return chunks, i32(fill), i32(cap - tot), i32(base), i32(cap // ROW_TILE), i32(n_used)


def _for_each_chunk(n, fn):
    for size in SEG_SIZES:
        done = n & ~(2 * size - 1)

        @pl.when((n & size) != 0)
        def _():
            fn(done, size)


def _segment_copies(cnt_ref, src_ref, dst_ref, tile, make_copy, start):
    for si, size in enumerate(SEG_SIZES):
        first = (tile * len(SEG_SIZES) + si) * N_EXPERTS

        def one(p, carry, si=si, size=size, first=first):
            copy = make_copy(pl.multiple_of(src_ref[first + p], SEG_ALIGN),
                             pl.multiple_of(dst_ref[first + p], SEG_ALIGN), size)
            if start:
                copy.start(priority=si % 2)
            else:
                copy.wait()
            return carry
        lax.fori_loop(0, cnt_ref[tile * len(SEG_SIZES) + si], one, 0)


SORT_BLOCK = 512


def _slot_matrix(slots, block, gates):
    row = (lax.broadcasted_iota(jnp.int32, (SORT_BLOCK, TOKEN_TILE), 0) + block * SORT_BLOCK).astype(F32)
    m = jnp.zeros((SORT_BLOCK, TOKEN_TILE), F32)
    for k in range(TOP_K):
        m = jnp.where(row == slots[k:k + 1, :], 1.0 if gates is None else gates[k:k + 1, :], m)
    return m.astype(BF16)


def _dispatch_kernel(seg_ref, src_ref, dst_ref, fill_ref, gap_ref, nu_ref, slot_ref, xn_ref, xs_ref, sorted_ref,
                     zero_ref, sem):
    tile = pl.program_id(0)

    @pl.when(tile == 0)
    def _():
        zero_ref[...] = jnp.zeros_like(zero_ref)

        def fill_copy(row, size):
            return pltpu.make_async_copy(zero_ref.at[pl.ds(0, size)],
                                         xs_ref.at[pl.ds(pl.multiple_of(row, SEG_ALIGN), size)], sem)

        def gaps(act):
            def per_expert(e, carry):
                _for_each_chunk(gap_ref[e], lambda done, size: act(fill_copy(fill_ref[e] + done, size)))
                return carry
            lax.fori_loop(0, N_EXPERTS, per_expert, 0)

        def tail(act):
            n_tail = xs_ref.shape[0] // ROW_TILE - nu_ref[0]
            lax.fori_loop(0, n_tail, lambda i, c: (act(fill_copy((nu_ref[0] + i) * ROW_TILE, ROW_TILE)), c)[1], 0)

        gaps(lambda c: c.start())
        tail(lambda c: c.start())
        gaps(lambda c: c.wait())
        tail(lambda c: c.wait())

    slots = slot_ref[...]
    xn = xn_ref[...]
    for rb in range(TILE_CAP // SORT_BLOCK):
        sorted_ref[rb * SORT_BLOCK:(rb + 1) * SORT_BLOCK, :] = _dot(
            _slot_matrix(slots, rb, None), xn).astype(BF16)

    def make_copy(src, dst, size):
        return pltpu.make_async_copy(sorted_ref.at[pl.ds(src, size)], xs_ref.at[pl.ds(dst, size)], sem)

    _segment_copies(seg_ref, src_ref, dst_ref, tile, make_copy, start=True)
    _segment_copies(seg_ref, src_ref, dst_ref, tile, make_copy, start=False)


def _dispatch(tables, slot_t, xn, n_rows_total):
    t = xn.shape[0]
    n_tiles = t // TOKEN_TILE
    return pl.pallas_call(
        _dispatch_kernel,
        grid_spec=pltpu.PrefetchScalarGridSpec(
            num_scalar_prefetch=len(tables),
            grid=(n_tiles,),
            in_specs=[
                pl.BlockSpec((SUBLANES, TOKEN_TILE), lambda i, *_: (0, i)),
                pl.BlockSpec((TOKEN_TILE, D_MODEL), lambda i, *_: (i, 0)),
            ],
            out_specs=pl.BlockSpec(memory_space=pl.ANY),
            scratch_shapes=[
                pltpu.VMEM((TILE_CAP, D_MODEL), BF16),
                pltpu.VMEM((ROW_TILE, D_MODEL), BF16),
                pltpu.SemaphoreType.DMA(()),
            ],
        ),
        out_shape=jax.ShapeDtypeStruct((n_rows_total, D_MODEL), BF16),
        compiler_params=pltpu.CompilerParams(
            dimension_semantics=("arbitrary",), vmem_limit_bytes=VMEM_LIMIT_BYTES),
        name="dispatch",
    )(*tables, slot_t, xn)


def _experts_kernel(base_ref, nt_ref, nu_ref, wg_ref, bg_ref, wu_ref, bu_ref, wd_ref, bd_ref, xs_ref, zs_ref,
                    wg_bf, wu_bf, wd_bf, xbuf, hbuf, zbuf, in_sem, out_sem):
    e = pl.program_id(0)
    n = nt_ref[e]
    base = base_ref[e]

    def x_copy(i, slot):
        rows = pl.ds(pl.multiple_of(base + i * ROW_TILE, ROW_TILE), ROW_TILE)
        return pltpu.make_async_copy(xs_ref.at[rows], xbuf.at[slot], in_sem.at[slot])

    def z_copy(row, slot):
        rows = pl.ds(pl.multiple_of(row, ROW_TILE), ROW_TILE)
        return pltpu.make_async_copy(zbuf.at[slot], zs_ref.at[rows], out_sem.at[slot])

    def hidden(slot):
        x = xbuf[slot]
        a = jnp.minimum(_dot(x, wg_bf[...]) + bg_ref[...], SWIGLU_LIMIT)
        u = jnp.clip(_dot(x, wu_bf[...]) + bu_ref[...], -SWIGLU_LIMIT, SWIGLU_LIMIT)
        hbuf[slot] = ((u + 1.0) * a * jax.nn.sigmoid(SWIGLU_ALPHA * a)).astype(BF16)

    def project_down(i, slot):
        @pl.when(i >= 2)
        def _():
            z_copy(base, slot).wait()
        zbuf[slot] = (_dot(hbuf[slot], wd_bf[...]) + bd_ref[...]).astype(BF16)

    @pl.when(n > 0)
    def _():
        x_copy(0, 0).start(priority=1)

        @pl.when(n > 1)
        def _():
            x_copy(1, 1).start(priority=1)

        wg_bf[...] = wg_ref[...].astype(BF16)
        wu_bf[...] = wu_ref[...].astype(BF16)
        wd_bf[...] = wd_ref[...].astype(BF16)
        x_copy(0, 0).wait()
        hidden(0)

        def step(i, carry):
            slot = i % 2
            x_copy(i + 1, 1 - slot).wait()

            @pl.when(i + 2 < n)
            def _():
                x_copy(i + 2, slot).start(priority=1)

            project_down(i, slot)
            hidden(1 - slot)
            z_copy(base + i * ROW_TILE, slot).start()
            return carry

        lax.fori_loop(0, n - 1, step, 0)
        project_down(n - 1, (n - 1) % 2)
        z_copy(base + (n - 1) * ROW_TILE, (n - 1) % 2).start()

        @pl.when(n >= 2)
        def _():
            z_copy(base, n % 2).wait()
        z_copy(base, (n - 1) % 2).wait()

    @pl.when(e == pl.num_programs(0) - 1)
    def _():
        zbuf[0] = jnp.zeros((ROW_TILE, D_MODEL), BF16)
        n_tail = zs_ref.shape[0] // ROW_TILE - nu_ref[0]
        tail = lambda act: lax.fori_loop(
            0, n_tail, lambda i, c: (act(z_copy((nu_ref[0] + i) * ROW_TILE, 0)), c)[1], 0)
        tail(lambda c: c.start())
        tail(lambda c: c.wait())


def _experts(base, n_tiles, n_used, xs, w_gate, b_gate, w_up, b_up, w_down, b_down):
    w_spec = pl.BlockSpec((None, D_MODEL, D_MODEL), lambda e, *_: (e, 0, 0))
    b_spec = pl.BlockSpec((None, 1, D_MODEL), lambda e, *_: (e, 0, 0))
    any_spec = pl.BlockSpec(memory_space=pl.ANY)
    return pl.pallas_call(
        _experts_kernel,
        grid_spec=pltpu.PrefetchScalarGridSpec(
            num_scalar_prefetch=3,
            grid=(N_EXPERTS,),
            in_specs=[w_spec, b_spec, w_spec, b_spec, w_spec, b_spec, any_spec],
            out_specs=any_spec,
            scratch_shapes=[pltpu.VMEM((D_MODEL, D_MODEL), BF16) for _ in range(3)] + [
                pltpu.VMEM((2, ROW_TILE, D_MODEL), BF16) for _ in range(3)] + [
                pltpu.SemaphoreType.DMA((2,)), pltpu.SemaphoreType.DMA((2,))],
        ),
        out_shape=jax.ShapeDtypeStruct(xs.shape, BF16),
        compiler_params=pltpu.CompilerParams(
            dimension_semantics=("arbitrary",), vmem_limit_bytes=VMEM_LIMIT_BYTES),
        name="experts",
    )(base, n_tiles, n_used, w_gate, b_gate[:, None, :], w_up, b_up[:, None, :], w_down, b_down[:, None, :], xs)


def _combine_kernel(n_first, seg_ref, src_ref, dst_ref, route_ref, x1_ref, fw_ref, zs_ref, ya_ref, yb_ref,
                    sorted_ref, sem):
    tile = pl.program_id(0)
    sorted_ref[...] = jnp.zeros_like(sorted_ref)

    def make_copy(src, dst, size):
        return pltpu.make_async_copy(zs_ref.at[pl.ds(dst, size)], sorted_ref.at[pl.ds(src, size)], sem)

    _segment_copies(seg_ref, src_ref, dst_ref, tile, make_copy, start=True)
    route = route_ref[...]
    slots, gates = route[0:TOP_K, :], route[TOP_K:2 * TOP_K, :]
    n_blocks = TILE_CAP // SORT_BLOCK
    early = [_slot_matrix(slots, cb, gates) for cb in range(2)]
    _segment_copies(seg_ref, src_ref, dst_ref, tile, make_copy, start=False)
    moe = jnp.zeros((TOKEN_TILE, D_MODEL), F32)
    for cb in range(n_blocks):
        m = early[cb] if cb < len(early) else _slot_matrix(slots, cb, gates)
        moe = moe + _dot_tn(m, sorted_ref[cb * SORT_BLOCK:(cb + 1) * SORT_BLOCK, :])
    y = _rmsnorm(x1_ref[...] + moe, fw_ref[...])

    @pl.when(tile < n_first)
    def _():
        ya_ref[...] = y

    @pl.when(tile >= n_first)
    def _():
        yb_ref[...] = y


def _combine(tables, route_t, x1, final_w, zs, rows_first):
    seg, src, dst = tables
    t = x1.shape[0]
    assert rows_first % TOKEN_TILE == 0 and 0 < rows_first < t and TILE_CAP % SORT_BLOCK == 0
    n_tiles = t // TOKEN_TILE
    n_first = rows_first // TOKEN_TILE
    return pl.pallas_call(
        functools.partial(_combine_kernel, n_first),
        grid_spec=pltpu.PrefetchScalarGridSpec(
            num_scalar_prefetch=3,
            grid=(n_tiles,),
            in_specs=[
                pl.BlockSpec((SUBLANES, TOKEN_TILE), lambda i, *_: (0, i)),
                pl.BlockSpec((TOKEN_TILE, D_MODEL), lambda i, *_: (i, 0)),
                pl.BlockSpec((1, D_MODEL), lambda i, *_: (0, 0)),
                pl.BlockSpec(memory_space=pl.ANY),
            ],
            out_specs=[
                pl.BlockSpec((TOKEN_TILE, D_MODEL), lambda i, *_: (jnp.minimum(i, n_first - 1), 0)),
                pl.BlockSpec((TOKEN_TILE, D_MODEL), lambda i, *_: (jnp.maximum(i - n_first, 0), 0)),
            ],
            scratch_shapes=[pltpu.VMEM((TILE_CAP, D_MODEL), BF16), pltpu.SemaphoreType.DMA(())],
        ),
        out_shape=[jax.ShapeDtypeStruct((rows_first, D_MODEL), F32),
                   jax.ShapeDtypeStruct((t - rows_first, D_MODEL), F32)],
        compiler_params=pltpu.CompilerParams(
            dimension_semantics=("arbitrary",), vmem_limit_bytes=VMEM_LIMIT_BYTES),
        name="combine",
    )(seg, src, dst, route_t, x1, final_w, zs)


def _moe_and_final_norm(x1, rows_first, norm_ffn_w, router_w, router_b, w_gate, b_gate, w_up, b_up, w_down, b_down,
                        final_w):
    t = x1.shape[0]
    assert t % TOKEN_TILE == 0
    n_tiles = t // TOKEN_TILE
    n_rows_total = TOP_K * t + n_tiles * N_EXPERTS * (SEG_ALIGN - 1) + N_EXPERTS * (ROW_TILE - 1)
    n_rows_total = (n_rows_total + ROW_TILE - 1) // ROW_TILE * ROW_TILE
    xn, meta, cnt = _router(x1, norm_ffn_w.reshape(1, D_MODEL), router_w, router_b)
    cnt = cnt[:, 0, :N_EXPERTS].astype(jnp.int32)
    chunks, fill, gap, base, n_row_tiles, n_used = _segment_tables(cnt)
    route_t = jnp.concatenate([meta[:, 2 * TOP_K:3 * TOP_K], meta[:, TOP_K:2 * TOP_K]], axis=1).T
    xs = _dispatch((*chunks, fill, gap, n_used), route_t, xn, n_rows_total)
    zs = _experts(base, n_row_tiles, n_used, xs, w_gate, b_gate, w_up, b_up, w_down, b_down)
    return _combine(chunks, route_t, x1, final_w.reshape(1, D_MODEL), zs, rows_first)


def kernel(x_prompt, x_sample, state_conv, state_hgrn, lb_logits, norm_mix_w, w_in, conv_w, conv_b, gnorm_w,
           w_out, norm_ffn_w, router_w, router_b, w_gate, b_gate, w_up, b_up, w_down, b_down, final_norm_w):
    assert norm_mix_w.shape[0] == 1 and lb_logits.shape[0] == 2, "single-layer step"
    b, l, _ = x_prompt.shape
    nb, ls, _ = x_sample.shape
    lbl = lb_logits.astype(F32)
    nw = norm_mix_w[0].reshape(1, D_MODEL)
    w_in_bf = w_in[0].astype(BF16)
    w_out_bf = w_out[0].astype(BF16)
    cw, cb = conv_w[0], conv_b[0].reshape(1, D_CONV)
    gnw = gnorm_w[0].reshape(1, D_HGRN)
    rows_p, rows_s = b * l, nb * ls
    x1, conv_p, hgrn_p = _mix_prompt(x_prompt, lbl, nw, w_in_bf, cw, cb, gnw, w_out_bf, rows_p + rows_s)
    x1, conv_s, hgrn_s = _mix_sample(x_sample, state_conv[0], state_hgrn[0], lbl, nw, w_in_bf, cw, cb, gnw,
                                     w_out_bf, x1, rows_p)
    y_p, y_s = _moe_and_final_norm(x1, rows_p, norm_ffn_w[0], router_w[0], router_b[0], w_gate[0], b_gate[0],
                                   w_up[0], b_up[0], w_down[0], b_down[0], final_norm_w)
    conv_p = conv_p[:, SUBLANES - 2:, :]
    conv_s = conv_s[:, ls - 2:, :]
    return (y_p.reshape(b, l, D_MODEL), y_s.reshape(nb, ls, D_MODEL), conv_p[None], hgrn_p[None], conv_s[None],
            hgrn_s[None])
```

```python
import functools

import numpy as np
import jax
import jax.numpy as jnp
from jax import lax
from jax.experimental import pallas as pl
from jax.experimental.pallas import tpu as pltpu

F32 = jnp.float32
BF16 = jnp.bfloat16

D_MODEL = 1024
D_CONV = 512
D_HGRN = 512
N_HEADS = 4
D_HEAD = 128
D_PROJ = 3 * D_CONV + 4 * D_HGRN
N_EXPERTS = 32
TOP_K = 4
SWIGLU_LIMIT = 7.0
SWIGLU_ALPHA = 1.702
EPS = 1e-6
PROMPT_CHUNK = 64

LANES = 128
SUBLANES = 8
BF16_ROWS = 16
VMEM_LIMIT_BYTES = 56 * 1024 * 1024

MIX_SLAB = 256
MIX_ROWS = 512
MIX_SEQS = 16
TOKEN_TILE = 512
SEG_ALIGN = BF16_ROWS
ROW_TILE = 512
SEG_SIZES = tuple(SEG_ALIGN << i for i in reversed(range(6)))
TILE_CAP = TOP_K * TOKEN_TILE + N_EXPERTS * SEG_ALIGN

NT_DIMS = (((1,), (1,)), ((), ()))
TN_DIMS = (((0,), (0,)), ((), ()))


def _dot(a, b):
    return jnp.dot(a, b, preferred_element_type=F32)


def _dot_nt(a, b):
    return lax.dot_general(a, b, NT_DIMS, preferred_element_type=F32)


def _dot_tn(a, b):
    return lax.dot_general(a, b, TN_DIMS, preferred_element_type=F32)


def _split2(x):
    hi = x.astype(BF16)
    lo = (x - hi.astype(F32)).astype(BF16)
    return hi, lo


def _rmsnorm(x, w):
    return x * lax.rsqrt(jnp.mean(x * x, axis=-1, keepdims=True) + EPS) * w


def _level_exponent_matrix(n, chunk, h):
    x = np.zeros((n, n), np.float32)
    for t in range(n):
        base = t - t % (2 * h)
        m = base + h - 1
        if t % (2 * h) >= h:
            x[t, m + 1:t + 1] = 1.0
        else:
            x[t, t + 1:m + 1] = 1.0
    return x


def _mix_constants(n, chunk, mxu_levels, all_levels):
    t = np.arange(n)
    same_chunk = (t[:, None] // chunk) == (t[None, :] // chunk)
    tri = (same_chunk & (t[None, :] <= t[:, None])).astype(np.float32)
    suf = (same_chunk & (t[None, :] > t[:, None])).astype(np.float32)
    cmat = np.concatenate([tri, suf] + [_level_exponent_matrix(n, chunk, h) for h in mxu_levels], axis=0)
    masks = []
    for h in all_levels:
        blk = (t[:, None] // (2 * h)) == (t[None, :] // (2 * h))
        masks.append((blk & ((t[:, None] % (2 * h)) >= h) & ((t[None, :] % (2 * h)) < h)).astype(np.float32))
    masks.append(np.eye(n, dtype=np.float32))
    return jnp.asarray(cmat, BF16), jnp.asarray(np.stack(masks), F32)


def _vpu_level_exponent(a, h):
    n = a.shape[0]
    pieces = []
    for j in range(n // (2 * h)):
        b = j * 2 * h
        ref = a[b + h - 1:b + h, :]
        pieces.append(ref - a[b:b + h, :])
        pieces.append(a[b + h:b + 2 * h, :] - ref)
    return jnp.concatenate(pieces, axis=0)


def _forget_lower_bound(lbl):
    m = jnp.max(lbl, axis=0, keepdims=True)
    e = jnp.exp(lbl - m)
    return e[0:1, :] / jnp.sum(e, axis=0, keepdims=True)


def _projections(x, nw, w_in_ref):
    h = _rmsnorm(x, nw).astype(BF16)
    return _dot(h, w_in_ref[...])


def _gates(proj, lb):
    q = proj[:, 3 * D_CONV:3 * D_CONV + D_HGRN]
    ff = proj[:, 3 * D_CONV + D_HGRN:3 * D_CONV + 2 * D_HGRN]
    vi = proj[:, 3 * D_CONV + 2 * D_HGRN:3 * D_CONV + 3 * D_HGRN]
    g = proj[:, 3 * D_CONV + 3 * D_HGRN:]
    e = jnp.exp(-jnp.abs(ff))
    r = 1.0 / (1.0 + e)
    pos = ff >= 0
    sig = jnp.where(pos, r, e * r)
    sig_neg = jnp.where(pos, e * r, r)
    logf = jnp.log(lb + (1.0 - lb) * sig)
    kk = (1.0 - lb) * sig_neg
    return q, logf, kk, vi, g


def _intra_scores_times_v(q, kk, vi, exps, masks_ref):
    n = q.shape[0]
    sc = [jnp.zeros((n, n), F32) for _ in range(N_HEADS)]
    for l, ex in enumerate(exps):
        if ex is None:
            qh, kh = q.astype(BF16), kk.astype(BF16)
        else:
            w = jnp.exp(ex)
            qh, kh = (q * w).astype(BF16), (kk * w).astype(BF16)
        mask = masks_ref[l]
        for hd in range(N_HEADS):
            hs = slice(hd * D_HEAD, (hd + 1) * D_HEAD)
            sc[hd] = sc[hd] + mask * _dot_nt(qh[:, hs], kh[:, hs])
    vb = vi.astype(BF16)
    return [_dot(sc[hd].astype(BF16), vb[:, hd * D_HEAD:(hd + 1) * D_HEAD]) for hd in range(N_HEADS)]


def _head_out(o, g, gnw):
    parts = []
    for hd in range(N_HEADS):
        oh = o[:, hd * D_HEAD:(hd + 1) * D_HEAD]
        parts.append(oh * lax.rsqrt(jnp.mean(oh * oh, axis=-1, keepdims=True) + EPS))
    on = jnp.concatenate(parts, axis=1)
    return on * gnw * (g * jax.nn.sigmoid(g))


def _pad_rows_bf16(a):
    return jnp.concatenate([a, jnp.zeros_like(a)], axis=0).astype(BF16)


PROMPT_VPU_LEVELS = (32, 16, 8)
MXU_LEVELS = (4, 2, 1)


def _mix_prompt_kernel(n_r, n_steps, *refs):
    step = pl.program_id(0)

    @pl.when(step < n_steps)
    def _():
        _mix_prompt_step(step % n_r, n_r, *refs)

    @pl.when(step >= n_steps)
    def _():
        x1_ref = refs[10]
        x1_ref[...] = jnp.zeros_like(x1_ref)


def _mix_prompt_step(r, n_r, x_ref, lbl_ref, nw_ref, win_ref, cw_ref, cb_ref, gnw_ref, wout_ref, cmat_ref,
                     masks_ref, x1_ref, nconv_ref, nstate_ref, st_ref, tail_ref, o_ref):
    rows = x_ref.shape[0]

    @pl.when(r == 0)
    def _():
        st_ref[...] = jnp.zeros_like(st_ref)
        tail_ref[...] = jnp.zeros_like(tail_ref)

    x = x_ref[...]
    proj = _projections(x, nw_ref[...], win_ref)
    lb = _forget_lower_bound(lbl_ref[...])

    bg = proj[:, 0:D_CONV]
    u = proj[:, D_CONV:2 * D_CONV] * proj[:, 2 * D_CONV:3 * D_CONV]
    rid = lax.broadcasted_iota(jnp.int32, (rows, 1), 0)
    t0 = tail_ref[SUBLANES - 2:SUBLANES - 1, :]
    t1 = tail_ref[SUBLANES - 1:SUBLANES, :]
    u1 = jnp.where(rid == 0, t1, pltpu.roll(u, 1, 0))
    u2 = jnp.where(rid == 0, t0, jnp.where(rid == 1, t1, pltpu.roll(u, 2, 0)))
    cw = cw_ref[...]
    y_conv = bg * (cw[0:1, :] * u2 + cw[1:2, :] * u1 + cw[2:3, :] * u + cb_ref[...])
    tail_ref[...] = u[rows - SUBLANES:rows, :]

    q, logf, kk, vi, g = _gates(proj, lb)
    lf_hi, lf_lo = _split2(logf)
    n_chunks = MIX_SLAB // PROMPT_CHUNK
    for s in range(rows // MIX_SLAB):
        sl = slice(s * MIX_SLAB, (s + 1) * MIX_SLAB)
        qs, ks, vs = q[sl], kk[sl], vi[sl]
        ex = _dot(cmat_ref[...], lf_hi[sl]) + _dot(cmat_ref[...], lf_lo[sl])
        a_pre = ex[0:MIX_SLAB]
        a_suf = ex[MIX_SLAB:2 * MIX_SLAB]
        exps = [_vpu_level_exponent(a_pre, h) for h in PROMPT_VPU_LEVELS]
        exps += [ex[(2 + i) * MIX_SLAB:(3 + i) * MIX_SLAB] for i in range(len(MXU_LEVELS))]
        exps.append(None)
        intra = _intra_scores_times_v(qs, ks, vs, exps, masks_ref)
        ea = jnp.exp(a_pre)
        qa = (qs * ea).astype(BF16)
        kb = (ks * jnp.exp(a_suf)).astype(BF16)
        vb = vs.astype(BF16)
        for c in range(n_chunks):
            cr = slice(c * PROMPT_CHUNK, (c + 1) * PROMPT_CHUNK)
            last = (c + 1) * PROMPT_CHUNK - 1
            for hd in range(N_HEADS):
                hs = slice(hd * D_HEAD, (hd + 1) * D_HEAD)
                st = st_ref[hd]
                inter = _dot_nt(qa[cr, hs], st.astype(BF16))
                o_ref[s * MIX_SLAB + c * PROMPT_CHUNK:s * MIX_SLAB + (c + 1) * PROMPT_CHUNK, hs] = (
                    inter + intra[hd][cr, :])
                st_ref[hd] = st * ea[last:last + 1, hs] + _dot_tn(vb[cr, hs], kb[cr, hs])

    o = _head_out(o_ref[...], g, gnw_ref[...])
    mix_in = jnp.concatenate([y_conv, o], axis=1).astype(BF16)
    x1_ref[...] = x + _dot(mix_in, wout_ref[...])

    @pl.when(r == n_r - 1)
    def _():
        nconv_ref[...] = u[rows - SUBLANES:rows, :]
        for hd in range(N_HEADS):
            nstate_ref[hd] = st_ref[hd].T


def _mix_prompt(x, lbl, nw, w_in, cw, cb, gnw, w_out, total_rows):
    b, l, _ = x.shape
    assert l % MIX_ROWS == 0 and MIX_ROWS % MIX_SLAB == 0 and l % PROMPT_CHUNK == 0
    assert total_rows % MIX_ROWS == 0
    n_r = l // MIX_ROWS
    n_steps = b * n_r
    seq = lambda s: jnp.minimum(s, n_steps - 1) // n_r
    cmat, masks = _mix_constants(MIX_SLAB, PROMPT_CHUNK, MXU_LEVELS, PROMPT_VPU_LEVELS + MXU_LEVELS)
    const = lambda shape: pl.BlockSpec(shape, lambda s: (0,) * len(shape))
    return pl.pallas_call(
        functools.partial(_mix_prompt_kernel, n_r, n_steps),
        grid=(total_rows // MIX_ROWS,),
        in_specs=[
            pl.BlockSpec((None, MIX_ROWS, D_MODEL), lambda s: (seq(s), jnp.minimum(s, n_steps - 1) % n_r, 0)),
            const((2, D_HGRN)), const((1, D_MODEL)), const((D_MODEL, D_PROJ)), const((3, D_CONV)),
            const((1, D_CONV)), const((1, D_HGRN)), const((D_MODEL, D_MODEL)), const(cmat.shape),
            const(masks.shape),
        ],
        out_specs=[
            pl.BlockSpec((MIX_ROWS, D_MODEL), lambda s: (s, 0)),
            pl.BlockSpec((None, SUBLANES, D_CONV), lambda s: (seq(s), 0, 0)),
            pl.BlockSpec((None, N_HEADS, D_HEAD, D_HEAD), lambda s: (seq(s), 0, 0, 0)),
        ],
        out_shape=[
            jax.ShapeDtypeStruct((total_rows, D_MODEL), F32),
            jax.ShapeDtypeStruct((b, SUBLANES, D_CONV), F32),
            jax.ShapeDtypeStruct((b, N_HEADS, D_HEAD, D_HEAD), F32),
        ],
        scratch_shapes=[
            pltpu.VMEM((N_HEADS, D_HEAD, D_HEAD), F32),
            pltpu.VMEM((SUBLANES, D_CONV), F32),
            pltpu.VMEM((MIX_ROWS, D_HGRN), F32),
        ],
        compiler_params=pltpu.CompilerParams(
            dimension_semantics=("arbitrary",), vmem_limit_bytes=VMEM_LIMIT_BYTES),
        name="mix_prompt",
    )(x, lbl, nw, w_in, cw, cb, gnw, w_out, cmat, masks)


def _mix_sample_kernel(x_ref, cs_ref, hs_ref, lbl_ref, nw_ref, win_ref, cw_ref, cb_ref, gnw_ref, wout_ref,
                       cmat_ref, masks_ref, x1_in_ref, x1_ref, nconv_ref, nstate_ref, o_ref):
    del x1_in_ref
    nseq, length, _ = nconv_ref.shape
    rows = nseq * length
    x = x_ref[...]
    proj = _projections(x, nw_ref[...], win_ref)
    lb = _forget_lower_bound(lbl_ref[...])

    bg = proj[:, 0:D_CONV]
    u = proj[:, D_CONV:2 * D_CONV] * proj[:, 2 * D_CONV:3 * D_CONV]
    cs = cs_ref[...]
    expand = lambda a: jnp.broadcast_to(a, (nseq, length, D_CONV)).reshape(rows, D_CONV)
    t0 = expand(cs[:, 0:1, :])
    t1 = expand(cs[:, 1:2, :])
    pos = lax.broadcasted_iota(jnp.int32, (rows, 1), 0) % length
    u1 = jnp.where(pos == 0, t1, pltpu.roll(u, 1, 0))
    u2 = jnp.where(pos == 0, t0, jnp.where(pos == 1, t1, pltpu.roll(u, 2, 0)))
    cw = cw_ref[...]
    y_conv = bg * (cw[0:1, :] * u2 + cw[1:2, :] * u1 + cw[2:3, :] * u + cb_ref[...])
    nconv_ref[...] = u.reshape(nseq, length, D_CONV)

    q, logf, kk, vi, g = _gates(proj, lb)
    lf_hi, lf_lo = _split2(logf)
    ex = _dot(cmat_ref[...], lf_hi) + _dot(cmat_ref[...], lf_lo)
    a_pre = ex[0:rows]
    a_suf = ex[rows:2 * rows]
    exps = [ex[(2 + i) * rows:(3 + i) * rows] for i in range(len(MXU_LEVELS))] + [None]
    intra = _intra_scores_times_v(q, kk, vi, exps, masks_ref)
    ea = jnp.exp(a_pre)
    qa = q * ea
    kb = kk * jnp.exp(a_suf)
    for s in range(nseq):
        cr = slice(s * length, (s + 1) * length)
        last = (s + 1) * length - 1
        for hd in range(N_HEADS):
            hs = slice(hd * D_HEAD, (hd + 1) * D_HEAD)
            st = hs_ref[s, hd].T
            inter = _dot_nt(_pad_rows_bf16(qa[cr, hs]), st.astype(BF16))[0:length, :]
            o_ref[cr, hs] = inter + intra[hd][cr, :]
            st_new = st * ea[last:last + 1, hs] + _dot_tn(_pad_rows_bf16(vi[cr, hs]), _pad_rows_bf16(kb[cr, hs]))
            nstate_ref[s, hd] = st_new.T

    o = _head_out(o_ref[...], g, gnw_ref[...])
    mix_in = jnp.concatenate([y_conv, o], axis=1).astype(BF16)
    x1_ref[...] = x + _dot(mix_in, wout_ref[...])


def _mix_sample(x, conv_state, hgrn_state, lbl, nw, w_in, cw, cb, gnw, w_out, x1_flat, row_offset):
    nb, length, _ = x.shape
    rows = MIX_SEQS * length
    assert nb % MIX_SEQS == 0 and length == SUBLANES and row_offset % rows == 0
    first_block = row_offset // rows
    cmat, masks = _mix_constants(rows, length, MXU_LEVELS, MXU_LEVELS)
    const = lambda shape: pl.BlockSpec(shape, lambda i: (0,) * len(shape))
    return pl.pallas_call(
        _mix_sample_kernel,
        grid=(nb // MIX_SEQS,),
        in_specs=[
            pl.BlockSpec((rows, D_MODEL), lambda i: (i, 0)),
            pl.BlockSpec((MIX_SEQS, 2, D_CONV), lambda i: (i, 0, 0)),
            pl.BlockSpec((MIX_SEQS, N_HEADS, D_HEAD, D_HEAD), lambda i: (i, 0, 0, 0)),
            const((2, D_HGRN)), const((1, D_MODEL)), const((D_MODEL, D_PROJ)), const((3, D_CONV)),
            const((1, D_CONV)), const((1, D_HGRN)), const((D_MODEL, D_MODEL)), const(cmat.shape),
            const(masks.shape), pl.BlockSpec(memory_space=pl.ANY),
        ],
        out_specs=[
            pl.BlockSpec((rows, D_MODEL), lambda i: (first_block + i, 0)),
            pl.BlockSpec((MIX_SEQS, length, D_CONV), lambda i: (i, 0, 0)),
            pl.BlockSpec((MIX_SEQS, N_HEADS, D_HEAD, D_HEAD), lambda i: (i, 0, 0, 0)),
        ],
        out_shape=[
            jax.ShapeDtypeStruct(x1_flat.shape, F32),
            jax.ShapeDtypeStruct((nb, length, D_CONV), F32),
            jax.ShapeDtypeStruct((nb, N_HEADS, D_HEAD, D_HEAD), F32),
        ],
        scratch_shapes=[pltpu.VMEM((rows, D_HGRN), F32)],
        input_output_aliases={12: 0},
        compiler_params=pltpu.CompilerParams(
            dimension_semantics=("arbitrary",), vmem_limit_bytes=VMEM_LIMIT_BYTES),
        name="mix_sample",
    )(x.reshape(nb * length, D_MODEL), conv_state, hgrn_state, lbl, nw, w_in, cw, cb, gnw, w_out, cmat, masks,
      x1_flat)


def _router_kernel(x1_ref, nw_ref, rw_hi_ref, rw_lo_ref, rb_ref, ltri_ref, utri_ref, xn_ref, meta_ref, cnt_ref):
    n = x1_ref.shape[0]
    xn = _rmsnorm(x1_ref[...], nw_ref[...])
    xn_ref[...] = xn.astype(BF16)
    x_hi, x_lo = _split2(xn)
    logits = (_dot(x_hi, rw_hi_ref[...]) + _dot(x_lo, rw_hi_ref[...]) + _dot(x_hi, rw_lo_ref[...])
              + rb_ref[...])
    lane = lax.broadcasted_iota(jnp.int32, (n, LANES), 1).astype(F32)
    work = logits
    vals, ids = [], []
    for _ in range(TOP_K):
        m = jnp.max(work, axis=-1, keepdims=True)
        i = jnp.min(jnp.where(work == m, lane, float(LANES)), axis=-1, keepdims=True)
        vals.append(m)
        ids.append(i)
        work = jnp.where(lane == i, -jnp.inf, work)
    es = [jnp.exp(v - vals[0]) for v in vals]
    den = es[0] + es[1] + es[2] + es[3]
    gates = [e / den for e in es]

    onehots = [(lane == i) for i in ids]
    multi = jnp.zeros((n, LANES), F32)
    for oh in onehots:
        multi = multi + oh.astype(F32)
    counts = jnp.sum(multi, axis=0, keepdims=True)
    before = _dot(ltri_ref[...], multi.astype(BF16))
    seg = jnp.ceil(counts * (1.0 / SEG_ALIGN)) * SEG_ALIGN
    seg_rows = jnp.broadcast_to(seg, (BF16_ROWS, LANES)).astype(BF16)
    seg_off = _dot(seg_rows, utri_ref[...])[0:1, :]
    slot_of = seg_off + before
    meta = jnp.zeros((n, LANES), F32)
    for k in range(TOP_K):
        slot = jnp.sum(jnp.where(onehots[k], slot_of, 0.0), axis=-1, keepdims=True)
        meta = jnp.where(lane == k, ids[k], meta)
        meta = jnp.where(lane == TOP_K + k, gates[k], meta)
        meta = jnp.where(lane == 2 * TOP_K + k, slot, meta)
    meta_ref[...] = meta
    cnt_ref[...] = jnp.broadcast_to(counts, (SUBLANES, LANES))


def _router(x1, nw, router_w, router_b):
    t = x1.shape[0]
    n_tiles = t // TOKEN_TILE
    rw = jnp.zeros((D_MODEL, LANES), F32).at[:, :N_EXPERTS].set(router_w)
    rw_hi, rw_lo = _split2(rw)
    rb = jnp.full((1, LANES), -1e30, F32).at[0, :N_EXPERTS].set(router_b)
    idx = np.arange(TOKEN_TILE)
    ltri = jnp.asarray(idx[None, :] < idx[:, None], BF16)
    lid = np.arange(LANES)
    utri = jnp.asarray(lid[:, None] < lid[None, :], BF16)
    const = lambda shape: pl.BlockSpec(shape, lambda i: (0,) * len(shape))
    return pl.pallas_call(
        _router_kernel,
        grid=(n_tiles,),
        in_specs=[
            pl.BlockSpec((TOKEN_TILE, D_MODEL), lambda i: (i, 0)),
            const((1, D_MODEL)), const((D_MODEL, LANES)), const((D_MODEL, LANES)), const((1, LANES)),
            const((TOKEN_TILE, TOKEN_TILE)), const((LANES, LANES)),
        ],
        out_specs=[
            pl.BlockSpec((TOKEN_TILE, D_MODEL), lambda i: (i, 0)),
            pl.BlockSpec((TOKEN_TILE, LANES), lambda i: (i, 0)),
            pl.BlockSpec((None, SUBLANES, LANES), lambda i: (i, 0, 0)),
        ],
        out_shape=[
            jax.ShapeDtypeStruct((t, D_MODEL), BF16),
            jax.ShapeDtypeStruct((t, LANES), F32),
            jax.ShapeDtypeStruct((n_tiles, SUBLANES, LANES), F32),
        ],
        compiler_params=pltpu.CompilerParams(
            dimension_semantics=("arbitrary",), vmem_limit_bytes=VMEM_LIMIT_BYTES),
        name="router",
    )(x1, nw, rw_hi, rw_lo, rb, ltri, utri)


def _segment_tables(cnt):
    seg = (cnt + SEG_ALIGN - 1) // SEG_ALIGN * SEG_ALIGN
    src = jnp.cumsum(seg, axis=1) - seg
    tot = jnp.sum(seg, axis=0)
    cap = (tot + ROW_TILE - 1) // ROW_TILE * ROW_TILE
    base = jnp.cumsum(cap) - cap
    dst = base[None, :] + jnp.cumsum(seg, axis=0) - seg
    n_used = jnp.sum(cap) // ROW_TILE
    fill = base + tot
    lane = jnp.arange(N_EXPERTS)
    counts, srcs, dsts = [], [], []
    for size in SEG_SIZES:
        has = (seg & size) != 0
        done = seg & ~(2 * size - 1)
        place = has[:, :, None] & ((jnp.cumsum(has, axis=1) - 1)[:, :, None] == lane[None, None, :])
        counts.append(jnp.sum(has, axis=1))
        srcs.append(jnp.sum(jnp.where(place, (src + done)[:, :, None], 0), axis=1))
        dsts.append(jnp.sum(jnp.where(place, (dst + done)[:, :, None], 0), axis=1))
    i32 = lambda a: a.astype(jnp.int32).reshape(-1)
    chunks = (i32(jnp.stack(counts, axis=1)), i32(jnp.stack(srcs, axis=1)), i32(jnp.stack(dsts, axis=1)))
    return chunks, i32(fill), i32(cap - tot), i32(base), i32(cap // ROW_TILE), i32(n_used)


def _for_each_chunk(n, fn):
    for size in SEG_SIZES:
        done = n & ~(2 * size - 1)

        @pl.when((n & size) != 0)
        def _():
            fn(done, size)


def _segment_copies(cnt_ref, src_ref, dst_ref, tile, make_copy, start):
    for si, size in enumerate(SEG_SIZES):
        first = (tile * len(SEG_SIZES) + si) * N_EXPERTS

        def one(p, carry, si=si, size=size, first=first):
            copy = make_copy(pl.multiple_of(src_ref[first + p], SEG_ALIGN),
                             pl.multiple_of(dst_ref[first + p], SEG_ALIGN), size)
            if start:
                copy.start(priority=si % 2)
            else:
                copy.wait()
            return carry
        lax.fori_loop(0, cnt_ref[tile * len(SEG_SIZES) + si], one, 0)


SORT_BLOCK = 512


def _slot_matrix(slots, block, gates):
    row = (lax.broadcasted_iota(jnp.int32, (SORT_BLOCK, TOKEN_TILE), 0) + block * SORT_BLOCK).astype(F32)
    m = jnp.zeros((SORT_BLOCK, TOKEN_TILE), F32)
    for k in range(TOP_K):
        m = jnp.where(row == slots[k:k + 1, :], 1.0 if gates is None else gates[k:k + 1, :], m)
    return m.astype(BF16)


def _dispatch_kernel(seg_ref, src_ref, dst_ref, fill_ref, gap_ref, nu_ref, slot_ref, xn_ref, xs_ref, sorted_ref,
                     zero_ref, sem):
    tile = pl.program_id(0)

    @pl.when(tile == 0)
    def _():
        zero_ref[...] = jnp.zeros_like(zero_ref)

        def fill_copy(row, size):
            return pltpu.make_async_copy(zero_ref.at[pl.ds(0, size)],
                                         xs_ref.at[pl.ds(pl.multiple_of(row, SEG_ALIGN), size)], sem)

        def gaps(act):
            def per_expert(e, carry):
                _for_each_chunk(gap_ref[e], lambda done, size: act(fill_copy(fill_ref[e] + done, size)))
                return carry
            lax.fori_loop(0, N_EXPERTS, per_expert, 0)

        def tail(act):
            n_tail = xs_ref.shape[0] // ROW_TILE - nu_ref[0]
            lax.fori_loop(0, n_tail, lambda i, c: (act(fill_copy((nu_ref[0] + i) * ROW_TILE, ROW_TILE)), c)[1], 0)

        gaps(lambda c: c.start())
        tail(lambda c: c.start())
        gaps(lambda c: c.wait())
        tail(lambda c: c.wait())

    slots = slot_ref[...]
    xn = xn_ref[...]
    for rb in range(TILE_CAP // SORT_BLOCK):
        sorted_ref[rb * SORT_BLOCK:(rb + 1) * SORT_BLOCK, :] = _dot(
            _slot_matrix(slots, rb, None), xn).astype(BF16)

    def make_copy(src, dst, size):
        return pltpu.make_async_copy(sorted_ref.at[pl.ds(src, size)], xs_ref.at[pl.ds(dst, size)], sem)

    _segment_copies(seg_ref, src_ref, dst_ref, tile, make_copy, start=True)
    _segment_copies(seg_ref, src_ref, dst_ref, tile, make_copy, start=False)


def _dispatch(tables, slot_t, xn, n_rows_total):
    t = xn.shape[0]
    n_tiles = t // TOKEN_TILE
    return pl.pallas_call(
        _dispatch_kernel,
        grid_spec=pltpu.PrefetchScalarGridSpec(
            num_scalar_prefetch=len(tables),
            grid=(n_tiles,),
            in_specs=[
                pl.BlockSpec((SUBLANES, TOKEN_TILE), lambda i, *_: (0, i)),
                pl.BlockSpec((TOKEN_TILE, D_MODEL), lambda i, *_: (i, 0)),
            ],
            out_specs=pl.BlockSpec(memory_space=pl.ANY),
            scratch_shapes=[
                pltpu.VMEM((TILE_CAP, D_MODEL), BF16),
                pltpu.VMEM((ROW_TILE, D_MODEL), BF16),
                pltpu.SemaphoreType.DMA(()),
            ],
        ),
        out_shape=jax.ShapeDtypeStruct((n_rows_total, D_MODEL), BF16),
        compiler_params=pltpu.CompilerParams(
            dimension_semantics=("arbitrary",), vmem_limit_bytes=VMEM_LIMIT_BYTES),
        name="dispatch",
    )(*tables, slot_t, xn)


def _experts_kernel(base_ref, nt_ref, nu_ref, bg_ref, bu_ref, bd_ref, wg_hbm, wu_hbm, wd_hbm, xs_ref, zs_ref,
                    w_f32, wg_bf, wu_bf, wd_bf, xbuf, hbuf, zbuf, w_sem, in_sem, out_sem):
    e = pl.program_id(0)
    n_e = pl.num_programs(0)
    n = nt_ref[e]
    base = base_ref[e]
    w_slot = e % 2

    def w_copies(expert, slot):
        return [pltpu.make_async_copy(w_hbm.at[expert], w_f32.at[slot, j], w_sem.at[slot])
                for j, w_hbm in enumerate((wg_hbm, wu_hbm, wd_hbm))]

    def x_copy(i, slot):
        rows = pl.ds(pl.multiple_of(base + i * ROW_TILE, ROW_TILE), ROW_TILE)
        return pltpu.make_async_copy(xs_ref.at[rows], xbuf.at[slot], in_sem.at[slot])

    def z_copy(row, slot):
        rows = pl.ds(pl.multiple_of(row, ROW_TILE), ROW_TILE)
        return pltpu.make_async_copy(zbuf.at[slot], zs_ref.at[rows], out_sem.at[slot])

    def hidden(slot):
        x = xbuf[slot]
        a = jnp.minimum(_dot(x, wg_bf[...]) + bg_ref[...], SWIGLU_LIMIT)
        u = jnp.clip(_dot(x, wu_bf[...]) + bu_ref[...], -SWIGLU_LIMIT, SWIGLU_LIMIT)
        hbuf[slot] = ((u + 1.0) * a * jax.nn.sigmoid(SWIGLU_ALPHA * a)).astype(BF16)

    def project_down(i, slot):
        @pl.when(i >= 2)
        def _():
            z_copy(base, slot).wait()
        zbuf[slot] = (_dot(hbuf[slot], wd_bf[...]) + bd_ref[...]).astype(BF16)

    @pl.when(e == 0)
    def _():
        for c in w_copies(0, 0):
            c.start()

    @pl.when(n > 0)
    def _():
        x_copy(0, 0).start(priority=1)

    @pl.when(n > 1)
    def _():
        x_copy(1, 1).start(priority=1)

    @pl.when(e + 1 < n_e)
    def _():
        for c in w_copies(e + 1, 1 - w_slot):
            c.start()

    for c in w_copies(e, w_slot):
        c.wait()

    @pl.when(n > 0)
    def _():
        wg_bf[...] = w_f32[w_slot, 0].astype(BF16)
        wu_bf[...] = w_f32[w_slot, 1].astype(BF16)
        wd_bf[...] = w_f32[w_slot, 2].astype(BF16)
        x_copy(0, 0).wait()
        hidden(0)

        def step(i, carry):
            slot = i % 2
            x_copy(i + 1, 1 - slot).wait()

            @pl.when(i + 2 < n)
            def _():
                x_copy(i + 2, slot).start(priority=1)

            project_down(i, slot)
            hidden(1 - slot)
            z_copy(base + i * ROW_TILE, slot).start()
            return carry

        lax.fori_loop(0, n - 1, step, 0)
        project_down(n - 1, (n - 1) % 2)
        z_copy(base + (n - 1) * ROW_TILE, (n - 1) % 2).start()

        @pl.when(n >= 2)
        def _():
            z_copy(base, n % 2).wait()
        z_copy(base, (n - 1) % 2).wait()

    @pl.when(e == pl.num_programs(0) - 1)
    def _():
        zbuf[0] = jnp.zeros((ROW_TILE, D_MODEL), BF16)
        n_tail = zs_ref.shape[0] // ROW_TILE - nu_ref[0]
        tail = lambda act: lax.fori_loop(
            0, n_tail, lambda i, c: (act(z_copy((nu_ref[0] + i) * ROW_TILE, 0)), c)[1], 0)
        tail(lambda c: c.start())
        tail(lambda c: c.wait())


def _experts(base, n_tiles, n_used, xs, w_gate, b_gate, w_up, b_up, w_down, b_down):
    b_spec = pl.BlockSpec((None, 1, D_MODEL), lambda e, *_: (e, 0, 0))
    any_spec = pl.BlockSpec(memory_space=pl.ANY)
    return pl.pallas_call(
        _experts_kernel,
        grid_spec=pltpu.PrefetchScalarGridSpec(
            num_scalar_prefetch=3,
            grid=(N_EXPERTS,),
            in_specs=[b_spec, b_spec, b_spec, any_spec, any_spec, any_spec, any_spec],
            out_specs=any_spec,
            scratch_shapes=[pltpu.VMEM((2, 3, D_MODEL, D_MODEL), F32)] + [
                pltpu.VMEM((D_MODEL, D_MODEL), BF16) for _ in range(3)] + [
                pltpu.VMEM((2, ROW_TILE, D_MODEL), BF16) for _ in range(3)] + [
                pltpu.SemaphoreType.DMA((2,)), pltpu.SemaphoreType.DMA((2,)), pltpu.SemaphoreType.DMA((2,))],
        ),
        out_shape=jax.ShapeDtypeStruct(xs.shape, BF16),
        compiler_params=pltpu.CompilerParams(
            dimension_semantics=("arbitrary",), vmem_limit_bytes=VMEM_LIMIT_BYTES),
        name="experts",
    )(base, n_tiles, n_used, b_gate[:, None, :], b_up[:, None, :], b_down[:, None, :], w_gate, w_up, w_down, xs)


def _combine_kernel(n_first, seg_ref, src_ref, dst_ref, route_ref, x1_ref, fw_ref, zs_ref, ya_ref, yb_ref,
                    sorted_ref, sem):
    tile = pl.program_id(0)
    sorted_ref[...] = jnp.zeros_like(sorted_ref)

    def make_copy(src, dst, size):
        return pltpu.make_async_copy(zs_ref.at[pl.ds(dst, size)], sorted_ref.at[pl.ds(src, size)], sem)

    _segment_copies(seg_ref, src_ref, dst_ref, tile, make_copy, start=True)
    route = route_ref[...]
    slots, gates = route[0:TOP_K, :], route[TOP_K:2 * TOP_K, :]
    n_blocks = TILE_CAP // SORT_BLOCK
    early = [_slot_matrix(slots, cb, gates) for cb in range(2)]
    _segment_copies(seg_ref, src_ref, dst_ref, tile, make_copy, start=False)
    moe = jnp.zeros((TOKEN_TILE, D_MODEL), F32)
    for cb in range(n_blocks):
        m = early[cb] if cb < len(early) else _slot_matrix(slots, cb, gates)
        moe = moe + _dot_tn(m, sorted_ref[cb * SORT_BLOCK:(cb + 1) * SORT_BLOCK, :])
    y = _rmsnorm(x1_ref[...] + moe, fw_ref[...])

    @pl.when(tile < n_first)
    def _():
        ya_ref[...] = y

    @pl.when(tile >= n_first)
    def _():
        yb_ref[...] = y


def _combine(tables, route_t, x1, final_w, zs, rows_first):
    seg, src, dst = tables
    t = x1.shape[0]
    assert rows_first % TOKEN_TILE == 0 and 0 < rows_first < t and TILE_CAP % SORT_BLOCK == 0
    n_tiles = t // TOKEN_TILE
    n_first = rows_first // TOKEN_TILE
    return pl.pallas_call(
        functools.partial(_combine_kernel, n_first),
        grid_spec=pltpu.PrefetchScalarGridSpec(
            num_scalar_prefetch=3,
            grid=(n_tiles,),
            in_specs=[
                pl.BlockSpec((SUBLANES, TOKEN_TILE), lambda i, *_: (0, i)),
                pl.BlockSpec((TOKEN_TILE, D_MODEL), lambda i, *_: (i, 0)),
                pl.BlockSpec((1, D_MODEL), lambda i, *_: (0, 0)),
                pl.BlockSpec(memory_space=pl.ANY),
            ],
            out_specs=[
                pl.BlockSpec((TOKEN_TILE, D_MODEL), lambda i, *_: (jnp.minimum(i, n_first - 1), 0)),
                pl.BlockSpec((TOKEN_TILE, D_MODEL), lambda i, *_: (jnp.maximum(i - n_first, 0), 0)),
            ],
            scratch_shapes=[pltpu.VMEM((TILE_CAP, D_MODEL), BF16), pltpu.SemaphoreType.DMA(())],
        ),
        out_shape=[jax.ShapeDtypeStruct((rows_first, D_MODEL), F32),
                   jax.ShapeDtypeStruct((t - rows_first, D_MODEL), F32)],
        compiler_params=pltpu.CompilerParams(
            dimension_semantics=("arbitrary",), vmem_limit_bytes=VMEM_LIMIT_BYTES),
        name="combine",
    )(seg, src, dst, route_t, x1, final_w, zs)


def _moe_and_final_norm(x1, rows_first, norm_ffn_w, router_w, router_b, w_gate, b_gate, w_up, b_up, w_down, b_down,
                        final_w):
    t = x1.shape[0]
    assert t % TOKEN_TILE == 0
    n_tiles = t // TOKEN_TILE
    n_rows_total = TOP_K * t + n_tiles * N_EXPERTS * (SEG_ALIGN - 1) + N_EXPERTS * (ROW_TILE - 1)
    n_rows_total = (n_rows_total + ROW_TILE - 1) // ROW_TILE * ROW_TILE
    xn, meta, cnt = _router(x1, norm_ffn_w.reshape(1, D_MODEL), router_w, router_b)
    cnt = cnt[:, 0, :N_EXPERTS].astype(jnp.int32)
    chunks, fill, gap, base, n_row_tiles, n_used = _segment_tables(cnt)
    route_t = jnp.concatenate([meta[:, 2 * TOP_K:3 * TOP_K], meta[:, TOP_K:2 * TOP_K]], axis=1).T
    xs = _dispatch((*chunks, fill, gap, n_used), route_t, xn, n_rows_total)
    zs = _experts(base, n_row_tiles, n_used, xs, w_gate, b_gate, w_up, b_up, w_down, b_down)
    return _combine(chunks, route_t, x1, final_w.reshape(1, D_MODEL), zs, rows_first)


def kernel(x_prompt, x_sample, state_conv, state_hgrn, lb_logits, norm_mix_w, w_in, conv_w, conv_b, gnorm_w,
           w_out, norm_ffn_w, router_w, router_b, w_gate, b_gate, w_up, b_up, w_down, b_down, final_norm_w):
    assert norm_mix_w.shape[0] == 1 and lb_logits.shape[0] == 2, "single-layer step"
    b, l, _ = x_prompt.shape
    nb, ls, _ = x_sample.shape
    lbl = lb_logits.astype(F32)
    nw = norm_mix_w[0].reshape(1, D_MODEL)
    w_in_bf = w_in[0].astype(BF16)
    w_out_bf = w_out[0].astype(BF16)
    cw, cb = conv_w[0], conv_b[0].reshape(1, D_CONV)
    gnw = gnorm_w[0].reshape(1, D_HGRN)
    rows_p, rows_s = b * l, nb * ls
    x1, conv_p, hgrn_p = _mix_prompt(x_prompt, lbl, nw, w_in_bf, cw, cb, gnw, w_out_bf, rows_p + rows_s)
    x1, conv_s, hgrn_s = _mix_sample(x_sample, state_conv[0], state_hgrn[0], lbl, nw, w_in_bf, cw, cb, gnw,
                                     w_out_bf, x1, rows_p)
    y_p, y_s = _moe_and_final_norm(x1, rows_p, norm_ffn_w[0], router_w[0], router_b[0], w_gate[0], b_gate[0],
                                   w_up[0], b_up[0], w_down[0], b_down[0], final_norm_w)
    conv_p = conv_p[:, SUBLANES - 2:, :]
    conv_s = conv_s[:, ls - 2:, :]
    return (y_p.reshape(b, l, D_MODEL), y_s.reshape(nb, ls, D_MODEL), conv_p[None], hgrn_p[None], conv_s[None],
            hgrn_s[None])
```

```python
import functools

import numpy as np
import jax
import jax.numpy as jnp
from jax import lax
from jax.experimental import pallas as pl
from jax.experimental.pallas import tpu as pltpu

F32 = jnp.float32
BF16 = jnp.bfloat16

D_MODEL = 1024
D_CONV = 512
D_HGRN = 512
N_HEADS = 4
D_HEAD = 128
D_PROJ = 3 * D_CONV + 4 * D_HGRN
N_EXPERTS = 32
TOP_K = 4
SWIGLU_LIMIT = 7.0
SWIGLU_ALPHA = 1.702
EPS = 1e-6
PROMPT_CHUNK = 64

LANES = 128
SUBLANES = 8
BF16_ROWS = 16
VMEM_LIMIT_BYTES = 56 * 1024 * 1024

MIX_SLAB = 256
MIX_ROWS = 512
MIX_SEQS = 16
TOKEN_TILE = 512
SEG_ALIGN = BF16_ROWS
ROW_TILE = 512
SEG_SIZES = tuple(SEG_ALIGN << i for i in reversed(range(6)))
TILE_CAP = TOP_K * TOKEN_TILE + N_EXPERTS * SEG_ALIGN

NT_DIMS = (((1,), (1,)), ((), ()))
TN_DIMS = (((0,), (0,)), ((), ()))


def _dot(a, b):
    return jnp.dot(a, b, preferred_element_type=F32)


def _dot_nt(a, b):
    return lax.dot_general(a, b, NT_DIMS, preferred_element_type=F32)


def _dot_tn(a, b):
    return lax.dot_general(a, b, TN_DIMS, preferred_element_type=F32)


def _split2(x):
    hi = x.astype(BF16)
    lo = (x - hi.astype(F32)).astype(BF16)
    return hi, lo


def _rmsnorm(x, w):
    return x * lax.rsqrt(jnp.mean(x * x, axis=-1, keepdims=True) + EPS) * w


def _level_exponent_matrix(n, chunk, h):
    x = np.zeros((n, n), np.float32)
    for t in range(n):
        base = t - t % (2 * h)
        m = base + h - 1
        if t % (2 * h) >= h:
            x[t, m + 1:t + 1] = 1.0
        else:
            x[t, t + 1:m + 1] = 1.0
    return x


def _mix_constants(n, chunk, mxu_levels, all_levels):
    t = np.arange(n)
    same_chunk = (t[:, None] // chunk) == (t[None, :] // chunk)
    tri = (same_chunk & (t[None, :] <= t[:, None])).astype(np.float32)
    suf = (same_chunk & (t[None, :] > t[:, None])).astype(np.float32)
    cmat = np.concatenate([tri, suf] + [_level_exponent_matrix(n, chunk, h) for h in mxu_levels], axis=0)
    masks = []
    for h in all_levels:
        blk = (t[:, None] // (2 * h)) == (t[None, :] // (2 * h))
        masks.append((blk & ((t[:, None] % (2 * h)) >= h) & ((t[None, :] % (2 * h)) < h)).astype(np.float32))
    masks.append(np.eye(n, dtype=np.float32))
    return jnp.asarray(cmat, BF16), jnp.asarray(np.stack(masks), F32)


def _vpu_level_exponent(a, h):
    n = a.shape[0]
    pieces = []
    for j in range(n // (2 * h)):
        b = j * 2 * h
        ref = a[b + h - 1:b + h, :]
        pieces.append(ref - a[b:b + h, :])
        pieces.append(a[b + h:b + 2 * h, :] - ref)
    return jnp.concatenate(pieces, axis=0)


def _forget_lower_bound(lbl):
    m = jnp.max(lbl, axis=0, keepdims=True)
    e = jnp.exp(lbl - m)
    return e[0:1, :] / jnp.sum(e, axis=0, keepdims=True)


def _projections(x, nw, w_in_ref):
    h = _rmsnorm(x, nw).astype(BF16)
    return _dot(h, w_in_ref[...])


def _gates(proj, lb):
    q = proj[:, 3 * D_CONV:3 * D_CONV + D_HGRN]
    ff = proj[:, 3 * D_CONV + D_HGRN:3 * D_CONV + 2 * D_HGRN]
    vi = proj[:, 3 * D_CONV + 2 * D_HGRN:3 * D_CONV + 3 * D_HGRN]
    g = proj[:, 3 * D_CONV + 3 * D_HGRN:]
    e = jnp.exp(-jnp.abs(ff))
    r = 1.0 / (1.0 + e)
    pos = ff >= 0
    sig = jnp.where(pos, r, e * r)
    sig_neg = jnp.where(pos, e * r, r)
    logf = jnp.log(lb + (1.0 - lb) * sig)
    kk = (1.0 - lb) * sig_neg
    return q, logf, kk, vi, g


def _intra_scores_times_v(q, kk, vi, exps, masks_ref):
    n = q.shape[0]
    sc = [jnp.zeros((n, n), F32) for _ in range(N_HEADS)]
    for l, ex in enumerate(exps):
        if ex is None:
            qh, kh = q.astype(BF16), kk.astype(BF16)
        else:
            w = jnp.exp(ex)
            qh, kh = (q * w).astype(BF16), (kk * w).astype(BF16)
        mask = masks_ref[l]
        for hd in range(N_HEADS):
            hs = slice(hd * D_HEAD, (hd + 1) * D_HEAD)
            sc[hd] = sc[hd] + mask * _dot_nt(qh[:, hs], kh[:, hs])
    vb = vi.astype(BF16)
    return [_dot(sc[hd].astype(BF16), vb[:, hd * D_HEAD:(hd + 1) * D_HEAD]) for hd in range(N_HEADS)]


def _head_out(o, g, gnw):
    parts = []
    for hd in range(N_HEADS):
        oh = o[:, hd * D_HEAD:(hd + 1) * D_HEAD]
        parts.append(oh * lax.rsqrt(jnp.mean(oh * oh, axis=-1, keepdims=True) + EPS))
    on = jnp.concatenate(parts, axis=1)
    return on * gnw * (g * jax.nn.sigmoid(g))


def _pad_rows_bf16(a):
    return jnp.concatenate([a, jnp.zeros_like(a)], axis=0).astype(BF16)


PROMPT_VPU_LEVELS = (32, 16, 8)
MXU_LEVELS = (4, 2, 1)


def _mix_prompt_kernel(n_r, n_steps, *refs):
    step = pl.program_id(0)

    @pl.when(step < n_steps)
    def _():
        _mix_prompt_step(step % n_r, n_r, *refs)

    @pl.when(step >= n_steps)
    def _():
        x1_ref = refs[10]
        x1_ref[...] = jnp.zeros_like(x1_ref)


def _mix_prompt_step(r, n_r, x_ref, lbl_ref, nw_ref, win_ref, cw_ref, cb_ref, gnw_ref, wout_ref, cmat_ref,
                     masks_ref, x1_ref, nconv_ref, nstate_ref, st_ref, tail_ref, o_ref):
    rows = x_ref.shape[0]

    @pl.when(r == 0)
    def _():
        st_ref[...] = jnp.zeros_like(st_ref)
        tail_ref[...] = jnp.zeros_like(tail_ref)

    x = x_ref[...]
    proj = _projections(x, nw_ref[...], win_ref)
    lb = _forget_lower_bound(lbl_ref[...])

    bg = proj[:, 0:D_CONV]
    u = proj[:, D_CONV:2 * D_CONV] * proj[:, 2 * D_CONV:3 * D_CONV]
    rid = lax.broadcasted_iota(jnp.int32, (rows, 1), 0)
    t0 = tail_ref[SUBLANES - 2:SUBLANES - 1, :]
    t1 = tail_ref[SUBLANES - 1:SUBLANES, :]
    u1 = jnp.where(rid == 0, t1, pltpu.roll(u, 1, 0))
    u2 = jnp.where(rid == 0, t0, jnp.where(rid == 1, t1, pltpu.roll(u, 2, 0)))
    cw = cw_ref[...]
    y_conv = bg * (cw[0:1, :] * u2 + cw[1:2, :] * u1 + cw[2:3, :] * u + cb_ref[...])
    tail_ref[...] = u[rows - SUBLANES:rows, :]

    q, logf, kk, vi, g = _gates(proj, lb)
    lf_hi, lf_lo = _split2(logf)
    n_chunks = MIX_SLAB // PROMPT_CHUNK
    for s in range(rows // MIX_SLAB):
        sl = slice(s * MIX_SLAB, (s + 1) * MIX_SLAB)
        qs, ks, vs = q[sl], kk[sl], vi[sl]
        ex = _dot(cmat_ref[...], lf_hi[sl]) + _dot(cmat_ref[...], lf_lo[sl])
        a_pre = ex[0:MIX_SLAB]
        a_suf = ex[MIX_SLAB:2 * MIX_SLAB]
        exps = [_vpu_level_exponent(a_pre, h) for h in PROMPT_VPU_LEVELS]
        exps += [ex[(2 + i) * MIX_SLAB:(3 + i) * MIX_SLAB] for i in range(len(MXU_LEVELS))]
        exps.append(None)
        intra = _intra_scores_times_v(qs, ks, vs, exps, masks_ref)
        ea = jnp.exp(a_pre)
        qa = (qs * ea).astype(BF16)
        kb = (ks * jnp.exp(a_suf)).astype(BF16)
        vb = vs.astype(BF16)
        for c in range(n_chunks):
            cr = slice(c * PROMPT_CHUNK, (c + 1) * PROMPT_CHUNK)
            last = (c + 1) * PROMPT_CHUNK - 1
            for hd in range(N_HEADS):
                hs = slice(hd * D_HEAD, (hd + 1) * D_HEAD)
                st = st_ref[hd]
                inter = _dot_nt(qa[cr, hs], st.astype(BF16))
                o_ref[s * MIX_SLAB + c * PROMPT_CHUNK:s * MIX_SLAB + (c + 1) * PROMPT_CHUNK, hs] = (
                    inter + intra[hd][cr, :])
                st_ref[hd] = st * ea[last:last + 1, hs] + _dot_tn(vb[cr, hs], kb[cr, hs])

    o = _head_out(o_ref[...], g, gnw_ref[...])
    mix_in = jnp.concatenate([y_conv, o], axis=1).astype(BF16)
    x1_ref[...] = x + _dot(mix_in, wout_ref[...])

    @pl.when(r == n_r - 1)
    def _():
        nconv_ref[...] = u[rows - SUBLANES:rows, :]
        for hd in range(N_HEADS):
            nstate_ref[hd] = st_ref[hd].T


def _mix_prompt(x, lbl, nw, w_in, cw, cb, gnw, w_out, total_rows):
    b, l, _ = x.shape
    assert l % MIX_ROWS == 0 and MIX_ROWS % MIX_SLAB == 0 and l % PROMPT_CHUNK == 0
    assert total_rows % MIX_ROWS == 0
    n_r = l // MIX_ROWS
    n_steps = b * n_r
    seq = lambda s: jnp.minimum(s, n_steps - 1) // n_r
    cmat, masks = _mix_constants(MIX_SLAB, PROMPT_CHUNK, MXU_LEVELS, PROMPT_VPU_LEVELS + MXU_LEVELS)
    const = lambda shape: pl.BlockSpec(shape, lambda s: (0,) * len(shape))
    return pl.pallas_call(
        functools.partial(_mix_prompt_kernel, n_r, n_steps),
        grid=(total_rows // MIX_ROWS,),
        in_specs=[
            pl.BlockSpec((None, MIX_ROWS, D_MODEL), lambda s: (seq(s), jnp.minimum(s, n_steps - 1) % n_r, 0)),
            const((2, D_HGRN)), const((1, D_MODEL)), const((D_MODEL, D_PROJ)), const((3, D_CONV)),
            const((1, D_CONV)), const((1, D_HGRN)), const((D_MODEL, D_MODEL)), const(cmat.shape),
            const(masks.shape),
        ],
        out_specs=[
            pl.BlockSpec((MIX_ROWS, D_MODEL), lambda s: (s, 0)),
            pl.BlockSpec((None, SUBLANES, D_CONV), lambda s: (seq(s), 0, 0)),
            pl.BlockSpec((None, N_HEADS, D_HEAD, D_HEAD), lambda s: (seq(s), 0, 0, 0)),
        ],
        out_shape=[
            jax.ShapeDtypeStruct((total_rows, D_MODEL), F32),
            jax.ShapeDtypeStruct((b, SUBLANES, D_CONV), F32),
            jax.ShapeDtypeStruct((b, N_HEADS, D_HEAD, D_HEAD), F32),
        ],
        scratch_shapes=[
            pltpu.VMEM((N_HEADS, D_HEAD, D_HEAD), F32),
            pltpu.VMEM((SUBLANES, D_CONV), F32),
            pltpu.VMEM((MIX_ROWS, D_HGRN), F32),
        ],
        compiler_params=pltpu.CompilerParams(
            dimension_semantics=("arbitrary",), vmem_limit_bytes=VMEM_LIMIT_BYTES),
        name="mix_prompt",
    )(x, lbl, nw, w_in, cw, cb, gnw, w_out, cmat, masks)


def _mix_sample_kernel(x_ref, cs_ref, hs_ref, lbl_ref, nw_ref, win_ref, cw_ref, cb_ref, gnw_ref, wout_ref,
                       cmat_ref, masks_ref, x1_in_ref, x1_ref, nconv_ref, nstate_ref, o_ref):
    del x1_in_ref
    nseq, length, _ = nconv_ref.shape
    rows = nseq * length
    x = x_ref[...]
    proj = _projections(x, nw_ref[...], win_ref)
    lb = _forget_lower_bound(lbl_ref[...])

    bg = proj[:, 0:D_CONV]
    u = proj[:, D_CONV:2 * D_CONV] * proj[:, 2 * D_CONV:3 * D_CONV]
    cs = cs_ref[...]
    expand = lambda a: jnp.broadcast_to(a, (nseq, length, D_CONV)).reshape(rows, D_CONV)
    t0 = expand(cs[:, 0:1, :])
    t1 = expand(cs[:, 1:2, :])
    pos = lax.broadcasted_iota(jnp.int32, (rows, 1), 0) % length
    u1 = jnp.where(pos == 0, t1, pltpu.roll(u, 1, 0))
    u2 = jnp.where(pos == 0, t0, jnp.where(pos == 1, t1, pltpu.roll(u, 2, 0)))
    cw = cw_ref[...]
    y_conv = bg * (cw[0:1, :] * u2 + cw[1:2, :] * u1 + cw[2:3, :] * u + cb_ref[...])
    nconv_ref[...] = u.reshape(nseq, length, D_CONV)

    q, logf, kk, vi, g = _gates(proj, lb)
    lf_hi, lf_lo = _split2(logf)
    ex = _dot(cmat_ref[...], lf_hi) + _dot(cmat_ref[...], lf_lo)
    a_pre = ex[0:rows]
    a_suf = ex[rows:2 * rows]
    exps = [ex[(2 + i) * rows:(3 + i) * rows] for i in range(len(MXU_LEVELS))] + [None]
    intra = _intra_scores_times_v(q, kk, vi, exps, masks_ref)
    ea = jnp.exp(a_pre)
    qa = q * ea
    kb = kk * jnp.exp(a_suf)
    for s in range(nseq):
        cr = slice(s * length, (s + 1) * length)
        last = (s + 1) * length - 1
        for hd in range(N_HEADS):
            hs = slice(hd * D_HEAD, (hd + 1) * D_HEAD)
            st = hs_ref[s, hd].T
            inter = _dot_nt(_pad_rows_bf16(qa[cr, hs]), st.astype(BF16))[0:length, :]
            o_ref[cr, hs] = inter + intra[hd][cr, :]
            st_new = st * ea[last:last + 1, hs] + _dot_tn(_pad_rows_bf16(vi[cr, hs]), _pad_rows_bf16(kb[cr, hs]))
            nstate_ref[s, hd] = st_new.T

    o = _head_out(o_ref[...], g, gnw_ref[...])
    mix_in = jnp.concatenate([y_conv, o], axis=1).astype(BF16)
    x1_ref[...] = x + _dot(mix_in, wout_ref[...])


def _mix_sample(x, conv_state, hgrn_state, lbl, nw, w_in, cw, cb, gnw, w_out, x1_flat, row_offset):
    nb, length, _ = x.shape
    rows = MIX_SEQS * length
    assert nb % MIX_SEQS == 0 and length == SUBLANES and row_offset % rows == 0
    first_block = row_offset // rows
    cmat, masks = _mix_constants(rows, length, MXU_LEVELS, MXU_LEVELS)
    const = lambda shape: pl.BlockSpec(shape, lambda i: (0,) * len(shape))
    return pl.pallas_call(
        _mix_sample_kernel,
        grid=(nb // MIX_SEQS,),
        in_specs=[
            pl.BlockSpec((rows, D_MODEL), lambda i: (i, 0)),
            pl.BlockSpec((MIX_SEQS, 2, D_CONV), lambda i: (i, 0, 0)),
            pl.BlockSpec((MIX_SEQS, N_HEADS, D_HEAD, D_HEAD), lambda i: (i, 0, 0, 0)),
            const((2, D_HGRN)), const((1, D_MODEL)), const((D_MODEL, D_PROJ)), const((3, D_CONV)),
            const((1, D_CONV)), const((1, D_HGRN)), const((D_MODEL, D_MODEL)), const(cmat.shape),
            const(masks.shape), pl.BlockSpec(memory_space=pl.ANY),
        ],
        out_specs=[
            pl.BlockSpec((rows, D_MODEL), lambda i: (first_block + i, 0)),
            pl.BlockSpec((MIX_SEQS, length, D_CONV), lambda i: (i, 0, 0)),
            pl.BlockSpec((MIX_SEQS, N_HEADS, D_HEAD, D_HEAD), lambda i: (i, 0, 0, 0)),
        ],
        out_shape=[
            jax.ShapeDtypeStruct(x1_flat.shape, F32),
            jax.ShapeDtypeStruct((nb, length, D_CONV), F32),
            jax.ShapeDtypeStruct((nb, N_HEADS, D_HEAD, D_HEAD), F32),
        ],
        scratch_shapes=[pltpu.VMEM((rows, D_HGRN), F32)],
        input_output_aliases={12: 0},
        compiler_params=pltpu.CompilerParams(
            dimension_semantics=("arbitrary",), vmem_limit_bytes=VMEM_LIMIT_BYTES),
        name="mix_sample",
    )(x.reshape(nb * length, D_MODEL), conv_state, hgrn_state, lbl, nw, w_in, cw, cb, gnw, w_out, cmat, masks,
      x1_flat)


def _router_kernel(x1_ref, nw_ref, rw_hi_ref, rw_lo_ref, rb_ref, ltri_ref, utri_ref, xn_ref, meta_ref, cnt_ref):
    n = x1_ref.shape[0]
    xn = _rmsnorm(x1_ref[...], nw_ref[...])
    xn_ref[...] = xn.astype(BF16)
    x_hi, x_lo = _split2(xn)
    logits = (_dot(x_hi, rw_hi_ref[...]) + _dot(x_lo, rw_hi_ref[...]) + _dot(x_hi, rw_lo_ref[...])
              + rb_ref[...])
    lane = lax.broadcasted_iota(jnp.int32, (n, LANES), 1).astype(F32)
    work = logits
    vals, ids = [], []
    for _ in range(TOP_K):
        m = jnp.max(work, axis=-1, keepdims=True)
        i = jnp.min(jnp.where(work == m, lane, float(LANES)), axis=-1, keepdims=True)
        vals.append(m)
        ids.append(i)
        work = jnp.where(lane == i, -jnp.inf, work)
    es = [jnp.exp(v - vals[0]) for v in vals]
    den = es[0] + es[1] + es[2] + es[3]
    gates = [e / den for e in es]

    onehots = [(lane == i) for i in ids]
    multi = jnp.zeros((n, LANES), F32)
    for oh in onehots:
        multi = multi + oh.astype(F32)
    counts = jnp.sum(multi, axis=0, keepdims=True)
    before = _dot(ltri_ref[...], multi.astype(BF16))
    seg = jnp.ceil(counts * (1.0 / SEG_ALIGN)) * SEG_ALIGN
    seg_rows = jnp.broadcast_to(seg, (BF16_ROWS, LANES)).astype(BF16)
    seg_off = _dot(seg_rows, utri_ref[...])[0:1, :]
    slot_of = seg_off + before
    meta = jnp.zeros((n, LANES), F32)
    for k in range(TOP_K):
        slot = jnp.sum(jnp.where(onehots[k], slot_of, 0.0), axis=-1, keepdims=True)
        meta = jnp.where(lane == k, ids[k], meta)
        meta = jnp.where(lane == TOP_K + k, gates[k], meta)
        meta = jnp.where(lane == 2 * TOP_K + k, slot, meta)
    meta_ref[...] = meta
    cnt_ref[...] = jnp.broadcast_to(counts, (SUBLANES, LANES))


def _router(x1, nw, router_w, router_b):
    t = x1.shape[0]
    n_tiles = t // TOKEN_TILE
    rw = jnp.zeros((D_MODEL, LANES), F32).at[:, :N_EXPERTS].set(router_w)
    rw_hi, rw_lo = _split2(rw)
    rb = jnp.full((1, LANES), -1e30, F32).at[0, :N_EXPERTS].set(router_b)
    idx = np.arange(TOKEN_TILE)
    ltri = jnp.asarray(idx[None, :] < idx[:, None], BF16)
    lid = np.arange(LANES)
    utri = jnp.asarray(lid[:, None] < lid[None, :], BF16)
    const = lambda shape: pl.BlockSpec(shape, lambda i: (0,) * len(shape))
    return pl.pallas_call(
        _router_kernel,
        grid=(n_tiles,),
        in_specs=[
            pl.BlockSpec((TOKEN_TILE, D_MODEL), lambda i: (i, 0)),
            const((1, D_MODEL)), const((D_MODEL, LANES)), const((D_MODEL, LANES)), const((1, LANES)),
            const((TOKEN_TILE, TOKEN_TILE)), const((LANES, LANES)),
        ],
        out_specs=[
            pl.BlockSpec((TOKEN_TILE, D_MODEL), lambda i: (i, 0)),
            pl.BlockSpec((TOKEN_TILE, LANES), lambda i: (i, 0)),
            pl.BlockSpec((None, SUBLANES, LANES), lambda i: (i, 0, 0)),
        ],
        out_shape=[
            jax.ShapeDtypeStruct((t, D_MODEL), BF16),
            jax.ShapeDtypeStruct((t, LANES), F32),
            jax.ShapeDtypeStruct((n_tiles, SUBLANES, LANES), F32),
        ],
        compiler_params=pltpu.CompilerParams(
            dimension_semantics=("arbitrary",), vmem_limit_bytes=VMEM_LIMIT_BYTES),
        name="router",
    )(x1, nw, rw_hi, rw_lo, rb, ltri, utri)


def _segment_tables(cnt):
    seg = (cnt + SEG_ALIGN - 1) // SEG_ALIGN * SEG_ALIGN
    src = jnp.cumsum(seg, axis=1) - seg
    tot = jnp.sum(seg, axis=0)
    cap = (tot + ROW_TILE - 1) // ROW_TILE * ROW_TILE
    base = jnp.cumsum(cap) - cap
    dst = base[None, :] + jnp.cumsum(seg, axis=0) - seg
    n_used = jnp.sum(cap) // ROW_TILE
    fill = base + tot
    lane = jnp.arange(N_EXPERTS)
    counts, srcs, dsts = [], [], []
    for size in SEG_SIZES:
        has = (seg & size) != 0
        done = seg & ~(2 * size - 1)
        place = has[:, :, None] & ((jnp.cumsum(has, axis=1) - 1)[:, :, None] == lane[None, None, :])
        counts.append(jnp.sum(has, axis=1))
        srcs.append(jnp.sum(jnp.where(place, (src + done)[:, :, None], 0), axis=1))
        dsts.append(jnp.sum(jnp.where(place, (dst + done)[:, :, None], 0), axis=1))
    i32 = lambda a: a.astype(jnp.int32).reshape(-1)
    chunks = (i32(jnp.stack(counts, axis=1)), i32(jnp.stack(srcs, axis=1)), i32(jnp.stack(dsts, axis=1)))
    return chunks, i32(fill), i32(cap - tot), i32(base), i32(cap // ROW_TILE), i32(n_used)


def _for_each_chunk(n, fn):
    for size in SEG_SIZES:
        done = n & ~(2 * size - 1)

        @pl.when((n & size) != 0)
        def _():
            fn(done, size)


def _segment_copies(cnt_ref, src_ref, dst_ref, tile, make_copy, start):
    for si, size in enumerate(SEG_SIZES):
        first = (tile * len(SEG_SIZES) + si) * N_EXPERTS

        def one(p, carry, si=si, size=size, first=first):
            copy = make_copy(pl.multiple_of(src_ref[first + p], SEG_ALIGN),
                             pl.multiple_of(dst_ref[first + p], SEG_ALIGN), size)
            if start:
                copy.start(priority=si % 2)
            else:
                copy.wait()
            return carry
        lax.fori_loop(0, cnt_ref[tile * len(SEG_SIZES) + si], one, 0)


SORT_BLOCK = 512


def _slot_matrix(slots, block, gates):
    row = (lax.broadcasted_iota(jnp.int32, (SORT_BLOCK, TOKEN_TILE), 0) + block * SORT_BLOCK).astype(F32)
    m = jnp.zeros((SORT_BLOCK, TOKEN_TILE), F32)
    for k in range(TOP_K):
        m = jnp.where(row == slots[k:k + 1, :], 1.0 if gates is None else gates[k:k + 1, :], m)
    return m.astype(BF16)


def _dispatch_kernel(seg_ref, src_ref, dst_ref, fill_ref, gap_ref, nu_ref, slot_ref, xn_ref, xs_ref, sorted_ref,
                     zero_ref, sem):
    tile = pl.program_id(0)

    @pl.when(tile == 0)
    def _():
        zero_ref[...] = jnp.zeros_like(zero_ref)

        def fill_copy(row, size):
            return pltpu.make_async_copy(zero_ref.at[pl.ds(0, size)],
                                         xs_ref.at[pl.ds(pl.multiple_of(row, SEG_ALIGN), size)], sem)

        def gaps(act):
            def per_expert(e, carry):
                _for_each_chunk(gap_ref[e], lambda done, size: act(fill_copy(fill_ref[e] + done, size)))
                return carry
            lax.fori_loop(0, N_EXPERTS, per_expert, 0)

        def tail(act):
            n_tail = xs_ref.shape[0] // ROW_TILE - nu_ref[0]
            lax.fori_loop(0, n_tail, lambda i, c: (act(fill_copy((nu_ref[0] + i) * ROW_TILE, ROW_TILE)), c)[1], 0)

        gaps(lambda c: c.start())
        tail(lambda c: c.start())
        gaps(lambda c: c.wait())
        tail(lambda c: c.wait())

    slots = slot_ref[...]
    xn = xn_ref[...]
    for rb in range(TILE_CAP // SORT_BLOCK):
        sorted_ref[rb * SORT_BLOCK:(rb + 1) * SORT_BLOCK, :] = _dot(
            _slot_matrix(slots, rb, None), xn).astype(BF16)

    def make_copy(src, dst, size):
        return pltpu.make_async_copy(sorted_ref.at[pl.ds(src, size)], xs_ref.at[pl.ds(dst, size)], sem)

    _segment_copies(seg_ref, src_ref, dst_ref, tile, make_copy, start=True)
    _segment_copies(seg_ref, src_ref, dst_ref, tile, make_copy, start=False)


def _dispatch(tables, slot_t, xn, n_rows_total):
    t = xn.shape[0]
    n_tiles = t // TOKEN_TILE
    return pl.pallas_call(
        _dispatch_kernel,
        grid_spec=pltpu.PrefetchScalarGridSpec(
            num_scalar_prefetch=len(tables),
            grid=(n_tiles,),
            in_specs=[
                pl.BlockSpec((SUBLANES, TOKEN_TILE), lambda i, *_: (0, i)),
                pl.BlockSpec((TOKEN_TILE, D_MODEL), lambda i, *_: (i, 0)),
            ],
            out_specs=pl.BlockSpec(memory_space=pl.ANY),
            scratch_shapes=[
                pltpu.VMEM((TILE_CAP, D_MODEL), BF16),
                pltpu.VMEM((ROW_TILE, D_MODEL), BF16),
                pltpu.SemaphoreType.DMA(()),
            ],
        ),
        out_shape=jax.ShapeDtypeStruct((n_rows_total, D_MODEL), BF16),
        compiler_params=pltpu.CompilerParams(
            dimension_semantics=("arbitrary",), vmem_limit_bytes=VMEM_LIMIT_BYTES),
        name="dispatch",
    )(*tables, slot_t, xn)


X_SLOTS = 3
W_FETCH_STEPS = (0, 1, 3)


def _experts_kernel(base_ref, nt_ref, nu_ref, bg_ref, bu_ref, bd_ref, wg_hbm, wu_hbm, wd_hbm, xs_ref, zs_ref,
                    w_f32, wg_bf, wu_bf, wd_bf, xbuf, hbuf, zbuf, w_sem, in_sem, out_sem):
    e = pl.program_id(0)
    n_e = pl.num_programs(0)
    n = nt_ref[e]
    base = base_ref[e]
    w_slot = e % 2
    has_next = e + 1 < n_e

    def w_copy(j, expert, slot):
        w_hbm = (wg_hbm, wu_hbm, wd_hbm)[j]
        return pltpu.make_async_copy(w_hbm.at[expert], w_f32.at[slot, j], w_sem.at[slot])

    def x_copy(i):
        rows = pl.ds(pl.multiple_of(base + i * ROW_TILE, ROW_TILE), ROW_TILE)
        return pltpu.make_async_copy(xs_ref.at[rows], xbuf.at[i % X_SLOTS], in_sem.at[i % X_SLOTS])

    def z_copy(row, slot):
        rows = pl.ds(pl.multiple_of(row, ROW_TILE), ROW_TILE)
        return pltpu.make_async_copy(zbuf.at[slot], zs_ref.at[rows], out_sem.at[slot])

    def hidden(i):
        x = xbuf[i % X_SLOTS]
        a = jnp.minimum(_dot(x, wg_bf[...]) + bg_ref[...], SWIGLU_LIMIT)
        u = jnp.clip(_dot(x, wu_bf[...]) + bu_ref[...], -SWIGLU_LIMIT, SWIGLU_LIMIT)
        hbuf[i % 2] = ((u + 1.0) * a * jax.nn.sigmoid(SWIGLU_ALPHA * a)).astype(BF16)

    def project_down(i, slot):
        @pl.when(i >= 2)
        def _():
            z_copy(base, slot).wait()
        zbuf[slot] = (_dot(hbuf[slot], wd_bf[...]) + bd_ref[...]).astype(BF16)

    @pl.when(e == 0)
    def _():
        for j in range(3):
            w_copy(j, 0, 0).start()

    for i in range(X_SLOTS):
        @pl.when(i < n)
        def _():
            x_copy(i).start(priority=1)

    def fetch_next_weights(j):
        @pl.when(has_next)
        def _():
            w_copy(j, e + 1, 1 - w_slot).start()

    fetch_next_weights(0)
    for j in range(3):
        w_copy(j, e, w_slot).wait()

    @pl.when(n > 0)
    def _():
        wg_bf[...] = w_f32[w_slot, 0].astype(BF16)
        wu_bf[...] = w_f32[w_slot, 1].astype(BF16)
        wd_bf[...] = w_f32[w_slot, 2].astype(BF16)
        x_copy(0).wait()
        hidden(0)

        def step(i, carry):
            x_copy(i + 1).wait()

            @pl.when(i + X_SLOTS < n)
            def _():
                x_copy(i + X_SLOTS).start(priority=1)

            for j in (1, 2):
                @pl.when(i == W_FETCH_STEPS[j])
                def _():
                    fetch_next_weights(j)

            project_down(i, i % 2)
            hidden(i + 1)
            z_copy(base + i * ROW_TILE, i % 2).start()
            return carry

        lax.fori_loop(0, n - 1, step, 0)
        project_down(n - 1, (n - 1) % 2)
        z_copy(base + (n - 1) * ROW_TILE, (n - 1) % 2).start()

        @pl.when(n >= 2)
        def _():
            z_copy(base, n % 2).wait()
        z_copy(base, (n - 1) % 2).wait()

    for j in (1, 2):
        @pl.when(jnp.maximum(n - 1, 0) <= W_FETCH_STEPS[j])
        def _():
            fetch_next_weights(j)

    @pl.when(e == pl.num_programs(0) - 1)
    def _():
        zbuf[0] = jnp.zeros((ROW_TILE, D_MODEL), BF16)
        n_tail = zs_ref.shape[0] // ROW_TILE - nu_ref[0]
        tail = lambda act: lax.fori_loop(
            0, n_tail, lambda i, c: (act(z_copy((nu_ref[0] + i) * ROW_TILE, 0)), c)[1], 0)
        tail(lambda c: c.start())
        tail(lambda c: c.wait())


def _experts(base, n_tiles, n_used, xs, w_gate, b_gate, w_up, b_up, w_down, b_down):
    b_spec = pl.BlockSpec((None, 1, D_MODEL), lambda e, *_: (e, 0, 0))
    any_spec = pl.BlockSpec(memory_space=pl.ANY)
    return pl.pallas_call(
        _experts_kernel,
        grid_spec=pltpu.PrefetchScalarGridSpec(
            num_scalar_prefetch=3,
            grid=(N_EXPERTS,),
            in_specs=[b_spec, b_spec, b_spec, any_spec, any_spec, any_spec, any_spec],
            out_specs=any_spec,
            scratch_shapes=[pltpu.VMEM((2, 3, D_MODEL, D_MODEL), F32)] + [
                pltpu.VMEM((D_MODEL, D_MODEL), BF16) for _ in range(3)] + [
                pltpu.VMEM((X_SLOTS, ROW_TILE, D_MODEL), BF16)] + [
                pltpu.VMEM((2, ROW_TILE, D_MODEL), BF16) for _ in range(2)] + [
                pltpu.SemaphoreType.DMA((2,)), pltpu.SemaphoreType.DMA((X_SLOTS,)), pltpu.SemaphoreType.DMA((2,))],
        ),
        out_shape=jax.ShapeDtypeStruct(xs.shape, BF16),
        compiler_params=pltpu.CompilerParams(
            dimension_semantics=("arbitrary",), vmem_limit_bytes=VMEM_LIMIT_BYTES),
        name="experts",
    )(base, n_tiles, n_used, b_gate[:, None, :], b_up[:, None, :], b_down[:, None, :], w_gate, w_up, w_down, xs)


def _combine_kernel(n_first, seg_ref, src_ref, dst_ref, route_ref, x1_ref, fw_ref, zs_ref, ya_ref, yb_ref,
                    sorted_ref, sem):
    tile = pl.program_id(0)
    sorted_ref[...] = jnp.zeros_like(sorted_ref)

    def make_copy(src, dst, size):
        return pltpu.make_async_copy(zs_ref.at[pl.ds(dst, size)], sorted_ref.at[pl.ds(src, size)], sem)

    _segment_copies(seg_ref, src_ref, dst_ref, tile, make_copy, start=True)
    route = route_ref[...]
    slots, gates = route[0:TOP_K, :], route[TOP_K:2 * TOP_K, :]
    n_blocks = TILE_CAP // SORT_BLOCK
    early = [_slot_matrix(slots, cb, gates) for cb in range(2)]
    _segment_copies(seg_ref, src_ref, dst_ref, tile, make_copy, start=False)
    moe = jnp.zeros((TOKEN_TILE, D_MODEL), F32)
    for cb in range(n_blocks):
        m = early[cb] if cb < len(early) else _slot_matrix(slots, cb, gates)
        moe = moe + _dot_tn(m, sorted_ref[cb * SORT_BLOCK:(cb + 1) * SORT_BLOCK, :])
    y = _rmsnorm(x1_ref[...] + moe, fw_ref[...])

    @pl.when(tile < n_first)
    def _():
        ya_ref[...] = y

    @pl.when(tile >= n_first)
    def _():
        yb_ref[...] = y


def _combine(tables, route_t, x1, final_w, zs, rows_first):
    seg, src, dst = tables
    t = x1.shape[0]
    assert rows_first % TOKEN_TILE == 0 and 0 < rows_first < t and TILE_CAP % SORT_BLOCK == 0
    n_tiles = t // TOKEN_TILE
    n_first = rows_first // TOKEN_TILE
    return pl.pallas_call(
        functools.partial(_combine_kernel, n_first),
        grid_spec=pltpu.PrefetchScalarGridSpec(
            num_scalar_prefetch=3,
            grid=(n_tiles,),
            in_specs=[
                pl.BlockSpec((SUBLANES, TOKEN_TILE), lambda i, *_: (0, i)),
                pl.BlockSpec((TOKEN_TILE, D_MODEL), lambda i, *_: (i, 0)),
                pl.BlockSpec((1, D_MODEL), lambda i, *_: (0, 0)),
                pl.BlockSpec(memory_space=pl.ANY),
            ],
            out_specs=[
                pl.BlockSpec((TOKEN_TILE, D_MODEL), lambda i, *_: (jnp.minimum(i, n_first - 1), 0)),
                pl.BlockSpec((TOKEN_TILE, D_MODEL), lambda i, *_: (jnp.maximum(i - n_first, 0), 0)),
            ],
            scratch_shapes=[pltpu.VMEM((TILE_CAP, D_MODEL), BF16), pltpu.SemaphoreType.DMA(())],
        ),
        out_shape=[jax.ShapeDtypeStruct((rows_first, D_MODEL), F32),
                   jax.ShapeDtypeStruct((t - rows_first, D_MODEL), F32)],
        compiler_params=pltpu.CompilerParams(
            dimension_semantics=("arbitrary",), vmem_limit_bytes=VMEM_LIMIT_BYTES),
        name="combine",
    )(seg, src, dst, route_t, x1, final_w, zs)


def _moe_and_final_norm(x1, rows_first, norm_ffn_w, router_w, router_b, w_gate, b_gate, w_up, b_up, w_down, b_down,
                        final_w):
    t = x1.shape[0]
    assert t % TOKEN_TILE == 0
    n_tiles = t // TOKEN_TILE
    n_rows_total = TOP_K * t + n_tiles * N_EXPERTS * (SEG_ALIGN - 1) + N_EXPERTS * (ROW_TILE - 1)
    n_rows_total = (n_rows_total + ROW_TILE - 1) // ROW_TILE * ROW_TILE
    xn, meta, cnt = _router(x1, norm_ffn_w.reshape(1, D_MODEL), router_w, router_b)
    cnt = cnt[:, 0, :N_EXPERTS].astype(jnp.int32)
    chunks, fill, gap, base, n_row_tiles, n_used = _segment_tables(cnt)
    route_t = jnp.concatenate([meta[:, 2 * TOP_K:3 * TOP_K], meta[:, TOP_K:2 * TOP_K]], axis=1).T
    xs = _dispatch((*chunks, fill, gap, n_used), route_t, xn, n_rows_total)
    zs = _experts(base, n_row_tiles, n_used, xs, w_gate, b_gate, w_up, b_up, w_down, b_down)
    return _combine(chunks, route_t, x1, final_w.reshape(1, D_MODEL), zs, rows_first)


def kernel(x_prompt, x_sample, state_conv, state_hgrn, lb_logits, norm_mix_w, w_in, conv_w, conv_b, gnorm_w,
           w_out, norm_ffn_w, router_w, router_b, w_gate, b_gate, w_up, b_up, w_down, b_down, final_norm_w):
    assert norm_mix_w.shape[0] == 1 and lb_logits.shape[0] == 2, "single-layer step"
    b, l, _ = x_prompt.shape
    nb, ls, _ = x_sample.shape
    lbl = lb_logits.astype(F32)
    nw = norm_mix_w[0].reshape(1, D_MODEL)
    w_in_bf = w_in[0].astype(BF16)
    w_out_bf = w_out[0].astype(BF16)
    cw, cb = conv_w[0], conv_b[0].reshape(1, D_CONV)
    gnw = gnorm_w[0].reshape(1, D_HGRN)
    rows_p, rows_s = b * l, nb * ls
    x1, conv_p, hgrn_p = _mix_prompt(x_prompt, lbl, nw, w_in_bf, cw, cb, gnw, w_out_bf, rows_p + rows_s)
    x1, conv_s, hgrn_s = _mix_sample(x_sample, state_conv[0], state_hgrn[0], lbl, nw, w_in_bf, cw, cb, gnw,
                                     w_out_bf, x1, rows_p)
    y_p, y_s = _moe_and_final_norm(x1, rows_p, norm_ffn_w[0], router_w[0], router_b[0], w_gate[0], b_gate[0],
                                   w_up[0], b_up[0], w_down[0], b_down[0], final_norm_w)
    conv_p = conv_p[:, SUBLANES - 2:, :]
    conv_s = conv_s[:, ls - 2:, :]
    return (y_p.reshape(b, l, D_MODEL), y_s.reshape(nb, ls, D_MODEL), conv_p[None], hgrn_p[None], conv_s[None],
            hgrn_s[None])
```

```python
import functools

import numpy as np
import jax
import jax.numpy as jnp
from jax import lax
from jax.experimental import pallas as pl
from jax.experimental.pallas import tpu as pltpu

F32 = jnp.float32
BF16 = jnp.bfloat16

D_MODEL = 1024
D_CONV = 512
D_HGRN = 512
N_HEADS = 4
D_HEAD = 128
D_PROJ = 3 * D_CONV + 4 * D_HGRN
N_EXPERTS = 32
TOP_K = 4
SWIGLU_LIMIT = 7.0
SWIGLU_ALPHA = 1.702
EPS = 1e-6
PROMPT_CHUNK = 64

LANES = 128
SUBLANES = 8
BF16_ROWS = 16
VMEM_LIMIT_BYTES = 56 * 1024 * 1024

MIX_SLAB = 256
MIX_ROWS = 512
MIX_SEQS = 16
TOKEN_TILE = 512
SEG_ALIGN = BF16_ROWS
ROW_TILE = 256
SEG_SIZES = tuple(SEG_ALIGN << i for i in reversed(range(6)))
TILE_CAP = TOP_K * TOKEN_TILE + N_EXPERTS * SEG_ALIGN

NT_DIMS = (((1,), (1,)), ((), ()))
TN_DIMS = (((0,), (0,)), ((), ()))


def _dot(a, b):
    return jnp.dot(a, b, preferred_element_type=F32)


def _dot_nt(a, b):
    return lax.dot_general(a, b, NT_DIMS, preferred_element_type=F32)


def _dot_tn(a, b):
    return lax.dot_general(a, b, TN_DIMS, preferred_element_type=F32)


def _split2(x):
    hi = x.astype(BF16)
    lo = (x - hi.astype(F32)).astype(BF16)
    return hi, lo


def _rmsnorm(x, w):
    return x * lax.rsqrt(jnp.mean(x * x, axis=-1, keepdims=True) + EPS) * w


def _level_exponent_matrix(n, chunk, h):
    x = np.zeros((n, n), np.float32)
    for t in range(n):
        base = t - t % (2 * h)
        m = base + h - 1
        if t % (2 * h) >= h:
            x[t, m + 1:t + 1] = 1.0
        else:
            x[t, t + 1:m + 1] = 1.0
    return x


def _mix_constants(n, chunk, mxu_levels, all_levels):
    t = np.arange(n)
    same_chunk = (t[:, None] // chunk) == (t[None, :] // chunk)
    tri = (same_chunk & (t[None, :] <= t[:, None])).astype(np.float32)
    suf = (same_chunk & (t[None, :] > t[:, None])).astype(np.float32)
    cmat = np.concatenate([tri, suf] + [_level_exponent_matrix(n, chunk, h) for h in mxu_levels], axis=0)
    masks = []
    for h in all_levels:
        blk = (t[:, None] // (2 * h)) == (t[None, :] // (2 * h))
        masks.append((blk & ((t[:, None] % (2 * h)) >= h) & ((t[None, :] % (2 * h)) < h)).astype(np.float32))
    masks.append(np.eye(n, dtype=np.float32))
    return jnp.asarray(cmat, BF16), jnp.asarray(np.stack(masks), F32)


def _vpu_level_exponent(a, h):
    n = a.shape[0]
    pieces = []
    for j in range(n // (2 * h)):
        b = j * 2 * h
        ref = a[b + h - 1:b + h, :]
        pieces.append(ref - a[b:b + h, :])
        pieces.append(a[b + h:b + 2 * h, :] - ref)
    return jnp.concatenate(pieces, axis=0)


def _forget_lower_bound(lbl):
    m = jnp.max(lbl, axis=0, keepdims=True)
    e = jnp.exp(lbl - m)
    return e[0:1, :] / jnp.sum(e, axis=0, keepdims=True)


def _projections(x, nw, w_in_ref):
    h = _rmsnorm(x, nw).astype(BF16)
    return _dot(h, w_in_ref[...])


def _gates(proj, lb):
    q = proj[:, 3 * D_CONV:3 * D_CONV + D_HGRN]
    ff = proj[:, 3 * D_CONV + D_HGRN:3 * D_CONV + 2 * D_HGRN]
    vi = proj[:, 3 * D_CONV + 2 * D_HGRN:3 * D_CONV + 3 * D_HGRN]
    g = proj[:, 3 * D_CONV + 3 * D_HGRN:]
    e = jnp.exp(-jnp.abs(ff))
    r = 1.0 / (1.0 + e)
    pos = ff >= 0
    sig = jnp.where(pos, r, e * r)
    sig_neg = jnp.where(pos, e * r, r)
    logf = jnp.log(lb + (1.0 - lb) * sig)
    kk = (1.0 - lb) * sig_neg
    return q, logf, kk, vi, g


def _intra_scores_times_v(q, kk, vi, exps, masks_ref):
    n = q.shape[0]
    sc = [jnp.zeros((n, n), F32) for _ in range(N_HEADS)]
    for l, ex in enumerate(exps):
        if ex is None:
            qh, kh = q.astype(BF16), kk.astype(BF16)
        else:
            w = jnp.exp(ex)
            qh, kh = (q * w).astype(BF16), (kk * w).astype(BF16)
        mask = masks_ref[l]
        for hd in range(N_HEADS):
            hs = slice(hd * D_HEAD, (hd + 1) * D_HEAD)
            sc[hd] = sc[hd] + mask * _dot_nt(qh[:, hs], kh[:, hs])
    vb = vi.astype(BF16)
    return [_dot(sc[hd].astype(BF16), vb[:, hd * D_HEAD:(hd + 1) * D_HEAD]) for hd in range(N_HEADS)]


def _head_out(o, g, gnw):
    parts = []
    for hd in range(N_HEADS):
        oh = o[:, hd * D_HEAD:(hd + 1) * D_HEAD]
        parts.append(oh * lax.rsqrt(jnp.mean(oh * oh, axis=-1, keepdims=True) + EPS))
    on = jnp.concatenate(parts, axis=1)
    return on * gnw * (g * jax.nn.sigmoid(g))


def _pad_rows_bf16(a):
    return jnp.concatenate([a, jnp.zeros_like(a)], axis=0).astype(BF16)


PROMPT_VPU_LEVELS = (32, 16, 8)
MXU_LEVELS = (4, 2, 1)


def _mix_prompt_kernel(n_r, n_steps, *refs):
    step = pl.program_id(0)

    @pl.when(step < n_steps)
    def _():
        _mix_prompt_step(step % n_r, n_r, *refs)

    @pl.when(step >= n_steps)
    def _():
        x1_ref = refs[10]
        x1_ref[...] = jnp.zeros_like(x1_ref)


def _mix_prompt_step(r, n_r, x_ref, lbl_ref, nw_ref, win_ref, cw_ref, cb_ref, gnw_ref, wout_ref, cmat_ref,
                     masks_ref, x1_ref, nconv_ref, nstate_ref, st_ref, tail_ref, o_ref):
    rows = x_ref.shape[0]

    @pl.when(r == 0)
    def _():
        st_ref[...] = jnp.zeros_like(st_ref)
        tail_ref[...] = jnp.zeros_like(tail_ref)

    x = x_ref[...]
    proj = _projections(x, nw_ref[...], win_ref)
    lb = _forget_lower_bound(lbl_ref[...])

    bg = proj[:, 0:D_CONV]
    u = proj[:, D_CONV:2 * D_CONV] * proj[:, 2 * D_CONV:3 * D_CONV]
    rid = lax.broadcasted_iota(jnp.int32, (rows, 1), 0)
    t0 = tail_ref[SUBLANES - 2:SUBLANES - 1, :]
    t1 = tail_ref[SUBLANES - 1:SUBLANES, :]
    u1 = jnp.where(rid == 0, t1, pltpu.roll(u, 1, 0))
    u2 = jnp.where(rid == 0, t0, jnp.where(rid == 1, t1, pltpu.roll(u, 2, 0)))
    cw = cw_ref[...]
    y_conv = bg * (cw[0:1, :] * u2 + cw[1:2, :] * u1 + cw[2:3, :] * u + cb_ref[...])
    tail_ref[...] = u[rows - SUBLANES:rows, :]

    q, logf, kk, vi, g = _gates(proj, lb)
    lf_hi, lf_lo = _split2(logf)
    n_chunks = MIX_SLAB // PROMPT_CHUNK
    for s in range(rows // MIX_SLAB):
        sl = slice(s * MIX_SLAB, (s + 1) * MIX_SLAB)
        qs, ks, vs = q[sl], kk[sl], vi[sl]
        ex = _dot(cmat_ref[...], lf_hi[sl]) + _dot(cmat_ref[...], lf_lo[sl])
        a_pre = ex[0:MIX_SLAB]
        a_suf = ex[MIX_SLAB:2 * MIX_SLAB]
        exps = [_vpu_level_exponent(a_pre, h) for h in PROMPT_VPU_LEVELS]
        exps += [ex[(2 + i) * MIX_SLAB:(3 + i) * MIX_SLAB] for i in range(len(MXU_LEVELS))]
        exps.append(None)
        intra = _intra_scores_times_v(qs, ks, vs, exps, masks_ref)
        ea = jnp.exp(a_pre)
        qa = (qs * ea).astype(BF16)
        kb = (ks * jnp.exp(a_suf)).astype(BF16)
        vb = vs.astype(BF16)
        for c in range(n_chunks):
            cr = slice(c * PROMPT_CHUNK, (c + 1) * PROMPT_CHUNK)
            last = (c + 1) * PROMPT_CHUNK - 1
            for hd in range(N_HEADS):
                hs = slice(hd * D_HEAD, (hd + 1) * D_HEAD)
                st = st_ref[hd]
                inter = _dot_nt(qa[cr, hs], st.astype(BF16))
                o_ref[s * MIX_SLAB + c * PROMPT_CHUNK:s * MIX_SLAB + (c + 1) * PROMPT_CHUNK, hs] = (
                    inter + intra[hd][cr, :])
                st_ref[hd] = st * ea[last:last + 1, hs] + _dot_tn(vb[cr, hs], kb[cr, hs])

    o = _head_out(o_ref[...], g, gnw_ref[...])
    mix_in = jnp.concatenate([y_conv, o], axis=1).astype(BF16)
    x1_ref[...] = x + _dot(mix_in, wout_ref[...])

    @pl.when(r == n_r - 1)
    def _():
        nconv_ref[...] = u[rows - SUBLANES:rows, :]
        for hd in range(N_HEADS):
            nstate_ref[hd] = st_ref[hd].T


def _mix_prompt(x, lbl, nw, w_in, cw, cb, gnw, w_out, total_rows):
    b, l, _ = x.shape
    assert l % MIX_ROWS == 0 and MIX_ROWS % MIX_SLAB == 0 and l % PROMPT_CHUNK == 0
    assert total_rows % MIX_ROWS == 0
    n_r = l // MIX_ROWS
    n_steps = b * n_r
    seq = lambda s: jnp.minimum(s, n_steps - 1) // n_r
    cmat, masks = _mix_constants(MIX_SLAB, PROMPT_CHUNK, MXU_LEVELS, PROMPT_VPU_LEVELS + MXU_LEVELS)
    const = lambda shape: pl.BlockSpec(shape, lambda s: (0,) * len(shape))
    return pl.pallas_call(
        functools.partial(_mix_prompt_kernel, n_r, n_steps),
        grid=(total_rows // MIX_ROWS,),
        in_specs=[
            pl.BlockSpec((None, MIX_ROWS, D_MODEL), lambda s: (seq(s), jnp.minimum(s, n_steps - 1) % n_r, 0)),
            const((2, D_HGRN)), const((1, D_MODEL)), const((D_MODEL, D_PROJ)), const((3, D_CONV)),
            const((1, D_CONV)), const((1, D_HGRN)), const((D_MODEL, D_MODEL)), const(cmat.shape),
            const(masks.shape),
        ],
        out_specs=[
            pl.BlockSpec((MIX_ROWS, D_MODEL), lambda s: (s, 0)),
            pl.BlockSpec((None, SUBLANES, D_CONV), lambda s: (seq(s), 0, 0)),
            pl.BlockSpec((None, N_HEADS, D_HEAD, D_HEAD), lambda s: (seq(s), 0, 0, 0)),
        ],
        out_shape=[
            jax.ShapeDtypeStruct((total_rows, D_MODEL), F32),
            jax.ShapeDtypeStruct((b, SUBLANES, D_CONV), F32),
            jax.ShapeDtypeStruct((b, N_HEADS, D_HEAD, D_HEAD), F32),
        ],
        scratch_shapes=[
            pltpu.VMEM((N_HEADS, D_HEAD, D_HEAD), F32),
            pltpu.VMEM((SUBLANES, D_CONV), F32),
            pltpu.VMEM((MIX_ROWS, D_HGRN), F32),
        ],
        compiler_params=pltpu.CompilerParams(
            dimension_semantics=("arbitrary",), vmem_limit_bytes=VMEM_LIMIT_BYTES),
        name="mix_prompt",
    )(x, lbl, nw, w_in, cw, cb, gnw, w_out, cmat, masks)


def _mix_sample_kernel(x_ref, cs_ref, hs_ref, lbl_ref, nw_ref, win_ref, cw_ref, cb_ref, gnw_ref, wout_ref,
                       cmat_ref, masks_ref, x1_in_ref, x1_ref, nconv_ref, nstate_ref, o_ref):
    del x1_in_ref
    nseq, length, _ = nconv_ref.shape
    rows = nseq * length
    x = x_ref[...]
    proj = _projections(x, nw_ref[...], win_ref)
    lb = _forget_lower_bound(lbl_ref[...])

    bg = proj[:, 0:D_CONV]
    u = proj[:, D_CONV:2 * D_CONV] * proj[:, 2 * D_CONV:3 * D_CONV]
    cs = cs_ref[...]
    expand = lambda a: jnp.broadcast_to(a, (nseq, length, D_CONV)).reshape(rows, D_CONV)
    t0 = expand(cs[:, 0:1, :])
    t1 = expand(cs[:, 1:2, :])
    pos = lax.broadcasted_iota(jnp.int32, (rows, 1), 0) % length
    u1 = jnp.where(pos == 0, t1, pltpu.roll(u, 1, 0))
    u2 = jnp.where(pos == 0, t0, jnp.where(pos == 1, t1, pltpu.roll(u, 2, 0)))
    cw = cw_ref[...]
    y_conv = bg * (cw[0:1, :] * u2 + cw[1:2, :] * u1 + cw[2:3, :] * u + cb_ref[...])
    nconv_ref[...] = u.reshape(nseq, length, D_CONV)

    q, logf, kk, vi, g = _gates(proj, lb)
    lf_hi, lf_lo = _split2(logf)
    ex = _dot(cmat_ref[...], lf_hi) + _dot(cmat_ref[...], lf_lo)
    a_pre = ex[0:rows]
    a_suf = ex[rows:2 * rows]
    exps = [ex[(2 + i) * rows:(3 + i) * rows] for i in range(len(MXU_LEVELS))] + [None]
    intra = _intra_scores_times_v(q, kk, vi, exps, masks_ref)
    ea = jnp.exp(a_pre)
    qa = q * ea
    kb = kk * jnp.exp(a_suf)
    for s in range(nseq):
        cr = slice(s * length, (s + 1) * length)
        last = (s + 1) * length - 1
        for hd in range(N_HEADS):
            hs = slice(hd * D_HEAD, (hd + 1) * D_HEAD)
            st = hs_ref[s, hd].T
            inter = _dot_nt(_pad_rows_bf16(qa[cr, hs]), st.astype(BF16))[0:length, :]
            o_ref[cr, hs] = inter + intra[hd][cr, :]
            st_new = st * ea[last:last + 1, hs] + _dot_tn(_pad_rows_bf16(vi[cr, hs]), _pad_rows_bf16(kb[cr, hs]))
            nstate_ref[s, hd] = st_new.T

    o = _head_out(o_ref[...], g, gnw_ref[...])
    mix_in = jnp.concatenate([y_conv, o], axis=1).astype(BF16)
    x1_ref[...] = x + _dot(mix_in, wout_ref[...])


def _mix_sample(x, conv_state, hgrn_state, lbl, nw, w_in, cw, cb, gnw, w_out, x1_flat, row_offset):
    nb, length, _ = x.shape
    rows = MIX_SEQS * length
    assert nb % MIX_SEQS == 0 and length == SUBLANES and row_offset % rows == 0
    first_block = row_offset // rows
    cmat, masks = _mix_constants(rows, length, MXU_LEVELS, MXU_LEVELS)
    const = lambda shape: pl.BlockSpec(shape, lambda i: (0,) * len(shape))
    return pl.pallas_call(
        _mix_sample_kernel,
        grid=(nb // MIX_SEQS,),
        in_specs=[
            pl.BlockSpec((rows, D_MODEL), lambda i: (i, 0)),
            pl.BlockSpec((MIX_SEQS, 2, D_CONV), lambda i: (i, 0, 0)),
            pl.BlockSpec((MIX_SEQS, N_HEADS, D_HEAD, D_HEAD), lambda i: (i, 0, 0, 0)),
            const((2, D_HGRN)), const((1, D_MODEL)), const((D_MODEL, D_PROJ)), const((3, D_CONV)),
            const((1, D_CONV)), const((1, D_HGRN)), const((D_MODEL, D_MODEL)), const(cmat.shape),
            const(masks.shape), pl.BlockSpec(memory_space=pl.ANY),
        ],
        out_specs=[
            pl.BlockSpec((rows, D_MODEL), lambda i: (first_block + i, 0)),
            pl.BlockSpec((MIX_SEQS, length, D_CONV), lambda i: (i, 0, 0)),
            pl.BlockSpec((MIX_SEQS, N_HEADS, D_HEAD, D_HEAD), lambda i: (i, 0, 0, 0)),
        ],
        out_shape=[
            jax.ShapeDtypeStruct(x1_flat.shape, F32),
            jax.ShapeDtypeStruct((nb, length, D_CONV), F32),
            jax.ShapeDtypeStruct((nb, N_HEADS, D_HEAD, D_HEAD), F32),
        ],
        scratch_shapes=[pltpu.VMEM((rows, D_HGRN), F32)],
        input_output_aliases={12: 0},
        compiler_params=pltpu.CompilerParams(
            dimension_semantics=("arbitrary",), vmem_limit_bytes=VMEM_LIMIT_BYTES),
        name="mix_sample",
    )(x.reshape(nb * length, D_MODEL), conv_state, hgrn_state, lbl, nw, w_in, cw, cb, gnw, w_out, cmat, masks,
      x1_flat)


def _router_kernel(x1_ref, nw_ref, rw_hi_ref, rw_lo_ref, rb_ref, ltri_ref, utri_ref, xn_ref, meta_ref, cnt_ref):
    n = x1_ref.shape[0]
    xn = _rmsnorm(x1_ref[...], nw_ref[...])
    xn_ref[...] = xn.astype(BF16)
    x_hi, x_lo = _split2(xn)
    logits = (_dot(x_hi, rw_hi_ref[...]) + _dot(x_lo, rw_hi_ref[...]) + _dot(x_hi, rw_lo_ref[...])
              + rb_ref[...])
    lane = lax.broadcasted_iota(jnp.int32, (n, LANES), 1).astype(F32)
    work = logits
    vals, ids = [], []
    for _ in range(TOP_K):
        m = jnp.max(work, axis=-1, keepdims=True)
        i = jnp.min(jnp.where(work == m, lane, float(LANES)), axis=-1, keepdims=True)
        vals.append(m)
        ids.append(i)
        work = jnp.where(lane == i, -jnp.inf, work)
    es = [jnp.exp(v - vals[0]) for v in vals]
    den = es[0] + es[1] + es[2] + es[3]
    gates = [e / den for e in es]

    onehots = [(lane == i) for i in ids]
    multi = jnp.zeros((n, LANES), F32)
    for oh in onehots:
        multi = multi + oh.astype(F32)
    counts = jnp.sum(multi, axis=0, keepdims=True)
    before = _dot(ltri_ref[...], multi.astype(BF16))
    seg = jnp.ceil(counts * (1.0 / SEG_ALIGN)) * SEG_ALIGN
    seg_rows = jnp.broadcast_to(seg, (BF16_ROWS, LANES)).astype(BF16)
    seg_off = _dot(seg_rows, utri_ref[...])[0:1, :]
    slot_of = seg_off + before
    meta = jnp.zeros((n, LANES), F32)
    for k in range(TOP_K):
        slot = jnp.sum(jnp.where(onehots[k], slot_of, 0.0), axis=-1, keepdims=True)
        meta = jnp.where(lane == k, ids[k], meta)
        meta = jnp.where(lane == TOP_K + k, gates[k], meta)
        meta = jnp.where(lane == 2 * TOP_K + k, slot, meta)
    meta_ref[...] = meta
    cnt_ref[...] = jnp.broadcast_to(counts, (SUBLANES, LANES))


def _router(x1, nw, router_w, router_b):
    t = x1.shape[0]
    n_tiles = t // TOKEN_TILE
    rw = jnp.zeros((D_MODEL, LANES), F32).at[:, :N_EXPERTS].set(router_w)
    rw_hi, rw_lo = _split2(rw)
    rb = jnp.full((1, LANES), -1e30, F32).at[0, :N_EXPERTS].set(router_b)
    idx = np.arange(TOKEN_TILE)
    ltri = jnp.asarray(idx[None, :] < idx[:, None], BF16)
    lid = np.arange(LANES)
    utri = jnp.asarray(lid[:, None] < lid[None, :], BF16)
    const = lambda shape: pl.BlockSpec(shape, lambda i: (0,) * len(shape))
    return pl.pallas_call(
        _router_kernel,
        grid=(n_tiles,),
        in_specs=[
            pl.BlockSpec((TOKEN_TILE, D_MODEL), lambda i: (i, 0)),
            const((1, D_MODEL)), const((D_MODEL, LANES)), const((D_MODEL, LANES)), const((1, LANES)),
            const((TOKEN_TILE, TOKEN_TILE)), const((LANES, LANES)),
        ],
        out_specs=[
            pl.BlockSpec((TOKEN_TILE, D_MODEL), lambda i: (i, 0)),
            pl.BlockSpec((TOKEN_TILE, LANES), lambda i: (i, 0)),
            pl.BlockSpec((None, SUBLANES, LANES), lambda i: (i, 0, 0)),
        ],
        out_shape=[
            jax.ShapeDtypeStruct((t, D_MODEL), BF16),
            jax.ShapeDtypeStruct((t, LANES), F32),
            jax.ShapeDtypeStruct((n_tiles, SUBLANES, LANES), F32),
        ],
        compiler_params=pltpu.CompilerParams(
            dimension_semantics=("arbitrary",), vmem_limit_bytes=VMEM_LIMIT_BYTES),
        name="router",
    )(x1, nw, rw_hi, rw_lo, rb, ltri, utri)


def _segment_tables(cnt):
    seg = (cnt + SEG_ALIGN - 1) // SEG_ALIGN * SEG_ALIGN
    src = jnp.cumsum(seg, axis=1) - seg
    tot = jnp.sum(seg, axis=0)
    cap = (tot + ROW_TILE - 1) // ROW_TILE * ROW_TILE
    base = jnp.cumsum(cap) - cap
    dst = base[None, :] + jnp.cumsum(seg, axis=0) - seg
    n_used = jnp.sum(cap) // ROW_TILE
    fill = base + tot
    lane = jnp.arange(N_EXPERTS)
    counts, srcs, dsts = [], [], []
    for size in SEG_SIZES:
        has = (seg & size) != 0
        done = seg & ~(2 * size - 1)
        place = has[:, :, None] & ((jnp.cumsum(has, axis=1) - 1)[:, :, None] == lane[None, None, :])
        counts.append(jnp.sum(has, axis=1))
        srcs.append(jnp.sum(jnp.where(place, (src + done)[:, :, None], 0), axis=1))
        dsts.append(jnp.sum(jnp.where(place, (dst + done)[:, :, None], 0), axis=1))
    i32 = lambda a: a.astype(jnp.int32).reshape(-1)
    chunks = (i32(jnp.stack(counts, axis=1)), i32(jnp.stack(srcs, axis=1)), i32(jnp.stack(dsts, axis=1)))
    return chunks, i32(fill), i32(cap - tot), i32(base), i32(cap // ROW_TILE), i32(n_used)


def _for_each_chunk(n, fn):
    for size in SEG_SIZES:
        done = n & ~(2 * size - 1)

        @pl.when((n & size) != 0)
        def _():
            fn(done, size)


def _segment_copies(cnt_ref, src_ref, dst_ref, tile, make_copy, start):
    for si, size in enumerate(SEG_SIZES):
        first = (tile * len(SEG_SIZES) + si) * N_EXPERTS

        def one(p, carry, si=si, size=size, first=first):
            copy = make_copy(pl.multiple_of(src_ref[first + p], SEG_ALIGN),
                             pl.multiple_of(dst_ref[first + p], SEG_ALIGN), size)
            if start:
                copy.start(priority=si % 2)
            else:
                copy.wait()
            return carry
        lax.fori_loop(0, cnt_ref[tile * len(SEG_SIZES) + si], one, 0)


SORT_BLOCK = 512


def _slot_matrix(slots, block, gates):
    row = (lax.broadcasted_iota(jnp.int32, (SORT_BLOCK, TOKEN_TILE), 0) + block * SORT_BLOCK).astype(F32)
    m = jnp.zeros((SORT_BLOCK, TOKEN_TILE), F32)
    for k in range(TOP_K):
        m = jnp.where(row == slots[k:k + 1, :], 1.0 if gates is None else gates[k:k + 1, :], m)
    return m.astype(BF16)


def _dispatch_kernel(seg_ref, src_ref, dst_ref, fill_ref, gap_ref, nu_ref, slot_ref, xn_ref, xs_ref, sorted_ref,
                     zero_ref, sem):
    tile = pl.program_id(0)

    @pl.when(tile == 0)
    def _():
        zero_ref[...] = jnp.zeros_like(zero_ref)

        def fill_copy(row, size):
            return pltpu.make_async_copy(zero_ref.at[pl.ds(0, size)],
                                         xs_ref.at[pl.ds(pl.multiple_of(row, SEG_ALIGN), size)], sem)

        def gaps(act):
            def per_expert(e, carry):
                _for_each_chunk(gap_ref[e], lambda done, size: act(fill_copy(fill_ref[e] + done, size)))
                return carry
            lax.fori_loop(0, N_EXPERTS, per_expert, 0)

        def tail(act):
            n_tail = xs_ref.shape[0] // ROW_TILE - nu_ref[0]
            lax.fori_loop(0, n_tail, lambda i, c: (act(fill_copy((nu_ref[0] + i) * ROW_TILE, ROW_TILE)), c)[1], 0)

        gaps(lambda c: c.start())
        tail(lambda c: c.start())
        gaps(lambda c: c.wait())
        tail(lambda c: c.wait())

    slots = slot_ref[...]
    xn = xn_ref[...]
    for rb in range(TILE_CAP // SORT_BLOCK):
        sorted_ref[rb * SORT_BLOCK:(rb + 1) * SORT_BLOCK, :] = _dot(
            _slot_matrix(slots, rb, None), xn).astype(BF16)

    def make_copy(src, dst, size):
        return pltpu.make_async_copy(sorted_ref.at[pl.ds(src, size)], xs_ref.at[pl.ds(dst, size)], sem)

    _segment_copies(seg_ref, src_ref, dst_ref, tile, make_copy, start=True)
    _segment_copies(seg_ref, src_ref, dst_ref, tile, make_copy, start=False)


def _dispatch(tables, slot_t, xn, n_rows_total):
    t = xn.shape[0]
    n_tiles = t // TOKEN_TILE
    return pl.pallas_call(
        _dispatch_kernel,
        grid_spec=pltpu.PrefetchScalarGridSpec(
            num_scalar_prefetch=len(tables),
            grid=(n_tiles,),
            in_specs=[
                pl.BlockSpec((SUBLANES, TOKEN_TILE), lambda i, *_: (0, i)),
                pl.BlockSpec((TOKEN_TILE, D_MODEL), lambda i, *_: (i, 0)),
            ],
            out_specs=pl.BlockSpec(memory_space=pl.ANY),
            scratch_shapes=[
                pltpu.VMEM((TILE_CAP, D_MODEL), BF16),
                pltpu.VMEM((ROW_TILE, D_MODEL), BF16),
                pltpu.SemaphoreType.DMA(()),
            ],
        ),
        out_shape=jax.ShapeDtypeStruct((n_rows_total, D_MODEL), BF16),
        compiler_params=pltpu.CompilerParams(
            dimension_semantics=("arbitrary",), vmem_limit_bytes=VMEM_LIMIT_BYTES),
        name="dispatch",
    )(*tables, slot_t, xn)


X_SLOTS = 3
W_FETCH_STEPS = (0, 1, 3)


def _experts_kernel(base_ref, nt_ref, nu_ref, bg_ref, bu_ref, bd_ref, wg_hbm, wu_hbm, wd_hbm, xs_ref, zs_ref,
                    w_f32, wg_bf, wu_bf, wd_bf, xbuf, hbuf, zbuf, w_sem, in_sem, out_sem):
    e = pl.program_id(0)
    n_e = pl.num_programs(0)
    n = nt_ref[e]
    base = base_ref[e]
    w_slot = e % 2
    has_next = e + 1 < n_e

    def w_copy(j, expert, slot):
        w_hbm = (wg_hbm, wu_hbm, wd_hbm)[j]
        return pltpu.make_async_copy(w_hbm.at[expert], w_f32.at[slot, j], w_sem.at[slot])

    def x_copy(i):
        rows = pl.ds(pl.multiple_of(base + i * ROW_TILE, ROW_TILE), ROW_TILE)
        return pltpu.make_async_copy(xs_ref.at[rows], xbuf.at[i % X_SLOTS], in_sem.at[i % X_SLOTS])

    def z_copy(row, slot):
        rows = pl.ds(pl.multiple_of(row, ROW_TILE), ROW_TILE)
        return pltpu.make_async_copy(zbuf.at[slot], zs_ref.at[rows], out_sem.at[slot])

    def hidden(i):
        x = xbuf[i % X_SLOTS]
        a = jnp.minimum(_dot(x, wg_bf[...]) + bg_ref[...], SWIGLU_LIMIT)
        u = jnp.clip(_dot(x, wu_bf[...]) + bu_ref[...], -SWIGLU_LIMIT, SWIGLU_LIMIT)
        hbuf[i % 2] = ((u + 1.0) * a * jax.nn.sigmoid(SWIGLU_ALPHA * a)).astype(BF16)

    def project_down(i, slot):
        @pl.when(i >= 2)
        def _():
            z_copy(base, slot).wait()
        zbuf[slot] = (_dot(hbuf[slot], wd_bf[...]) + bd_ref[...]).astype(BF16)

    @pl.when(e == 0)
    def _():
        for j in range(3):
            w_copy(j, 0, 0).start()

    for i in range(X_SLOTS):
        @pl.when(i < n)
        def _():
            x_copy(i).start(priority=1)

    def fetch_next_weights(j):
        @pl.when(has_next)
        def _():
            w_copy(j, e + 1, 1 - w_slot).start()

    fetch_next_weights(0)
    for j in range(3):
        w_copy(j, e, w_slot).wait()

    @pl.when(n > 0)
    def _():
        wg_bf[...] = w_f32[w_slot, 0].astype(BF16)
        wu_bf[...] = w_f32[w_slot, 1].astype(BF16)
        wd_bf[...] = w_f32[w_slot, 2].astype(BF16)
        x_copy(0).wait()
        hidden(0)

        def step(i, carry):
            x_copy(i + 1).wait()

            @pl.when(i + X_SLOTS < n)
            def _():
                x_copy(i + X_SLOTS).start(priority=1)

            for j in (1, 2):
                @pl.when(i == W_FETCH_STEPS[j])
                def _():
                    fetch_next_weights(j)

            project_down(i, i % 2)
            hidden(i + 1)
            z_copy(base + i * ROW_TILE, i % 2).start()
            return carry

        lax.fori_loop(0, n - 1, step, 0)
        project_down(n - 1, (n - 1) % 2)
        z_copy(base + (n - 1) * ROW_TILE, (n - 1) % 2).start()

        @pl.when(n >= 2)
        def _():
            z_copy(base, n % 2).wait()
        z_copy(base, (n - 1) % 2).wait()

    for j in (1, 2):
        @pl.when(jnp.maximum(n - 1, 0) <= W_FETCH_STEPS[j])
        def _():
            fetch_next_weights(j)

    @pl.when(e == pl.num_programs(0) - 1)
    def _():
        zbuf[0] = jnp.zeros((ROW_TILE, D_MODEL), BF16)
        n_tail = zs_ref.shape[0] // ROW_TILE - nu_ref[0]
        tail = lambda act: lax.fori_loop(
            0, n_tail, lambda i, c: (act(z_copy((nu_ref[0] + i) * ROW_TILE, 0)), c)[1], 0)
        tail(lambda c: c.start())
        tail(lambda c: c.wait())


def _experts(base, n_tiles, n_used, xs, w_gate, b_gate, w_up, b_up, w_down, b_down):
    b_spec = pl.BlockSpec((None, 1, D_MODEL), lambda e, *_: (e, 0, 0))
    any_spec = pl.BlockSpec(memory_space=pl.ANY)
    return pl.pallas_call(
        _experts_kernel,
        grid_spec=pltpu.PrefetchScalarGridSpec(
            num_scalar_prefetch=3,
            grid=(N_EXPERTS,),
            in_specs=[b_spec, b_spec, b_spec, any_spec, any_spec, any_spec, any_spec],
            out_specs=any_spec,
            scratch_shapes=[pltpu.VMEM((2, 3, D_MODEL, D_MODEL), F32)] + [
                pltpu.VMEM((D_MODEL, D_MODEL), BF16) for _ in range(3)] + [
                pltpu.VMEM((X_SLOTS, ROW_TILE, D_MODEL), BF16)] + [
                pltpu.VMEM((2, ROW_TILE, D_MODEL), BF16) for _ in range(2)] + [
                pltpu.SemaphoreType.DMA((2,)), pltpu.SemaphoreType.DMA((X_SLOTS,)), pltpu.SemaphoreType.DMA((2,))],
        ),
        out_shape=jax.ShapeDtypeStruct(xs.shape, BF16),
        compiler_params=pltpu.CompilerParams(
            dimension_semantics=("arbitrary",), vmem_limit_bytes=VMEM_LIMIT_BYTES),
        name="experts",
    )(base, n_tiles, n_used, b_gate[:, None, :], b_up[:, None, :], b_down[:, None, :], w_gate, w_up, w_down, xs)


def _combine_kernel(n_first, seg_ref, src_ref, dst_ref, route_ref, x1_ref, fw_ref, zs_ref, ya_ref, yb_ref,
                    sorted_ref, sem):
    tile = pl.program_id(0)
    sorted_ref[...] = jnp.zeros_like(sorted_ref)

    def make_copy(src, dst, size):
        return pltpu.make_async_copy(zs_ref.at[pl.ds(dst, size)], sorted_ref.at[pl.ds(src, size)], sem)

    _segment_copies(seg_ref, src_ref, dst_ref, tile, make_copy, start=True)
    route = route_ref[...]
    slots, gates = route[0:TOP_K, :], route[TOP_K:2 * TOP_K, :]
    n_blocks = TILE_CAP // SORT_BLOCK
    early = [_slot_matrix(slots, cb, gates) for cb in range(2)]
    _segment_copies(seg_ref, src_ref, dst_ref, tile, make_copy, start=False)
    moe = jnp.zeros((TOKEN_TILE, D_MODEL), F32)
    for cb in range(n_blocks):
        m = early[cb] if cb < len(early) else _slot_matrix(slots, cb, gates)
        moe = moe + _dot_tn(m, sorted_ref[cb * SORT_BLOCK:(cb + 1) * SORT_BLOCK, :])
    y = _rmsnorm(x1_ref[...] + moe, fw_ref[...])

    @pl.when(tile < n_first)
    def _():
        ya_ref[...] = y

    @pl.when(tile >= n_first)
    def _():
        yb_ref[...] = y


def _combine(tables, route_t, x1, final_w, zs, rows_first):
    seg, src, dst = tables
    t = x1.shape[0]
    assert rows_first % TOKEN_TILE == 0 and 0 < rows_first < t and TILE_CAP % SORT_BLOCK == 0
    n_tiles = t // TOKEN_TILE
    n_first = rows_first // TOKEN_TILE
    return pl.pallas_call(
        functools.partial(_combine_kernel, n_first),
        grid_spec=pltpu.PrefetchScalarGridSpec(
            num_scalar_prefetch=3,
            grid=(n_tiles,),
            in_specs=[
                pl.BlockSpec((SUBLANES, TOKEN_TILE), lambda i, *_: (0, i)),
                pl.BlockSpec((TOKEN_TILE, D_MODEL), lambda i, *_: (i, 0)),
                pl.BlockSpec((1, D_MODEL), lambda i, *_: (0, 0)),
                pl.BlockSpec(memory_space=pl.ANY),
            ],
            out_specs=[
                pl.BlockSpec((TOKEN_TILE, D_MODEL), lambda i, *_: (jnp.minimum(i, n_first - 1), 0)),
                pl.BlockSpec((TOKEN_TILE, D_MODEL), lambda i, *_: (jnp.maximum(i - n_first, 0), 0)),
            ],
            scratch_shapes=[pltpu.VMEM((TILE_CAP, D_MODEL), BF16), pltpu.SemaphoreType.DMA(())],
        ),
        out_shape=[jax.ShapeDtypeStruct((rows_first, D_MODEL), F32),
                   jax.ShapeDtypeStruct((t - rows_first, D_MODEL), F32)],
        compiler_params=pltpu.CompilerParams(
            dimension_semantics=("arbitrary",), vmem_limit_bytes=VMEM_LIMIT_BYTES),
        name="combine",
    )(seg, src, dst, route_t, x1, final_w, zs)


def _moe_and_final_norm(x1, rows_first, norm_ffn_w, router_w, router_b, w_gate, b_gate, w_up, b_up, w_down, b_down,
                        final_w):
    t = x1.shape[0]
    assert t % TOKEN_TILE == 0
    n_tiles = t // TOKEN_TILE
    n_rows_total = TOP_K * t + n_tiles * N_EXPERTS * (SEG_ALIGN - 1) + N_EXPERTS * (ROW_TILE - 1)
    n_rows_total = (n_rows_total + ROW_TILE - 1) // ROW_TILE * ROW_TILE
    xn, meta, cnt = _router(x1, norm_ffn_w.reshape(1, D_MODEL), router_w, router_b)
    cnt = cnt[:, 0, :N_EXPERTS].astype(jnp.int32)
    chunks, fill, gap, base, n_row_tiles, n_used = _segment_tables(cnt)
    route_t = jnp.concatenate([meta[:, 2 * TOP_K:3 * TOP_K], meta[:, TOP_K:2 * TOP_K]], axis=1).T
    xs = _dispatch((*chunks, fill, gap, n_used), route_t, xn, n_rows_total)
    zs = _experts(base, n_row_tiles, n_used, xs, w_gate, b_gate, w_up, b_up, w_down, b_down)
    return _combine(chunks, route_t, x1, final_w.reshape(1, D_MODEL), zs, rows_first)


def kernel(x_prompt, x_sample, state_conv, state_hgrn, lb_logits, norm_mix_w, w_in, conv_w, conv_b, gnorm_w,
           w_out, norm_ffn_w, router_w, router_b, w_gate, b_gate, w_up, b_up, w_down, b_down, final_norm_w):
    assert norm_mix_w.shape[0] == 1 and lb_logits.shape[0] == 2, "single-layer step"
    b, l, _ = x_prompt.shape
    nb, ls, _ = x_sample.shape
    lbl = lb_logits.astype(F32)
    nw = norm_mix_w[0].reshape(1, D_MODEL)
    w_in_bf = w_in[0].astype(BF16)
    w_out_bf = w_out[0].astype(BF16)
    cw, cb = conv_w[0], conv_b[0].reshape(1, D_CONV)
    gnw = gnorm_w[0].reshape(1, D_HGRN)
    rows_p, rows_s = b * l, nb * ls
    x1, conv_p, hgrn_p = _mix_prompt(x_prompt, lbl, nw, w_in_bf, cw, cb, gnw, w_out_bf, rows_p + rows_s)
    x1, conv_s, hgrn_s = _mix_sample(x_sample, state_conv[0], state_hgrn[0], lbl, nw, w_in_bf, cw, cb, gnw,
                                     w_out_bf, x1, rows_p)
    y_p, y_s = _moe_and_final_norm(x1, rows_p, norm_ffn_w[0], router_w[0], router_b[0], w_gate[0], b_gate[0],
                                   w_up[0], b_up[0], w_down[0], b_down[0], final_norm_w)
    conv_p = conv_p[:, SUBLANES - 2:, :]
    conv_s = conv_s[:, ls - 2:, :]
    return (y_p.reshape(b, l, D_MODEL), y_s.reshape(nb, ls, D_MODEL), conv_p[None], hgrn_p[None], conv_s[None],
            hgrn_s[None])
```

```python
import functools

import numpy as np
import jax
import jax.numpy as jnp
from jax import lax
from jax.experimental import pallas as pl
from jax.experimental.pallas import tpu as pltpu

F32 = jnp.float32
BF16 = jnp.bfloat16

D_MODEL = 1024
D_CONV = 512
D_HGRN = 512
N_HEADS = 4
D_HEAD = 128
D_PROJ = 3 * D_CONV + 4 * D_HGRN
N_EXPERTS = 32
TOP_K = 4
SWIGLU_LIMIT = 7.0
SWIGLU_ALPHA = 1.702
EPS = 1e-6
PROMPT_CHUNK = 64

LANES = 128
SUBLANES = 8
BF16_ROWS = 16
VMEM_LIMIT_BYTES = 56 * 1024 * 1024

MIX_SLAB = 128
MIX_ROWS = 512
MIX_PART = 256
MIX_SEQS = 16
TOKEN_TILE = 512
SEG_ALIGN = BF16_ROWS
ROW_TILE = 512
SEG_SIZES = tuple(SEG_ALIGN << i for i in reversed(range(6)))
TILE_CAP = TOP_K * TOKEN_TILE + N_EXPERTS * SEG_ALIGN

NT_DIMS = (((1,), (1,)), ((), ()))
TN_DIMS = (((0,), (0,)), ((), ()))


def _dot(a, b):
    return jnp.dot(a, b, preferred_element_type=F32)


def _dot_nt(a, b):
    return lax.dot_general(a, b, NT_DIMS, preferred_element_type=F32)


def _dot_tn(a, b):
    return lax.dot_general(a, b, TN_DIMS, preferred_element_type=F32)


def _split2(x):
    hi = x.astype(BF16)
    lo = (x - hi.astype(F32)).astype(BF16)
    return hi, lo


def _rmsnorm(x, w):
    return x * lax.rsqrt(jnp.mean(x * x, axis=-1, keepdims=True) + EPS) * w


def _level_exponent_matrix(n, chunk, h):
    x = np.zeros((n, n), np.float32)
    for t in range(n):
        base = t - t % (2 * h)
        m = base + h - 1
        if t % (2 * h) >= h:
            x[t, m + 1:t + 1] = 1.0
        else:
            x[t, t + 1:m + 1] = 1.0
    return x


def _mix_constants(n, chunk, mxu_levels, all_levels):
    t = np.arange(n)
    same_chunk = (t[:, None] // chunk) == (t[None, :] // chunk)
    tri = (same_chunk & (t[None, :] <= t[:, None])).astype(np.float32)
    suf = (same_chunk & (t[None, :] > t[:, None])).astype(np.float32)
    cmat = np.concatenate([tri, suf] + [_level_exponent_matrix(n, chunk, h) for h in mxu_levels], axis=0)
    masks = []
    for h in all_levels:
        blk = (t[:, None] // (2 * h)) == (t[None, :] // (2 * h))
        masks.append((blk & ((t[:, None] % (2 * h)) >= h) & ((t[None, :] % (2 * h)) < h)).astype(np.float32))
    masks.append(np.eye(n, dtype=np.float32))
    return jnp.asarray(cmat, BF16), jnp.asarray(np.stack(masks), F32)


def _vpu_level_exponent(a, h):
    n = a.shape[0]
    pieces = []
    for j in range(n // (2 * h)):
        b = j * 2 * h
        ref = a[b + h - 1:b + h, :]
        pieces.append(ref - a[b:b + h, :])
        pieces.append(a[b + h:b + 2 * h, :] - ref)
    return jnp.concatenate(pieces, axis=0)


def _chunk_prefix_sums(x, chunk):
    n, width = x.shape
    x3 = x.reshape(n // SUBLANES, SUBLANES, width)
    pos = lax.broadcasted_iota(jnp.int32, (1, SUBLANES, 1), 1)
    d = 1
    while d < SUBLANES:
        x3 = x3 + jnp.where(pos >= d, pltpu.roll(x3, d, 1), 0.0)
        d *= 2
    x = x3.reshape(n, width)
    pieces = []
    for b in range(0, n, SUBLANES):
        blk = x[b:b + SUBLANES, :]
        if b % chunk:
            blk = blk + pieces[-1][SUBLANES - 1:SUBLANES, :]
        pieces.append(blk)
    return jnp.concatenate(pieces, axis=0)


def _small_level_exponent(a, logf, h):
    n = a.shape[0]
    pos = lax.broadcasted_iota(jnp.int32, (SUBLANES, 1), 0)
    if h == 1:
        odd = lax.broadcasted_iota(jnp.int32, (n, 1), 0) % 2 == 1
        return jnp.where(odd, logf, 0.0)
    pieces = []
    for b in range(0, n, SUBLANES):
        blk = a[b:b + SUBLANES, :]
        if h == 4:
            ref = blk[3:4, :]
        else:
            ref = jnp.where(pos < 4, blk[1:2, :], blk[5:6, :])
        pieces.append(jnp.where(pos % (2 * h) >= h, blk - ref, ref - blk))
    return jnp.concatenate(pieces, axis=0)


def _forget_lower_bound(lbl):
    m = jnp.max(lbl, axis=0, keepdims=True)
    e = jnp.exp(lbl - m)
    return e[0:1, :] / jnp.sum(e, axis=0, keepdims=True)


def _projections(x, nw, w_in_ref):
    h = _rmsnorm(x, nw).astype(BF16)
    return _dot(h, w_in_ref[...])


def _gates(proj, lb):
    q = proj[:, 3 * D_CONV:3 * D_CONV + D_HGRN]
    ff = proj[:, 3 * D_CONV + D_HGRN:3 * D_CONV + 2 * D_HGRN]
    vi = proj[:, 3 * D_CONV + 2 * D_HGRN:3 * D_CONV + 3 * D_HGRN]
    g = proj[:, 3 * D_CONV + 3 * D_HGRN:]
    e = jnp.exp(-jnp.abs(ff))
    r = 1.0 / (1.0 + e)
    pos = ff >= 0
    sig = jnp.where(pos, r, e * r)
    sig_neg = jnp.where(pos, e * r, r)
    logf = jnp.log(lb + (1.0 - lb) * sig)
    kk = (1.0 - lb) * sig_neg
    return q, logf, kk, vi, g


def _intra_scores_times_v(q, kk, vi, exps, masks_ref):
    n = q.shape[0]
    out = []
    for hd in range(N_HEADS):
        hs = slice(hd * D_HEAD, (hd + 1) * D_HEAD)
        qd, kd = q[:, hs], kk[:, hs]
        sc = jnp.zeros((n, n), F32)
        for l, ex in enumerate(exps):
            if ex is None:
                qh, kh = qd.astype(BF16), kd.astype(BF16)
            else:
                w = jnp.exp(ex[:, hs])
                qh, kh = (qd * w).astype(BF16), (kd * w).astype(BF16)
            sc = sc + masks_ref[l] * _dot_nt(qh, kh)
        out.append(_dot(sc.astype(BF16), vi[:, hs].astype(BF16)))
    return out


def _head_out(o, g, gnw):
    parts = []
    for hd in range(N_HEADS):
        oh = o[:, hd * D_HEAD:(hd + 1) * D_HEAD]
        parts.append(oh * lax.rsqrt(jnp.mean(oh * oh, axis=-1, keepdims=True) + EPS))
    on = jnp.concatenate(parts, axis=1)
    return on * gnw * (g * jax.nn.sigmoid(g))


def _pad_rows_bf16(a):
    return jnp.concatenate([a, jnp.zeros_like(a)], axis=0).astype(BF16)


PROMPT_VPU_LEVELS = (32, 16, 8)
MXU_LEVELS = (4, 2, 1)


def _mix_prompt_kernel(n_r, n_steps, *refs):
    step = pl.program_id(0)

    @pl.when(step < n_steps)
    def _():
        _mix_prompt_step(step % n_r, n_r, *refs)

    @pl.when(step >= n_steps)
    def _():
        x1_ref = refs[N_MIX_PROMPT_INPUTS]
        x1_ref[...] = jnp.zeros_like(x1_ref)


N_MIX_PROMPT_INPUTS = 9


def _mix_prompt_step(r, n_r, x_ref, lbl_ref, nw_ref, win_ref, cw_ref, cb_ref, gnw_ref, wout_ref, masks_ref,
                     x1_ref, nconv_ref, nstate_ref, st_ref, tail_ref):
    rows = x_ref.shape[0]

    @pl.when(r == 0)
    def _():
        st_ref[...] = jnp.zeros_like(st_ref)
        tail_ref[...] = jnp.zeros_like(tail_ref)

    lb = _forget_lower_bound(lbl_ref[...])
    cw = cw_ref[...]
    n_chunks = MIX_SLAB // PROMPT_CHUNK
    rid = lax.broadcasted_iota(jnp.int32, (MIX_PART, 1), 0)
    t0 = tail_ref[SUBLANES - 2:SUBLANES - 1, :]
    t1 = tail_ref[SUBLANES - 1:SUBLANES, :]
    states = [st_ref[hd] for hd in range(N_HEADS)]

    for p in range(rows // MIX_PART):
        pr = slice(p * MIX_PART, (p + 1) * MIX_PART)
        x = x_ref[pr, :]
        proj = _projections(x, nw_ref[...], win_ref)

        bg = proj[:, 0:D_CONV]
        u = proj[:, D_CONV:2 * D_CONV] * proj[:, 2 * D_CONV:3 * D_CONV]
        u1 = jnp.where(rid == 0, t1, pltpu.roll(u, 1, 0))
        u2 = jnp.where(rid == 0, t0, jnp.where(rid == 1, t1, pltpu.roll(u, 2, 0)))
        y_conv = bg * (cw[0:1, :] * u2 + cw[1:2, :] * u1 + cw[2:3, :] * u + cb_ref[...])
        u_last = u[MIX_PART - SUBLANES:MIX_PART, :]
        t0, t1 = u_last[SUBLANES - 2:SUBLANES - 1, :], u_last[SUBLANES - 1:SUBLANES, :]

        q, logf, kk, vi, g = _gates(proj, lb)
        o_parts = []
        for s in range(MIX_PART // MIX_SLAB):
            sl = slice(s * MIX_SLAB, (s + 1) * MIX_SLAB)
            qs, ks, vs = q[sl], kk[sl], vi[sl]
            a_pre = _chunk_prefix_sums(logf[sl], PROMPT_CHUNK)
            a_suf = jnp.concatenate(
                [a_pre[(c + 1) * PROMPT_CHUNK - 1:(c + 1) * PROMPT_CHUNK, :]
                 - a_pre[c * PROMPT_CHUNK:(c + 1) * PROMPT_CHUNK, :] for c in range(n_chunks)], axis=0)
            exps = [_vpu_level_exponent(a_pre, h) for h in PROMPT_VPU_LEVELS]
            exps += [_small_level_exponent(a_pre, logf[sl], h) for h in MXU_LEVELS]
            exps.append(None)
            intra = _intra_scores_times_v(qs, ks, vs, exps, masks_ref)
            ea = jnp.exp(a_pre)
            qa = (qs * ea).astype(BF16)
            kb = (ks * jnp.exp(a_suf)).astype(BF16)
            vb = vs.astype(BF16)
            for c in range(n_chunks):
                cr = slice(c * PROMPT_CHUNK, (c + 1) * PROMPT_CHUNK)
                last = (c + 1) * PROMPT_CHUNK - 1
                heads = []
                for hd in range(N_HEADS):
                    hs = slice(hd * D_HEAD, (hd + 1) * D_HEAD)
                    heads.append(_dot_nt(qa[cr, hs], states[hd].astype(BF16)) + intra[hd][cr, :])
                    states[hd] = states[hd] * ea[last:last + 1, hs] + _dot_tn(vb[cr, hs], kb[cr, hs])
                o_parts.append(jnp.concatenate(heads, axis=1))

        o = _head_out(jnp.concatenate(o_parts, axis=0), g, gnw_ref[...])
        mix_in = jnp.concatenate([y_conv, o], axis=1).astype(BF16)
        x1_ref[pr, :] = x + _dot(mix_in, wout_ref[...])

    tail_ref[...] = u_last
    for hd in range(N_HEADS):
        st_ref[hd] = states[hd]

    @pl.when(r == n_r - 1)
    def _():
        nconv_ref[...] = u_last
        for hd in range(N_HEADS):
            nstate_ref[hd] = states[hd].T


def _mix_prompt(x, lbl, nw, w_in, cw, cb, gnw, w_out, total_rows):
    b, l, _ = x.shape
    assert l % MIX_ROWS == 0 and MIX_ROWS % MIX_PART == 0 and MIX_PART % MIX_SLAB == 0 and l % PROMPT_CHUNK == 0
    assert total_rows % MIX_ROWS == 0
    n_r = l // MIX_ROWS
    n_steps = b * n_r
    seq = lambda s: jnp.minimum(s, n_steps - 1) // n_r
    _, masks = _mix_constants(MIX_SLAB, PROMPT_CHUNK, (), PROMPT_VPU_LEVELS + MXU_LEVELS)
    const = lambda shape: pl.BlockSpec(shape, lambda s: (0,) * len(shape))
    return pl.pallas_call(
        functools.partial(_mix_prompt_kernel, n_r, n_steps),
        grid=(total_rows // MIX_ROWS,),
        in_specs=[
            pl.BlockSpec((None, MIX_ROWS, D_MODEL), lambda s: (seq(s), jnp.minimum(s, n_steps - 1) % n_r, 0)),
            const((2, D_HGRN)), const((1, D_MODEL)), const((D_MODEL, D_PROJ)), const((3, D_CONV)),
            const((1, D_CONV)), const((1, D_HGRN)), const((D_MODEL, D_MODEL)), const(masks.shape),
        ],
        out_specs=[
            pl.BlockSpec((MIX_ROWS, D_MODEL), lambda s: (s, 0)),
            pl.BlockSpec((None, SUBLANES, D_CONV), lambda s: (seq(s), 0, 0)),
            pl.BlockSpec((None, N_HEADS, D_HEAD, D_HEAD), lambda s: (seq(s), 0, 0, 0)),
        ],
        out_shape=[
            jax.ShapeDtypeStruct((total_rows, D_MODEL), F32),
            jax.ShapeDtypeStruct((b, SUBLANES, D_CONV), F32),
            jax.ShapeDtypeStruct((b, N_HEADS, D_HEAD, D_HEAD), F32),
        ],
        scratch_shapes=[
            pltpu.VMEM((N_HEADS, D_HEAD, D_HEAD), F32),
            pltpu.VMEM((SUBLANES, D_CONV), F32),
        ],
        compiler_params=pltpu.CompilerParams(
            dimension_semantics=("arbitrary",), vmem_limit_bytes=VMEM_LIMIT_BYTES),
        name="mix_prompt",
    )(x, lbl, nw, w_in, cw, cb, gnw, w_out, masks)


def _mix_sample_kernel(x_ref, cs_ref, hs_ref, lbl_ref, nw_ref, win_ref, cw_ref, cb_ref, gnw_ref, wout_ref,
                       cmat_ref, masks_ref, x1_in_ref, x1_ref, nconv_ref, nstate_ref, o_ref):
    del x1_in_ref
    nseq, length, _ = nconv_ref.shape
    rows = nseq * length
    x = x_ref[...]
    proj = _projections(x, nw_ref[...], win_ref)
    lb = _forget_lower_bound(lbl_ref[...])

    bg = proj[:, 0:D_CONV]
    u = proj[:, D_CONV:2 * D_CONV] * proj[:, 2 * D_CONV:3 * D_CONV]
    cs = cs_ref[...]
    expand = lambda a: jnp.broadcast_to(a, (nseq, length, D_CONV)).reshape(rows, D_CONV)
    t0 = expand(cs[:, 0:1, :])
    t1 = expand(cs[:, 1:2, :])
    pos = lax.broadcasted_iota(jnp.int32, (rows, 1), 0) % length
    u1 = jnp.where(pos == 0, t1, pltpu.roll(u, 1, 0))
    u2 = jnp.where(pos == 0, t0, jnp.where(pos == 1, t1, pltpu.roll(u, 2, 0)))
    cw = cw_ref[...]
    y_conv = bg * (cw[0:1, :] * u2 + cw[1:2, :] * u1 + cw[2:3, :] * u + cb_ref[...])
    nconv_ref[...] = u.reshape(nseq, length, D_CONV)

    q, logf, kk, vi, g = _gates(proj, lb)
    lf_hi, lf_lo = _split2(logf)
    ex = _dot(cmat_ref[...], lf_hi) + _dot(cmat_ref[...], lf_lo)
    a_pre = ex[0:rows]
    a_suf = ex[rows:2 * rows]
    exps = [ex[(2 + i) * rows:(3 + i) * rows] for i in range(len(MXU_LEVELS))] + [None]
    intra = _intra_scores_times_v(q, kk, vi, exps, masks_ref)
    ea = jnp.exp(a_pre)
    qa = q * ea
    kb = kk * jnp.exp(a_suf)
    for s in range(nseq):
        cr = slice(s * length, (s + 1) * length)
        last = (s + 1) * length - 1
        for hd in range(N_HEADS):
            hs = slice(hd * D_HEAD, (hd + 1) * D_HEAD)
            st = hs_ref[s, hd].T
            inter = _dot_nt(_pad_rows_bf16(qa[cr, hs]), st.astype(BF16))[0:length, :]
            o_ref[cr, hs] = inter + intra[hd][cr, :]
            st_new = st * ea[last:last + 1, hs] + _dot_tn(_pad_rows_bf16(vi[cr, hs]), _pad_rows_bf16(kb[cr, hs]))
            nstate_ref[s, hd] = st_new.T

    o = _head_out(o_ref[...], g, gnw_ref[...])
    mix_in = jnp.concatenate([y_conv, o], axis=1).astype(BF16)
    x1_ref[...] = x + _dot(mix_in, wout_ref[...])


def _mix_sample(x, conv_state, hgrn_state, lbl, nw, w_in, cw, cb, gnw, w_out, x1_flat, row_offset):
    nb, length, _ = x.shape
    rows = MIX_SEQS * length
    assert nb % MIX_SEQS == 0 and length == SUBLANES and row_offset % rows == 0
    first_block = row_offset // rows
    cmat, masks = _mix_constants(rows, length, MXU_LEVELS, MXU_LEVELS)
    const = lambda shape: pl.BlockSpec(shape, lambda i: (0,) * len(shape))
    return pl.pallas_call(
        _mix_sample_kernel,
        grid=(nb // MIX_SEQS,),
        in_specs=[
            pl.BlockSpec((rows, D_MODEL), lambda i: (i, 0)),
            pl.BlockSpec((MIX_SEQS, 2, D_CONV), lambda i: (i, 0, 0)),
            pl.BlockSpec((MIX_SEQS, N_HEADS, D_HEAD, D_HEAD), lambda i: (i, 0, 0, 0)),
            const((2, D_HGRN)), const((1, D_MODEL)), const((D_MODEL, D_PROJ)), const((3, D_CONV)),
            const((1, D_CONV)), const((1, D_HGRN)), const((D_MODEL, D_MODEL)), const(cmat.shape),
            const(masks.shape), pl.BlockSpec(memory_space=pl.ANY),
        ],
        out_specs=[
            pl.BlockSpec((rows, D_MODEL), lambda i: (first_block + i, 0)),
            pl.BlockSpec((MIX_SEQS, length, D_CONV), lambda i: (i, 0, 0)),
            pl.BlockSpec((MIX_SEQS, N_HEADS, D_HEAD, D_HEAD), lambda i: (i, 0, 0, 0)),
        ],
        out_shape=[
            jax.ShapeDtypeStruct(x1_flat.shape, F32),
            jax.ShapeDtypeStruct((nb, length, D_CONV), F32),
            jax.ShapeDtypeStruct((nb, N_HEADS, D_HEAD, D_HEAD), F32),
        ],
        scratch_shapes=[pltpu.VMEM((rows, D_HGRN), F32)],
        input_output_aliases={12: 0},
        compiler_params=pltpu.CompilerParams(
            dimension_semantics=("arbitrary",), vmem_limit_bytes=VMEM_LIMIT_BYTES),
        name="mix_sample",
    )(x.reshape(nb * length, D_MODEL), conv_state, hgrn_state, lbl, nw, w_in, cw, cb, gnw, w_out, cmat, masks,
      x1_flat)


def _router_kernel(x1_ref, nw_ref, rw_hi_ref, rw_lo_ref, rb_ref, ltri_ref, utri_ref, xn_ref, meta_ref, cnt_ref):
    n = x1_ref.shape[0]
    xn = _rmsnorm(x1_ref[...], nw_ref[...])
    xn_ref[...] = xn.astype(BF16)
    x_hi, x_lo = _split2(xn)
    logits = (_dot(x_hi, rw_hi_ref[...]) + _dot(x_lo, rw_hi_ref[...]) + _dot(x_hi, rw_lo_ref[...])
              + rb_ref[...])
    lane = lax.broadcasted_iota(jnp.int32, (n, LANES), 1).astype(F32)
    work = logits
    vals, ids = [], []
    for _ in range(TOP_K):
        m = jnp.max(work, axis=-1, keepdims=True)
        i = jnp.min(jnp.where(work == m, lane, float(LANES)), axis=-1, keepdims=True)
        vals.append(m)
        ids.append(i)
        work = jnp.where(lane == i, -jnp.inf, work)
    es = [jnp.exp(v - vals[0]) for v in vals]
    den = es[0] + es[1] + es[2] + es[3]
    gates = [e / den for e in es]

    onehots = [(lane == i) for i in ids]
    multi = jnp.zeros((n, LANES), F32)
    for oh in onehots:
        multi = multi + oh.astype(F32)
    counts = jnp.sum(multi, axis=0, keepdims=True)
    before = _dot(ltri_ref[...], multi.astype(BF16))
    seg = jnp.ceil(counts * (1.0 / SEG_ALIGN)) * SEG_ALIGN
    seg_rows = jnp.broadcast_to(seg, (BF16_ROWS, LANES)).astype(BF16)
    seg_off = _dot(seg_rows, utri_ref[...])[0:1, :]
    slot_of = seg_off + before
    meta = jnp.zeros((n, LANES), F32)
    for k in range(TOP_K):
        slot = jnp.sum(jnp.where(onehots[k], slot_of, 0.0), axis=-1, keepdims=True)
        meta = jnp.where(lane == k, ids[k], meta)
        meta = jnp.where(lane == TOP_K + k, gates[k], meta)
        meta = jnp.where(lane == 2 * TOP_K + k, slot, meta)
    meta_ref[...] = meta
    cnt_ref[...] = jnp.broadcast_to(counts, (SUBLANES, LANES))


def _router(x1, nw, router_w, router_b):
    t = x1.shape[0]
    n_tiles = t // TOKEN_TILE
    rw = jnp.zeros((D_MODEL, LANES), F32).at[:, :N_EXPERTS].set(router_w)
    rw_hi, rw_lo = _split2(rw)
    rb = jnp.full((1, LANES), -1e30, F32).at[0, :N_EXPERTS].set(router_b)
    idx = np.arange(TOKEN_TILE)
    ltri = jnp.asarray(idx[None, :] < idx[:, None], BF16)
    lid = np.arange(LANES)
    utri = jnp.asarray(lid[:, None] < lid[None, :], BF16)
    const = lambda shape: pl.BlockSpec(shape, lambda i: (0,) * len(shape))
    return pl.pallas_call(
        _router_kernel,
        grid=(n_tiles,),
        in_specs=[
            pl.BlockSpec((TOKEN_TILE, D_MODEL), lambda i: (i, 0)),
            const((1, D_MODEL)), const((D_MODEL, LANES)), const((D_MODEL, LANES)), const((1, LANES)),
            const((TOKEN_TILE, TOKEN_TILE)), const((LANES, LANES)),
        ],
        out_specs=[
            pl.BlockSpec((TOKEN_TILE, D_MODEL), lambda i: (i, 0)),
            pl.BlockSpec((TOKEN_TILE, LANES), lambda i: (i, 0)),
            pl.BlockSpec((None, SUBLANES, LANES), lambda i: (i, 0, 0)),
        ],
        out_shape=[
            jax.ShapeDtypeStruct((t, D_MODEL), BF16),
            jax.ShapeDtypeStruct((t, LANES), F32),
            jax.ShapeDtypeStruct((n_tiles, SUBLANES, LANES), F32),
        ],
        compiler_params=pltpu.CompilerParams(
            dimension_semantics=("arbitrary",), vmem_limit_bytes=VMEM_LIMIT_BYTES),
        name="router",
    )(x1, nw, rw_hi, rw_lo, rb, ltri, utri)


def _segment_tables(cnt):
    seg = (cnt + SEG_ALIGN - 1) // SEG_ALIGN * SEG_ALIGN
    src = jnp.cumsum(seg, axis=1) - seg
    tot = jnp.sum(seg, axis=0)
    cap = (tot + ROW_TILE - 1) // ROW_TILE * ROW_TILE
    base = jnp.cumsum(cap) - cap
    dst = base[None, :] + jnp.cumsum(seg, axis=0) - seg
    n_used = jnp.sum(cap) // ROW_TILE
    fill = base + tot
    lane = jnp.arange(N_EXPERTS)
    counts, srcs, dsts = [], [], []
    for size in SEG_SIZES:
        has = (seg & size) != 0
        done = seg & ~(2 * size - 1)
        place = has[:, :, None] & ((jnp.cumsum(has, axis=1) - 1)[:, :, None] == lane[None, None, :])
        counts.append(jnp.sum(has, axis=1))
        srcs.append(jnp.sum(jnp.where(place, (src + done)[:, :, None], 0), axis=1))
        dsts.append(jnp.sum(jnp.where(place, (dst + done)[:, :, None], 0), axis=1))
    i32 = lambda a: a.astype(jnp.int32).reshape(-1)
    chunks = (i32(jnp.stack(counts, axis=1)), i32(jnp.stack(srcs, axis=1)), i32(jnp.stack(dsts, axis=1)))
    return chunks, i32(fill), i32(cap - tot), i32(base), i32(cap // ROW_TILE), i32(n_used)


def _for_each_chunk(n, fn):
    for size in SEG_SIZES:
        done = n & ~(2 * size - 1)

        @pl.when((n & size) != 0)
        def _():
            fn(done, size)


def _segment_copies(cnt_ref, src_ref, dst_ref, tile, make_copy, start):
    for si, size in enumerate(SEG_SIZES):
        first = (tile * len(SEG_SIZES) + si) * N_EXPERTS

        def one(p, carry, si=si, size=size, first=first):
            copy = make_copy(pl.multiple_of(src_ref[first + p], SEG_ALIGN),
                             pl.multiple_of(dst_ref[first + p], SEG_ALIGN), size)
            if start:
                copy.start(priority=si % 2)
            else:
                copy.wait()
            return carry
        lax.fori_loop(0, cnt_ref[tile * len(SEG_SIZES) + si], one, 0)


SORT_BLOCK = 512


def _slot_matrix(slots, block, gates):
    row = (lax.broadcasted_iota(jnp.int32, (SORT_BLOCK, TOKEN_TILE), 0) + block * SORT_BLOCK).astype(F32)
    m = jnp.zeros((SORT_BLOCK, TOKEN_TILE), F32)
    for k in range(TOP_K):
        m = jnp.where(row == slots[k:k + 1, :], 1.0 if gates is None else gates[k:k + 1, :], m)
    return m.astype(BF16)


def _dispatch_kernel(seg_ref, src_ref, dst_ref, fill_ref, gap_ref, nu_ref, slot_ref, xn_ref, xs_ref, sorted_ref,
                     zero_ref, sem):
    tile = pl.program_id(0)

    @pl.when(tile == 0)
    def _():
        zero_ref[...] = jnp.zeros_like(zero_ref)

        def fill_copy(row, size):
            return pltpu.make_async_copy(zero_ref.at[pl.ds(0, size)],
                                         xs_ref.at[pl.ds(pl.multiple_of(row, SEG_ALIGN), size)], sem)

        def gaps(act):
            def per_expert(e, carry):
                _for_each_chunk(gap_ref[e], lambda done, size: act(fill_copy(fill_ref[e] + done, size)))
                return carry
            lax.fori_loop(0, N_EXPERTS, per_expert, 0)

        def tail(act):
            n_tail = xs_ref.shape[0] // ROW_TILE - nu_ref[0]
            lax.fori_loop(0, n_tail, lambda i, c: (act(fill_copy((nu_ref[0] + i) * ROW_TILE, ROW_TILE)), c)[1], 0)

        gaps(lambda c: c.start())
        tail(lambda c: c.start())
        gaps(lambda c: c.wait())
        tail(lambda c: c.wait())

    slots = slot_ref[...]
    xn = xn_ref[...]
    for rb in range(TILE_CAP // SORT_BLOCK):
        sorted_ref[rb * SORT_BLOCK:(rb + 1) * SORT_BLOCK, :] = _dot(
            _slot_matrix(slots, rb, None), xn).astype(BF16)

    def make_copy(src, dst, size):
        return pltpu.make_async_copy(sorted_ref.at[pl.ds(src, size)], xs_ref.at[pl.ds(dst, size)], sem)

    _segment_copies(seg_ref, src_ref, dst_ref, tile, make_copy, start=True)
    _segment_copies(seg_ref, src_ref, dst_ref, tile, make_copy, start=False)


def _dispatch(tables, slot_t, xn, n_rows_total):
    t = xn.shape[0]
    n_tiles = t // TOKEN_TILE
    return pl.pallas_call(
        _dispatch_kernel,
        grid_spec=pltpu.PrefetchScalarGridSpec(
            num_scalar_prefetch=len(tables),
            grid=(n_tiles,),
            in_specs=[
                pl.BlockSpec((SUBLANES, TOKEN_TILE), lambda i, *_: (0, i)),
                pl.BlockSpec((TOKEN_TILE, D_MODEL), lambda i, *_: (i, 0)),
            ],
            out_specs=pl.BlockSpec(memory_space=pl.ANY),
            scratch_shapes=[
                pltpu.VMEM((TILE_CAP, D_MODEL), BF16),
                pltpu.VMEM((ROW_TILE, D_MODEL), BF16),
                pltpu.SemaphoreType.DMA(()),
            ],
        ),
        out_shape=jax.ShapeDtypeStruct((n_rows_total, D_MODEL), BF16),
        compiler_params=pltpu.CompilerParams(
            dimension_semantics=("arbitrary",), vmem_limit_bytes=VMEM_LIMIT_BYTES),
        name="dispatch",
    )(*tables, slot_t, xn)


X_SLOTS = 3
W_FETCH_STEPS = (0, 1, 3)


def _experts_kernel(base_ref, nt_ref, nu_ref, bg_ref, bu_ref, bd_ref, wg_hbm, wu_hbm, wd_hbm, xs_ref, zs_ref,
                    w_f32, wg_bf, wu_bf, wd_bf, xbuf, hbuf, zbuf, w_sem, in_sem, out_sem):
    e = pl.program_id(0)
    n_e = pl.num_programs(0)
    n = nt_ref[e]
    base = base_ref[e]
    w_slot = e % 2
    has_next = e + 1 < n_e

    def w_copy(j, expert, slot):
        w_hbm = (wg_hbm, wu_hbm, wd_hbm)[j]
        return pltpu.make_async_copy(w_hbm.at[expert], w_f32.at[slot, j], w_sem.at[slot])

    def x_copy(i):
        rows = pl.ds(pl.multiple_of(base + i * ROW_TILE, ROW_TILE), ROW_TILE)
        return pltpu.make_async_copy(xs_ref.at[rows], xbuf.at[i % X_SLOTS], in_sem.at[i % X_SLOTS])

    def z_copy(row, slot):
        rows = pl.ds(pl.multiple_of(row, ROW_TILE), ROW_TILE)
        return pltpu.make_async_copy(zbuf.at[slot], zs_ref.at[rows], out_sem.at[slot])

    def hidden(i):
        x = xbuf[i % X_SLOTS]
        a = jnp.minimum(_dot(x, wg_bf[...]) + bg_ref[...], SWIGLU_LIMIT)
        u = jnp.clip(_dot(x, wu_bf[...]) + bu_ref[...], -SWIGLU_LIMIT, SWIGLU_LIMIT)
        hbuf[i % 2] = ((u + 1.0) * a * jax.nn.sigmoid(SWIGLU_ALPHA * a)).astype(BF16)

    def project_down(i, slot):
        @pl.when(i >= 2)
        def _():
            z_copy(base, slot).wait()
        zbuf[slot] = (_dot(hbuf[slot], wd_bf[...]) + bd_ref[...]).astype(BF16)

    @pl.when(e == 0)
    def _():
        for j in range(3):
            w_copy(j, 0, 0).start()

    for i in range(X_SLOTS):
        @pl.when(i < n)
        def _():
            x_copy(i).start(priority=1)

    def fetch_next_weights(j):
        @pl.when(has_next)
        def _():
            w_copy(j, e + 1, 1 - w_slot).start()

    fetch_next_weights(0)
    for j in range(3):
        w_copy(j, e, w_slot).wait()

    @pl.when(n > 0)
    def _():
        wg_bf[...] = w_f32[w_slot, 0].astype(BF16)
        wu_bf[...] = w_f32[w_slot, 1].astype(BF16)
        wd_bf[...] = w_f32[w_slot, 2].astype(BF16)
        x_copy(0).wait()
        hidden(0)

        def step(i, carry):
            x_copy(i + 1).wait()

            @pl.when(i + X_SLOTS < n)
            def _():
                x_copy(i + X_SLOTS).start(priority=1)

            for j in (1, 2):
                @pl.when(i == W_FETCH_STEPS[j])
                def _():
                    fetch_next_weights(j)

            project_down(i, i % 2)
            hidden(i + 1)
            z_copy(base + i * ROW_TILE, i % 2).start()
            return carry

        lax.fori_loop(0, n - 1, step, 0)
        project_down(n - 1, (n - 1) % 2)
        z_copy(base + (n - 1) * ROW_TILE, (n - 1) % 2).start()

        @pl.when(n >= 2)
        def _():
            z_copy(base, n % 2).wait()
        z_copy(base, (n - 1) % 2).wait()

    for j in (1, 2):
        @pl.when(jnp.maximum(n - 1, 0) <= W_FETCH_STEPS[j])
        def _():
            fetch_next_weights(j)

    @pl.when(e == pl.num_programs(0) - 1)
    def _():
        zbuf[0] = jnp.zeros((ROW_TILE, D_MODEL), BF16)
        n_tail = zs_ref.shape[0] // ROW_TILE - nu_ref[0]
        tail = lambda act: lax.fori_loop(
            0, n_tail, lambda i, c: (act(z_copy((nu_ref[0] + i) * ROW_TILE, 0)), c)[1], 0)
        tail(lambda c: c.start())
        tail(lambda c: c.wait())


def _experts(base, n_tiles, n_used, xs, w_gate, b_gate, w_up, b_up, w_down, b_down):
    b_spec = pl.BlockSpec((None, 1, D_MODEL), lambda e, *_: (e, 0, 0))
    any_spec = pl.BlockSpec(memory_space=pl.ANY)
    return pl.pallas_call(
        _experts_kernel,
        grid_spec=pltpu.PrefetchScalarGridSpec(
            num_scalar_prefetch=3,
            grid=(N_EXPERTS,),
            in_specs=[b_spec, b_spec, b_spec, any_spec, any_spec, any_spec, any_spec],
            out_specs=any_spec,
            scratch_shapes=[pltpu.VMEM((2, 3, D_MODEL, D_MODEL), F32)] + [
                pltpu.VMEM((D_MODEL, D_MODEL), BF16) for _ in range(3)] + [
                pltpu.VMEM((X_SLOTS, ROW_TILE, D_MODEL), BF16)] + [
                pltpu.VMEM((2, ROW_TILE, D_MODEL), BF16) for _ in range(2)] + [
                pltpu.SemaphoreType.DMA((2,)), pltpu.SemaphoreType.DMA((X_SLOTS,)), pltpu.SemaphoreType.DMA((2,))],
        ),
        out_shape=jax.ShapeDtypeStruct(xs.shape, BF16),
        compiler_params=pltpu.CompilerParams(
            dimension_semantics=("arbitrary",), vmem_limit_bytes=VMEM_LIMIT_BYTES),
        name="experts",
    )(base, n_tiles, n_used, b_gate[:, None, :], b_up[:, None, :], b_down[:, None, :], w_gate, w_up, w_down, xs)


def _combine_kernel(n_first, seg_ref, src_ref, dst_ref, route_ref, x1_ref, fw_ref, zs_ref, ya_ref, yb_ref,
                    sorted_ref, sem):
    tile = pl.program_id(0)
    sorted_ref[...] = jnp.zeros_like(sorted_ref)

    def make_copy(src, dst, size):
        return pltpu.make_async_copy(zs_ref.at[pl.ds(dst, size)], sorted_ref.at[pl.ds(src, size)], sem)

    _segment_copies(seg_ref, src_ref, dst_ref, tile, make_copy, start=True)
    route = route_ref[...]
    slots, gates = route[0:TOP_K, :], route[TOP_K:2 * TOP_K, :]
    n_blocks = TILE_CAP // SORT_BLOCK
    early = [_slot_matrix(slots, cb, gates) for cb in range(2)]
    _segment_copies(seg_ref, src_ref, dst_ref, tile, make_copy, start=False)
    moe = jnp.zeros((TOKEN_TILE, D_MODEL), F32)
    for cb in range(n_blocks):
        m = early[cb] if cb < len(early) else _slot_matrix(slots, cb, gates)
        moe = moe + _dot_tn(m, sorted_ref[cb * SORT_BLOCK:(cb + 1) * SORT_BLOCK, :])
    y = _rmsnorm(x1_ref[...] + moe, fw_ref[...])

    @pl.when(tile < n_first)
    def _():
        ya_ref[...] = y

    @pl.when(tile >= n_first)
    def _():
        yb_ref[...] = y


def _combine(tables, route_t, x1, final_w, zs, rows_first):
    seg, src, dst = tables
    t = x1.shape[0]
    assert rows_first % TOKEN_TILE == 0 and 0 < rows_first < t and TILE_CAP % SORT_BLOCK == 0
    n_tiles = t // TOKEN_TILE
    n_first = rows_first // TOKEN_TILE
    return pl.pallas_call(
        functools.partial(_combine_kernel, n_first),
        grid_spec=pltpu.PrefetchScalarGridSpec(
            num_scalar_prefetch=3,
            grid=(n_tiles,),
            in_specs=[
                pl.BlockSpec((SUBLANES, TOKEN_TILE), lambda i, *_: (0, i)),
                pl.BlockSpec((TOKEN_TILE, D_MODEL), lambda i, *_: (i, 0)),
                pl.BlockSpec((1, D_MODEL), lambda i, *_: (0, 0)),
                pl.BlockSpec(memory_space=pl.ANY),
            ],
            out_specs=[
                pl.BlockSpec((TOKEN_TILE, D_MODEL), lambda i, *_: (jnp.minimum(i, n_first - 1), 0)),
                pl.BlockSpec((TOKEN_TILE, D_MODEL), lambda i, *_: (jnp.maximum(i - n_first, 0), 0)),
            ],
            scratch_shapes=[pltpu.VMEM((TILE_CAP, D_MODEL), BF16), pltpu.SemaphoreType.DMA(())],
        ),
        out_shape=[jax.ShapeDtypeStruct((rows_first, D_MODEL), F32),
                   jax.ShapeDtypeStruct((t - rows_first, D_MODEL), F32)],
        compiler_params=pltpu.CompilerParams(
            dimension_semantics=("arbitrary",), vmem_limit_bytes=VMEM_LIMIT_BYTES),
        name="combine",
    )(seg, src, dst, route_t, x1, final_w, zs)


def _moe_and_final_norm(x1, rows_first, norm_ffn_w, router_w, router_b, w_gate, b_gate, w_up, b_up, w_down, b_down,
                        final_w):
    t = x1.shape[0]
    assert t % TOKEN_TILE == 0
    n_tiles = t // TOKEN_TILE
    n_rows_total = TOP_K * t + n_tiles * N_EXPERTS * (SEG_ALIGN - 1) + N_EXPERTS * (ROW_TILE - 1)
    n_rows_total = (n_rows_total + ROW_TILE - 1) // ROW_TILE * ROW_TILE
    xn, meta, cnt = _router(x1, norm_ffn_w.reshape(1, D_MODEL), router_w, router_b)
    cnt = cnt[:, 0, :N_EXPERTS].astype(jnp.int32)
    chunks, fill, gap, base, n_row_tiles, n_used = _segment_tables(cnt)
    route_t = jnp.concatenate([meta[:, 2 * TOP_K:3 * TOP_K], meta[:, TOP_K:2 * TOP_K]], axis=1).T
    xs = _dispatch((*chunks, fill, gap, n_used), route_t, xn, n_rows_total)
    zs = _experts(base, n_row_tiles, n_used, xs, w_gate, b_gate, w_up, b_up, w_down, b_down)
    return _combine(chunks, route_t, x1, final_w.reshape(1, D_MODEL), zs, rows_first)


def kernel(x_prompt, x_sample, state_conv, state_hgrn, lb_logits, norm_mix_w, w_in, conv_w, conv_b, gnorm_w,
           w_out, norm_ffn_w, router_w, router_b, w_gate, b_gate, w_up, b_up, w_down, b_down, final_norm_w):
    assert norm_mix_w.shape[0] == 1 and lb_logits.shape[0] == 2, "single-layer step"
    b, l, _ = x_prompt.shape
    nb, ls, _ = x_sample.shape
    lbl = lb_logits.astype(F32)
    nw = norm_mix_w[0].reshape(1, D_MODEL)
    w_in_bf = w_in[0].astype(BF16)
    w_out_bf = w_out[0].astype(BF16)
    cw, cb = conv_w[0], conv_b[0].reshape(1, D_CONV)
    gnw = gnorm_w[0].reshape(1, D_HGRN)
    rows_p, rows_s = b * l, nb * ls
    x1, conv_p, hgrn_p = _mix_prompt(x_prompt, lbl, nw, w_in_bf, cw, cb, gnw, w_out_bf, rows_p + rows_s)
    x1, conv_s, hgrn_s = _mix_sample(x_sample, state_conv[0], state_hgrn[0], lbl, nw, w_in_bf, cw, cb, gnw,
                                     w_out_bf, x1, rows_p)
    y_p, y_s = _moe_and_final_norm(x1, rows_p, norm_ffn_w[0], router_w[0], router_b[0], w_gate[0], b_gate[0],
                                   w_up[0], b_up[0], w_down[0], b_down[0], final_norm_w)
    conv_p = conv_p[:, SUBLANES - 2:, :]
    conv_s = conv_s[:, ls - 2:, :]
    return (y_p.reshape(b, l, D_MODEL), y_s.reshape(nb, ls, D_MODEL), conv_p[None], hgrn_p[None], conv_s[None],
            hgrn_s[None])
```

```python
import functools

import numpy as np
import jax
import jax.numpy as jnp
from jax import lax
from jax.experimental import pallas as pl
from jax.experimental.pallas import tpu as pltpu

F32 = jnp.float32
BF16 = jnp.bfloat16

D_MODEL = 1024
D_CONV = 512
D_HGRN = 512
N_HEADS = 4
D_HEAD = 128
D_PROJ = 3 * D_CONV + 4 * D_HGRN
N_EXPERTS = 32
TOP_K = 4
SWIGLU_LIMIT = 7.0
SWIGLU_ALPHA = 1.702
EPS = 1e-6
PROMPT_CHUNK = 64

LANES = 128
SUBLANES = 8
BF16_ROWS = 16
VMEM_LIMIT_BYTES = 56 * 1024 * 1024

MIX_SLAB = 128
MIX_ROWS = 512
MIX_PART = 256
MIX_SEQS = 16
TOKEN_TILE = 512
SEG_ALIGN = BF16_ROWS
ROW_TILE = 512
SEG_SIZES = tuple(SEG_ALIGN << i for i in reversed(range(6)))
TILE_CAP = TOP_K * TOKEN_TILE + N_EXPERTS * SEG_ALIGN

NT_DIMS = (((1,), (1,)), ((), ()))
TN_DIMS = (((0,), (0,)), ((), ()))


def _dot(a, b):
    return jnp.dot(a, b, preferred_element_type=F32)


def _dot_nt(a, b):
    return lax.dot_general(a, b, NT_DIMS, preferred_element_type=F32)


def _dot_tn(a, b):
    return lax.dot_general(a, b, TN_DIMS, preferred_element_type=F32)


def _split2(x):
    hi = x.astype(BF16)
    lo = (x - hi.astype(F32)).astype(BF16)
    return hi, lo


def _rmsnorm(x, w):
    return x * lax.rsqrt(jnp.mean(x * x, axis=-1, keepdims=True) + EPS) * w


def _level_exponent_matrix(n, chunk, h):
    x = np.zeros((n, n), np.float32)
    for t in range(n):
        base = t - t % (2 * h)
        m = base + h - 1
        if t % (2 * h) >= h:
            x[t, m + 1:t + 1] = 1.0
        else:
            x[t, t + 1:m + 1] = 1.0
    return x


def _mix_constants(n, chunk, mxu_levels, all_levels):
    t = np.arange(n)
    same_chunk = (t[:, None] // chunk) == (t[None, :] // chunk)
    tri = (same_chunk & (t[None, :] <= t[:, None])).astype(np.float32)
    suf = (same_chunk & (t[None, :] > t[:, None])).astype(np.float32)
    cmat = np.concatenate([tri, suf] + [_level_exponent_matrix(n, chunk, h) for h in mxu_levels], axis=0)
    masks = []
    for h in all_levels:
        blk = (t[:, None] // (2 * h)) == (t[None, :] // (2 * h))
        masks.append((blk & ((t[:, None] % (2 * h)) >= h) & ((t[None, :] % (2 * h)) < h)).astype(np.float32))
    masks.append(np.eye(n, dtype=np.float32))
    return jnp.asarray(cmat, BF16), jnp.asarray(np.stack(masks), F32)


def _vpu_level_exponent(a, h):
    n = a.shape[0]
    pieces = []
    for j in range(n // (2 * h)):
        b = j * 2 * h
        ref = a[b + h - 1:b + h, :]
        pieces.append(ref - a[b:b + h, :])
        pieces.append(a[b + h:b + 2 * h, :] - ref)
    return jnp.concatenate(pieces, axis=0)


def _chunk_prefix_sums(x, chunk):
    n, width = x.shape
    x3 = x.reshape(n // SUBLANES, SUBLANES, width)
    pos = lax.broadcasted_iota(jnp.int32, (1, SUBLANES, 1), 1)
    d = 1
    while d < SUBLANES:
        x3 = x3 + jnp.where(pos >= d, pltpu.roll(x3, d, 1), 0.0)
        d *= 2
    x = x3.reshape(n, width)
    pieces = []
    for b in range(0, n, SUBLANES):
        blk = x[b:b + SUBLANES, :]
        if b % chunk:
            blk = blk + pieces[-1][SUBLANES - 1:SUBLANES, :]
        pieces.append(blk)
    return jnp.concatenate(pieces, axis=0)


def _small_level_exponent(a, logf, h):
    n = a.shape[0]
    pos = lax.broadcasted_iota(jnp.int32, (SUBLANES, 1), 0)
    if h == 1:
        odd = lax.broadcasted_iota(jnp.int32, (n, 1), 0) % 2 == 1
        return jnp.where(odd, logf, 0.0)
    pieces = []
    for b in range(0, n, SUBLANES):
        blk = a[b:b + SUBLANES, :]
        if h == 4:
            ref = blk[3:4, :]
        else:
            ref = jnp.where(pos < 4, blk[1:2, :], blk[5:6, :])
        pieces.append(jnp.where(pos % (2 * h) >= h, blk - ref, ref - blk))
    return jnp.concatenate(pieces, axis=0)


def _forget_lower_bound(lbl):
    m = jnp.max(lbl, axis=0, keepdims=True)
    e = jnp.exp(lbl - m)
    return e[0:1, :] / jnp.sum(e, axis=0, keepdims=True)


def _projections(x, nw, w_in_ref):
    h = _rmsnorm(x, nw).astype(BF16)
    return _dot(h, w_in_ref[...])


def _gates(proj, lb):
    q = proj[:, 3 * D_CONV:3 * D_CONV + D_HGRN]
    ff = proj[:, 3 * D_CONV + D_HGRN:3 * D_CONV + 2 * D_HGRN]
    vi = proj[:, 3 * D_CONV + 2 * D_HGRN:3 * D_CONV + 3 * D_HGRN]
    g = proj[:, 3 * D_CONV + 3 * D_HGRN:]
    e = jnp.exp(-jnp.abs(ff))
    r = 1.0 / (1.0 + e)
    pos = ff >= 0
    sig = jnp.where(pos, r, e * r)
    sig_neg = jnp.where(pos, e * r, r)
    logf = jnp.log(lb + (1.0 - lb) * sig)
    kk = (1.0 - lb) * sig_neg
    return q, logf, kk, vi, g


def _intra_scores_times_v(q, kk, vi, exps, masks_ref):
    n = q.shape[0]
    sc = [jnp.zeros((n, n), F32) for _ in range(N_HEADS)]
    for l, ex in enumerate(exps):
        if ex is None:
            qh, kh = q.astype(BF16), kk.astype(BF16)
        else:
            w = jnp.exp(ex)
            qh, kh = (q * w).astype(BF16), (kk * w).astype(BF16)
        mask = masks_ref[l]
        for hd in range(N_HEADS):
            hs = slice(hd * D_HEAD, (hd + 1) * D_HEAD)
            sc[hd] = sc[hd] + mask * _dot_nt(qh[:, hs], kh[:, hs])
    vb = vi.astype(BF16)
    return [_dot(sc[hd].astype(BF16), vb[:, hd * D_HEAD:(hd + 1) * D_HEAD]) for hd in range(N_HEADS)]


def _head_out(o, g, gnw):
    parts = []
    for hd in range(N_HEADS):
        oh = o[:, hd * D_HEAD:(hd + 1) * D_HEAD]
        parts.append(oh * lax.rsqrt(jnp.mean(oh * oh, axis=-1, keepdims=True) + EPS))
    on = jnp.concatenate(parts, axis=1)
    return on * gnw * (g * jax.nn.sigmoid(g))


def _pad_rows_bf16(a):
    return jnp.concatenate([a, jnp.zeros_like(a)], axis=0).astype(BF16)


PROMPT_VPU_LEVELS = (32, 16, 8)
MXU_LEVELS = (4, 2, 1)


def _mix_prompt_kernel(n_r, n_steps, *refs):
    step = pl.program_id(0)

    @pl.when(step < n_steps)
    def _():
        _mix_prompt_step(step % n_r, n_r, *refs)

    @pl.when(step >= n_steps)
    def _():
        x1_ref = refs[N_MIX_PROMPT_INPUTS]
        x1_ref[...] = jnp.zeros_like(x1_ref)


N_MIX_PROMPT_INPUTS = 9


def _mix_prompt_step(r, n_r, x_ref, lbl_ref, nw_ref, win_ref, cw_ref, cb_ref, gnw_ref, wout_ref, masks_ref,
                     x1_ref, nconv_ref, nstate_ref, st_ref, tail_ref):
    rows = x_ref.shape[0]

    @pl.when(r == 0)
    def _():
        st_ref[...] = jnp.zeros_like(st_ref)
        tail_ref[...] = jnp.zeros_like(tail_ref)

    lb = _forget_lower_bound(lbl_ref[...])
    cw = cw_ref[...]
    n_chunks = MIX_SLAB // PROMPT_CHUNK
    rid = lax.broadcasted_iota(jnp.int32, (MIX_PART, 1), 0)
    t0 = tail_ref[SUBLANES - 2:SUBLANES - 1, :]
    t1 = tail_ref[SUBLANES - 1:SUBLANES, :]
    states = [st_ref[hd] for hd in range(N_HEADS)]

    for p in range(rows // MIX_PART):
        pr = slice(p * MIX_PART, (p + 1) * MIX_PART)
        x = x_ref[pr, :]
        proj = _projections(x, nw_ref[...], win_ref)

        bg = proj[:, 0:D_CONV]
        u = proj[:, D_CONV:2 * D_CONV] * proj[:, 2 * D_CONV:3 * D_CONV]
        u1 = jnp.where(rid == 0, t1, pltpu.roll(u, 1, 0))
        u2 = jnp.where(rid == 0, t0, jnp.where(rid == 1, t1, pltpu.roll(u, 2, 0)))
        y_conv = bg * (cw[0:1, :] * u2 + cw[1:2, :] * u1 + cw[2:3, :] * u + cb_ref[...])
        u_last = u[MIX_PART - SUBLANES:MIX_PART, :]
        t0, t1 = u_last[SUBLANES - 2:SUBLANES - 1, :], u_last[SUBLANES - 1:SUBLANES, :]

        q, logf, kk, vi, g = _gates(proj, lb)
        o_parts = []
        for s in range(MIX_PART // MIX_SLAB):
            sl = slice(s * MIX_SLAB, (s + 1) * MIX_SLAB)
            qs, ks, vs = q[sl], kk[sl], vi[sl]
            a_pre = _chunk_prefix_sums(logf[sl], PROMPT_CHUNK)
            a_suf = jnp.concatenate(
                [a_pre[(c + 1) * PROMPT_CHUNK - 1:(c + 1) * PROMPT_CHUNK, :]
                 - a_pre[c * PROMPT_CHUNK:(c + 1) * PROMPT_CHUNK, :] for c in range(n_chunks)], axis=0)
            exps = [_vpu_level_exponent(a_pre, h) for h in PROMPT_VPU_LEVELS]
            exps += [_small_level_exponent(a_pre, logf[sl], h) for h in MXU_LEVELS]
            exps.append(None)
            intra = _intra_scores_times_v(qs, ks, vs, exps, masks_ref)
            ea = jnp.exp(a_pre)
            qa = (qs * ea).astype(BF16)
            kb = (ks * jnp.exp(a_suf)).astype(BF16)
            vb = vs.astype(BF16)
            for c in range(n_chunks):
                cr = slice(c * PROMPT_CHUNK, (c + 1) * PROMPT_CHUNK)
                last = (c + 1) * PROMPT_CHUNK - 1
                heads = []
                for hd in range(N_HEADS):
                    hs = slice(hd * D_HEAD, (hd + 1) * D_HEAD)
                    heads.append(_dot_nt(qa[cr, hs], states[hd].astype(BF16)) + intra[hd][cr, :])
                    states[hd] = states[hd] * ea[last:last + 1, hs] + _dot_tn(vb[cr, hs], kb[cr, hs])
                o_parts.append(jnp.concatenate(heads, axis=1))

        o = _head_out(jnp.concatenate(o_parts, axis=0), g, gnw_ref[...])
        mix_in = jnp.concatenate([y_conv, o], axis=1).astype(BF16)
        x1_ref[pr, :] = x + _dot(mix_in, wout_ref[...])

    tail_ref[...] = u_last
    for hd in range(N_HEADS):
        st_ref[hd] = states[hd]

    @pl.when(r == n_r - 1)
    def _():
        nconv_ref[...] = u_last
        for hd in range(N_HEADS):
            nstate_ref[hd] = states[hd].T


def _mix_prompt(x, lbl, nw, w_in, cw, cb, gnw, w_out, total_rows):
    b, l, _ = x.shape
    assert l % MIX_ROWS == 0 and MIX_ROWS % MIX_PART == 0 and MIX_PART % MIX_SLAB == 0 and l % PROMPT_CHUNK == 0
    assert total_rows % MIX_ROWS == 0
    n_r = l // MIX_ROWS
    n_steps = b * n_r
    seq = lambda s: jnp.minimum(s, n_steps - 1) // n_r
    _, masks = _mix_constants(MIX_SLAB, PROMPT_CHUNK, (), PROMPT_VPU_LEVELS + MXU_LEVELS)
    const = lambda shape: pl.BlockSpec(shape, lambda s: (0,) * len(shape))
    return pl.pallas_call(
        functools.partial(_mix_prompt_kernel, n_r, n_steps),
        grid=(total_rows // MIX_ROWS,),
        in_specs=[
            pl.BlockSpec((None, MIX_ROWS, D_MODEL), lambda s: (seq(s), jnp.minimum(s, n_steps - 1) % n_r, 0)),
            const((2, D_HGRN)), const((1, D_MODEL)), const((D_MODEL, D_PROJ)), const((3, D_CONV)),
            const((1, D_CONV)), const((1, D_HGRN)), const((D_MODEL, D_MODEL)), const(masks.shape),
        ],
        out_specs=[
            pl.BlockSpec((MIX_ROWS, D_MODEL), lambda s: (s, 0)),
            pl.BlockSpec((None, SUBLANES, D_CONV), lambda s: (seq(s), 0, 0)),
            pl.BlockSpec((None, N_HEADS, D_HEAD, D_HEAD), lambda s: (seq(s), 0, 0, 0)),
        ],
        out_shape=[
            jax.ShapeDtypeStruct((total_rows, D_MODEL), F32),
            jax.ShapeDtypeStruct((b, SUBLANES, D_CONV), F32),
            jax.ShapeDtypeStruct((b, N_HEADS, D_HEAD, D_HEAD), F32),
        ],
        scratch_shapes=[
            pltpu.VMEM((N_HEADS, D_HEAD, D_HEAD), F32),
            pltpu.VMEM((SUBLANES, D_CONV), F32),
        ],
        compiler_params=pltpu.CompilerParams(
            dimension_semantics=("arbitrary",), vmem_limit_bytes=VMEM_LIMIT_BYTES),
        name="mix_prompt",
    )(x, lbl, nw, w_in, cw, cb, gnw, w_out, masks)


def _mix_sample_kernel(x_ref, cs_ref, hs_ref, lbl_ref, nw_ref, win_ref, cw_ref, cb_ref, gnw_ref, wout_ref,
                       cmat_ref, masks_ref, x1_in_ref, x1_ref, nconv_ref, nstate_ref, o_ref):
    del x1_in_ref
    nseq, length, _ = nconv_ref.shape
    rows = nseq * length
    x = x_ref[...]
    proj = _projections(x, nw_ref[...], win_ref)
    lb = _forget_lower_bound(lbl_ref[...])

    bg = proj[:, 0:D_CONV]
    u = proj[:, D_CONV:2 * D_CONV] * proj[:, 2 * D_CONV:3 * D_CONV]
    cs = cs_ref[...]
    expand = lambda a: jnp.broadcast_to(a, (nseq, length, D_CONV)).reshape(rows, D_CONV)
    t0 = expand(cs[:, 0:1, :])
    t1 = expand(cs[:, 1:2, :])
    pos = lax.broadcasted_iota(jnp.int32, (rows, 1), 0) % length
    u1 = jnp.where(pos == 0, t1, pltpu.roll(u, 1, 0))
    u2 = jnp.where(pos == 0, t0, jnp.where(pos == 1, t1, pltpu.roll(u, 2, 0)))
    cw = cw_ref[...]
    y_conv = bg * (cw[0:1, :] * u2 + cw[1:2, :] * u1 + cw[2:3, :] * u + cb_ref[...])
    nconv_ref[...] = u.reshape(nseq, length, D_CONV)

    q, logf, kk, vi, g = _gates(proj, lb)
    lf_hi, lf_lo = _split2(logf)
    ex = _dot(cmat_ref[...], lf_hi) + _dot(cmat_ref[...], lf_lo)
    a_pre = ex[0:rows]
    a_suf = ex[rows:2 * rows]
    exps = [ex[(2 + i) * rows:(3 + i) * rows] for i in range(len(MXU_LEVELS))] + [None]
    intra = _intra_scores_times_v(q, kk, vi, exps, masks_ref)
    ea = jnp.exp(a_pre)
    qa = q * ea
    kb = kk * jnp.exp(a_suf)
    for s in range(nseq):
        cr = slice(s * length, (s + 1) * length)
        last = (s + 1) * length - 1
        for hd in range(N_HEADS):
            hs = slice(hd * D_HEAD, (hd + 1) * D_HEAD)
            st = hs_ref[s, hd].T
            inter = _dot_nt(_pad_rows_bf16(qa[cr, hs]), st.astype(BF16))[0:length, :]
            o_ref[cr, hs] = inter + intra[hd][cr, :]
            st_new = st * ea[last:last + 1, hs] + _dot_tn(_pad_rows_bf16(vi[cr, hs]), _pad_rows_bf16(kb[cr, hs]))
            nstate_ref[s, hd] = st_new.T

    o = _head_out(o_ref[...], g, gnw_ref[...])
    mix_in = jnp.concatenate([y_conv, o], axis=1).astype(BF16)
    x1_ref[...] = x + _dot(mix_in, wout_ref[...])


def _mix_sample(x, conv_state, hgrn_state, lbl, nw, w_in, cw, cb, gnw, w_out, x1_flat, row_offset):
    nb, length, _ = x.shape
    rows = MIX_SEQS * length
    assert nb % MIX_SEQS == 0 and length == SUBLANES and row_offset % rows == 0
    first_block = row_offset // rows
    cmat, masks = _mix_constants(rows, length, MXU_LEVELS, MXU_LEVELS)
    const = lambda shape: pl.BlockSpec(shape, lambda i: (0,) * len(shape))
    return pl.pallas_call(
        _mix_sample_kernel,
        grid=(nb // MIX_SEQS,),
        in_specs=[
            pl.BlockSpec((rows, D_MODEL), lambda i: (i, 0)),
            pl.BlockSpec((MIX_SEQS, 2, D_CONV), lambda i: (i, 0, 0)),
            pl.BlockSpec((MIX_SEQS, N_HEADS, D_HEAD, D_HEAD), lambda i: (i, 0, 0, 0)),
            const((2, D_HGRN)), const((1, D_MODEL)), const((D_MODEL, D_PROJ)), const((3, D_CONV)),
            const((1, D_CONV)), const((1, D_HGRN)), const((D_MODEL, D_MODEL)), const(cmat.shape),
            const(masks.shape), pl.BlockSpec(memory_space=pl.ANY),
        ],
        out_specs=[
            pl.BlockSpec((rows, D_MODEL), lambda i: (first_block + i, 0)),
            pl.BlockSpec((MIX_SEQS, length, D_CONV), lambda i: (i, 0, 0)),
            pl.BlockSpec((MIX_SEQS, N_HEADS, D_HEAD, D_HEAD), lambda i: (i, 0, 0, 0)),
        ],
        out_shape=[
            jax.ShapeDtypeStruct(x1_flat.shape, F32),
            jax.ShapeDtypeStruct((nb, length, D_CONV), F32),
            jax.ShapeDtypeStruct((nb, N_HEADS, D_HEAD, D_HEAD), F32),
        ],
        scratch_shapes=[pltpu.VMEM((rows, D_HGRN), F32)],
        input_output_aliases={12: 0},
        compiler_params=pltpu.CompilerParams(
            dimension_semantics=("arbitrary",), vmem_limit_bytes=VMEM_LIMIT_BYTES),
        name="mix_sample",
    )(x.reshape(nb * length, D_MODEL), conv_state, hgrn_state, lbl, nw, w_in, cw, cb, gnw, w_out, cmat, masks,
      x1_flat)


def _router_kernel(x1_ref, nw_ref, rw_hi_ref, rw_lo_ref, rb_ref, ltri_ref, utri_ref, xn_ref, meta_ref, cnt_ref):
    n = x1_ref.shape[0]
    xn = _rmsnorm(x1_ref[...], nw_ref[...])
    xn_ref[...] = xn.astype(BF16)
    x_hi, x_lo = _split2(xn)
    logits = (_dot(x_hi, rw_hi_ref[...]) + _dot(x_lo, rw_hi_ref[...]) + _dot(x_hi, rw_lo_ref[...])
              + rb_ref[...])
    lane = lax.broadcasted_iota(jnp.int32, (n, LANES), 1).astype(F32)
    work = logits
    vals, ids = [], []
    for _ in range(TOP_K):
        m = jnp.max(work, axis=-1, keepdims=True)
        i = jnp.min(jnp.where(work == m, lane, float(LANES)), axis=-1, keepdims=True)
        vals.append(m)
        ids.append(i)
        work = jnp.where(lane == i, -jnp.inf, work)
    es = [jnp.exp(v - vals[0]) for v in vals]
    den = es[0] + es[1] + es[2] + es[3]
    gates = [e / den for e in es]

    onehots = [(lane == i) for i in ids]
    multi = jnp.zeros((n, LANES), F32)
    for oh in onehots:
        multi = multi + oh.astype(F32)
    counts = jnp.sum(multi, axis=0, keepdims=True)
    before = _dot(ltri_ref[...], multi.astype(BF16))
    seg = jnp.ceil(counts * (1.0 / SEG_ALIGN)) * SEG_ALIGN
    seg_rows = jnp.broadcast_to(seg, (BF16_ROWS, LANES)).astype(BF16)
    seg_off = _dot(seg_rows, utri_ref[...])[0:1, :]
    slot_of = seg_off + before
    meta = jnp.zeros((n, LANES), F32)
    for k in range(TOP_K):
        slot = jnp.sum(jnp.where(onehots[k], slot_of, 0.0), axis=-1, keepdims=True)
        meta = jnp.where(lane == k, ids[k], meta)
        meta = jnp.where(lane == TOP_K + k, gates[k], meta)
        meta = jnp.where(lane == 2 * TOP_K + k, slot, meta)
    meta_ref[...] = meta
    cnt_ref[...] = jnp.broadcast_to(counts, (SUBLANES, LANES))


def _router(x1, nw, router_w, router_b):
    t = x1.shape[0]
    n_tiles = t // TOKEN_TILE
    rw = jnp.zeros((D_MODEL, LANES), F32).at[:, :N_EXPERTS].set(router_w)
    rw_hi, rw_lo = _split2(rw)
    rb = jnp.full((1, LANES), -1e30, F32).at[0, :N_EXPERTS].set(router_b)
    idx = np.arange(TOKEN_TILE)
    ltri = jnp.asarray(idx[None, :] < idx[:, None], BF16)
    lid = np.arange(LANES)
    utri = jnp.asarray(lid[:, None] < lid[None, :], BF16)
    const = lambda shape: pl.BlockSpec(shape, lambda i: (0,) * len(shape))
    return pl.pallas_call(
        _router_kernel,
        grid=(n_tiles,),
        in_specs=[
            pl.BlockSpec((TOKEN_TILE, D_MODEL), lambda i: (i, 0)),
            const((1, D_MODEL)), const((D_MODEL, LANES)), const((D_MODEL, LANES)), const((1, LANES)),
            const((TOKEN_TILE, TOKEN_TILE)), const((LANES, LANES)),
        ],
        out_specs=[
            pl.BlockSpec((TOKEN_TILE, D_MODEL), lambda i: (i, 0)),
            pl.BlockSpec((TOKEN_TILE, LANES), lambda i: (i, 0)),
            pl.BlockSpec((None, SUBLANES, LANES), lambda i: (i, 0, 0)),
        ],
        out_shape=[
            jax.ShapeDtypeStruct((t, D_MODEL), BF16),
            jax.ShapeDtypeStruct((t, LANES), F32),
            jax.ShapeDtypeStruct((n_tiles, SUBLANES, LANES), F32),
        ],
        compiler_params=pltpu.CompilerParams(
            dimension_semantics=("arbitrary",), vmem_limit_bytes=VMEM_LIMIT_BYTES),
        name="router",
    )(x1, nw, rw_hi, rw_lo, rb, ltri, utri)


def _segment_tables(cnt):
    seg = (cnt + SEG_ALIGN - 1) // SEG_ALIGN * SEG_ALIGN
    src = jnp.cumsum(seg, axis=1) - seg
    tot = jnp.sum(seg, axis=0)
    cap = (tot + ROW_TILE - 1) // ROW_TILE * ROW_TILE
    base = jnp.cumsum(cap) - cap
    dst = base[None, :] + jnp.cumsum(seg, axis=0) - seg
    n_used = jnp.sum(cap) // ROW_TILE
    fill = base + tot
    lane = jnp.arange(N_EXPERTS)
    counts, srcs, dsts = [], [], []
    for size in SEG_SIZES:
        has = (seg & size) != 0
        done = seg & ~(2 * size - 1)
        place = has[:, :, None] & ((jnp.cumsum(has, axis=1) - 1)[:, :, None] == lane[None, None, :])
        counts.append(jnp.sum(has, axis=1))
        srcs.append(jnp.sum(jnp.where(place, (src + done)[:, :, None], 0), axis=1))
        dsts.append(jnp.sum(jnp.where(place, (dst + done)[:, :, None], 0), axis=1))
    i32 = lambda a: a.astype(jnp.int32).reshape(-1)
    chunks = (i32(jnp.stack(counts, axis=1)), i32(jnp.stack(srcs, axis=1)), i32(jnp.stack(dsts, axis=1)))
    return chunks, i32(fill), i32(cap - tot), i32(base), i32(cap // ROW_TILE), i32(n_used)


def _for_each_chunk(n, fn):
    for size in SEG_SIZES:
        done = n & ~(2 * size - 1)

        @pl.when((n & size) != 0)
        def _():
            fn(done, size)


def _segment_copies(cnt_ref, src_ref, dst_ref, tile, make_copy, start):
    for si, size in enumerate(SEG_SIZES):
        first = (tile * len(SEG_SIZES) + si) * N_EXPERTS

        def one(p, carry, si=si, size=size, first=first):
            copy = make_copy(pl.multiple_of(src_ref[first + p], SEG_ALIGN),
                             pl.multiple_of(dst_ref[first + p], SEG_ALIGN), size)
            if start:
                copy.start(priority=si % 2)
            else:
                copy.wait()
            return carry
        lax.fori_loop(0, cnt_ref[tile * len(SEG_SIZES) + si], one, 0)


SORT_BLOCK = 512


def _slot_matrix(slots, block, gates):
    row = (lax.broadcasted_iota(jnp.int32, (SORT_BLOCK, TOKEN_TILE), 0) + block * SORT_BLOCK).astype(F32)
    m = jnp.zeros((SORT_BLOCK, TOKEN_TILE), F32)
    for k in range(TOP_K):
        m = jnp.where(row == slots[k:k + 1, :], 1.0 if gates is None else gates[k:k + 1, :], m)
    return m.astype(BF16)


def _dispatch_kernel(seg_ref, src_ref, dst_ref, fill_ref, gap_ref, nu_ref, slot_ref, xn_ref, xs_ref, sorted_ref,
                     zero_ref, sem):
    tile = pl.program_id(0)

    @pl.when(tile == 0)
    def _():
        zero_ref[...] = jnp.zeros_like(zero_ref)

        def fill_copy(row, size):
            return pltpu.make_async_copy(zero_ref.at[pl.ds(0, size)],
                                         xs_ref.at[pl.ds(pl.multiple_of(row, SEG_ALIGN), size)], sem)

        def gaps(act):
            def per_expert(e, carry):
                _for_each_chunk(gap_ref[e], lambda done, size: act(fill_copy(fill_ref[e] + done, size)))
                return carry
            lax.fori_loop(0, N_EXPERTS, per_expert, 0)

        def tail(act):
            n_tail = xs_ref.shape[0] // ROW_TILE - nu_ref[0]
            lax.fori_loop(0, n_tail, lambda i, c: (act(fill_copy((nu_ref[0] + i) * ROW_TILE, ROW_TILE)), c)[1], 0)

        gaps(lambda c: c.start())
        tail(lambda c: c.start())
        gaps(lambda c: c.wait())
        tail(lambda c: c.wait())

    slots = slot_ref[...]
    xn = xn_ref[...]
    for rb in range(TILE_CAP // SORT_BLOCK):
        sorted_ref[rb * SORT_BLOCK:(rb + 1) * SORT_BLOCK, :] = _dot(
            _slot_matrix(slots, rb, None), xn).astype(BF16)

    def make_copy(src, dst, size):
        return pltpu.make_async_copy(sorted_ref.at[pl.ds(src, size)], xs_ref.at[pl.ds(dst, size)], sem)

    _segment_copies(seg_ref, src_ref, dst_ref, tile, make_copy, start=True)
    _segment_copies(seg_ref, src_ref, dst_ref, tile, make_copy, start=False)


def _dispatch(tables, slot_t, xn, n_rows_total):
    t = xn.shape[0]
    n_tiles = t // TOKEN_TILE
    return pl.pallas_call(
        _dispatch_kernel,
        grid_spec=pltpu.PrefetchScalarGridSpec(
            num_scalar_prefetch=len(tables),
            grid=(n_tiles,),
            in_specs=[
                pl.BlockSpec((SUBLANES, TOKEN_TILE), lambda i, *_: (0, i)),
                pl.BlockSpec((TOKEN_TILE, D_MODEL), lambda i, *_: (i, 0)),
            ],
            out_specs=pl.BlockSpec(memory_space=pl.ANY),
            scratch_shapes=[
                pltpu.VMEM((TILE_CAP, D_MODEL), BF16),
                pltpu.VMEM((ROW_TILE, D_MODEL), BF16),
                pltpu.SemaphoreType.DMA(()),
            ],
        ),
        out_shape=jax.ShapeDtypeStruct((n_rows_total, D_MODEL), BF16),
        compiler_params=pltpu.CompilerParams(
            dimension_semantics=("arbitrary",), vmem_limit_bytes=VMEM_LIMIT_BYTES),
        name="dispatch",
    )(*tables, slot_t, xn)


X_SLOTS = 3
W_FETCH_STEPS = (0, 1, 3)


def _experts_kernel(base_ref, nt_ref, nu_ref, bg_ref, bu_ref, bd_ref, wg_hbm, wu_hbm, wd_hbm, xs_ref, zs_ref,
                    w_f32, wg_bf, wu_bf, wd_bf, xbuf, hbuf, zbuf, w_sem, in_sem, out_sem):
    e = pl.program_id(0)
    n_e = pl.num_programs(0)
    n = nt_ref[e]
    base = base_ref[e]
    w_slot = e % 2
    has_next = e + 1 < n_e

    def w_copy(j, expert, slot):
        w_hbm = (wg_hbm, wu_hbm, wd_hbm)[j]
        return pltpu.make_async_copy(w_hbm.at[expert], w_f32.at[slot, j], w_sem.at[slot])

    def x_copy(i):
        rows = pl.ds(pl.multiple_of(base + i * ROW_TILE, ROW_TILE), ROW_TILE)
        return pltpu.make_async_copy(xs_ref.at[rows], xbuf.at[i % X_SLOTS], in_sem.at[i % X_SLOTS])

    def z_copy(row, slot):
        rows = pl.ds(pl.multiple_of(row, ROW_TILE), ROW_TILE)
        return pltpu.make_async_copy(zbuf.at[slot], zs_ref.at[rows], out_sem.at[slot])

    def hidden(i):
        x = xbuf[i % X_SLOTS]
        a = jnp.minimum(_dot(x, wg_bf[...]) + bg_ref[...], SWIGLU_LIMIT)
        u = jnp.clip(_dot(x, wu_bf[...]) + bu_ref[...], -SWIGLU_LIMIT, SWIGLU_LIMIT)
        hbuf[i % 2] = ((u + 1.0) * a * jax.nn.sigmoid(SWIGLU_ALPHA * a)).astype(BF16)

    def project_down(i, slot):
        @pl.when(i >= 2)
        def _():
            z_copy(base, slot).wait()
        zbuf[slot] = (_dot(hbuf[slot], wd_bf[...]) + bd_ref[...]).astype(BF16)

    @pl.when(e == 0)
    def _():
        for j in range(3):
            w_copy(j, 0, 0).start()

    for i in range(X_SLOTS):
        @pl.when(i < n)
        def _():
            x_copy(i).start(priority=1)

    def fetch_next_weights(j):
        @pl.when(has_next)
        def _():
            w_copy(j, e + 1, 1 - w_slot).start()

    fetch_next_weights(0)
    for j in range(3):
        w_copy(j, e, w_slot).wait()

    @pl.when(n > 0)
    def _():
        wg_bf[...] = w_f32[w_slot, 0].astype(BF16)
        wu_bf[...] = w_f32[w_slot, 1].astype(BF16)
        wd_bf[...] = w_f32[w_slot, 2].astype(BF16)
        x_copy(0).wait()
        hidden(0)

        def step(i, carry):
            x_copy(i + 1).wait()

            @pl.when(i + X_SLOTS < n)
            def _():
                x_copy(i + X_SLOTS).start(priority=1)

            for j in (1, 2):
                @pl.when(i == W_FETCH_STEPS[j])
                def _():
                    fetch_next_weights(j)

            project_down(i, i % 2)
            hidden(i + 1)
            z_copy(base + i * ROW_TILE, i % 2).start()
            return carry

        lax.fori_loop(0, n - 1, step, 0)
        project_down(n - 1, (n - 1) % 2)
        z_copy(base + (n - 1) * ROW_TILE, (n - 1) % 2).start()

        @pl.when(n >= 2)
        def _():
            z_copy(base, n % 2).wait()
        z_copy(base, (n - 1) % 2).wait()

    for j in (1, 2):
        @pl.when(jnp.maximum(n - 1, 0) <= W_FETCH_STEPS[j])
        def _():
            fetch_next_weights(j)

    @pl.when(e == pl.num_programs(0) - 1)
    def _():
        zbuf[0] = jnp.zeros((ROW_TILE, D_MODEL), BF16)
        n_tail = zs_ref.shape[0] // ROW_TILE - nu_ref[0]
        tail = lambda act: lax.fori_loop(
            0, n_tail, lambda i, c: (act(z_copy((nu_ref[0] + i) * ROW_TILE, 0)), c)[1], 0)
        tail(lambda c: c.start())
        tail(lambda c: c.wait())


def _experts(base, n_tiles, n_used, xs, w_gate, b_gate, w_up, b_up, w_down, b_down):
    b_spec = pl.BlockSpec((None, 1, D_MODEL), lambda e, *_: (e, 0, 0))
    any_spec = pl.BlockSpec(memory_space=pl.ANY)
    return pl.pallas_call(
        _experts_kernel,
        grid_spec=pltpu.PrefetchScalarGridSpec(
            num_scalar_prefetch=3,
            grid=(N_EXPERTS,),
            in_specs=[b_spec, b_spec, b_spec, any_spec, any_spec, any_spec, any_spec],
            out_specs=any_spec,
            scratch_shapes=[pltpu.VMEM((2, 3, D_MODEL, D_MODEL), F32)] + [
                pltpu.VMEM((D_MODEL, D_MODEL), BF16) for _ in range(3)] + [
                pltpu.VMEM((X_SLOTS, ROW_TILE, D_MODEL), BF16)] + [
                pltpu.VMEM((2, ROW_TILE, D_MODEL), BF16) for _ in range(2)] + [
                pltpu.SemaphoreType.DMA((2,)), pltpu.SemaphoreType.DMA((X_SLOTS,)), pltpu.SemaphoreType.DMA((2,))],
        ),
        out_shape=jax.ShapeDtypeStruct(xs.shape, BF16),
        compiler_params=pltpu.CompilerParams(
            dimension_semantics=("arbitrary",), vmem_limit_bytes=VMEM_LIMIT_BYTES),
        name="experts",
    )(base, n_tiles, n_used, b_gate[:, None, :], b_up[:, None, :], b_down[:, None, :], w_gate, w_up, w_down, xs)


def _combine_kernel(n_first, seg_ref, src_ref, dst_ref, route_ref, x1_ref, fw_ref, zs_ref, ya_ref, yb_ref,
                    sorted_ref, sem):
    tile = pl.program_id(0)
    sorted_ref[...] = jnp.zeros_like(sorted_ref)

    def make_copy(src, dst, size):
        return pltpu.make_async_copy(zs_ref.at[pl.ds(dst, size)], sorted_ref.at[pl.ds(src, size)], sem)

    _segment_copies(seg_ref, src_ref, dst_ref, tile, make_copy, start=True)
    route = route_ref[...]
    slots, gates = route[0:TOP_K, :], route[TOP_K:2 * TOP_K, :]
    n_blocks = TILE_CAP // SORT_BLOCK
    early = [_slot_matrix(slots, cb, gates) for cb in range(2)]
    _segment_copies(seg_ref, src_ref, dst_ref, tile, make_copy, start=False)
    moe = jnp.zeros((TOKEN_TILE, D_MODEL), F32)
    for cb in range(n_blocks):
        m = early[cb] if cb < len(early) else _slot_matrix(slots, cb, gates)
        moe = moe + _dot_tn(m, sorted_ref[cb * SORT_BLOCK:(cb + 1) * SORT_BLOCK, :])
    y = _rmsnorm(x1_ref[...] + moe, fw_ref[...])

    @pl.when(tile < n_first)
    def _():
        ya_ref[...] = y

    @pl.when(tile >= n_first)
    def _():
        yb_ref[...] = y


def _combine(tables, route_t, x1, final_w, zs, rows_first):
    seg, src, dst = tables
    t = x1.shape[0]
    assert rows_first % TOKEN_TILE == 0 and 0 < rows_first < t and TILE_CAP % SORT_BLOCK == 0
    n_tiles = t // TOKEN_TILE
    n_first = rows_first // TOKEN_TILE
    return pl.pallas_call(
        functools.partial(_combine_kernel, n_first),
        grid_spec=pltpu.PrefetchScalarGridSpec(
            num_scalar_prefetch=3,
            grid=(n_tiles,),
            in_specs=[
                pl.BlockSpec((SUBLANES, TOKEN_TILE), lambda i, *_: (0, i)),
                pl.BlockSpec((TOKEN_TILE, D_MODEL), lambda i, *_: (i, 0)),
                pl.BlockSpec((1, D_MODEL), lambda i, *_: (0, 0)),
                pl.BlockSpec(memory_space=pl.ANY),
            ],
            out_specs=[
                pl.BlockSpec((TOKEN_TILE, D_MODEL), lambda i, *_: (jnp.minimum(i, n_first - 1), 0)),
                pl.BlockSpec((TOKEN_TILE, D_MODEL), lambda i, *_: (jnp.maximum(i - n_first, 0), 0)),
            ],
            scratch_shapes=[pltpu.VMEM((TILE_CAP, D_MODEL), BF16), pltpu.SemaphoreType.DMA(())],
        ),
        out_shape=[jax.ShapeDtypeStruct((rows_first, D_MODEL), F32),
                   jax.ShapeDtypeStruct((t - rows_first, D_MODEL), F32)],
        compiler_params=pltpu.CompilerParams(
            dimension_semantics=("arbitrary",), vmem_limit_bytes=VMEM_LIMIT_BYTES),
        name="combine",
    )(seg, src, dst, route_t, x1, final_w, zs)


def _moe_and_final_norm(x1, rows_first, norm_ffn_w, router_w, router_b, w_gate, b_gate, w_up, b_up, w_down, b_down,
                        final_w):
    t = x1.shape[0]
    assert t % TOKEN_TILE == 0
    n_tiles = t // TOKEN_TILE
    n_rows_total = TOP_K * t + n_tiles * N_EXPERTS * (SEG_ALIGN - 1) + N_EXPERTS * (ROW_TILE - 1)
    n_rows_total = (n_rows_total + ROW_TILE - 1) // ROW_TILE * ROW_TILE
    xn, meta, cnt = _router(x1, norm_ffn_w.reshape(1, D_MODEL), router_w, router_b)
    cnt = cnt[:, 0, :N_EXPERTS].astype(jnp.int32)
    chunks, fill, gap, base, n_row_tiles, n_used = _segment_tables(cnt)
    route_t = jnp.concatenate([meta[:, 2 * TOP_K:3 * TOP_K], meta[:, TOP_K:2 * TOP_K]], axis=1).T
    xs = _dispatch((*chunks, fill, gap, n_used), route_t, xn, n_rows_total)
    zs = _experts(base, n_row_tiles, n_used, xs, w_gate, b_gate, w_up, b_up, w_down, b_down)
    return _combine(chunks, route_t, x1, final_w.reshape(1, D_MODEL), zs, rows_first)


def kernel(x_prompt, x_sample, state_conv, state_hgrn, lb_logits, norm_mix_w, w_in, conv_w, conv_b, gnorm_w,
           w_out, norm_ffn_w, router_w, router_b, w_gate, b_gate, w_up, b_up, w_down, b_down, final_norm_w):
    assert norm_mix_w.shape[0] == 1 and lb_logits.shape[0] == 2, "single-layer step"
    b, l, _ = x_prompt.shape
    nb, ls, _ = x_sample.shape
    lbl = lb_logits.astype(F32)
    nw = norm_mix_w[0].reshape(1, D_MODEL)
    w_in_bf = w_in[0].astype(BF16)
    w_out_bf = w_out[0].astype(BF16)
    cw, cb = conv_w[0], conv_b[0].reshape(1, D_CONV)
    gnw = gnorm_w[0].reshape(1, D_HGRN)
    rows_p, rows_s = b * l, nb * ls
    x1, conv_p, hgrn_p = _mix_prompt(x_prompt, lbl, nw, w_in_bf, cw, cb, gnw, w_out_bf, rows_p + rows_s)
    x1, conv_s, hgrn_s = _mix_sample(x_sample, state_conv[0], state_hgrn[0], lbl, nw, w_in_bf, cw, cb, gnw,
                                     w_out_bf, x1, rows_p)
    y_p, y_s = _moe_and_final_norm(x1, rows_p, norm_ffn_w[0], router_w[0], router_b[0], w_gate[0], b_gate[0],
                                   w_up[0], b_up[0], w_down[0], b_down[0], final_norm_w)
    conv_p = conv_p[:, SUBLANES - 2:, :]
    conv_s = conv_s[:, ls - 2:, :]
    return (y_p.reshape(b, l, D_MODEL), y_s.reshape(nb, ls, D_MODEL), conv_p[None], hgrn_p[None], conv_s[None],
            hgrn_s[None])
```

```python
import functools

import numpy as np
import jax
import jax.numpy as jnp
from jax import lax
from jax.experimental import pallas as pl
from jax.experimental.pallas import tpu as pltpu

F32 = jnp.float32
BF16 = jnp.bfloat16

D_MODEL = 1024
D_CONV = 512
D_HGRN = 512
N_HEADS = 4
D_HEAD = 128
D_PROJ = 3 * D_CONV + 4 * D_HGRN
N_EXPERTS = 32
TOP_K = 4
SWIGLU_LIMIT = 7.0
SWIGLU_ALPHA = 1.702
EPS = 1e-6
PROMPT_CHUNK = 64

LANES = 128
SUBLANES = 8
BF16_ROWS = 16
VMEM_LIMIT_BYTES = 56 * 1024 * 1024

MIX_SLAB = 128
MIX_ROWS = 512
MIX_PART = 256
MIX_SEQS = 16
TOKEN_TILE = 512
SEG_ALIGN = BF16_ROWS
ROW_TILE = 512
SEG_SIZES = tuple(SEG_ALIGN << i for i in reversed(range(6)))
TILE_CAP = TOP_K * TOKEN_TILE + N_EXPERTS * SEG_ALIGN

NT_DIMS = (((1,), (1,)), ((), ()))
TN_DIMS = (((0,), (0,)), ((), ()))


def _dot(a, b):
    return jnp.dot(a, b, preferred_element_type=F32)


def _dot_nt(a, b):
    return lax.dot_general(a, b, NT_DIMS, preferred_element_type=F32)


def _dot_tn(a, b):
    return lax.dot_general(a, b, TN_DIMS, preferred_element_type=F32)


def _split2(x):
    hi = x.astype(BF16)
    lo = (x - hi.astype(F32)).astype(BF16)
    return hi, lo


def _rmsnorm(x, w):
    return x * lax.rsqrt(jnp.mean(x * x, axis=-1, keepdims=True) + EPS) * w


def _level_exponent_matrix(n, chunk, h):
    x = np.zeros((n, n), np.float32)
    for t in range(n):
        base = t - t % (2 * h)
        m = base + h - 1
        if t % (2 * h) >= h:
            x[t, m + 1:t + 1] = 1.0
        else:
            x[t, t + 1:m + 1] = 1.0
    return x


def _mix_constants(n, chunk, mxu_levels, all_levels):
    t = np.arange(n)
    same_chunk = (t[:, None] // chunk) == (t[None, :] // chunk)
    tri = (same_chunk & (t[None, :] <= t[:, None])).astype(np.float32)
    suf = (same_chunk & (t[None, :] > t[:, None])).astype(np.float32)
    cmat = np.concatenate([tri, suf] + [_level_exponent_matrix(n, chunk, h) for h in mxu_levels], axis=0)
    masks = []
    for h in all_levels:
        blk = (t[:, None] // (2 * h)) == (t[None, :] // (2 * h))
        masks.append((blk & ((t[:, None] % (2 * h)) >= h) & ((t[None, :] % (2 * h)) < h)).astype(np.float32))
    masks.append(np.eye(n, dtype=np.float32))
    return jnp.asarray(cmat, BF16), jnp.asarray(np.stack(masks), F32)


def _vpu_level_exponent(a, h):
    n = a.shape[0]
    pieces = []
    for j in range(n // (2 * h)):
        b = j * 2 * h
        ref = a[b + h - 1:b + h, :]
        pieces.append(ref - a[b:b + h, :])
        pieces.append(a[b + h:b + 2 * h, :] - ref)
    return jnp.concatenate(pieces, axis=0)


def _chunk_prefix_sums(x, chunk):
    n, width = x.shape
    x3 = x.reshape(n // SUBLANES, SUBLANES, width)
    pos = lax.broadcasted_iota(jnp.int32, (1, SUBLANES, 1), 1)
    d = 1
    while d < SUBLANES:
        x3 = x3 + jnp.where(pos >= d, pltpu.roll(x3, d, 1), 0.0)
        d *= 2
    x = x3.reshape(n, width)
    pieces = []
    for b in range(0, n, SUBLANES):
        blk = x[b:b + SUBLANES, :]
        if b % chunk:
            blk = blk + pieces[-1][SUBLANES - 1:SUBLANES, :]
        pieces.append(blk)
    return jnp.concatenate(pieces, axis=0)


def _small_level_exponent(a, logf, h):
    n = a.shape[0]
    pos = lax.broadcasted_iota(jnp.int32, (SUBLANES, 1), 0)
    if h == 1:
        odd = lax.broadcasted_iota(jnp.int32, (n, 1), 0) % 2 == 1
        return jnp.where(odd, logf, 0.0)
    pieces = []
    for b in range(0, n, SUBLANES):
        blk = a[b:b + SUBLANES, :]
        if h == 4:
            ref = blk[3:4, :]
        else:
            ref = jnp.where(pos < 4, blk[1:2, :], blk[5:6, :])
        pieces.append(jnp.where(pos % (2 * h) >= h, blk - ref, ref - blk))
    return jnp.concatenate(pieces, axis=0)


def _forget_lower_bound(lbl):
    m = jnp.max(lbl, axis=0, keepdims=True)
    e = jnp.exp(lbl - m)
    return e[0:1, :] / jnp.sum(e, axis=0, keepdims=True)


PROJ_BLOCK = 512
N_PROJ_BLOCKS = D_PROJ // PROJ_BLOCK


class _ColumnBlocks:
    def __init__(self, h, w_in_ref):
        self.h, self.w, self.done = h, w_in_ref, []

    def next_block(self):
        c = len(self.done)
        if c < N_PROJ_BLOCKS:
            self.done.append(_dot(self.h, self.w[:, c * PROJ_BLOCK:(c + 1) * PROJ_BLOCK]))

    def block(self, c):
        while len(self.done) <= c:
            self.next_block()
        return self.done[c]


def _gates(ff, lb):
    e = jnp.exp(-jnp.abs(ff))
    r = 1.0 / (1.0 + e)
    pos = ff >= 0
    sig = jnp.where(pos, r, e * r)
    sig_neg = jnp.where(pos, e * r, r)
    logf = jnp.log(lb + (1.0 - lb) * sig)
    kk = (1.0 - lb) * sig_neg
    return logf, kk


def _intra_scores_times_v(q, kk, vi, exps, masks_ref):
    n = q.shape[0]
    sc = [jnp.zeros((n, n), F32) for _ in range(N_HEADS)]
    for l, ex in enumerate(exps):
        if ex is None:
            qh, kh = q.astype(BF16), kk.astype(BF16)
        else:
            w = jnp.exp(ex)
            qh, kh = (q * w).astype(BF16), (kk * w).astype(BF16)
        mask = masks_ref[l]
        for hd in range(N_HEADS):
            hs = slice(hd * D_HEAD, (hd + 1) * D_HEAD)
            sc[hd] = sc[hd] + mask * _dot_nt(qh[:, hs], kh[:, hs])
    vb = vi.astype(BF16)
    return [_dot(sc[hd].astype(BF16), vb[:, hd * D_HEAD:(hd + 1) * D_HEAD]) for hd in range(N_HEADS)]


def _head_out(o, g, gnw):
    parts = []
    for hd in range(N_HEADS):
        oh = o[:, hd * D_HEAD:(hd + 1) * D_HEAD]
        parts.append(oh * lax.rsqrt(jnp.mean(oh * oh, axis=-1, keepdims=True) + EPS))
    on = jnp.concatenate(parts, axis=1)
    return on * gnw * (g * jax.nn.sigmoid(g))


def _pad_rows_bf16(a):
    return jnp.concatenate([a, jnp.zeros_like(a)], axis=0).astype(BF16)


PROMPT_VPU_LEVELS = (32, 16, 8)
MXU_LEVELS = (4, 2, 1)


def _mix_prompt_kernel(n_r, n_steps, *refs):
    step = pl.program_id(0)

    @pl.when(step < n_steps)
    def _():
        _mix_prompt_step(step % n_r, n_r, *refs)

    @pl.when(step >= n_steps)
    def _():
        x1_ref = refs[N_MIX_PROMPT_INPUTS]
        x1_ref[...] = jnp.zeros_like(x1_ref)


N_MIX_PROMPT_INPUTS = 9


def _mix_prompt_step(r, n_r, x_ref, lbl_ref, nw_ref, win_ref, cw_ref, cb_ref, gnw_ref, wout_ref, masks_ref,
                     x1_ref, nconv_ref, nstate_ref, st_ref, tail_ref):
    rows = x_ref.shape[0]

    @pl.when(r == 0)
    def _():
        st_ref[...] = jnp.zeros_like(st_ref)
        tail_ref[...] = jnp.zeros_like(tail_ref)

    lb = _forget_lower_bound(lbl_ref[...])
    cw = cw_ref[...]
    n_chunks = MIX_SLAB // PROMPT_CHUNK
    rid = lax.broadcasted_iota(jnp.int32, (MIX_PART, 1), 0)
    t0 = tail_ref[SUBLANES - 2:SUBLANES - 1, :]
    t1 = tail_ref[SUBLANES - 1:SUBLANES, :]
    states = [st_ref[hd] for hd in range(N_HEADS)]

    n_parts = rows // MIX_PART
    part_rows = lambda p: slice(p * MIX_PART, (p + 1) * MIX_PART)
    project = lambda p: _ColumnBlocks(_rmsnorm(x_ref[part_rows(p), :], nw_ref[...]).astype(BF16), win_ref)
    nxt = project(0)
    for p in range(n_parts):
        pr = part_rows(p)
        x = x_ref[pr, :]
        bg, cg, vc, q, ff, vi, g = (nxt.block(c) for c in range(N_PROJ_BLOCKS))
        if p + 1 < n_parts:
            nxt = project(p + 1)
            nxt.block(N_PROJ_BLOCKS - 1)

        u = cg * vc
        u1 = jnp.where(rid == 0, t1, pltpu.roll(u, 1, 0))
        u2 = jnp.where(rid == 0, t0, jnp.where(rid == 1, t1, pltpu.roll(u, 2, 0)))
        y_conv = bg * (cw[0:1, :] * u2 + cw[1:2, :] * u1 + cw[2:3, :] * u + cb_ref[...])
        u_last = u[MIX_PART - SUBLANES:MIX_PART, :]
        t0, t1 = u_last[SUBLANES - 2:SUBLANES - 1, :], u_last[SUBLANES - 1:SUBLANES, :]

        logf, kk = _gates(ff, lb)
        o_parts = []
        for s in range(MIX_PART // MIX_SLAB):
            sl = slice(s * MIX_SLAB, (s + 1) * MIX_SLAB)
            qs, ks, vs = q[sl], kk[sl], vi[sl]
            a_pre = _chunk_prefix_sums(logf[sl], PROMPT_CHUNK)
            a_suf = jnp.concatenate(
                [a_pre[(c + 1) * PROMPT_CHUNK - 1:(c + 1) * PROMPT_CHUNK, :]
                 - a_pre[c * PROMPT_CHUNK:(c + 1) * PROMPT_CHUNK, :] for c in range(n_chunks)], axis=0)
            exps = [_vpu_level_exponent(a_pre, h) for h in PROMPT_VPU_LEVELS]
            exps += [_small_level_exponent(a_pre, logf[sl], h) for h in MXU_LEVELS]
            exps.append(None)
            intra = _intra_scores_times_v(qs, ks, vs, exps, masks_ref)
            ea = jnp.exp(a_pre)
            qa = (qs * ea).astype(BF16)
            kb = (ks * jnp.exp(a_suf)).astype(BF16)
            vb = vs.astype(BF16)
            for c in range(n_chunks):
                cr = slice(c * PROMPT_CHUNK, (c + 1) * PROMPT_CHUNK)
                last = (c + 1) * PROMPT_CHUNK - 1
                heads = []
                for hd in range(N_HEADS):
                    hs = slice(hd * D_HEAD, (hd + 1) * D_HEAD)
                    heads.append(_dot_nt(qa[cr, hs], states[hd].astype(BF16)) + intra[hd][cr, :])
                    states[hd] = states[hd] * ea[last:last + 1, hs] + _dot_tn(vb[cr, hs], kb[cr, hs])
                o_parts.append(jnp.concatenate(heads, axis=1))

        o = _head_out(jnp.concatenate(o_parts, axis=0), g, gnw_ref[...])
        mix_in = jnp.concatenate([y_conv, o], axis=1).astype(BF16)
        x1_ref[pr, :] = x + _dot(mix_in, wout_ref[...])

    tail_ref[...] = u_last
    for hd in range(N_HEADS):
        st_ref[hd] = states[hd]

    @pl.when(r == n_r - 1)
    def _():
        nconv_ref[...] = u_last
        for hd in range(N_HEADS):
            nstate_ref[hd] = states[hd].T


def _mix_prompt(x, lbl, nw, w_in, cw, cb, gnw, w_out, total_rows):
    b, l, _ = x.shape
    assert l % MIX_ROWS == 0 and MIX_ROWS % MIX_PART == 0 and MIX_PART % MIX_SLAB == 0 and l % PROMPT_CHUNK == 0
    assert total_rows % MIX_ROWS == 0
    n_r = l // MIX_ROWS
    n_steps = b * n_r
    seq = lambda s: jnp.minimum(s, n_steps - 1) // n_r
    _, masks = _mix_constants(MIX_SLAB, PROMPT_CHUNK, (), PROMPT_VPU_LEVELS + MXU_LEVELS)
    const = lambda shape: pl.BlockSpec(shape, lambda s: (0,) * len(shape))
    return pl.pallas_call(
        functools.partial(_mix_prompt_kernel, n_r, n_steps),
        grid=(total_rows // MIX_ROWS,),
        in_specs=[
            pl.BlockSpec((None, MIX_ROWS, D_MODEL), lambda s: (seq(s), jnp.minimum(s, n_steps - 1) % n_r, 0)),
            const((2, D_HGRN)), const((1, D_MODEL)), const((D_MODEL, D_PROJ)), const((3, D_CONV)),
            const((1, D_CONV)), const((1, D_HGRN)), const((D_MODEL, D_MODEL)), const(masks.shape),
        ],
        out_specs=[
            pl.BlockSpec((MIX_ROWS, D_MODEL), lambda s: (s, 0)),
            pl.BlockSpec((None, SUBLANES, D_CONV), lambda s: (seq(s), 0, 0)),
            pl.BlockSpec((None, N_HEADS, D_HEAD, D_HEAD), lambda s: (seq(s), 0, 0, 0)),
        ],
        out_shape=[
            jax.ShapeDtypeStruct((total_rows, D_MODEL), F32),
            jax.ShapeDtypeStruct((b, SUBLANES, D_CONV), F32),
            jax.ShapeDtypeStruct((b, N_HEADS, D_HEAD, D_HEAD), F32),
        ],
        scratch_shapes=[
            pltpu.VMEM((N_HEADS, D_HEAD, D_HEAD), F32),
            pltpu.VMEM((SUBLANES, D_CONV), F32),
        ],
        compiler_params=pltpu.CompilerParams(
            dimension_semantics=("arbitrary",), vmem_limit_bytes=VMEM_LIMIT_BYTES),
        name="mix_prompt",
    )(x, lbl, nw, w_in, cw, cb, gnw, w_out, masks)


def _mix_sample_kernel(x_ref, cs_ref, hs_ref, lbl_ref, nw_ref, win_ref, cw_ref, cb_ref, gnw_ref, wout_ref,
                       cmat_ref, masks_ref, x1_in_ref, x1_ref, nconv_ref, nstate_ref, o_ref):
    del x1_in_ref
    nseq, length, _ = nconv_ref.shape
    rows = nseq * length
    x = x_ref[...]
    proj = _ColumnBlocks(_rmsnorm(x, nw_ref[...]).astype(BF16), win_ref)
    bg, cg, vc, q, ff, vi, g = (proj.block(c) for c in range(N_PROJ_BLOCKS))
    lb = _forget_lower_bound(lbl_ref[...])

    u = cg * vc
    cs = cs_ref[...]
    expand = lambda a: jnp.broadcast_to(a, (nseq, length, D_CONV)).reshape(rows, D_CONV)
    t0 = expand(cs[:, 0:1, :])
    t1 = expand(cs[:, 1:2, :])
    pos = lax.broadcasted_iota(jnp.int32, (rows, 1), 0) % length
    u1 = jnp.where(pos == 0, t1, pltpu.roll(u, 1, 0))
    u2 = jnp.where(pos == 0, t0, jnp.where(pos == 1, t1, pltpu.roll(u, 2, 0)))
    cw = cw_ref[...]
    y_conv = bg * (cw[0:1, :] * u2 + cw[1:2, :] * u1 + cw[2:3, :] * u + cb_ref[...])
    nconv_ref[...] = u.reshape(nseq, length, D_CONV)

    logf, kk = _gates(ff, lb)
    lf_hi, lf_lo = _split2(logf)
    ex = _dot(cmat_ref[...], lf_hi) + _dot(cmat_ref[...], lf_lo)
    a_pre = ex[0:rows]
    a_suf = ex[rows:2 * rows]
    exps = [ex[(2 + i) * rows:(3 + i) * rows] for i in range(len(MXU_LEVELS))] + [None]
    intra = _intra_scores_times_v(q, kk, vi, exps, masks_ref)
    ea = jnp.exp(a_pre)
    qa = q * ea
    kb = kk * jnp.exp(a_suf)
    for s in range(nseq):
        cr = slice(s * length, (s + 1) * length)
        last = (s + 1) * length - 1
        for hd in range(N_HEADS):
            hs = slice(hd * D_HEAD, (hd + 1) * D_HEAD)
            st = hs_ref[s, hd].T
            inter = _dot_nt(_pad_rows_bf16(qa[cr, hs]), st.astype(BF16))[0:length, :]
            o_ref[cr, hs] = inter + intra[hd][cr, :]
            st_new = st * ea[last:last + 1, hs] + _dot_tn(_pad_rows_bf16(vi[cr, hs]), _pad_rows_bf16(kb[cr, hs]))
            nstate_ref[s, hd] = st_new.T

    o = _head_out(o_ref[...], g, gnw_ref[...])
    mix_in = jnp.concatenate([y_conv, o], axis=1).astype(BF16)
    x1_ref[...] = x + _dot(mix_in, wout_ref[...])


def _mix_sample(x, conv_state, hgrn_state, lbl, nw, w_in, cw, cb, gnw, w_out, x1_flat, row_offset):
    nb, length, _ = x.shape
    rows = MIX_SEQS * length
    assert nb % MIX_SEQS == 0 and length == SUBLANES and row_offset % rows == 0
    first_block = row_offset // rows
    cmat, masks = _mix_constants(rows, length, MXU_LEVELS, MXU_LEVELS)
    const = lambda shape: pl.BlockSpec(shape, lambda i: (0,) * len(shape))
    return pl.pallas_call(
        _mix_sample_kernel,
        grid=(nb // MIX_SEQS,),
        in_specs=[
            pl.BlockSpec((rows, D_MODEL), lambda i: (i, 0)),
            pl.BlockSpec((MIX_SEQS, 2, D_CONV), lambda i: (i, 0, 0)),
            pl.BlockSpec((MIX_SEQS, N_HEADS, D_HEAD, D_HEAD), lambda i: (i, 0, 0, 0)),
            const((2, D_HGRN)), const((1, D_MODEL)), const((D_MODEL, D_PROJ)), const((3, D_CONV)),
            const((1, D_CONV)), const((1, D_HGRN)), const((D_MODEL, D_MODEL)), const(cmat.shape),
            const(masks.shape), pl.BlockSpec(memory_space=pl.ANY),
        ],
        out_specs=[
            pl.BlockSpec((rows, D_MODEL), lambda i: (first_block + i, 0)),
            pl.BlockSpec((MIX_SEQS, length, D_CONV), lambda i: (i, 0, 0)),
            pl.BlockSpec((MIX_SEQS, N_HEADS, D_HEAD, D_HEAD), lambda i: (i, 0, 0, 0)),
        ],
        out_shape=[
            jax.ShapeDtypeStruct(x1_flat.shape, F32),
            jax.ShapeDtypeStruct((nb, length, D_CONV), F32),
            jax.ShapeDtypeStruct((nb, N_HEADS, D_HEAD, D_HEAD), F32),
        ],
        scratch_shapes=[pltpu.VMEM((rows, D_HGRN), F32)],
        input_output_aliases={12: 0},
        compiler_params=pltpu.CompilerParams(
            dimension_semantics=("arbitrary",), vmem_limit_bytes=VMEM_LIMIT_BYTES),
        name="mix_sample",
    )(x.reshape(nb * length, D_MODEL), conv_state, hgrn_state, lbl, nw, w_in, cw, cb, gnw, w_out, cmat, masks,
      x1_flat)


def _router_kernel(x1_ref, nw_ref, rw_hi_ref, rw_lo_ref, rb_ref, ltri_ref, utri_ref, xn_ref, meta_ref, cnt_ref):
    n = x1_ref.shape[0]
    xn = _rmsnorm(x1_ref[...], nw_ref[...])
    xn_ref[...] = xn.astype(BF16)
    x_hi, x_lo = _split2(xn)
    logits = (_dot(x_hi, rw_hi_ref[...]) + _dot(x_lo, rw_hi_ref[...]) + _dot(x_hi, rw_lo_ref[...])
              + rb_ref[...])
    lane = lax.broadcasted_iota(jnp.int32, (n, LANES), 1).astype(F32)
    work = logits
    vals, ids = [], []
    for _ in range(TOP_K):
        m = jnp.max(work, axis=-1, keepdims=True)
        i = jnp.min(jnp.where(work == m, lane, float(LANES)), axis=-1, keepdims=True)
        vals.append(m)
        ids.append(i)
        work = jnp.where(lane == i, -jnp.inf, work)
    es = [jnp.exp(v - vals[0]) for v in vals]
    den = es[0] + es[1] + es[2] + es[3]
    gates = [e / den for e in es]

    onehots = [(lane == i) for i in ids]
    multi = jnp.zeros((n, LANES), F32)
    for oh in onehots:
        multi = multi + oh.astype(F32)
    counts = jnp.sum(multi, axis=0, keepdims=True)
    before = _dot(ltri_ref[...], multi.astype(BF16))
    seg = jnp.ceil(counts * (1.0 / SEG_ALIGN)) * SEG_ALIGN
    seg_rows = jnp.broadcast_to(seg, (BF16_ROWS, LANES)).astype(BF16)
    seg_off = _dot(seg_rows, utri_ref[...])[0:1, :]
    slot_of = seg_off + before
    meta = jnp.zeros((n, LANES), F32)
    for k in range(TOP_K):
        slot = jnp.sum(jnp.where(onehots[k], slot_of, 0.0), axis=-1, keepdims=True)
        meta = jnp.where(lane == k, ids[k], meta)
        meta = jnp.where(lane == TOP_K + k, gates[k], meta)
        meta = jnp.where(lane == 2 * TOP_K + k, slot, meta)
    meta_ref[...] = meta
    cnt_ref[...] = jnp.broadcast_to(counts, (SUBLANES, LANES))


def _router(x1, nw, router_w, router_b):
    t = x1.shape[0]
    n_tiles = t // TOKEN_TILE
    rw = jnp.zeros((D_MODEL, LANES), F32).at[:, :N_EXPERTS].set(router_w)
    rw_hi, rw_lo = _split2(rw)
    rb = jnp.full((1, LANES), -1e30, F32).at[0, :N_EXPERTS].set(router_b)
    idx = np.arange(TOKEN_TILE)
    ltri = jnp.asarray(idx[None, :] < idx[:, None], BF16)
    lid = np.arange(LANES)
    utri = jnp.asarray(lid[:, None] < lid[None, :], BF16)
    const = lambda shape: pl.BlockSpec(shape, lambda i: (0,) * len(shape))
    return pl.pallas_call(
        _router_kernel,
        grid=(n_tiles,),
        in_specs=[
            pl.BlockSpec((TOKEN_TILE, D_MODEL), lambda i: (i, 0)),
            const((1, D_MODEL)), const((D_MODEL, LANES)), const((D_MODEL, LANES)), const((1, LANES)),
            const((TOKEN_TILE, TOKEN_TILE)), const((LANES, LANES)),
        ],
        out_specs=[
            pl.BlockSpec((TOKEN_TILE, D_MODEL), lambda i: (i, 0)),
            pl.BlockSpec((TOKEN_TILE, LANES), lambda i: (i, 0)),
            pl.BlockSpec((None, SUBLANES, LANES), lambda i: (i, 0, 0)),
        ],
        out_shape=[
            jax.ShapeDtypeStruct((t, D_MODEL), BF16),
            jax.ShapeDtypeStruct((t, LANES), F32),
            jax.ShapeDtypeStruct((n_tiles, SUBLANES, LANES), F32),
        ],
        compiler_params=pltpu.CompilerParams(
            dimension_semantics=("arbitrary",), vmem_limit_bytes=VMEM_LIMIT_BYTES),
        name="router",
    )(x1, nw, rw_hi, rw_lo, rb, ltri, utri)


def _segment_tables(cnt):
    seg = (cnt + SEG_ALIGN - 1) // SEG_ALIGN * SEG_ALIGN
    src = jnp.cumsum(seg, axis=1) - seg
    tot = jnp.sum(seg, axis=0)
    cap = (tot + ROW_TILE - 1) // ROW_TILE * ROW_TILE
    base = jnp.cumsum(cap) - cap
    dst = base[None, :] + jnp.cumsum(seg, axis=0) - seg
    n_used = jnp.sum(cap) // ROW_TILE
    fill = base + tot
    lane = jnp.arange(N_EXPERTS)
    counts, srcs, dsts = [], [], []
    for size in SEG_SIZES:
        has = (seg & size) != 0
        done = seg & ~(2 * size - 1)
        place = has[:, :, None] & ((jnp.cumsum(has, axis=1) - 1)[:, :, None] == lane[None, None, :])
        counts.append(jnp.sum(has, axis=1))
        srcs.append(jnp.sum(jnp.where(place, (src + done)[:, :, None], 0), axis=1))
        dsts.append(jnp.sum(jnp.where(place, (dst + done)[:, :, None], 0), axis=1))
    i32 = lambda a: a.astype(jnp.int32).reshape(-1)
    chunks = (i32(jnp.stack(counts, axis=1)), i32(jnp.stack(srcs, axis=1)), i32(jnp.stack(dsts, axis=1)))
    return chunks, i32(fill), i32(cap - tot), i32(base), i32(cap // ROW_TILE), i32(n_used)


def _for_each_chunk(n, fn):
    for size in SEG_SIZES:
        done = n & ~(2 * size - 1)

        @pl.when((n & size) != 0)
        def _():
            fn(done, size)


def _segment_copies(cnt_ref, src_ref, dst_ref, tile, make_copy, start):
    for si, size in enumerate(SEG_SIZES):
        first = (tile * len(SEG_SIZES) + si) * N_EXPERTS

        def one(p, carry, si=si, size=size, first=first):
            copy = make_copy(pl.multiple_of(src_ref[first + p], SEG_ALIGN),
                             pl.multiple_of(dst_ref[first + p], SEG_ALIGN), size)
            if start:
                copy.start(priority=si % 2)
            else:
                copy.wait()
            return carry
        lax.fori_loop(0, cnt_ref[tile * len(SEG_SIZES) + si], one, 0)


SORT_BLOCK = 512


def _slot_matrix(slots, block, gates):
    row = (lax.broadcasted_iota(jnp.int32, (SORT_BLOCK, TOKEN_TILE), 0) + block * SORT_BLOCK).astype(F32)
    m = jnp.zeros((SORT_BLOCK, TOKEN_TILE), F32)
    for k in range(TOP_K):
        m = jnp.where(row == slots[k:k + 1, :], 1.0 if gates is None else gates[k:k + 1, :], m)
    return m.astype(BF16)


def _dispatch_kernel(seg_ref, src_ref, dst_ref, fill_ref, gap_ref, nu_ref, slot_ref, xn_ref, xs_ref, sorted_ref,
                     zero_ref, sem):
    tile = pl.program_id(0)

    @pl.when(tile == 0)
    def _():
        zero_ref[...] = jnp.zeros_like(zero_ref)

        def fill_copy(row, size):
            return pltpu.make_async_copy(zero_ref.at[pl.ds(0, size)],
                                         xs_ref.at[pl.ds(pl.multiple_of(row, SEG_ALIGN), size)], sem)

        def gaps(act):
            def per_expert(e, carry):
                _for_each_chunk(gap_ref[e], lambda done, size: act(fill_copy(fill_ref[e] + done, size)))
                return carry
            lax.fori_loop(0, N_EXPERTS, per_expert, 0)

        def tail(act):
            n_tail = xs_ref.shape[0] // ROW_TILE - nu_ref[0]
            lax.fori_loop(0, n_tail, lambda i, c: (act(fill_copy((nu_ref[0] + i) * ROW_TILE, ROW_TILE)), c)[1], 0)

        gaps(lambda c: c.start())
        tail(lambda c: c.start())
        gaps(lambda c: c.wait())
        tail(lambda c: c.wait())

    slots = slot_ref[...]
    xn = xn_ref[...]
    for rb in range(TILE_CAP // SORT_BLOCK):
        sorted_ref[rb * SORT_BLOCK:(rb + 1) * SORT_BLOCK, :] = _dot(
            _slot_matrix(slots, rb, None), xn).astype(BF16)

    def make_copy(src, dst, size):
        return pltpu.make_async_copy(sorted_ref.at[pl.ds(src, size)], xs_ref.at[pl.ds(dst, size)], sem)

    _segment_copies(seg_ref, src_ref, dst_ref, tile, make_copy, start=True)
    _segment_copies(seg_ref, src_ref, dst_ref, tile, make_copy, start=False)


def _dispatch(tables, slot_t, xn, n_rows_total):
    t = xn.shape[0]
    n_tiles = t // TOKEN_TILE
    return pl.pallas_call(
        _dispatch_kernel,
        grid_spec=pltpu.PrefetchScalarGridSpec(
            num_scalar_prefetch=len(tables),
            grid=(n_tiles,),
            in_specs=[
                pl.BlockSpec((SUBLANES, TOKEN_TILE), lambda i, *_: (0, i)),
                pl.BlockSpec((TOKEN_TILE, D_MODEL), lambda i, *_: (i, 0)),
            ],
            out_specs=pl.BlockSpec(memory_space=pl.ANY),
            scratch_shapes=[
                pltpu.VMEM((TILE_CAP, D_MODEL), BF16),
                pltpu.VMEM((ROW_TILE, D_MODEL), BF16),
                pltpu.SemaphoreType.DMA(()),
            ],
        ),
        out_shape=jax.ShapeDtypeStruct((n_rows_total, D_MODEL), BF16),
        compiler_params=pltpu.CompilerParams(
            dimension_semantics=("arbitrary",), vmem_limit_bytes=VMEM_LIMIT_BYTES),
        name="dispatch",
    )(*tables, slot_t, xn)


X_SLOTS = 3
W_FETCH_STEPS = (0, 1, 3)


def _experts_kernel(base_ref, nt_ref, nu_ref, bg_ref, bu_ref, bd_ref, wg_hbm, wu_hbm, wd_hbm, xs_ref, zs_ref,
                    w_f32, wg_bf, wu_bf, wd_bf, xbuf, hbuf, zbuf, w_sem, in_sem, out_sem):
    e = pl.program_id(0)
    n_e = pl.num_programs(0)
    n = nt_ref[e]
    base = base_ref[e]
    w_slot = e % 2
    has_next = e + 1 < n_e

    def w_copy(j, expert, slot):
        w_hbm = (wg_hbm, wu_hbm, wd_hbm)[j]
        return pltpu.make_async_copy(w_hbm.at[expert], w_f32.at[slot, j], w_sem.at[slot])

    def x_copy(i):
        rows = pl.ds(pl.multiple_of(base + i * ROW_TILE, ROW_TILE), ROW_TILE)
        return pltpu.make_async_copy(xs_ref.at[rows], xbuf.at[i % X_SLOTS], in_sem.at[i % X_SLOTS])

    def z_copy(row, slot):
        rows = pl.ds(pl.multiple_of(row, ROW_TILE), ROW_TILE)
        return pltpu.make_async_copy(zbuf.at[slot], zs_ref.at[rows], out_sem.at[slot])

    def hidden(i):
        x = xbuf[i % X_SLOTS]
        a = jnp.minimum(_dot(x, wg_bf[...]) + bg_ref[...], SWIGLU_LIMIT)
        u = jnp.clip(_dot(x, wu_bf[...]) + bu_ref[...], -SWIGLU_LIMIT, SWIGLU_LIMIT)
        hbuf[i % 2] = ((u + 1.0) * a * jax.nn.sigmoid(SWIGLU_ALPHA * a)).astype(BF16)

    def project_down(i, slot):
        @pl.when(i >= 2)
        def _():
            z_copy(base, slot).wait()
        zbuf[slot] = (_dot(hbuf[slot], wd_bf[...]) + bd_ref[...]).astype(BF16)

    @pl.when(e == 0)
    def _():
        for j in range(3):
            w_copy(j, 0, 0).start()

    for i in range(X_SLOTS):
        @pl.when(i < n)
        def _():
            x_copy(i).start(priority=1)

    def fetch_next_weights(j):
        @pl.when(has_next)
        def _():
            w_copy(j, e + 1, 1 - w_slot).start()

    fetch_next_weights(0)
    for j in range(3):
        w_copy(j, e, w_slot).wait()

    @pl.when(n > 0)
    def _():
        wg_bf[...] = w_f32[w_slot, 0].astype(BF16)
        wu_bf[...] = w_f32[w_slot, 1].astype(BF16)
        wd_bf[...] = w_f32[w_slot, 2].astype(BF16)
        x_copy(0).wait()
        hidden(0)

        def step(i, carry):
            x_copy(i + 1).wait()

            @pl.when(i + X_SLOTS < n)
            def _():
                x_copy(i + X_SLOTS).start(priority=1)

            for j in (1, 2):
                @pl.when(i == W_FETCH_STEPS[j])
                def _():
                    fetch_next_weights(j)

            project_down(i, i % 2)
            hidden(i + 1)
            z_copy(base + i * ROW_TILE, i % 2).start()
            return carry

        lax.fori_loop(0, n - 1, step, 0)
        project_down(n - 1, (n - 1) % 2)
        z_copy(base + (n - 1) * ROW_TILE, (n - 1) % 2).start()

        @pl.when(n >= 2)
        def _():
            z_copy(base, n % 2).wait()
        z_copy(base, (n - 1) % 2).wait()

    for j in (1, 2):
        @pl.when(jnp.maximum(n - 1, 0) <= W_FETCH_STEPS[j])
        def _():
            fetch_next_weights(j)

    @pl.when(e == pl.num_programs(0) - 1)
    def _():
        zbuf[0] = jnp.zeros((ROW_TILE, D_MODEL), BF16)
        n_tail = zs_ref.shape[0] // ROW_TILE - nu_ref[0]
        tail = lambda act: lax.fori_loop(
            0, n_tail, lambda i, c: (act(z_copy((nu_ref[0] + i) * ROW_TILE, 0)), c)[1], 0)
        tail(lambda c: c.start())
        tail(lambda c: c.wait())


def _experts(base, n_tiles, n_used, xs, w_gate, b_gate, w_up, b_up, w_down, b_down):
    b_spec = pl.BlockSpec((None, 1, D_MODEL), lambda e, *_: (e, 0, 0))
    any_spec = pl.BlockSpec(memory_space=pl.ANY)
    return pl.pallas_call(
        _experts_kernel,
        grid_spec=pltpu.PrefetchScalarGridSpec(
            num_scalar_prefetch=3,
            grid=(N_EXPERTS,),
            in_specs=[b_spec, b_spec, b_spec, any_spec, any_spec, any_spec, any_spec],
            out_specs=any_spec,
            scratch_shapes=[pltpu.VMEM((2, 3, D_MODEL, D_MODEL), F32)] + [
                pltpu.VMEM((D_MODEL, D_MODEL), BF16) for _ in range(3)] + [
                pltpu.VMEM((X_SLOTS, ROW_TILE, D_MODEL), BF16)] + [
                pltpu.VMEM((2, ROW_TILE, D_MODEL), BF16) for _ in range(2)] + [
                pltpu.SemaphoreType.DMA((2,)), pltpu.SemaphoreType.DMA((X_SLOTS,)), pltpu.SemaphoreType.DMA((2,))],
        ),
        out_shape=jax.ShapeDtypeStruct(xs.shape, BF16),
        compiler_params=pltpu.CompilerParams(
            dimension_semantics=("arbitrary",), vmem_limit_bytes=VMEM_LIMIT_BYTES),
        name="experts",
    )(base, n_tiles, n_used, b_gate[:, None, :], b_up[:, None, :], b_down[:, None, :], w_gate, w_up, w_down, xs)


def _combine_kernel(n_first, seg_ref, src_ref, dst_ref, route_ref, x1_ref, fw_ref, zs_ref, ya_ref, yb_ref,
                    sorted_ref, sem):
    tile = pl.program_id(0)
    sorted_ref[...] = jnp.zeros_like(sorted_ref)

    def make_copy(src, dst, size):
        return pltpu.make_async_copy(zs_ref.at[pl.ds(dst, size)], sorted_ref.at[pl.ds(src, size)], sem)

    _segment_copies(seg_ref, src_ref, dst_ref, tile, make_copy, start=True)
    route = route_ref[...]
    slots, gates = route[0:TOP_K, :], route[TOP_K:2 * TOP_K, :]
    n_blocks = TILE_CAP // SORT_BLOCK
    early = [_slot_matrix(slots, cb, gates) for cb in range(2)]
    _segment_copies(seg_ref, src_ref, dst_ref, tile, make_copy, start=False)
    moe = jnp.zeros((TOKEN_TILE, D_MODEL), F32)
    for cb in range(n_blocks):
        m = early[cb] if cb < len(early) else _slot_matrix(slots, cb, gates)
        moe = moe + _dot_tn(m, sorted_ref[cb * SORT_BLOCK:(cb + 1) * SORT_BLOCK, :])
    y = _rmsnorm(x1_ref[...] + moe, fw_ref[...])

    @pl.when(tile < n_first)
    def _():
        ya_ref[...] = y

    @pl.when(tile >= n_first)
    def _():
        yb_ref[...] = y


def _combine(tables, route_t, x1, final_w, zs, rows_first):
    seg, src, dst = tables
    t = x1.shape[0]
    assert rows_first % TOKEN_TILE == 0 and 0 < rows_first < t and TILE_CAP % SORT_BLOCK == 0
    n_tiles = t // TOKEN_TILE
    n_first = rows_first // TOKEN_TILE
    return pl.pallas_call(
        functools.partial(_combine_kernel, n_first),
        grid_spec=pltpu.PrefetchScalarGridSpec(
            num_scalar_prefetch=3,
            grid=(n_tiles,),
            in_specs=[
                pl.BlockSpec((SUBLANES, TOKEN_TILE), lambda i, *_: (0, i)),
                pl.BlockSpec((TOKEN_TILE, D_MODEL), lambda i, *_: (i, 0)),
                pl.BlockSpec((1, D_MODEL), lambda i, *_: (0, 0)),
                pl.BlockSpec(memory_space=pl.ANY),
            ],
            out_specs=[
                pl.BlockSpec((TOKEN_TILE, D_MODEL), lambda i, *_: (jnp.minimum(i, n_first - 1), 0)),
                pl.BlockSpec((TOKEN_TILE, D_MODEL), lambda i, *_: (jnp.maximum(i - n_first, 0), 0)),
            ],
            scratch_shapes=[pltpu.VMEM((TILE_CAP, D_MODEL), BF16), pltpu.SemaphoreType.DMA(())],
        ),
        out_shape=[jax.ShapeDtypeStruct((rows_first, D_MODEL), F32),
                   jax.ShapeDtypeStruct((t - rows_first, D_MODEL), F32)],
        compiler_params=pltpu.CompilerParams(
            dimension_semantics=("arbitrary",), vmem_limit_bytes=VMEM_LIMIT_BYTES),
        name="combine",
    )(seg, src, dst, route_t, x1, final_w, zs)


def _moe_and_final_norm(x1, rows_first, norm_ffn_w, router_w, router_b, w_gate, b_gate, w_up, b_up, w_down, b_down,
                        final_w):
    t = x1.shape[0]
    assert t % TOKEN_TILE == 0
    n_tiles = t // TOKEN_TILE
    n_rows_total = TOP_K * t + n_tiles * N_EXPERTS * (SEG_ALIGN - 1) + N_EXPERTS * (ROW_TILE - 1)
    n_rows_total = (n_rows_total + ROW_TILE - 1) // ROW_TILE * ROW_TILE
    xn, meta, cnt = _router(x1, norm_ffn_w.reshape(1, D_MODEL), router_w, router_b)
    cnt = cnt[:, 0, :N_EXPERTS].astype(jnp.int32)
    chunks, fill, gap, base, n_row_tiles, n_used = _segment_tables(cnt)
    route_t = jnp.concatenate([meta[:, 2 * TOP_K:3 * TOP_K], meta[:, TOP_K:2 * TOP_K]], axis=1).T
    xs = _dispatch((*chunks, fill, gap, n_used), route_t, xn, n_rows_total)
    zs = _experts(base, n_row_tiles, n_used, xs, w_gate, b_gate, w_up, b_up, w_down, b_down)
    return _combine(chunks, route_t, x1, final_w.reshape(1, D_MODEL), zs, rows_first)


def kernel(x_prompt, x_sample, state_conv, state_hgrn, lb_logits, norm_mix_w, w_in, conv_w, conv_b, gnorm_w,
           w_out, norm_ffn_w, router_w, router_b, w_gate, b_gate, w_up, b_up, w_down, b_down, final_norm_w):
    assert norm_mix_w.shape[0] == 1 and lb_logits.shape[0] == 2, "single-layer step"
    b, l, _ = x_prompt.shape
    nb, ls, _ = x_sample.shape
    lbl = lb_logits.astype(F32)
    nw = norm_mix_w[0].reshape(1, D_MODEL)
    w_in_bf = w_in[0].astype(BF16)
    w_out_bf = w_out[0].astype(BF16)
    cw, cb = conv_w[0], conv_b[0].reshape(1, D_CONV)
    gnw = gnorm_w[0].reshape(1, D_HGRN)
    rows_p, rows_s = b * l, nb * ls
    x1, conv_p, hgrn_p = _mix_prompt(x_prompt, lbl, nw, w_in_bf, cw, cb, gnw, w_out_bf, rows_p + rows_s)
    x1, conv_s, hgrn_s = _mix_sample(x_sample, state_conv[0], state_hgrn[0], lbl, nw, w_in_bf, cw, cb, gnw,
                                     w_out_bf, x1, rows_p)
    y_p, y_s = _moe_and_final_norm(x1, rows_p, norm_ffn_w[0], router_w[0], router_b[0], w_gate[0], b_gate[0],
                                   w_up[0], b_up[0], w_down[0], b_down[0], final_norm_w)
    conv_p = conv_p[:, SUBLANES - 2:, :]
    conv_s = conv_s[:, ls - 2:, :]
    return (y_p.reshape(b, l, D_MODEL), y_s.reshape(nb, ls, D_MODEL), conv_p[None], hgrn_p[None], conv_s[None],
            hgrn_s[None])
```

```python
import functools

import numpy as np
import jax
import jax.numpy as jnp
from jax import lax
from jax.experimental import pallas as pl
from jax.experimental.pallas import tpu as pltpu

F32 = jnp.float32
BF16 = jnp.bfloat16

D_MODEL = 1024
D_CONV = 512
D_HGRN = 512
N_HEADS = 4
D_HEAD = 128
D_PROJ = 3 * D_CONV + 4 * D_HGRN
N_EXPERTS = 32
TOP_K = 4
SWIGLU_LIMIT = 7.0
SWIGLU_ALPHA = 1.702
EPS = 1e-6
PROMPT_CHUNK = 64

LANES = 128
SUBLANES = 8
BF16_ROWS = 16
VMEM_LIMIT_BYTES = 56 * 1024 * 1024

MIX_SLAB = 128
MIX_ROWS = 512
MIX_PART = 256
MIX_SEQS = 16
TOKEN_TILE = 512
SEG_ALIGN = BF16_ROWS
ROW_TILE = 512
SEG_SIZES = tuple(SEG_ALIGN << i for i in reversed(range(6)))
TILE_CAP = TOP_K * TOKEN_TILE + N_EXPERTS * SEG_ALIGN

NT_DIMS = (((1,), (1,)), ((), ()))
TN_DIMS = (((0,), (0,)), ((), ()))


def _dot(a, b):
    return jnp.dot(a, b, preferred_element_type=F32)


def _dot_nt(a, b):
    return lax.dot_general(a, b, NT_DIMS, preferred_element_type=F32)


def _dot_tn(a, b):
    return lax.dot_general(a, b, TN_DIMS, preferred_element_type=F32)


def _split2(x):
    hi = x.astype(BF16)
    lo = (x - hi.astype(F32)).astype(BF16)
    return hi, lo


def _rmsnorm(x, w):
    return x * lax.rsqrt(jnp.mean(x * x, axis=-1, keepdims=True) + EPS) * w


def _level_exponent_matrix(n, chunk, h):
    x = np.zeros((n, n), np.float32)
    for t in range(n):
        base = t - t % (2 * h)
        m = base + h - 1
        if t % (2 * h) >= h:
            x[t, m + 1:t + 1] = 1.0
        else:
            x[t, t + 1:m + 1] = 1.0
    return x


def _mix_constants(n, chunk, mxu_levels, all_levels):
    t = np.arange(n)
    same_chunk = (t[:, None] // chunk) == (t[None, :] // chunk)
    tri = (same_chunk & (t[None, :] <= t[:, None])).astype(np.float32)
    suf = (same_chunk & (t[None, :] > t[:, None])).astype(np.float32)
    cmat = np.concatenate([tri, suf] + [_level_exponent_matrix(n, chunk, h) for h in mxu_levels], axis=0)
    masks = []
    for h in all_levels:
        blk = (t[:, None] // (2 * h)) == (t[None, :] // (2 * h))
        masks.append((blk & ((t[:, None] % (2 * h)) >= h) & ((t[None, :] % (2 * h)) < h)).astype(np.float32))
    masks.append(np.eye(n, dtype=np.float32))
    return jnp.asarray(cmat, BF16), jnp.asarray(np.stack(masks), F32)


def _vpu_level_exponent(a, h):
    n = a.shape[0]
    pieces = []
    for j in range(n // (2 * h)):
        b = j * 2 * h
        ref = a[b + h - 1:b + h, :]
        pieces.append(ref - a[b:b + h, :])
        pieces.append(a[b + h:b + 2 * h, :] - ref)
    return jnp.concatenate(pieces, axis=0)


def _chunk_prefix_sums(x, chunk):
    n, width = x.shape
    x3 = x.reshape(n // SUBLANES, SUBLANES, width)
    pos = lax.broadcasted_iota(jnp.int32, (1, SUBLANES, 1), 1)
    d = 1
    while d < SUBLANES:
        x3 = x3 + jnp.where(pos >= d, pltpu.roll(x3, d, 1), 0.0)
        d *= 2
    x = x3.reshape(n, width)
    pieces = []
    for b in range(0, n, SUBLANES):
        blk = x[b:b + SUBLANES, :]
        if b % chunk:
            blk = blk + pieces[-1][SUBLANES - 1:SUBLANES, :]
        pieces.append(blk)
    return jnp.concatenate(pieces, axis=0)


def _small_level_exponent(a, logf, h):
    n = a.shape[0]
    pos = lax.broadcasted_iota(jnp.int32, (SUBLANES, 1), 0)
    if h == 1:
        odd = lax.broadcasted_iota(jnp.int32, (n, 1), 0) % 2 == 1
        return jnp.where(odd, logf, 0.0)
    pieces = []
    for b in range(0, n, SUBLANES):
        blk = a[b:b + SUBLANES, :]
        if h == 4:
            ref = blk[3:4, :]
        else:
            ref = jnp.where(pos < 4, blk[1:2, :], blk[5:6, :])
        pieces.append(jnp.where(pos % (2 * h) >= h, blk - ref, ref - blk))
    return jnp.concatenate(pieces, axis=0)


def _forget_lower_bound(lbl):
    m = jnp.max(lbl, axis=0, keepdims=True)
    e = jnp.exp(lbl - m)
    return e[0:1, :] / jnp.sum(e, axis=0, keepdims=True)


PROJ_BLOCK = 512
N_PROJ_BLOCKS = D_PROJ // PROJ_BLOCK


class _ColumnBlocks:
    def __init__(self, h, w_in_ref):
        self.h, self.w, self.done = h, w_in_ref, []

    def next_block(self):
        c = len(self.done)
        if c < N_PROJ_BLOCKS:
            self.done.append(_dot(self.h, self.w[:, c * PROJ_BLOCK:(c + 1) * PROJ_BLOCK]))

    def block(self, c):
        while len(self.done) <= c:
            self.next_block()
        return self.done[c]


def _gates(ff, lb):
    e = jnp.exp(-jnp.abs(ff))
    r = 1.0 / (1.0 + e)
    pos = ff >= 0
    sig = jnp.where(pos, r, e * r)
    sig_neg = jnp.where(pos, e * r, r)
    logf = jnp.log(lb + (1.0 - lb) * sig)
    kk = (1.0 - lb) * sig_neg
    return logf, kk


def _intra_scores_times_v(q, kk, vi, exps, masks_ref):
    n = q.shape[0]
    sc = [jnp.zeros((n, n), F32) for _ in range(N_HEADS)]
    for l, ex in enumerate(exps):
        if ex is None:
            qh, kh = q.astype(BF16), kk.astype(BF16)
        else:
            w = jnp.exp(ex)
            qh, kh = (q * w).astype(BF16), (kk * w).astype(BF16)
        mask = masks_ref[l]
        for hd in range(N_HEADS):
            hs = slice(hd * D_HEAD, (hd + 1) * D_HEAD)
            sc[hd] = sc[hd] + mask * _dot_nt(qh[:, hs], kh[:, hs])
    vb = vi.astype(BF16)
    return [_dot(sc[hd].astype(BF16), vb[:, hd * D_HEAD:(hd + 1) * D_HEAD]) for hd in range(N_HEADS)]


def _head_out(o, g, gnw):
    parts = []
    for hd in range(N_HEADS):
        oh = o[:, hd * D_HEAD:(hd + 1) * D_HEAD]
        parts.append(oh * lax.rsqrt(jnp.mean(oh * oh, axis=-1, keepdims=True) + EPS))
    on = jnp.concatenate(parts, axis=1)
    return on * gnw * (g * jax.nn.sigmoid(g))


def _pad_rows_bf16(a):
    return jnp.concatenate([a, jnp.zeros_like(a)], axis=0).astype(BF16)


PROMPT_VPU_LEVELS = (32, 16, 8)
MXU_LEVELS = (4, 2, 1)


def _mix_prompt_kernel(n_r, n_steps, *refs):
    step = pl.program_id(0)

    @pl.when(step < n_steps)
    def _():
        _mix_prompt_step(step % n_r, n_r, *refs)

    @pl.when(step >= n_steps)
    def _():
        x1_ref = refs[N_MIX_PROMPT_INPUTS]
        x1_ref[...] = jnp.zeros_like(x1_ref)


N_MIX_PROMPT_INPUTS = 9


def _mix_prompt_step(r, n_r, x_ref, lbl_ref, nw_ref, win_ref, cw_ref, cb_ref, gnw_ref, wout_ref, masks_ref,
                     x1_ref, nconv_ref, nstate_ref, st_ref, tail_ref):
    rows = x_ref.shape[0]

    @pl.when(r == 0)
    def _():
        st_ref[...] = jnp.zeros_like(st_ref)
        tail_ref[...] = jnp.zeros_like(tail_ref)

    lb = _forget_lower_bound(lbl_ref[...])
    cw = cw_ref[...]
    n_chunks = MIX_SLAB // PROMPT_CHUNK
    rid = lax.broadcasted_iota(jnp.int32, (MIX_PART, 1), 0)
    t0 = tail_ref[SUBLANES - 2:SUBLANES - 1, :]
    t1 = tail_ref[SUBLANES - 1:SUBLANES, :]
    states = [st_ref[hd] for hd in range(N_HEADS)]

    n_parts = rows // MIX_PART
    part_rows = lambda p: slice(p * MIX_PART, (p + 1) * MIX_PART)
    project = lambda p: _ColumnBlocks(_rmsnorm(x_ref[part_rows(p), :], nw_ref[...]).astype(BF16), win_ref)
    nxt = project(0)
    for p in range(n_parts):
        pr = part_rows(p)
        x = x_ref[pr, :]
        bg, cg, vc, q, ff, vi, g = (nxt.block(c) for c in range(N_PROJ_BLOCKS))
        if p + 1 < n_parts:
            nxt = project(p + 1)
            nxt.block(N_PROJ_BLOCKS - 1)

        u = cg * vc
        u1 = jnp.where(rid == 0, t1, pltpu.roll(u, 1, 0))
        u2 = jnp.where(rid == 0, t0, jnp.where(rid == 1, t1, pltpu.roll(u, 2, 0)))
        y_conv = bg * (cw[0:1, :] * u2 + cw[1:2, :] * u1 + cw[2:3, :] * u + cb_ref[...])
        u_last = u[MIX_PART - SUBLANES:MIX_PART, :]
        t0, t1 = u_last[SUBLANES - 2:SUBLANES - 1, :], u_last[SUBLANES - 1:SUBLANES, :]

        logf, kk = _gates(ff, lb)
        o_parts = []
        for s in range(MIX_PART // MIX_SLAB):
            sl = slice(s * MIX_SLAB, (s + 1) * MIX_SLAB)
            qs, ks, vs = q[sl], kk[sl], vi[sl]
            a_pre = _chunk_prefix_sums(logf[sl], PROMPT_CHUNK)
            a_suf = jnp.concatenate(
                [a_pre[(c + 1) * PROMPT_CHUNK - 1:(c + 1) * PROMPT_CHUNK, :]
                 - a_pre[c * PROMPT_CHUNK:(c + 1) * PROMPT_CHUNK, :] for c in range(n_chunks)], axis=0)
            exps = [_vpu_level_exponent(a_pre, h) for h in PROMPT_VPU_LEVELS]
            exps += [_small_level_exponent(a_pre, logf[sl], h) for h in MXU_LEVELS]
            exps.append(None)
            intra = _intra_scores_times_v(qs, ks, vs, exps, masks_ref)
            ea = jnp.exp(a_pre)
            qa = (qs * ea).astype(BF16)
            kb = (ks * jnp.exp(a_suf)).astype(BF16)
            vb = vs.astype(BF16)
            for c in range(n_chunks):
                cr = slice(c * PROMPT_CHUNK, (c + 1) * PROMPT_CHUNK)
                last = (c + 1) * PROMPT_CHUNK - 1
                heads = []
                for hd in range(N_HEADS):
                    hs = slice(hd * D_HEAD, (hd + 1) * D_HEAD)
                    heads.append(_dot_nt(qa[cr, hs], states[hd].astype(BF16)) + intra[hd][cr, :])
                    states[hd] = states[hd] * ea[last:last + 1, hs] + _dot_tn(vb[cr, hs], kb[cr, hs])
                o_parts.append(jnp.concatenate(heads, axis=1))

        o = _head_out(jnp.concatenate(o_parts, axis=0), g, gnw_ref[...])
        mix_in = jnp.concatenate([y_conv, o], axis=1).astype(BF16)
        x1_ref[pr, :] = x + _dot(mix_in, wout_ref[...])

    tail_ref[...] = u_last
    for hd in range(N_HEADS):
        st_ref[hd] = states[hd]

    @pl.when(r == n_r - 1)
    def _():
        nconv_ref[...] = u_last
        for hd in range(N_HEADS):
            nstate_ref[hd] = states[hd].T


def _mix_prompt(x, lbl, nw, w_in, cw, cb, gnw, w_out, total_rows):
    b, l, _ = x.shape
    assert l % MIX_ROWS == 0 and MIX_ROWS % MIX_PART == 0 and MIX_PART % MIX_SLAB == 0 and l % PROMPT_CHUNK == 0
    assert total_rows % MIX_ROWS == 0
    n_r = l // MIX_ROWS
    n_steps = b * n_r
    seq = lambda s: jnp.minimum(s, n_steps - 1) // n_r
    _, masks = _mix_constants(MIX_SLAB, PROMPT_CHUNK, (), PROMPT_VPU_LEVELS + MXU_LEVELS)
    const = lambda shape: pl.BlockSpec(shape, lambda s: (0,) * len(shape))
    return pl.pallas_call(
        functools.partial(_mix_prompt_kernel, n_r, n_steps),
        grid=(total_rows // MIX_ROWS,),
        in_specs=[
            pl.BlockSpec((None, MIX_ROWS, D_MODEL), lambda s: (seq(s), jnp.minimum(s, n_steps - 1) % n_r, 0)),
            const((2, D_HGRN)), const((1, D_MODEL)), const((D_MODEL, D_PROJ)), const((3, D_CONV)),
            const((1, D_CONV)), const((1, D_HGRN)), const((D_MODEL, D_MODEL)), const(masks.shape),
        ],
        out_specs=[
            pl.BlockSpec((MIX_ROWS, D_MODEL), lambda s: (s, 0)),
            pl.BlockSpec((None, SUBLANES, D_CONV), lambda s: (seq(s), 0, 0)),
            pl.BlockSpec((None, N_HEADS, D_HEAD, D_HEAD), lambda s: (seq(s), 0, 0, 0)),
        ],
        out_shape=[
            jax.ShapeDtypeStruct((total_rows, D_MODEL), F32),
            jax.ShapeDtypeStruct((b, SUBLANES, D_CONV), F32),
            jax.ShapeDtypeStruct((b, N_HEADS, D_HEAD, D_HEAD), F32),
        ],
        scratch_shapes=[
            pltpu.VMEM((N_HEADS, D_HEAD, D_HEAD), F32),
            pltpu.VMEM((SUBLANES, D_CONV), F32),
        ],
        compiler_params=pltpu.CompilerParams(
            dimension_semantics=("arbitrary",), vmem_limit_bytes=VMEM_LIMIT_BYTES),
        name="mix_prompt",
    )(x, lbl, nw, w_in, cw, cb, gnw, w_out, masks)


def _mix_sample_kernel(x_ref, cs_ref, hs_ref, lbl_ref, nw_ref, win_ref, cw_ref, cb_ref, gnw_ref, wout_ref,
                       cmat_ref, masks_ref, x1_in_ref, x1_ref, nconv_ref, nstate_ref, o_ref):
    del x1_in_ref
    nseq, length, _ = nconv_ref.shape
    rows = nseq * length
    x = x_ref[...]
    proj = _ColumnBlocks(_rmsnorm(x, nw_ref[...]).astype(BF16), win_ref)
    bg, cg, vc, q, ff, vi, g = (proj.block(c) for c in range(N_PROJ_BLOCKS))
    lb = _forget_lower_bound(lbl_ref[...])

    u = cg * vc
    cs = cs_ref[...]
    expand = lambda a: jnp.broadcast_to(a, (nseq, length, D_CONV)).reshape(rows, D_CONV)
    t0 = expand(cs[:, 0:1, :])
    t1 = expand(cs[:, 1:2, :])
    pos = lax.broadcasted_iota(jnp.int32, (rows, 1), 0) % length
    u1 = jnp.where(pos == 0, t1, pltpu.roll(u, 1, 0))
    u2 = jnp.where(pos == 0, t0, jnp.where(pos == 1, t1, pltpu.roll(u, 2, 0)))
    cw = cw_ref[...]
    y_conv = bg * (cw[0:1, :] * u2 + cw[1:2, :] * u1 + cw[2:3, :] * u + cb_ref[...])
    nconv_ref[...] = u.reshape(nseq, length, D_CONV)

    logf, kk = _gates(ff, lb)
    lf_hi, lf_lo = _split2(logf)
    ex = _dot(cmat_ref[...], lf_hi) + _dot(cmat_ref[...], lf_lo)
    a_pre = ex[0:rows]
    a_suf = ex[rows:2 * rows]
    exps = [ex[(2 + i) * rows:(3 + i) * rows] for i in range(len(MXU_LEVELS))] + [None]
    intra = _intra_scores_times_v(q, kk, vi, exps, masks_ref)
    ea = jnp.exp(a_pre)
    qa = q * ea
    kb = kk * jnp.exp(a_suf)
    for s in range(nseq):
        cr = slice(s * length, (s + 1) * length)
        last = (s + 1) * length - 1
        for hd in range(N_HEADS):
            hs = slice(hd * D_HEAD, (hd + 1) * D_HEAD)
            st = hs_ref[s, hd].T
            inter = _dot_nt(_pad_rows_bf16(qa[cr, hs]), st.astype(BF16))[0:length, :]
            o_ref[cr, hs] = inter + intra[hd][cr, :]
            st_new = st * ea[last:last + 1, hs] + _dot_tn(_pad_rows_bf16(vi[cr, hs]), _pad_rows_bf16(kb[cr, hs]))
            nstate_ref[s, hd] = st_new.T

    o = _head_out(o_ref[...], g, gnw_ref[...])
    mix_in = jnp.concatenate([y_conv, o], axis=1).astype(BF16)
    x1_ref[...] = x + _dot(mix_in, wout_ref[...])


def _mix_sample(x, conv_state, hgrn_state, lbl, nw, w_in, cw, cb, gnw, w_out, x1_flat, row_offset):
    nb, length, _ = x.shape
    rows = MIX_SEQS * length
    assert nb % MIX_SEQS == 0 and length == SUBLANES and row_offset % rows == 0
    first_block = row_offset // rows
    cmat, masks = _mix_constants(rows, length, MXU_LEVELS, MXU_LEVELS)
    const = lambda shape: pl.BlockSpec(shape, lambda i: (0,) * len(shape))
    return pl.pallas_call(
        _mix_sample_kernel,
        grid=(nb // MIX_SEQS,),
        in_specs=[
            pl.BlockSpec((rows, D_MODEL), lambda i: (i, 0)),
            pl.BlockSpec((MIX_SEQS, 2, D_CONV), lambda i: (i, 0, 0)),
            pl.BlockSpec((MIX_SEQS, N_HEADS, D_HEAD, D_HEAD), lambda i: (i, 0, 0, 0)),
            const((2, D_HGRN)), const((1, D_MODEL)), const((D_MODEL, D_PROJ)), const((3, D_CONV)),
            const((1, D_CONV)), const((1, D_HGRN)), const((D_MODEL, D_MODEL)), const(cmat.shape),
            const(masks.shape), pl.BlockSpec(memory_space=pl.ANY),
        ],
        out_specs=[
            pl.BlockSpec((rows, D_MODEL), lambda i: (first_block + i, 0)),
            pl.BlockSpec((MIX_SEQS, length, D_CONV), lambda i: (i, 0, 0)),
            pl.BlockSpec((MIX_SEQS, N_HEADS, D_HEAD, D_HEAD), lambda i: (i, 0, 0, 0)),
        ],
        out_shape=[
            jax.ShapeDtypeStruct(x1_flat.shape, F32),
            jax.ShapeDtypeStruct((nb, length, D_CONV), F32),
            jax.ShapeDtypeStruct((nb, N_HEADS, D_HEAD, D_HEAD), F32),
        ],
        scratch_shapes=[pltpu.VMEM((rows, D_HGRN), F32)],
        input_output_aliases={12: 0},
        compiler_params=pltpu.CompilerParams(
            dimension_semantics=("arbitrary",), vmem_limit_bytes=VMEM_LIMIT_BYTES),
        name="mix_sample",
    )(x.reshape(nb * length, D_MODEL), conv_state, hgrn_state, lbl, nw, w_in, cw, cb, gnw, w_out, cmat, masks,
      x1_flat)


def _router_kernel(x1_ref, nw_ref, rw_hi_ref, rw_lo_ref, rb_ref, ltri_ref, utri_ref, xn_ref, meta_ref, cnt_ref):
    n = x1_ref.shape[0]
    xn = _rmsnorm(x1_ref[...], nw_ref[...])
    xn_ref[...] = xn.astype(BF16)
    x_hi, x_lo = _split2(xn)
    logits = (_dot(x_hi, rw_hi_ref[...]) + _dot(x_lo, rw_hi_ref[...]) + _dot(x_hi, rw_lo_ref[...])
              + rb_ref[...])
    lane = lax.broadcasted_iota(jnp.int32, (n, LANES), 1).astype(F32)
    work = logits
    vals, ids = [], []
    for _ in range(TOP_K):
        m = jnp.max(work, axis=-1, keepdims=True)
        i = jnp.min(jnp.where(work == m, lane, float(LANES)), axis=-1, keepdims=True)
        vals.append(m)
        ids.append(i)
        work = jnp.where(lane == i, -jnp.inf, work)
    es = [jnp.exp(v - vals[0]) for v in vals]
    den = es[0] + es[1] + es[2] + es[3]
    gates = [e / den for e in es]

    onehots = [(lane == i) for i in ids]
    multi = jnp.zeros((n, LANES), F32)
    for oh in onehots:
        multi = multi + oh.astype(F32)
    counts = jnp.sum(multi, axis=0, keepdims=True)
    before = _dot(ltri_ref[...], multi.astype(BF16))
    seg = jnp.ceil(counts * (1.0 / SEG_ALIGN)) * SEG_ALIGN
    seg_rows = jnp.broadcast_to(seg, (BF16_ROWS, LANES)).astype(BF16)
    seg_off = _dot(seg_rows, utri_ref[...])[0:1, :]
    slot_of = seg_off + before
    meta = jnp.zeros((n, LANES), F32)
    for k in range(TOP_K):
        slot = jnp.sum(jnp.where(onehots[k], slot_of, 0.0), axis=-1, keepdims=True)
        meta = jnp.where(lane == k, ids[k], meta)
        meta = jnp.where(lane == TOP_K + k, gates[k], meta)
        meta = jnp.where(lane == 2 * TOP_K + k, slot, meta)
    meta_ref[...] = meta
    cnt_ref[...] = jnp.broadcast_to(counts, (SUBLANES, LANES))


def _router(x1, nw, router_w, router_b):
    t = x1.shape[0]
    n_tiles = t // TOKEN_TILE
    rw = jnp.zeros((D_MODEL, LANES), F32).at[:, :N_EXPERTS].set(router_w)
    rw_hi, rw_lo = _split2(rw)
    rb = jnp.full((1, LANES), -1e30, F32).at[0, :N_EXPERTS].set(router_b)
    idx = np.arange(TOKEN_TILE)
    ltri = jnp.asarray(idx[None, :] < idx[:, None], BF16)
    lid = np.arange(LANES)
    utri = jnp.asarray(lid[:, None] < lid[None, :], BF16)
    const = lambda shape: pl.BlockSpec(shape, lambda i: (0,) * len(shape))
    return pl.pallas_call(
        _router_kernel,
        grid=(n_tiles,),
        in_specs=[
            pl.BlockSpec((TOKEN_TILE, D_MODEL), lambda i: (i, 0)),
            const((1, D_MODEL)), const((D_MODEL, LANES)), const((D_MODEL, LANES)), const((1, LANES)),
            const((TOKEN_TILE, TOKEN_TILE)), const((LANES, LANES)),
        ],
        out_specs=[
            pl.BlockSpec((TOKEN_TILE, D_MODEL), lambda i: (i, 0)),
            pl.BlockSpec((TOKEN_TILE, LANES), lambda i: (i, 0)),
            pl.BlockSpec((None, SUBLANES, LANES), lambda i: (i, 0, 0)),
        ],
        out_shape=[
            jax.ShapeDtypeStruct((t, D_MODEL), BF16),
            jax.ShapeDtypeStruct((t, LANES), F32),
            jax.ShapeDtypeStruct((n_tiles, SUBLANES, LANES), F32),
        ],
        compiler_params=pltpu.CompilerParams(
            dimension_semantics=("arbitrary",), vmem_limit_bytes=VMEM_LIMIT_BYTES),
        name="router",
    )(x1, nw, rw_hi, rw_lo, rb, ltri, utri)


def _segment_tables(cnt):
    seg = (cnt + SEG_ALIGN - 1) // SEG_ALIGN * SEG_ALIGN
    src = jnp.cumsum(seg, axis=1) - seg
    tot = jnp.sum(seg, axis=0)
    cap = (tot + ROW_TILE - 1) // ROW_TILE * ROW_TILE
    base = jnp.cumsum(cap) - cap
    dst = base[None, :] + jnp.cumsum(seg, axis=0) - seg
    n_used = jnp.sum(cap) // ROW_TILE
    fill = base + tot
    lane = jnp.arange(N_EXPERTS)
    counts, srcs, dsts = [], [], []
    for size in SEG_SIZES:
        has = (seg & size) != 0
        done = seg & ~(2 * size - 1)
        place = has[:, :, None] & ((jnp.cumsum(has, axis=1) - 1)[:, :, None] == lane[None, None, :])
        counts.append(jnp.sum(has, axis=1))
        srcs.append(jnp.sum(jnp.where(place, (src + done)[:, :, None], 0), axis=1))
        dsts.append(jnp.sum(jnp.where(place, (dst + done)[:, :, None], 0), axis=1))
    i32 = lambda a: a.astype(jnp.int32).reshape(-1)
    chunks = (i32(jnp.stack(counts, axis=1)), i32(jnp.stack(srcs, axis=1)), i32(jnp.stack(dsts, axis=1)))
    return chunks, i32(fill), i32(cap - tot), i32(base), i32(cap // ROW_TILE), i32(n_used)


def _for_each_chunk(n, fn):
    for size in SEG_SIZES:
        done = n & ~(2 * size - 1)

        @pl.when((n & size) != 0)
        def _():
            fn(done, size)


def _segment_copies(cnt_ref, src_ref, dst_ref, tile, make_copy, start):
    for si, size in enumerate(SEG_SIZES):
        first = (tile * len(SEG_SIZES) + si) * N_EXPERTS

        def one(p, carry, si=si, size=size, first=first):
            copy = make_copy(pl.multiple_of(src_ref[first + p], SEG_ALIGN),
                             pl.multiple_of(dst_ref[first + p], SEG_ALIGN), size)
            if start:
                copy.start(priority=si % 2)
            else:
                copy.wait()
            return carry
        lax.fori_loop(0, cnt_ref[tile * len(SEG_SIZES) + si], one, 0)


SORT_BLOCK = 512


def _slot_matrix(slots, block, gates):
    row = (lax.broadcasted_iota(jnp.int32, (SORT_BLOCK, TOKEN_TILE), 0) + block * SORT_BLOCK).astype(F32)
    m = jnp.zeros((SORT_BLOCK, TOKEN_TILE), F32)
    for k in range(TOP_K):
        m = jnp.where(row == slots[k:k + 1, :], 1.0 if gates is None else gates[k:k + 1, :], m)
    return m.astype(BF16)


def _dispatch_kernel(seg_ref, src_ref, dst_ref, fill_ref, gap_ref, nu_ref, slot_ref, xn_ref, xs_ref, sorted_ref,
                     zero_ref, sem, seg_sem):
    tile = pl.program_id(0)
    n_tiles = pl.num_programs(0)
    buf = tile % 2

    @pl.when(tile == 0)
    def _():
        zero_ref[...] = jnp.zeros_like(zero_ref)

        def fill_copy(row, size):
            return pltpu.make_async_copy(zero_ref.at[pl.ds(0, size)],
                                         xs_ref.at[pl.ds(pl.multiple_of(row, SEG_ALIGN), size)], sem)

        def gaps(act):
            def per_expert(e, carry):
                _for_each_chunk(gap_ref[e], lambda done, size: act(fill_copy(fill_ref[e] + done, size)))
                return carry
            lax.fori_loop(0, N_EXPERTS, per_expert, 0)

        def tail(act):
            n_tail = xs_ref.shape[0] // ROW_TILE - nu_ref[0]
            lax.fori_loop(0, n_tail, lambda i, c: (act(fill_copy((nu_ref[0] + i) * ROW_TILE, ROW_TILE)), c)[1], 0)

        gaps(lambda c: c.start())
        tail(lambda c: c.start())
        gaps(lambda c: c.wait())
        tail(lambda c: c.wait())

    slots = slot_ref[...]
    xn = xn_ref[...]
    for rb in range(TILE_CAP // SORT_BLOCK):
        sorted_ref[buf, rb * SORT_BLOCK:(rb + 1) * SORT_BLOCK, :] = _dot(
            _slot_matrix(slots, rb, None), xn).astype(BF16)

    def copier(half):
        return lambda src, dst, size: pltpu.make_async_copy(
            sorted_ref.at[half, pl.ds(src, size)], xs_ref.at[pl.ds(dst, size)], seg_sem.at[half])

    _segment_copies(seg_ref, src_ref, dst_ref, tile, copier(buf), start=True)

    @pl.when(tile > 0)
    def _():
        _segment_copies(seg_ref, src_ref, dst_ref, tile - 1, copier(1 - buf), start=False)

    @pl.when(tile == n_tiles - 1)
    def _():
        _segment_copies(seg_ref, src_ref, dst_ref, tile, copier(buf), start=False)


def _dispatch(tables, slot_t, xn, n_rows_total):
    t = xn.shape[0]
    n_tiles = t // TOKEN_TILE
    return pl.pallas_call(
        _dispatch_kernel,
        grid_spec=pltpu.PrefetchScalarGridSpec(
            num_scalar_prefetch=len(tables),
            grid=(n_tiles,),
            in_specs=[
                pl.BlockSpec((SUBLANES, TOKEN_TILE), lambda i, *_: (0, i)),
                pl.BlockSpec((TOKEN_TILE, D_MODEL), lambda i, *_: (i, 0)),
            ],
            out_specs=pl.BlockSpec(memory_space=pl.ANY),
            scratch_shapes=[
                pltpu.VMEM((2, TILE_CAP, D_MODEL), BF16),
                pltpu.VMEM((ROW_TILE, D_MODEL), BF16),
                pltpu.SemaphoreType.DMA(()),
                pltpu.SemaphoreType.DMA((2,)),
            ],
        ),
        out_shape=jax.ShapeDtypeStruct((n_rows_total, D_MODEL), BF16),
        compiler_params=pltpu.CompilerParams(
            dimension_semantics=("arbitrary",), vmem_limit_bytes=VMEM_LIMIT_BYTES),
        name="dispatch",
    )(*tables, slot_t, xn)


X_SLOTS = 3
W_FETCH_STEPS = (0, 1, 3)


def _experts_kernel(base_ref, nt_ref, nu_ref, bg_ref, bu_ref, bd_ref, wg_hbm, wu_hbm, wd_hbm, xs_ref, zs_ref,
                    w_f32, wg_bf, wu_bf, wd_bf, xbuf, hbuf, zbuf, w_sem, in_sem, out_sem):
    e = pl.program_id(0)
    n_e = pl.num_programs(0)
    n = nt_ref[e]
    base = base_ref[e]
    w_slot = e % 2
    has_next = e + 1 < n_e

    def w_copy(j, expert, slot):
        w_hbm = (wg_hbm, wu_hbm, wd_hbm)[j]
        return pltpu.make_async_copy(w_hbm.at[expert], w_f32.at[slot, j], w_sem.at[slot])

    def x_copy(i):
        rows = pl.ds(pl.multiple_of(base + i * ROW_TILE, ROW_TILE), ROW_TILE)
        return pltpu.make_async_copy(xs_ref.at[rows], xbuf.at[i % X_SLOTS], in_sem.at[i % X_SLOTS])

    def z_copy(row, slot):
        rows = pl.ds(pl.multiple_of(row, ROW_TILE), ROW_TILE)
        return pltpu.make_async_copy(zbuf.at[slot], zs_ref.at[rows], out_sem.at[slot])

    def hidden(i):
        x = xbuf[i % X_SLOTS]
        a = jnp.minimum(_dot(x, wg_bf[...]) + bg_ref[...], SWIGLU_LIMIT)
        u = jnp.clip(_dot(x, wu_bf[...]) + bu_ref[...], -SWIGLU_LIMIT, SWIGLU_LIMIT)
        hbuf[i % 2] = ((u + 1.0) * a * jax.nn.sigmoid(SWIGLU_ALPHA * a)).astype(BF16)

    def project_down(i, slot):
        @pl.when(i >= 2)
        def _():
            z_copy(base, slot).wait()
        zbuf[slot] = (_dot(hbuf[slot], wd_bf[...]) + bd_ref[...]).astype(BF16)

    @pl.when(e == 0)
    def _():
        for j in range(3):
            w_copy(j, 0, 0).start()

    for i in range(X_SLOTS):
        @pl.when(i < n)
        def _():
            x_copy(i).start(priority=1)

    def fetch_next_weights(j):
        @pl.when(has_next)
        def _():
            w_copy(j, e + 1, 1 - w_slot).start()

    fetch_next_weights(0)
    for j in range(3):
        w_copy(j, e, w_slot).wait()

    @pl.when(n > 0)
    def _():
        wg_bf[...] = w_f32[w_slot, 0].astype(BF16)
        wu_bf[...] = w_f32[w_slot, 1].astype(BF16)
        wd_bf[...] = w_f32[w_slot, 2].astype(BF16)
        x_copy(0).wait()
        hidden(0)

        def step(i, carry):
            x_copy(i + 1).wait()

            @pl.when(i + X_SLOTS < n)
            def _():
                x_copy(i + X_SLOTS).start(priority=1)

            for j in (1, 2):
                @pl.when(i == W_FETCH_STEPS[j])
                def _():
                    fetch_next_weights(j)

            project_down(i, i % 2)
            hidden(i + 1)
            z_copy(base + i * ROW_TILE, i % 2).start()
            return carry

        lax.fori_loop(0, n - 1, step, 0)
        project_down(n - 1, (n - 1) % 2)
        z_copy(base + (n - 1) * ROW_TILE, (n - 1) % 2).start()

        @pl.when(n >= 2)
        def _():
            z_copy(base, n % 2).wait()
        z_copy(base, (n - 1) % 2).wait()

    for j in (1, 2):
        @pl.when(jnp.maximum(n - 1, 0) <= W_FETCH_STEPS[j])
        def _():
            fetch_next_weights(j)

    @pl.when(e == pl.num_programs(0) - 1)
    def _():
        zbuf[0] = jnp.zeros((ROW_TILE, D_MODEL), BF16)
        n_tail = zs_ref.shape[0] // ROW_TILE - nu_ref[0]
        tail = lambda act: lax.fori_loop(
            0, n_tail, lambda i, c: (act(z_copy((nu_ref[0] + i) * ROW_TILE, 0)), c)[1], 0)
        tail(lambda c: c.start())
        tail(lambda c: c.wait())


def _experts(base, n_tiles, n_used, xs, w_gate, b_gate, w_up, b_up, w_down, b_down):
    b_spec = pl.BlockSpec((None, 1, D_MODEL), lambda e, *_: (e, 0, 0))
    any_spec = pl.BlockSpec(memory_space=pl.ANY)
    return pl.pallas_call(
        _experts_kernel,
        grid_spec=pltpu.PrefetchScalarGridSpec(
            num_scalar_prefetch=3,
            grid=(N_EXPERTS,),
            in_specs=[b_spec, b_spec, b_spec, any_spec, any_spec, any_spec, any_spec],
            out_specs=any_spec,
            scratch_shapes=[pltpu.VMEM((2, 3, D_MODEL, D_MODEL), F32)] + [
                pltpu.VMEM((D_MODEL, D_MODEL), BF16) for _ in range(3)] + [
                pltpu.VMEM((X_SLOTS, ROW_TILE, D_MODEL), BF16)] + [
                pltpu.VMEM((2, ROW_TILE, D_MODEL), BF16) for _ in range(2)] + [
                pltpu.SemaphoreType.DMA((2,)), pltpu.SemaphoreType.DMA((X_SLOTS,)), pltpu.SemaphoreType.DMA((2,))],
        ),
        out_shape=jax.ShapeDtypeStruct(xs.shape, BF16),
        compiler_params=pltpu.CompilerParams(
            dimension_semantics=("arbitrary",), vmem_limit_bytes=VMEM_LIMIT_BYTES),
        name="experts",
    )(base, n_tiles, n_used, b_gate[:, None, :], b_up[:, None, :], b_down[:, None, :], w_gate, w_up, w_down, xs)


def _combine_kernel(n_first, seg_ref, src_ref, dst_ref, route_ref, x1_ref, fw_ref, zs_ref, ya_ref, yb_ref,
                    sorted_ref, sem):
    tile = pl.program_id(0)
    n_tiles = pl.num_programs(0)
    buf = tile % 2

    def fetch(t, half):
        sorted_ref[half] = jnp.zeros((TILE_CAP, D_MODEL), BF16)
        _segment_copies(seg_ref, src_ref, dst_ref, t, copier(half), start=True)

    def copier(half):
        return lambda src, dst, size: pltpu.make_async_copy(
            zs_ref.at[pl.ds(dst, size)], sorted_ref.at[half, pl.ds(src, size)], sem.at[half])

    @pl.when(tile == 0)
    def _():
        fetch(tile, buf)

    @pl.when(tile + 1 < n_tiles)
    def _():
        fetch(tile + 1, 1 - buf)

    _segment_copies(seg_ref, src_ref, dst_ref, tile, copier(buf), start=False)
    route = route_ref[...]
    slots, gates = route[0:TOP_K, :], route[TOP_K:2 * TOP_K, :]
    moe = jnp.zeros((TOKEN_TILE, D_MODEL), F32)
    for cb in range(TILE_CAP // SORT_BLOCK):
        moe = moe + _dot_tn(_slot_matrix(slots, cb, gates), sorted_ref[buf, cb * SORT_BLOCK:(cb + 1) * SORT_BLOCK, :])
    y = _rmsnorm(x1_ref[...] + moe, fw_ref[...])

    @pl.when(tile < n_first)
    def _():
        ya_ref[...] = y

    @pl.when(tile >= n_first)
    def _():
        yb_ref[...] = y


def _combine(tables, route_t, x1, final_w, zs, rows_first):
    seg, src, dst = tables
    t = x1.shape[0]
    assert rows_first % TOKEN_TILE == 0 and 0 < rows_first < t and TILE_CAP % SORT_BLOCK == 0
    n_tiles = t // TOKEN_TILE
    n_first = rows_first // TOKEN_TILE
    return pl.pallas_call(
        functools.partial(_combine_kernel, n_first),
        grid_spec=pltpu.PrefetchScalarGridSpec(
            num_scalar_prefetch=3,
            grid=(n_tiles,),
            in_specs=[
                pl.BlockSpec((SUBLANES, TOKEN_TILE), lambda i, *_: (0, i)),
                pl.BlockSpec((TOKEN_TILE, D_MODEL), lambda i, *_: (i, 0)),
                pl.BlockSpec((1, D_MODEL), lambda i, *_: (0, 0)),
                pl.BlockSpec(memory_space=pl.ANY),
            ],
            out_specs=[
                pl.BlockSpec((TOKEN_TILE, D_MODEL), lambda i, *_: (jnp.minimum(i, n_first - 1), 0)),
                pl.BlockSpec((TOKEN_TILE, D_MODEL), lambda i, *_: (jnp.maximum(i - n_first, 0), 0)),
            ],
            scratch_shapes=[pltpu.VMEM((2, TILE_CAP, D_MODEL), BF16), pltpu.SemaphoreType.DMA((2,))],
        ),
        out_shape=[jax.ShapeDtypeStruct((rows_first, D_MODEL), F32),
                   jax.ShapeDtypeStruct((t - rows_first, D_MODEL), F32)],
        compiler_params=pltpu.CompilerParams(
            dimension_semantics=("arbitrary",), vmem_limit_bytes=VMEM_LIMIT_BYTES),
        name="combine",
    )(seg, src, dst, route_t, x1, final_w, zs)


def _moe_and_final_norm(x1, rows_first, norm_ffn_w, router_w, router_b, w_gate, b_gate, w_up, b_up, w_down, b_down,
                        final_w):
    t = x1.shape[0]
    assert t % TOKEN_TILE == 0
    n_tiles = t // TOKEN_TILE
    n_rows_total = TOP_K * t + n_tiles * N_EXPERTS * (SEG_ALIGN - 1) + N_EXPERTS * (ROW_TILE - 1)
    n_rows_total = (n_rows_total + ROW_TILE - 1) // ROW_TILE * ROW_TILE
    xn, meta, cnt = _router(x1, norm_ffn_w.reshape(1, D_MODEL), router_w, router_b)
    cnt = cnt[:, 0, :N_EXPERTS].astype(jnp.int32)
    chunks, fill, gap, base, n_row_tiles, n_used = _segment_tables(cnt)
    route_t = jnp.concatenate([meta[:, 2 * TOP_K:3 * TOP_K], meta[:, TOP_K:2 * TOP_K]], axis=1).T
    xs = _dispatch((*chunks, fill, gap, n_used), route_t, xn, n_rows_total)
    zs = _experts(base, n_row_tiles, n_used, xs, w_gate, b_gate, w_up, b_up, w_down, b_down)
    return _combine(chunks, route_t, x1, final_w.reshape(1, D_MODEL), zs, rows_first)


def kernel(x_prompt, x_sample, state_conv, state_hgrn, lb_logits, norm_mix_w, w_in, conv_w, conv_b, gnorm_w,
           w_out, norm_ffn_w, router_w, router_b, w_gate, b_gate, w_up, b_up, w_down, b_down, final_norm_w):
    assert norm_mix_w.shape[0] == 1 and lb_logits.shape[0] == 2, "single-layer step"
    b, l, _ = x_prompt.shape
    nb, ls, _ = x_sample.shape
    lbl = lb_logits.astype(F32)
    nw = norm_mix_w[0].reshape(1, D_MODEL)
    w_in_bf = w_in[0].astype(BF16)
    w_out_bf = w_out[0].astype(BF16)
    cw, cb = conv_w[0], conv_b[0].reshape(1, D_CONV)
    gnw = gnorm_w[0].reshape(1, D_HGRN)
    rows_p, rows_s = b * l, nb * ls
    x1, conv_p, hgrn_p = _mix_prompt(x_prompt, lbl, nw, w_in_bf, cw, cb, gnw, w_out_bf, rows_p + rows_s)
    x1, conv_s, hgrn_s = _mix_sample(x_sample, state_conv[0], state_hgrn[0], lbl, nw, w_in_bf, cw, cb, gnw,
                                     w_out_bf, x1, rows_p)
    y_p, y_s = _moe_and_final_norm(x1, rows_p, norm_ffn_w[0], router_w[0], router_b[0], w_gate[0], b_gate[0],
                                   w_up[0], b_up[0], w_down[0], b_down[0], final_norm_w)
    conv_p = conv_p[:, SUBLANES - 2:, :]
    conv_s = conv_s[:, ls - 2:, :]
    return (y_p.reshape(b, l, D_MODEL), y_s.reshape(nb, ls, D_MODEL), conv_p[None], hgrn_p[None], conv_s[None],
            hgrn_s[None])
```

```python
import functools

import numpy as np
import jax
import jax.numpy as jnp
from jax import lax
from jax.experimental import pallas as pl
from jax.experimental.pallas import tpu as pltpu

F32 = jnp.float32
BF16 = jnp.bfloat16

D_MODEL = 1024
D_CONV = 512
D_HGRN = 512
N_HEADS = 4
D_HEAD = 128
D_PROJ = 3 * D_CONV + 4 * D_HGRN
N_EXPERTS = 32
TOP_K = 4
SWIGLU_LIMIT = 7.0
SWIGLU_ALPHA = 1.702
EPS = 1e-6
PROMPT_CHUNK = 64

LANES = 128
SUBLANES = 8
BF16_ROWS = 16
VMEM_LIMIT_BYTES = 56 * 1024 * 1024

MIX_SLAB = 128
MIX_ROWS = 512
MIX_PART = 256
MIX_SEQS = 16
TOKEN_TILE = 512
SEG_ALIGN = BF16_ROWS
ROW_TILE = 512
SEG_SIZES = tuple(SEG_ALIGN << i for i in reversed(range(6)))
TILE_CAP = TOP_K * TOKEN_TILE + N_EXPERTS * SEG_ALIGN

NT_DIMS = (((1,), (1,)), ((), ()))
TN_DIMS = (((0,), (0,)), ((), ()))


def _dot(a, b):
    return jnp.dot(a, b, preferred_element_type=F32)


def _dot_nt(a, b):
    return lax.dot_general(a, b, NT_DIMS, preferred_element_type=F32)


def _dot_tn(a, b):
    return lax.dot_general(a, b, TN_DIMS, preferred_element_type=F32)


def _split2(x):
    hi = x.astype(BF16)
    lo = (x - hi.astype(F32)).astype(BF16)
    return hi, lo


def _rmsnorm(x, w):
    return x * lax.rsqrt(jnp.mean(x * x, axis=-1, keepdims=True) + EPS) * w


def _level_exponent_matrix(n, chunk, h):
    x = np.zeros((n, n), np.float32)
    for t in range(n):
        base = t - t % (2 * h)
        m = base + h - 1
        if t % (2 * h) >= h:
            x[t, m + 1:t + 1] = 1.0
        else:
            x[t, t + 1:m + 1] = 1.0
    return x


def _mix_constants(n, chunk, mxu_levels, all_levels):
    t = np.arange(n)
    same_chunk = (t[:, None] // chunk) == (t[None, :] // chunk)
    tri = (same_chunk & (t[None, :] <= t[:, None])).astype(np.float32)
    suf = (same_chunk & (t[None, :] > t[:, None])).astype(np.float32)
    cmat = np.concatenate([tri, suf] + [_level_exponent_matrix(n, chunk, h) for h in mxu_levels], axis=0)
    masks = []
    for h in all_levels:
        blk = (t[:, None] // (2 * h)) == (t[None, :] // (2 * h))
        masks.append((blk & ((t[:, None] % (2 * h)) >= h) & ((t[None, :] % (2 * h)) < h)).astype(np.float32))
    masks.append(np.eye(n, dtype=np.float32))
    return jnp.asarray(cmat, BF16), jnp.asarray(np.stack(masks), F32)


def _vpu_level_exponent(a, h):
    n = a.shape[0]
    pieces = []
    for j in range(n // (2 * h)):
        b = j * 2 * h
        ref = a[b + h - 1:b + h, :]
        pieces.append(ref - a[b:b + h, :])
        pieces.append(a[b + h:b + 2 * h, :] - ref)
    return jnp.concatenate(pieces, axis=0)


def _chunk_prefix_sums(x, chunk):
    n, width = x.shape
    x3 = x.reshape(n // SUBLANES, SUBLANES, width)
    pos = lax.broadcasted_iota(jnp.int32, (1, SUBLANES, 1), 1)
    d = 1
    while d < SUBLANES:
        x3 = x3 + jnp.where(pos >= d, pltpu.roll(x3, d, 1), 0.0)
        d *= 2
    x = x3.reshape(n, width)
    pieces = []
    for b in range(0, n, SUBLANES):
        blk = x[b:b + SUBLANES, :]
        if b % chunk:
            blk = blk + pieces[-1][SUBLANES - 1:SUBLANES, :]
        pieces.append(blk)
    return jnp.concatenate(pieces, axis=0)


def _small_level_exponent(a, logf, h):
    n = a.shape[0]
    pos = lax.broadcasted_iota(jnp.int32, (SUBLANES, 1), 0)
    if h == 1:
        odd = lax.broadcasted_iota(jnp.int32, (n, 1), 0) % 2 == 1
        return jnp.where(odd, logf, 0.0)
    pieces = []
    for b in range(0, n, SUBLANES):
        blk = a[b:b + SUBLANES, :]
        if h == 4:
            ref = blk[3:4, :]
        else:
            ref = jnp.where(pos < 4, blk[1:2, :], blk[5:6, :])
        pieces.append(jnp.where(pos % (2 * h) >= h, blk - ref, ref - blk))
    return jnp.concatenate(pieces, axis=0)


def _forget_lower_bound(lbl):
    m = jnp.max(lbl, axis=0, keepdims=True)
    e = jnp.exp(lbl - m)
    return e[0:1, :] / jnp.sum(e, axis=0, keepdims=True)


PROJ_BLOCK = 512
N_PROJ_BLOCKS = D_PROJ // PROJ_BLOCK


class _ColumnBlocks:
    def __init__(self, h, w_in_ref):
        self.h, self.w, self.done = h, w_in_ref, []

    def next_block(self):
        c = len(self.done)
        if c < N_PROJ_BLOCKS:
            self.done.append(_dot(self.h, self.w[:, c * PROJ_BLOCK:(c + 1) * PROJ_BLOCK]))

    def block(self, c):
        while len(self.done) <= c:
            self.next_block()
        return self.done[c]


def _gates(ff, lb):
    e = jnp.exp(-jnp.abs(ff))
    r = 1.0 / (1.0 + e)
    pos = ff >= 0
    sig = jnp.where(pos, r, e * r)
    sig_neg = jnp.where(pos, e * r, r)
    logf = jnp.log(lb + (1.0 - lb) * sig)
    kk = (1.0 - lb) * sig_neg
    return logf, kk


def _intra_scores_times_v(q, kk, vi, exps, masks_ref):
    n = q.shape[0]
    sc = [jnp.zeros((n, n), F32) for _ in range(N_HEADS)]
    for l, ex in enumerate(exps):
        if ex is None:
            qh, kh = q.astype(BF16), kk.astype(BF16)
        else:
            w = jnp.exp(ex)
            qh, kh = (q * w).astype(BF16), (kk * w).astype(BF16)
        mask = masks_ref[l]
        for hd in range(N_HEADS):
            hs = slice(hd * D_HEAD, (hd + 1) * D_HEAD)
            sc[hd] = sc[hd] + mask * _dot_nt(qh[:, hs], kh[:, hs])
    vb = vi.astype(BF16)
    return [_dot(sc[hd].astype(BF16), vb[:, hd * D_HEAD:(hd + 1) * D_HEAD]) for hd in range(N_HEADS)]


def _head_out(o, g, gnw):
    parts = []
    for hd in range(N_HEADS):
        oh = o[:, hd * D_HEAD:(hd + 1) * D_HEAD]
        parts.append(oh * lax.rsqrt(jnp.mean(oh * oh, axis=-1, keepdims=True) + EPS))
    on = jnp.concatenate(parts, axis=1)
    return on * gnw * (g * jax.nn.sigmoid(g))


def _pad_rows_bf16(a):
    return jnp.concatenate([a, jnp.zeros_like(a)], axis=0).astype(BF16)


PROMPT_VPU_LEVELS = (32, 16, 8)
MXU_LEVELS = (4, 2, 1)


def _mix_prompt_kernel(n_r, n_steps, *refs):
    step = pl.program_id(0)

    @pl.when(step < n_steps)
    def _():
        _mix_prompt_step(step % n_r, n_r, *refs)

    @pl.when(step >= n_steps)
    def _():
        x1_ref = refs[N_MIX_PROMPT_INPUTS]
        x1_ref[...] = jnp.zeros_like(x1_ref)


N_MIX_PROMPT_INPUTS = 9


def _mix_prompt_step(r, n_r, x_ref, lbl_ref, nw_ref, win_ref, cw_ref, cb_ref, gnw_ref, wout_ref, masks_ref,
                     x1_ref, nconv_ref, nstate_ref, st_ref, tail_ref):
    rows = x_ref.shape[0]

    @pl.when(r == 0)
    def _():
        st_ref[...] = jnp.zeros_like(st_ref)
        tail_ref[...] = jnp.zeros_like(tail_ref)

    lb = _forget_lower_bound(lbl_ref[...])
    cw = cw_ref[...]
    n_chunks = MIX_SLAB // PROMPT_CHUNK
    rid = lax.broadcasted_iota(jnp.int32, (MIX_PART, 1), 0)
    t0 = tail_ref[SUBLANES - 2:SUBLANES - 1, :]
    t1 = tail_ref[SUBLANES - 1:SUBLANES, :]
    states = [st_ref[hd] for hd in range(N_HEADS)]

    n_parts = rows // MIX_PART
    part_rows = lambda p: slice(p * MIX_PART, (p + 1) * MIX_PART)
    project = lambda p: _ColumnBlocks(_rmsnorm(x_ref[part_rows(p), :], nw_ref[...]).astype(BF16), win_ref)
    nxt = project(0)
    for p in range(n_parts):
        pr = part_rows(p)
        x = x_ref[pr, :]
        bg, cg, vc, q, ff, vi, g = (nxt.block(c) for c in range(N_PROJ_BLOCKS))
        if p + 1 < n_parts:
            nxt = project(p + 1)
            nxt.block(N_PROJ_BLOCKS - 1)

        u = cg * vc
        u1 = jnp.where(rid == 0, t1, pltpu.roll(u, 1, 0))
        u2 = jnp.where(rid == 0, t0, jnp.where(rid == 1, t1, pltpu.roll(u, 2, 0)))
        y_conv = bg * (cw[0:1, :] * u2 + cw[1:2, :] * u1 + cw[2:3, :] * u + cb_ref[...])
        u_last = u[MIX_PART - SUBLANES:MIX_PART, :]
        t0, t1 = u_last[SUBLANES - 2:SUBLANES - 1, :], u_last[SUBLANES - 1:SUBLANES, :]

        logf, kk = _gates(ff, lb)
        o_parts = []
        for s in range(MIX_PART // MIX_SLAB):
            sl = slice(s * MIX_SLAB, (s + 1) * MIX_SLAB)
            qs, ks, vs = q[sl], kk[sl], vi[sl]
            a_pre = _chunk_prefix_sums(logf[sl], PROMPT_CHUNK)
            a_suf = jnp.concatenate(
                [a_pre[(c + 1) * PROMPT_CHUNK - 1:(c + 1) * PROMPT_CHUNK, :]
                 - a_pre[c * PROMPT_CHUNK:(c + 1) * PROMPT_CHUNK, :] for c in range(n_chunks)], axis=0)
            exps = [_vpu_level_exponent(a_pre, h) for h in PROMPT_VPU_LEVELS]
            exps += [_small_level_exponent(a_pre, logf[sl], h) for h in MXU_LEVELS]
            exps.append(None)
            intra = _intra_scores_times_v(qs, ks, vs, exps, masks_ref)
            ea = jnp.exp(a_pre)
            qa = (qs * ea).astype(BF16)
            kb = (ks * jnp.exp(a_suf)).astype(BF16)
            vb = vs.astype(BF16)
            for c in range(n_chunks):
                cr = slice(c * PROMPT_CHUNK, (c + 1) * PROMPT_CHUNK)
                last = (c + 1) * PROMPT_CHUNK - 1
                heads = []
                for hd in range(N_HEADS):
                    hs = slice(hd * D_HEAD, (hd + 1) * D_HEAD)
                    heads.append(_dot_nt(qa[cr, hs], states[hd].astype(BF16)) + intra[hd][cr, :])
                    states[hd] = states[hd] * ea[last:last + 1, hs] + _dot_tn(vb[cr, hs], kb[cr, hs])
                o_parts.append(jnp.concatenate(heads, axis=1))

        o = _head_out(jnp.concatenate(o_parts, axis=0), g, gnw_ref[...])
        mix_in = jnp.concatenate([y_conv, o], axis=1).astype(BF16)
        x1_ref[pr, :] = x + _dot(mix_in, wout_ref[...])

    tail_ref[...] = u_last
    for hd in range(N_HEADS):
        st_ref[hd] = states[hd]

    @pl.when(r == n_r - 1)
    def _():
        nconv_ref[...] = u_last
        for hd in range(N_HEADS):
            nstate_ref[hd] = states[hd].T


def _mix_prompt(x, lbl, nw, w_in, cw, cb, gnw, w_out, total_rows):
    b, l, _ = x.shape
    assert l % MIX_ROWS == 0 and MIX_ROWS % MIX_PART == 0 and MIX_PART % MIX_SLAB == 0 and l % PROMPT_CHUNK == 0
    assert total_rows % MIX_ROWS == 0
    n_r = l // MIX_ROWS
    n_steps = b * n_r
    seq = lambda s: jnp.minimum(s, n_steps - 1) // n_r
    _, masks = _mix_constants(MIX_SLAB, PROMPT_CHUNK, (), PROMPT_VPU_LEVELS + MXU_LEVELS)
    const = lambda shape: pl.BlockSpec(shape, lambda s: (0,) * len(shape))
    return pl.pallas_call(
        functools.partial(_mix_prompt_kernel, n_r, n_steps),
        grid=(total_rows // MIX_ROWS,),
        in_specs=[
            pl.BlockSpec((None, MIX_ROWS, D_MODEL), lambda s: (seq(s), jnp.minimum(s, n_steps - 1) % n_r, 0)),
            const((2, D_HGRN)), const((1, D_MODEL)), const((D_MODEL, D_PROJ)), const((3, D_CONV)),
            const((1, D_CONV)), const((1, D_HGRN)), const((D_MODEL, D_MODEL)), const(masks.shape),
        ],
        out_specs=[
            pl.BlockSpec((MIX_ROWS, D_MODEL), lambda s: (s, 0)),
            pl.BlockSpec((None, SUBLANES, D_CONV), lambda s: (seq(s), 0, 0)),
            pl.BlockSpec((None, N_HEADS, D_HEAD, D_HEAD), lambda s: (seq(s), 0, 0, 0)),
        ],
        out_shape=[
            jax.ShapeDtypeStruct((total_rows, D_MODEL), F32),
            jax.ShapeDtypeStruct((b, SUBLANES, D_CONV), F32),
            jax.ShapeDtypeStruct((b, N_HEADS, D_HEAD, D_HEAD), F32),
        ],
        scratch_shapes=[
            pltpu.VMEM((N_HEADS, D_HEAD, D_HEAD), F32),
            pltpu.VMEM((SUBLANES, D_CONV), F32),
        ],
        compiler_params=pltpu.CompilerParams(
            dimension_semantics=("arbitrary",), vmem_limit_bytes=VMEM_LIMIT_BYTES),
        name="mix_prompt",
    )(x, lbl, nw, w_in, cw, cb, gnw, w_out, masks)


def _mix_sample_kernel(x_ref, cs_ref, hs_ref, lbl_ref, nw_ref, win_ref, cw_ref, cb_ref, gnw_ref, wout_ref,
                       cmat_ref, masks_ref, x1_in_ref, x1_ref, nconv_ref, nstate_ref, o_ref):
    del x1_in_ref
    nseq, length, _ = nconv_ref.shape
    rows = nseq * length
    x = x_ref[...]
    proj = _ColumnBlocks(_rmsnorm(x, nw_ref[...]).astype(BF16), win_ref)
    bg, cg, vc, q, ff, vi, g = (proj.block(c) for c in range(N_PROJ_BLOCKS))
    lb = _forget_lower_bound(lbl_ref[...])

    u = cg * vc
    cs = cs_ref[...]
    expand = lambda a: jnp.broadcast_to(a, (nseq, length, D_CONV)).reshape(rows, D_CONV)
    t0 = expand(cs[:, 0:1, :])
    t1 = expand(cs[:, 1:2, :])
    pos = lax.broadcasted_iota(jnp.int32, (rows, 1), 0) % length
    u1 = jnp.where(pos == 0, t1, pltpu.roll(u, 1, 0))
    u2 = jnp.where(pos == 0, t0, jnp.where(pos == 1, t1, pltpu.roll(u, 2, 0)))
    cw = cw_ref[...]
    y_conv = bg * (cw[0:1, :] * u2 + cw[1:2, :] * u1 + cw[2:3, :] * u + cb_ref[...])
    nconv_ref[...] = u.reshape(nseq, length, D_CONV)

    logf, kk = _gates(ff, lb)
    lf_hi, lf_lo = _split2(logf)
    ex = _dot(cmat_ref[...], lf_hi) + _dot(cmat_ref[...], lf_lo)
    a_pre = ex[0:rows]
    a_suf = ex[rows:2 * rows]
    exps = [ex[(2 + i) * rows:(3 + i) * rows] for i in range(len(MXU_LEVELS))] + [None]
    intra = _intra_scores_times_v(q, kk, vi, exps, masks_ref)
    ea = jnp.exp(a_pre)
    qa = q * ea
    kb = kk * jnp.exp(a_suf)
    for s in range(nseq):
        cr = slice(s * length, (s + 1) * length)
        last = (s + 1) * length - 1
        for hd in range(N_HEADS):
            hs = slice(hd * D_HEAD, (hd + 1) * D_HEAD)
            st = hs_ref[s, hd].T
            inter = _dot_nt(_pad_rows_bf16(qa[cr, hs]), st.astype(BF16))[0:length, :]
            o_ref[cr, hs] = inter + intra[hd][cr, :]
            st_new = st * ea[last:last + 1, hs] + _dot_tn(_pad_rows_bf16(vi[cr, hs]), _pad_rows_bf16(kb[cr, hs]))
            nstate_ref[s, hd] = st_new.T

    o = _head_out(o_ref[...], g, gnw_ref[...])
    mix_in = jnp.concatenate([y_conv, o], axis=1).astype(BF16)
    x1_ref[...] = x + _dot(mix_in, wout_ref[...])


def _mix_sample(x, conv_state, hgrn_state, lbl, nw, w_in, cw, cb, gnw, w_out, x1_flat, row_offset):
    nb, length, _ = x.shape
    rows = MIX_SEQS * length
    assert nb % MIX_SEQS == 0 and length == SUBLANES and row_offset % rows == 0
    first_block = row_offset // rows
    cmat, masks = _mix_constants(rows, length, MXU_LEVELS, MXU_LEVELS)
    const = lambda shape: pl.BlockSpec(shape, lambda i: (0,) * len(shape))
    return pl.pallas_call(
        _mix_sample_kernel,
        grid=(nb // MIX_SEQS,),
        in_specs=[
            pl.BlockSpec((rows, D_MODEL), lambda i: (i, 0)),
            pl.BlockSpec((MIX_SEQS, 2, D_CONV), lambda i: (i, 0, 0)),
            pl.BlockSpec((MIX_SEQS, N_HEADS, D_HEAD, D_HEAD), lambda i: (i, 0, 0, 0)),
            const((2, D_HGRN)), const((1, D_MODEL)), const((D_MODEL, D_PROJ)), const((3, D_CONV)),
            const((1, D_CONV)), const((1, D_HGRN)), const((D_MODEL, D_MODEL)), const(cmat.shape),
            const(masks.shape), pl.BlockSpec(memory_space=pl.ANY),
        ],
        out_specs=[
            pl.BlockSpec((rows, D_MODEL), lambda i: (first_block + i, 0)),
            pl.BlockSpec((MIX_SEQS, length, D_CONV), lambda i: (i, 0, 0)),
            pl.BlockSpec((MIX_SEQS, N_HEADS, D_HEAD, D_HEAD), lambda i: (i, 0, 0, 0)),
        ],
        out_shape=[
            jax.ShapeDtypeStruct(x1_flat.shape, F32),
            jax.ShapeDtypeStruct((nb, length, D_CONV), F32),
            jax.ShapeDtypeStruct((nb, N_HEADS, D_HEAD, D_HEAD), F32),
        ],
        scratch_shapes=[pltpu.VMEM((rows, D_HGRN), F32)],
        input_output_aliases={12: 0},
        compiler_params=pltpu.CompilerParams(
            dimension_semantics=("arbitrary",), vmem_limit_bytes=VMEM_LIMIT_BYTES),
        name="mix_sample",
    )(x.reshape(nb * length, D_MODEL), conv_state, hgrn_state, lbl, nw, w_in, cw, cb, gnw, w_out, cmat, masks,
      x1_flat)


def _router_kernel(x1_ref, nw_ref, rw_hi_ref, rw_lo_ref, rb_ref, ltri_ref, utri_ref, xn_ref, meta_ref, cnt_ref):
    n = x1_ref.shape[0]
    xn = _rmsnorm(x1_ref[...], nw_ref[...])
    xn_ref[...] = xn.astype(BF16)
    x_hi, x_lo = _split2(xn)
    logits = (_dot(x_hi, rw_hi_ref[...]) + _dot(x_lo, rw_hi_ref[...]) + _dot(x_hi, rw_lo_ref[...])
              + rb_ref[...])
    lane = lax.broadcasted_iota(jnp.int32, (n, LANES), 1).astype(F32)
    work = logits
    vals, ids = [], []
    for _ in range(TOP_K):
        m = jnp.max(work, axis=-1, keepdims=True)
        i = jnp.min(jnp.where(work == m, lane, float(LANES)), axis=-1, keepdims=True)
        vals.append(m)
        ids.append(i)
        work = jnp.where(lane == i, -jnp.inf, work)
    es = [jnp.exp(v - vals[0]) for v in vals]
    den = es[0] + es[1] + es[2] + es[3]
    gates = [e / den for e in es]

    onehots = [(lane == i) for i in ids]
    multi = jnp.zeros((n, LANES), F32)
    for oh in onehots:
        multi = multi + oh.astype(F32)
    counts = jnp.sum(multi, axis=0, keepdims=True)
    before = _dot(ltri_ref[...], multi.astype(BF16))
    seg = jnp.ceil(counts * (1.0 / SEG_ALIGN)) * SEG_ALIGN
    seg_rows = jnp.broadcast_to(seg, (BF16_ROWS, LANES)).astype(BF16)
    seg_off = _dot(seg_rows, utri_ref[...])[0:1, :]
    slot_of = seg_off + before
    meta = jnp.zeros((n, LANES), F32)
    for k in range(TOP_K):
        slot = jnp.sum(jnp.where(onehots[k], slot_of, 0.0), axis=-1, keepdims=True)
        meta = jnp.where(lane == k, ids[k], meta)
        meta = jnp.where(lane == TOP_K + k, gates[k], meta)
        meta = jnp.where(lane == 2 * TOP_K + k, slot, meta)
    meta_ref[...] = meta
    cnt_ref[...] = jnp.broadcast_to(counts, (SUBLANES, LANES))


def _router(x1, nw, router_w, router_b):
    t = x1.shape[0]
    n_tiles = t // TOKEN_TILE
    rw = jnp.zeros((D_MODEL, LANES), F32).at[:, :N_EXPERTS].set(router_w)
    rw_hi, rw_lo = _split2(rw)
    rb = jnp.full((1, LANES), -1e30, F32).at[0, :N_EXPERTS].set(router_b)
    idx = np.arange(TOKEN_TILE)
    ltri = jnp.asarray(idx[None, :] < idx[:, None], BF16)
    lid = np.arange(LANES)
    utri = jnp.asarray(lid[:, None] < lid[None, :], BF16)
    const = lambda shape: pl.BlockSpec(shape, lambda i: (0,) * len(shape))
    return pl.pallas_call(
        _router_kernel,
        grid=(n_tiles,),
        in_specs=[
            pl.BlockSpec((TOKEN_TILE, D_MODEL), lambda i: (i, 0)),
            const((1, D_MODEL)), const((D_MODEL, LANES)), const((D_MODEL, LANES)), const((1, LANES)),
            const((TOKEN_TILE, TOKEN_TILE)), const((LANES, LANES)),
        ],
        out_specs=[
            pl.BlockSpec((TOKEN_TILE, D_MODEL), lambda i: (i, 0)),
            pl.BlockSpec((TOKEN_TILE, LANES), lambda i: (i, 0)),
            pl.BlockSpec((None, SUBLANES, LANES), lambda i: (i, 0, 0)),
        ],
        out_shape=[
            jax.ShapeDtypeStruct((t, D_MODEL), BF16),
            jax.ShapeDtypeStruct((t, LANES), F32),
            jax.ShapeDtypeStruct((n_tiles, SUBLANES, LANES), F32),
        ],
        compiler_params=pltpu.CompilerParams(
            dimension_semantics=("arbitrary",), vmem_limit_bytes=VMEM_LIMIT_BYTES),
        name="router",
    )(x1, nw, rw_hi, rw_lo, rb, ltri, utri)


def _segment_tables(cnt):
    seg = (cnt + SEG_ALIGN - 1) // SEG_ALIGN * SEG_ALIGN
    src = jnp.cumsum(seg, axis=1) - seg
    tot = jnp.sum(seg, axis=0)
    cap = (tot + ROW_TILE - 1) // ROW_TILE * ROW_TILE
    base = jnp.cumsum(cap) - cap
    dst = base[None, :] + jnp.cumsum(seg, axis=0) - seg
    n_used = jnp.sum(cap) // ROW_TILE
    fill = base + tot
    lane = jnp.arange(N_EXPERTS)
    counts, srcs, dsts = [], [], []
    for size in SEG_SIZES:
        has = (seg & size) != 0
        done = seg & ~(2 * size - 1)
        place = has[:, :, None] & ((jnp.cumsum(has, axis=1) - 1)[:, :, None] == lane[None, None, :])
        counts.append(jnp.sum(has, axis=1))
        srcs.append(jnp.sum(jnp.where(place, (src + done)[:, :, None], 0), axis=1))
        dsts.append(jnp.sum(jnp.where(place, (dst + done)[:, :, None], 0), axis=1))
    i32 = lambda a: a.astype(jnp.int32).reshape(-1)
    chunks = (i32(jnp.stack(counts, axis=1)), i32(jnp.stack(srcs, axis=1)), i32(jnp.stack(dsts, axis=1)))
    return chunks, i32(fill), i32(cap - tot), i32(base), i32(cap // ROW_TILE), i32(n_used)


def _for_each_chunk(n, fn):
    for size in SEG_SIZES:
        done = n & ~(2 * size - 1)

        @pl.when((n & size) != 0)
        def _():
            fn(done, size)


def _segment_copies(cnt_ref, src_ref, dst_ref, tile, make_copy, start):
    for si, size in enumerate(SEG_SIZES):
        first = (tile * len(SEG_SIZES) + si) * N_EXPERTS

        def one(p, carry, si=si, size=size, first=first):
            copy = make_copy(pl.multiple_of(src_ref[first + p], SEG_ALIGN),
                             pl.multiple_of(dst_ref[first + p], SEG_ALIGN), size)
            if start:
                copy.start(priority=si % 2)
            else:
                copy.wait()
            return carry
        lax.fori_loop(0, cnt_ref[tile * len(SEG_SIZES) + si], one, 0)


SORT_BLOCK = 512


def _slot_matrix(slots, block, gates):
    row = (lax.broadcasted_iota(jnp.int32, (SORT_BLOCK, TOKEN_TILE), 0) + block * SORT_BLOCK).astype(F32)
    m = jnp.zeros((SORT_BLOCK, TOKEN_TILE), F32)
    for k in range(TOP_K):
        m = jnp.where(row == slots[k:k + 1, :], 1.0 if gates is None else gates[k:k + 1, :], m)
    return m.astype(BF16)


def _dispatch_kernel(seg_ref, src_ref, dst_ref, fill_ref, gap_ref, nu_ref, slot_ref, xn_ref, xs_ref, sorted_ref,
                     zero_ref, sem, seg_sem):
    tile = pl.program_id(0)
    n_tiles = pl.num_programs(0)
    buf = tile % 2

    @pl.when(tile == 0)
    def _():
        zero_ref[...] = jnp.zeros_like(zero_ref)

        def fill_copy(row, size):
            return pltpu.make_async_copy(zero_ref.at[pl.ds(0, size)],
                                         xs_ref.at[pl.ds(pl.multiple_of(row, SEG_ALIGN), size)], sem)

        def gaps(act):
            def per_expert(e, carry):
                _for_each_chunk(gap_ref[e], lambda done, size: act(fill_copy(fill_ref[e] + done, size)))
                return carry
            lax.fori_loop(0, N_EXPERTS, per_expert, 0)

        def tail(act):
            n_tail = xs_ref.shape[0] // ROW_TILE - nu_ref[0]
            lax.fori_loop(0, n_tail, lambda i, c: (act(fill_copy((nu_ref[0] + i) * ROW_TILE, ROW_TILE)), c)[1], 0)

        gaps(lambda c: c.start())
        tail(lambda c: c.start())
        gaps(lambda c: c.wait())
        tail(lambda c: c.wait())

    slots = slot_ref[...]
    xn = xn_ref[...]
    for rb in range(TILE_CAP // SORT_BLOCK):
        sorted_ref[buf, rb * SORT_BLOCK:(rb + 1) * SORT_BLOCK, :] = _dot(
            _slot_matrix(slots, rb, None), xn).astype(BF16)

    def copier(half):
        return lambda src, dst, size: pltpu.make_async_copy(
            sorted_ref.at[half, pl.ds(src, size)], xs_ref.at[pl.ds(dst, size)], seg_sem.at[half])

    _segment_copies(seg_ref, src_ref, dst_ref, tile, copier(buf), start=True)

    @pl.when(tile > 0)
    def _():
        _segment_copies(seg_ref, src_ref, dst_ref, tile - 1, copier(1 - buf), start=False)

    @pl.when(tile == n_tiles - 1)
    def _():
        _segment_copies(seg_ref, src_ref, dst_ref, tile, copier(buf), start=False)


def _dispatch(tables, slot_t, xn, n_rows_total):
    t = xn.shape[0]
    n_tiles = t // TOKEN_TILE
    return pl.pallas_call(
        _dispatch_kernel,
        grid_spec=pltpu.PrefetchScalarGridSpec(
            num_scalar_prefetch=len(tables),
            grid=(n_tiles,),
            in_specs=[
                pl.BlockSpec((SUBLANES, TOKEN_TILE), lambda i, *_: (0, i)),
                pl.BlockSpec((TOKEN_TILE, D_MODEL), lambda i, *_: (i, 0)),
            ],
            out_specs=pl.BlockSpec(memory_space=pl.ANY),
            scratch_shapes=[
                pltpu.VMEM((2, TILE_CAP, D_MODEL), BF16),
                pltpu.VMEM((ROW_TILE, D_MODEL), BF16),
                pltpu.SemaphoreType.DMA(()),
                pltpu.SemaphoreType.DMA((2,)),
            ],
        ),
        out_shape=jax.ShapeDtypeStruct((n_rows_total, D_MODEL), BF16),
        compiler_params=pltpu.CompilerParams(
            dimension_semantics=("arbitrary",), vmem_limit_bytes=VMEM_LIMIT_BYTES),
        name="dispatch",
    )(*tables, slot_t, xn)


X_SLOTS = 3
W_FETCH_STEPS = (0, 1, 3)


def _experts_kernel(base_ref, nt_ref, nu_ref, bg_ref, bu_ref, bd_ref, wg_hbm, wu_hbm, wd_hbm, xs_ref, zs_ref,
                    w_f32, wg_bf, wu_bf, wd_bf, xbuf, hbuf, zbuf, w_sem, in_sem, out_sem):
    e = pl.program_id(0)
    n_e = pl.num_programs(0)
    n = nt_ref[e]
    base = base_ref[e]
    w_slot = e % 2
    has_next = e + 1 < n_e

    def w_copy(j, expert, slot):
        w_hbm = (wg_hbm, wu_hbm, wd_hbm)[j]
        return pltpu.make_async_copy(w_hbm.at[expert], w_f32.at[slot, j], w_sem.at[slot])

    def x_copy(i, first_row=base):
        rows = pl.ds(pl.multiple_of(first_row + i * ROW_TILE, ROW_TILE), ROW_TILE)
        return pltpu.make_async_copy(xs_ref.at[rows], xbuf.at[i % X_SLOTS], in_sem.at[i % X_SLOTS])

    def start_first_tiles(expert):
        for i in range(X_SLOTS):
            @pl.when(i < nt_ref[expert])
            def _():
                x_copy(i, base_ref[expert]).start(priority=1)

    def z_copy(row, slot):
        rows = pl.ds(pl.multiple_of(row, ROW_TILE), ROW_TILE)
        return pltpu.make_async_copy(zbuf.at[slot], zs_ref.at[rows], out_sem.at[slot])

    def hidden(i):
        x = xbuf[i % X_SLOTS]
        a = jnp.minimum(_dot(x, wg_bf[...]) + bg_ref[...], SWIGLU_LIMIT)
        u = jnp.clip(_dot(x, wu_bf[...]) + bu_ref[...], -SWIGLU_LIMIT, SWIGLU_LIMIT)
        hbuf[i % 2] = ((u + 1.0) * a * jax.nn.sigmoid(SWIGLU_ALPHA * a)).astype(BF16)

    def project_down(i, slot):
        @pl.when(i >= 2)
        def _():
            z_copy(base, slot).wait()
        zbuf[slot] = (_dot(hbuf[slot], wd_bf[...]) + bd_ref[...]).astype(BF16)

    @pl.when(e == 0)
    def _():
        for j in range(3):
            w_copy(j, 0, 0).start()
        start_first_tiles(0)

    def fetch_next_weights(j):
        @pl.when(has_next)
        def _():
            w_copy(j, e + 1, 1 - w_slot).start()

    fetch_next_weights(0)
    for j in range(3):
        w_copy(j, e, w_slot).wait()

    @pl.when(n > 0)
    def _():
        x_copy(0).wait()
        wg_bf[...] = w_f32[w_slot, 0].astype(BF16)
        wu_bf[...] = w_f32[w_slot, 1].astype(BF16)
        wd_bf[...] = w_f32[w_slot, 2].astype(BF16)
        hidden(0)

        def step(i, carry):
            x_copy(i + 1).wait()

            @pl.when(i + X_SLOTS < n)
            def _():
                x_copy(i + X_SLOTS).start(priority=1)

            for j in (1, 2):
                @pl.when(i == W_FETCH_STEPS[j])
                def _():
                    fetch_next_weights(j)

            project_down(i, i % 2)
            hidden(i + 1)
            z_copy(base + i * ROW_TILE, i % 2).start()
            return carry

        lax.fori_loop(0, n - 1, step, 0)
        project_down(n - 1, (n - 1) % 2)
        z_copy(base + (n - 1) * ROW_TILE, (n - 1) % 2).start()

        @pl.when(n >= 2)
        def _():
            z_copy(base, n % 2).wait()
        z_copy(base, (n - 1) % 2).wait()

    for j in (1, 2):
        @pl.when(jnp.maximum(n - 1, 0) <= W_FETCH_STEPS[j])
        def _():
            fetch_next_weights(j)

    @pl.when(has_next)
    def _():
        start_first_tiles(e + 1)

    @pl.when(e == pl.num_programs(0) - 1)
    def _():
        zbuf[0] = jnp.zeros((ROW_TILE, D_MODEL), BF16)
        n_tail = zs_ref.shape[0] // ROW_TILE - nu_ref[0]
        tail = lambda act: lax.fori_loop(
            0, n_tail, lambda i, c: (act(z_copy((nu_ref[0] + i) * ROW_TILE, 0)), c)[1], 0)
        tail(lambda c: c.start())
        tail(lambda c: c.wait())


def _experts(base, n_tiles, n_used, xs, w_gate, b_gate, w_up, b_up, w_down, b_down):
    b_spec = pl.BlockSpec((None, 1, D_MODEL), lambda e, *_: (e, 0, 0))
    any_spec = pl.BlockSpec(memory_space=pl.ANY)
    return pl.pallas_call(
        _experts_kernel,
        grid_spec=pltpu.PrefetchScalarGridSpec(
            num_scalar_prefetch=3,
            grid=(N_EXPERTS,),
            in_specs=[b_spec, b_spec, b_spec, any_spec, any_spec, any_spec, any_spec],
            out_specs=any_spec,
            scratch_shapes=[pltpu.VMEM((2, 3, D_MODEL, D_MODEL), F32)] + [
                pltpu.VMEM((D_MODEL, D_MODEL), BF16) for _ in range(3)] + [
                pltpu.VMEM((X_SLOTS, ROW_TILE, D_MODEL), BF16)] + [
                pltpu.VMEM((2, ROW_TILE, D_MODEL), BF16) for _ in range(2)] + [
                pltpu.SemaphoreType.DMA((2,)), pltpu.SemaphoreType.DMA((X_SLOTS,)), pltpu.SemaphoreType.DMA((2,))],
        ),
        out_shape=jax.ShapeDtypeStruct(xs.shape, BF16),
        compiler_params=pltpu.CompilerParams(
            dimension_semantics=("arbitrary",), vmem_limit_bytes=VMEM_LIMIT_BYTES),
        name="experts",
    )(base, n_tiles, n_used, b_gate[:, None, :], b_up[:, None, :], b_down[:, None, :], w_gate, w_up, w_down, xs)


def _combine_kernel(n_first, seg_ref, src_ref, dst_ref, route_ref, x1_ref, fw_ref, zs_ref, ya_ref, yb_ref,
                    sorted_ref, sem):
    tile = pl.program_id(0)
    n_tiles = pl.num_programs(0)
    buf = tile % 2

    def fetch(t, half):
        sorted_ref[half] = jnp.zeros((TILE_CAP, D_MODEL), BF16)
        _segment_copies(seg_ref, src_ref, dst_ref, t, copier(half), start=True)

    def copier(half):
        return lambda src, dst, size: pltpu.make_async_copy(
            zs_ref.at[pl.ds(dst, size)], sorted_ref.at[half, pl.ds(src, size)], sem.at[half])

    @pl.when(tile == 0)
    def _():
        fetch(tile, buf)

    @pl.when(tile + 1 < n_tiles)
    def _():
        fetch(tile + 1, 1 - buf)

    _segment_copies(seg_ref, src_ref, dst_ref, tile, copier(buf), start=False)
    route = route_ref[...]
    slots, gates = route[0:TOP_K, :], route[TOP_K:2 * TOP_K, :]
    moe = jnp.zeros((TOKEN_TILE, D_MODEL), F32)
    for cb in range(TILE_CAP // SORT_BLOCK):
        moe = moe + _dot_tn(_slot_matrix(slots, cb, gates), sorted_ref[buf, cb * SORT_BLOCK:(cb + 1) * SORT_BLOCK, :])
    y = _rmsnorm(x1_ref[...] + moe, fw_ref[...])

    @pl.when(tile < n_first)
    def _():
        ya_ref[...] = y

    @pl.when(tile >= n_first)
    def _():
        yb_ref[...] = y


def _combine(tables, route_t, x1, final_w, zs, rows_first):
    seg, src, dst = tables
    t = x1.shape[0]
    assert rows_first % TOKEN_TILE == 0 and 0 < rows_first < t and TILE_CAP % SORT_BLOCK == 0
    n_tiles = t // TOKEN_TILE
    n_first = rows_first // TOKEN_TILE
    return pl.pallas_call(
        functools.partial(_combine_kernel, n_first),
        grid_spec=pltpu.PrefetchScalarGridSpec(
            num_scalar_prefetch=3,
            grid=(n_tiles,),
            in_specs=[
                pl.BlockSpec((SUBLANES, TOKEN_TILE), lambda i, *_: (0, i)),
                pl.BlockSpec((TOKEN_TILE, D_MODEL), lambda i, *_: (i, 0)),
                pl.BlockSpec((1, D_MODEL), lambda i, *_: (0, 0)),
                pl.BlockSpec(memory_space=pl.ANY),
            ],
            out_specs=[
                pl.BlockSpec((TOKEN_TILE, D_MODEL), lambda i, *_: (jnp.minimum(i, n_first - 1), 0)),
                pl.BlockSpec((TOKEN_TILE, D_MODEL), lambda i, *_: (jnp.maximum(i - n_first, 0), 0)),
            ],
            scratch_shapes=[pltpu.VMEM((2, TILE_CAP, D_MODEL), BF16), pltpu.SemaphoreType.DMA((2,))],
        ),
        out_shape=[jax.ShapeDtypeStruct((rows_first, D_MODEL), F32),
                   jax.ShapeDtypeStruct((t - rows_first, D_MODEL), F32)],
        compiler_params=pltpu.CompilerParams(
            dimension_semantics=("arbitrary",), vmem_limit_bytes=VMEM_LIMIT_BYTES),
        name="combine",
    )(seg, src, dst, route_t, x1, final_w, zs)


def _moe_and_final_norm(x1, rows_first, norm_ffn_w, router_w, router_b, w_gate, b_gate, w_up, b_up, w_down, b_down,
                        final_w):
    t = x1.shape[0]
    assert t % TOKEN_TILE == 0
    n_tiles = t // TOKEN_TILE
    n_rows_total = TOP_K * t + n_tiles * N_EXPERTS * (SEG_ALIGN - 1) + N_EXPERTS * (ROW_TILE - 1)
    n_rows_total = (n_rows_total + ROW_TILE - 1) // ROW_TILE * ROW_TILE
    xn, meta, cnt = _router(x1, norm_ffn_w.reshape(1, D_MODEL), router_w, router_b)
    cnt = cnt[:, 0, :N_EXPERTS].astype(jnp.int32)
    chunks, fill, gap, base, n_row_tiles, n_used = _segment_tables(cnt)
    route_t = jnp.concatenate([meta[:, 2 * TOP_K:3 * TOP_K], meta[:, TOP_K:2 * TOP_K]], axis=1).T
    xs = _dispatch((*chunks, fill, gap, n_used), route_t, xn, n_rows_total)
    zs = _experts(base, n_row_tiles, n_used, xs, w_gate, b_gate, w_up, b_up, w_down, b_down)
    return _combine(chunks, route_t, x1, final_w.reshape(1, D_MODEL), zs, rows_first)


def kernel(x_prompt, x_sample, state_conv, state_hgrn, lb_logits, norm_mix_w, w_in, conv_w, conv_b, gnorm_w,
           w_out, norm_ffn_w, router_w, router_b, w_gate, b_gate, w_up, b_up, w_down, b_down, final_norm_w):
    assert norm_mix_w.shape[0] == 1 and lb_logits.shape[0] == 2, "single-layer step"
    b, l, _ = x_prompt.shape
    nb, ls, _ = x_sample.shape
    lbl = lb_logits.astype(F32)
    nw = norm_mix_w[0].reshape(1, D_MODEL)
    w_in_bf = w_in[0].astype(BF16)
    w_out_bf = w_out[0].astype(BF16)
    cw, cb = conv_w[0], conv_b[0].reshape(1, D_CONV)
    gnw = gnorm_w[0].reshape(1, D_HGRN)
    rows_p, rows_s = b * l, nb * ls
    x1, conv_p, hgrn_p = _mix_prompt(x_prompt, lbl, nw, w_in_bf, cw, cb, gnw, w_out_bf, rows_p + rows_s)
    x1, conv_s, hgrn_s = _mix_sample(x_sample, state_conv[0], state_hgrn[0], lbl, nw, w_in_bf, cw, cb, gnw,
                                     w_out_bf, x1, rows_p)
    y_p, y_s = _moe_and_final_norm(x1, rows_p, norm_ffn_w[0], router_w[0], router_b[0], w_gate[0], b_gate[0],
                                   w_up[0], b_up[0], w_down[0], b_down[0], final_norm_w)
    conv_p = conv_p[:, SUBLANES - 2:, :]
    conv_s = conv_s[:, ls - 2:, :]
    return (y_p.reshape(b, l, D_MODEL), y_s.reshape(nb, ls, D_MODEL), conv_p[None], hgrn_p[None], conv_s[None],
            hgrn_s[None])
```

```python
import functools

import numpy as np
import jax
import jax.numpy as jnp
from jax import lax
from jax.experimental import pallas as pl
from jax.experimental.pallas import tpu as pltpu

F32 = jnp.float32
BF16 = jnp.bfloat16

D_MODEL = 1024
D_CONV = 512
D_HGRN = 512
N_HEADS = 4
D_HEAD = 128
D_PROJ = 3 * D_CONV + 4 * D_HGRN
N_EXPERTS = 32
TOP_K = 4
SWIGLU_LIMIT = 7.0
SWIGLU_ALPHA = 1.702
EPS = 1e-6
PROMPT_CHUNK = 64

LANES = 128
SUBLANES = 8
BF16_ROWS = 16
VMEM_LIMIT_BYTES = 56 * 1024 * 1024

MIX_SLAB = 128
MIX_ROWS = 512
MIX_PART = 256
MIX_SEQS = 16
TOKEN_TILE = 512
SEG_ALIGN = BF16_ROWS
ROW_TILE = 512
SEG_SIZES = tuple(SEG_ALIGN << i for i in reversed(range(6)))
CHUNK_ROWS = 128
CHUNK_SIZES = tuple(range(CHUNK_ROWS, 0, -SEG_ALIGN))
TILE_CAP = TOP_K * TOKEN_TILE + N_EXPERTS * SEG_ALIGN
assert TILE_CAP // CHUNK_ROWS <= N_EXPERTS

NT_DIMS = (((1,), (1,)), ((), ()))
TN_DIMS = (((0,), (0,)), ((), ()))


def _dot(a, b):
    return jnp.dot(a, b, preferred_element_type=F32)


def _dot_nt(a, b):
    return lax.dot_general(a, b, NT_DIMS, preferred_element_type=F32)


def _dot_tn(a, b):
    return lax.dot_general(a, b, TN_DIMS, preferred_element_type=F32)


def _split2(x):
    hi = x.astype(BF16)
    lo = (x - hi.astype(F32)).astype(BF16)
    return hi, lo


def _rmsnorm(x, w):
    return x * lax.rsqrt(jnp.mean(x * x, axis=-1, keepdims=True) + EPS) * w


def _level_exponent_matrix(n, chunk, h):
    x = np.zeros((n, n), np.float32)
    for t in range(n):
        base = t - t % (2 * h)
        m = base + h - 1
        if t % (2 * h) >= h:
            x[t, m + 1:t + 1] = 1.0
        else:
            x[t, t + 1:m + 1] = 1.0
    return x


def _mix_constants(n, chunk, mxu_levels, all_levels):
    t = np.arange(n)
    same_chunk = (t[:, None] // chunk) == (t[None, :] // chunk)
    tri = (same_chunk & (t[None, :] <= t[:, None])).astype(np.float32)
    suf = (same_chunk & (t[None, :] > t[:, None])).astype(np.float32)
    cmat = np.concatenate([tri, suf] + [_level_exponent_matrix(n, chunk, h) for h in mxu_levels], axis=0)
    masks = []
    for h in all_levels:
        blk = (t[:, None] // (2 * h)) == (t[None, :] // (2 * h))
        masks.append((blk & ((t[:, None] % (2 * h)) >= h) & ((t[None, :] % (2 * h)) < h)).astype(np.float32))
    masks.append(np.eye(n, dtype=np.float32))
    return jnp.asarray(cmat, BF16), jnp.asarray(np.stack(masks), F32)


def _vpu_level_exponent(a, h):
    n = a.shape[0]
    pieces = []
    for j in range(n // (2 * h)):
        b = j * 2 * h
        ref = a[b + h - 1:b + h, :]
        pieces.append(ref - a[b:b + h, :])
        pieces.append(a[b + h:b + 2 * h, :] - ref)
    return jnp.concatenate(pieces, axis=0)


def _chunk_prefix_sums(x, chunk):
    n, width = x.shape
    x3 = x.reshape(n // SUBLANES, SUBLANES, width)
    pos = lax.broadcasted_iota(jnp.int32, (1, SUBLANES, 1), 1)
    d = 1
    while d < SUBLANES:
        x3 = x3 + jnp.where(pos >= d, pltpu.roll(x3, d, 1), 0.0)
        d *= 2
    x = x3.reshape(n, width)
    pieces = []
    for b in range(0, n, SUBLANES):
        blk = x[b:b + SUBLANES, :]
        if b % chunk:
            blk = blk + pieces[-1][SUBLANES - 1:SUBLANES, :]
        pieces.append(blk)
    return jnp.concatenate(pieces, axis=0)


def _small_level_exponent(a, logf, h):
    n = a.shape[0]
    pos = lax.broadcasted_iota(jnp.int32, (SUBLANES, 1), 0)
    if h == 1:
        odd = lax.broadcasted_iota(jnp.int32, (n, 1), 0) % 2 == 1
        return jnp.where(odd, logf, 0.0)
    pieces = []
    for b in range(0, n, SUBLANES):
        blk = a[b:b + SUBLANES, :]
        if h == 4:
            ref = blk[3:4, :]
        else:
            ref = jnp.where(pos < 4, blk[1:2, :], blk[5:6, :])
        pieces.append(jnp.where(pos % (2 * h) >= h, blk - ref, ref - blk))
    return jnp.concatenate(pieces, axis=0)


def _forget_lower_bound(lbl):
    m = jnp.max(lbl, axis=0, keepdims=True)
    e = jnp.exp(lbl - m)
    return e[0:1, :] / jnp.sum(e, axis=0, keepdims=True)


PROJ_BLOCK = 512
N_PROJ_BLOCKS = D_PROJ // PROJ_BLOCK


class _ColumnBlocks:
    def __init__(self, h, w_in_ref):
        self.h, self.w, self.done = h, w_in_ref, []

    def next_block(self):
        c = len(self.done)
        if c < N_PROJ_BLOCKS:
            self.done.append(_dot(self.h, self.w[:, c * PROJ_BLOCK:(c + 1) * PROJ_BLOCK]))

    def block(self, c):
        while len(self.done) <= c:
            self.next_block()
        return self.done[c]


def _gates(ff, lb):
    e = jnp.exp(-jnp.abs(ff))
    r = 1.0 / (1.0 + e)
    pos = ff >= 0
    sig = jnp.where(pos, r, e * r)
    sig_neg = jnp.where(pos, e * r, r)
    logf = jnp.log(lb + (1.0 - lb) * sig)
    kk = (1.0 - lb) * sig_neg
    return logf, kk


def _intra_scores_times_v(q, kk, vi, exps, masks_ref):
    n = q.shape[0]
    sc = [jnp.zeros((n, n), F32) for _ in range(N_HEADS)]
    for l, ex in enumerate(exps):
        if ex is None:
            qh, kh = q.astype(BF16), kk.astype(BF16)
        else:
            w = jnp.exp(ex)
            qh, kh = (q * w).astype(BF16), (kk * w).astype(BF16)
        mask = masks_ref[l]
        for hd in range(N_HEADS):
            hs = slice(hd * D_HEAD, (hd + 1) * D_HEAD)
            sc[hd] = sc[hd] + mask * _dot_nt(qh[:, hs], kh[:, hs])
    vb = vi.astype(BF16)
    return [_dot(sc[hd].astype(BF16), vb[:, hd * D_HEAD:(hd + 1) * D_HEAD]) for hd in range(N_HEADS)]


def _head_out(o, g, gnw):
    parts = []
    for hd in range(N_HEADS):
        oh = o[:, hd * D_HEAD:(hd + 1) * D_HEAD]
        parts.append(oh * lax.rsqrt(jnp.mean(oh * oh, axis=-1, keepdims=True) + EPS))
    on = jnp.concatenate(parts, axis=1)
    return on * gnw * (g * jax.nn.sigmoid(g))


def _pad_rows_bf16(a):
    return jnp.concatenate([a, jnp.zeros_like(a)], axis=0).astype(BF16)


PROMPT_VPU_LEVELS = (32, 16, 8)
MXU_LEVELS = (4, 2, 1)


def _mix_prompt_kernel(n_r, n_steps, *refs):
    step = pl.program_id(0)

    @pl.when(step < n_steps)
    def _():
        _mix_prompt_step(step % n_r, n_r, *refs)

    @pl.when(step >= n_steps)
    def _():
        x1_ref = refs[N_MIX_PROMPT_INPUTS]
        x1_ref[...] = jnp.zeros_like(x1_ref)


N_MIX_PROMPT_INPUTS = 9


def _mix_prompt_step(r, n_r, x_ref, lbl_ref, nw_ref, win_ref, cw_ref, cb_ref, gnw_ref, wout_ref, masks_ref,
                     x1_ref, nconv_ref, nstate_ref, st_ref, tail_ref):
    rows = x_ref.shape[0]

    @pl.when(r == 0)
    def _():
        st_ref[...] = jnp.zeros_like(st_ref)
        tail_ref[...] = jnp.zeros_like(tail_ref)

    lb = _forget_lower_bound(lbl_ref[...])
    cw = cw_ref[...]
    n_chunks = MIX_SLAB // PROMPT_CHUNK
    rid = lax.broadcasted_iota(jnp.int32, (MIX_PART, 1), 0)
    t0 = tail_ref[SUBLANES - 2:SUBLANES - 1, :]
    t1 = tail_ref[SUBLANES - 1:SUBLANES, :]
    states = [st_ref[hd] for hd in range(N_HEADS)]

    n_parts = rows // MIX_PART
    part_rows = lambda p: slice(p * MIX_PART, (p + 1) * MIX_PART)
    project = lambda p: _ColumnBlocks(_rmsnorm(x_ref[part_rows(p), :], nw_ref[...]).astype(BF16), win_ref)
    nxt = project(0)
    for p in range(n_parts):
        pr = part_rows(p)
        x = x_ref[pr, :]
        bg, cg, vc, q, ff, vi, g = (nxt.block(c) for c in range(N_PROJ_BLOCKS))
        if p + 1 < n_parts:
            nxt = project(p + 1)
            nxt.block(N_PROJ_BLOCKS - 1)

        u = cg * vc
        u1 = jnp.where(rid == 0, t1, pltpu.roll(u, 1, 0))
        u2 = jnp.where(rid == 0, t0, jnp.where(rid == 1, t1, pltpu.roll(u, 2, 0)))
        y_conv = bg * (cw[0:1, :] * u2 + cw[1:2, :] * u1 + cw[2:3, :] * u + cb_ref[...])
        u_last = u[MIX_PART - SUBLANES:MIX_PART, :]
        t0, t1 = u_last[SUBLANES - 2:SUBLANES - 1, :], u_last[SUBLANES - 1:SUBLANES, :]

        logf, kk = _gates(ff, lb)
        o_parts = []
        for s in range(MIX_PART // MIX_SLAB):
            sl = slice(s * MIX_SLAB, (s + 1) * MIX_SLAB)
            qs, ks, vs = q[sl], kk[sl], vi[sl]
            a_pre = _chunk_prefix_sums(logf[sl], PROMPT_CHUNK)
            a_suf = jnp.concatenate(
                [a_pre[(c + 1) * PROMPT_CHUNK - 1:(c + 1) * PROMPT_CHUNK, :]
                 - a_pre[c * PROMPT_CHUNK:(c + 1) * PROMPT_CHUNK, :] for c in range(n_chunks)], axis=0)
            exps = [_vpu_level_exponent(a_pre, h) for h in PROMPT_VPU_LEVELS]
            exps += [_small_level_exponent(a_pre, logf[sl], h) for h in MXU_LEVELS]
            exps.append(None)
            intra = _intra_scores_times_v(qs, ks, vs, exps, masks_ref)
            ea = jnp.exp(a_pre)
            qa = (qs * ea).astype(BF16)
            kb = (ks * jnp.exp(a_suf)).astype(BF16)
            vb = vs.astype(BF16)
            for c in range(n_chunks):
                cr = slice(c * PROMPT_CHUNK, (c + 1) * PROMPT_CHUNK)
                last = (c + 1) * PROMPT_CHUNK - 1
                heads = []
                for hd in range(N_HEADS):
                    hs = slice(hd * D_HEAD, (hd + 1) * D_HEAD)
                    heads.append(_dot_nt(qa[cr, hs], states[hd].astype(BF16)) + intra[hd][cr, :])
                    states[hd] = states[hd] * ea[last:last + 1, hs] + _dot_tn(vb[cr, hs], kb[cr, hs])
                o_parts.append(jnp.concatenate(heads, axis=1))

        o = _head_out(jnp.concatenate(o_parts, axis=0), g, gnw_ref[...])
        mix_in = jnp.concatenate([y_conv, o], axis=1).astype(BF16)
        x1_ref[pr, :] = x + _dot(mix_in, wout_ref[...])

    tail_ref[...] = u_last
    for hd in range(N_HEADS):
        st_ref[hd] = states[hd]

    @pl.when(r == n_r - 1)
    def _():
        nconv_ref[...] = u_last
        for hd in range(N_HEADS):
            nstate_ref[hd] = states[hd].T


def _mix_prompt(x, lbl, nw, w_in, cw, cb, gnw, w_out, total_rows):
    b, l, _ = x.shape
    assert l % MIX_ROWS == 0 and MIX_ROWS % MIX_PART == 0 and MIX_PART % MIX_SLAB == 0 and l % PROMPT_CHUNK == 0
    assert total_rows % MIX_ROWS == 0
    n_r = l // MIX_ROWS
    n_steps = b * n_r
    seq = lambda s: jnp.minimum(s, n_steps - 1) // n_r
    _, masks = _mix_constants(MIX_SLAB, PROMPT_CHUNK, (), PROMPT_VPU_LEVELS + MXU_LEVELS)
    const = lambda shape: pl.BlockSpec(shape, lambda s: (0,) * len(shape))
    return pl.pallas_call(
        functools.partial(_mix_prompt_kernel, n_r, n_steps),
        grid=(total_rows // MIX_ROWS,),
        in_specs=[
            pl.BlockSpec((None, MIX_ROWS, D_MODEL), lambda s: (seq(s), jnp.minimum(s, n_steps - 1) % n_r, 0)),
            const((2, D_HGRN)), const((1, D_MODEL)), const((D_MODEL, D_PROJ)), const((3, D_CONV)),
            const((1, D_CONV)), const((1, D_HGRN)), const((D_MODEL, D_MODEL)), const(masks.shape),
        ],
        out_specs=[
            pl.BlockSpec((MIX_ROWS, D_MODEL), lambda s: (s, 0)),
            pl.BlockSpec((None, SUBLANES, D_CONV), lambda s: (seq(s), 0, 0)),
            pl.BlockSpec((None, N_HEADS, D_HEAD, D_HEAD), lambda s: (seq(s), 0, 0, 0)),
        ],
        out_shape=[
            jax.ShapeDtypeStruct((total_rows, D_MODEL), F32),
            jax.ShapeDtypeStruct((b, SUBLANES, D_CONV), F32),
            jax.ShapeDtypeStruct((b, N_HEADS, D_HEAD, D_HEAD), F32),
        ],
        scratch_shapes=[
            pltpu.VMEM((N_HEADS, D_HEAD, D_HEAD), F32),
            pltpu.VMEM((SUBLANES, D_CONV), F32),
        ],
        compiler_params=pltpu.CompilerParams(
            dimension_semantics=("arbitrary",), vmem_limit_bytes=VMEM_LIMIT_BYTES),
        name="mix_prompt",
    )(x, lbl, nw, w_in, cw, cb, gnw, w_out, masks)


def _mix_sample_kernel(x_ref, cs_ref, hs_ref, lbl_ref, nw_ref, win_ref, cw_ref, cb_ref, gnw_ref, wout_ref,
                       cmat_ref, masks_ref, x1_in_ref, x1_ref, nconv_ref, nstate_ref, o_ref):
    del x1_in_ref
    nseq, length, _ = nconv_ref.shape
    rows = nseq * length
    x = x_ref[...]
    proj = _ColumnBlocks(_rmsnorm(x, nw_ref[...]).astype(BF16), win_ref)
    bg, cg, vc, q, ff, vi, g = (proj.block(c) for c in range(N_PROJ_BLOCKS))
    lb = _forget_lower_bound(lbl_ref[...])

    u = cg * vc
    cs = cs_ref[...]
    expand = lambda a: jnp.broadcast_to(a, (nseq, length, D_CONV)).reshape(rows, D_CONV)
    t0 = expand(cs[:, 0:1, :])
    t1 = expand(cs[:, 1:2, :])
    pos = lax.broadcasted_iota(jnp.int32, (rows, 1), 0) % length
    u1 = jnp.where(pos == 0, t1, pltpu.roll(u, 1, 0))
    u2 = jnp.where(pos == 0, t0, jnp.where(pos == 1, t1, pltpu.roll(u, 2, 0)))
    cw = cw_ref[...]
    y_conv = bg * (cw[0:1, :] * u2 + cw[1:2, :] * u1 + cw[2:3, :] * u + cb_ref[...])
    nconv_ref[...] = u.reshape(nseq, length, D_CONV)

    logf, kk = _gates(ff, lb)
    lf_hi, lf_lo = _split2(logf)
    ex = _dot(cmat_ref[...], lf_hi) + _dot(cmat_ref[...], lf_lo)
    a_pre = ex[0:rows]
    a_suf = ex[rows:2 * rows]
    exps = [ex[(2 + i) * rows:(3 + i) * rows] for i in range(len(MXU_LEVELS))] + [None]
    intra = _intra_scores_times_v(q, kk, vi, exps, masks_ref)
    ea = jnp.exp(a_pre)
    qa = q * ea
    kb = kk * jnp.exp(a_suf)
    for s in range(nseq):
        cr = slice(s * length, (s + 1) * length)
        last = (s + 1) * length - 1
        for hd in range(N_HEADS):
            hs = slice(hd * D_HEAD, (hd + 1) * D_HEAD)
            st = hs_ref[s, hd].T
            inter = _dot_nt(_pad_rows_bf16(qa[cr, hs]), st.astype(BF16))[0:length, :]
            o_ref[cr, hs] = inter + intra[hd][cr, :]
            st_new = st * ea[last:last + 1, hs] + _dot_tn(_pad_rows_bf16(vi[cr, hs]), _pad_rows_bf16(kb[cr, hs]))
            nstate_ref[s, hd] = st_new.T

    o = _head_out(o_ref[...], g, gnw_ref[...])
    mix_in = jnp.concatenate([y_conv, o], axis=1).astype(BF16)
    x1_ref[...] = x + _dot(mix_in, wout_ref[...])


def _mix_sample(x, conv_state, hgrn_state, lbl, nw, w_in, cw, cb, gnw, w_out, x1_flat, row_offset):
    nb, length, _ = x.shape
    rows = MIX_SEQS * length
    assert nb % MIX_SEQS == 0 and length == SUBLANES and row_offset % rows == 0
    first_block = row_offset // rows
    cmat, masks = _mix_constants(rows, length, MXU_LEVELS, MXU_LEVELS)
    const = lambda shape: pl.BlockSpec(shape, lambda i: (0,) * len(shape))
    return pl.pallas_call(
        _mix_sample_kernel,
        grid=(nb // MIX_SEQS,),
        in_specs=[
            pl.BlockSpec((rows, D_MODEL), lambda i: (i, 0)),
            pl.BlockSpec((MIX_SEQS, 2, D_CONV), lambda i: (i, 0, 0)),
            pl.BlockSpec((MIX_SEQS, N_HEADS, D_HEAD, D_HEAD), lambda i: (i, 0, 0, 0)),
            const((2, D_HGRN)), const((1, D_MODEL)), const((D_MODEL, D_PROJ)), const((3, D_CONV)),
            const((1, D_CONV)), const((1, D_HGRN)), const((D_MODEL, D_MODEL)), const(cmat.shape),
            const(masks.shape), pl.BlockSpec(memory_space=pl.ANY),
        ],
        out_specs=[
            pl.BlockSpec((rows, D_MODEL), lambda i: (first_block + i, 0)),
            pl.BlockSpec((MIX_SEQS, length, D_CONV), lambda i: (i, 0, 0)),
            pl.BlockSpec((MIX_SEQS, N_HEADS, D_HEAD, D_HEAD), lambda i: (i, 0, 0, 0)),
        ],
        out_shape=[
            jax.ShapeDtypeStruct(x1_flat.shape, F32),
            jax.ShapeDtypeStruct((nb, length, D_CONV), F32),
            jax.ShapeDtypeStruct((nb, N_HEADS, D_HEAD, D_HEAD), F32),
        ],
        scratch_shapes=[pltpu.VMEM((rows, D_HGRN), F32)],
        input_output_aliases={12: 0},
        compiler_params=pltpu.CompilerParams(
            dimension_semantics=("arbitrary",), vmem_limit_bytes=VMEM_LIMIT_BYTES),
        name="mix_sample",
    )(x.reshape(nb * length, D_MODEL), conv_state, hgrn_state, lbl, nw, w_in, cw, cb, gnw, w_out, cmat, masks,
      x1_flat)


def _router_kernel(x1_ref, nw_ref, rw_hi_ref, rw_lo_ref, rb_ref, ltri_ref, utri_ref, xn_ref, meta_ref, cnt_ref):
    n = x1_ref.shape[0]
    xn = _rmsnorm(x1_ref[...], nw_ref[...])
    xn_ref[...] = xn.astype(BF16)
    x_hi, x_lo = _split2(xn)
    logits = (_dot(x_hi, rw_hi_ref[...]) + _dot(x_lo, rw_hi_ref[...]) + _dot(x_hi, rw_lo_ref[...])
              + rb_ref[...])
    lane = lax.broadcasted_iota(jnp.int32, (n, LANES), 1).astype(F32)
    work = logits
    vals, ids = [], []
    for _ in range(TOP_K):
        m = jnp.max(work, axis=-1, keepdims=True)
        i = jnp.min(jnp.where(work == m, lane, float(LANES)), axis=-1, keepdims=True)
        vals.append(m)
        ids.append(i)
        work = jnp.where(lane == i, -jnp.inf, work)
    es = [jnp.exp(v - vals[0]) for v in vals]
    den = es[0] + es[1] + es[2] + es[3]
    gates = [e / den for e in es]

    onehots = [(lane == i) for i in ids]
    multi = jnp.zeros((n, LANES), F32)
    for oh in onehots:
        multi = multi + oh.astype(F32)
    counts = jnp.sum(multi, axis=0, keepdims=True)
    before = _dot(ltri_ref[...], multi.astype(BF16))
    seg = jnp.ceil(counts * (1.0 / SEG_ALIGN)) * SEG_ALIGN
    seg_rows = jnp.broadcast_to(seg, (BF16_ROWS, LANES)).astype(BF16)
    seg_off = _dot(seg_rows, utri_ref[...])[0:1, :]
    slot_of = seg_off + before
    meta = jnp.zeros((n, LANES), F32)
    for k in range(TOP_K):
        slot = jnp.sum(jnp.where(onehots[k], slot_of, 0.0), axis=-1, keepdims=True)
        meta = jnp.where(lane == k, ids[k], meta)
        meta = jnp.where(lane == TOP_K + k, gates[k], meta)
        meta = jnp.where(lane == 2 * TOP_K + k, slot, meta)
    meta_ref[...] = meta
    cnt_ref[...] = jnp.broadcast_to(counts, (SUBLANES, LANES))


def _router(x1, nw, router_w, router_b):
    t = x1.shape[0]
    n_tiles = t // TOKEN_TILE
    rw = jnp.zeros((D_MODEL, LANES), F32).at[:, :N_EXPERTS].set(router_w)
    rw_hi, rw_lo = _split2(rw)
    rb = jnp.full((1, LANES), -1e30, F32).at[0, :N_EXPERTS].set(router_b)
    idx = np.arange(TOKEN_TILE)
    ltri = jnp.asarray(idx[None, :] < idx[:, None], BF16)
    lid = np.arange(LANES)
    utri = jnp.asarray(lid[:, None] < lid[None, :], BF16)
    const = lambda shape: pl.BlockSpec(shape, lambda i: (0,) * len(shape))
    return pl.pallas_call(
        _router_kernel,
        grid=(n_tiles,),
        in_specs=[
            pl.BlockSpec((TOKEN_TILE, D_MODEL), lambda i: (i, 0)),
            const((1, D_MODEL)), const((D_MODEL, LANES)), const((D_MODEL, LANES)), const((1, LANES)),
            const((TOKEN_TILE, TOKEN_TILE)), const((LANES, LANES)),
        ],
        out_specs=[
            pl.BlockSpec((TOKEN_TILE, D_MODEL), lambda i: (i, 0)),
            pl.BlockSpec((TOKEN_TILE, LANES), lambda i: (i, 0)),
            pl.BlockSpec((None, SUBLANES, LANES), lambda i: (i, 0, 0)),
        ],
        out_shape=[
            jax.ShapeDtypeStruct((t, D_MODEL), BF16),
            jax.ShapeDtypeStruct((t, LANES), F32),
            jax.ShapeDtypeStruct((n_tiles, SUBLANES, LANES), F32),
        ],
        compiler_params=pltpu.CompilerParams(
            dimension_semantics=("arbitrary",), vmem_limit_bytes=VMEM_LIMIT_BYTES),
        name="router",
    )(x1, nw, rw_hi, rw_lo, rb, ltri, utri)


def _segment_tables(cnt):
    seg = (cnt + SEG_ALIGN - 1) // SEG_ALIGN * SEG_ALIGN
    src = jnp.cumsum(seg, axis=1) - seg
    tot = jnp.sum(seg, axis=0)
    cap = (tot + ROW_TILE - 1) // ROW_TILE * ROW_TILE
    base = jnp.cumsum(cap) - cap
    dst = base[None, :] + jnp.cumsum(seg, axis=0) - seg
    n_used = jnp.sum(cap) // ROW_TILE
    fill = base + tot
    n_tiles = seg.shape[0]
    lane = jnp.arange(N_EXPERTS)

    def compact(has, *values):
        place = has[:, :, None] & ((jnp.cumsum(has, axis=1) - 1)[:, :, None] == lane[None, None, :])
        return [jnp.sum(has, axis=1)] + [jnp.sum(jnp.where(place, v[:, :, None], 0), axis=1) for v in values]

    whole, rest = seg // CHUNK_ROWS, seg % CHUNK_ROWS
    k = jnp.arange(TOKEN_TILE // CHUNK_ROWS) * CHUNK_ROWS
    flat = lambda a: a.reshape(n_tiles, -1)
    pieces = [compact(flat(seg[:, :, None] >= k + CHUNK_ROWS), flat(src[:, :, None] + k), flat(dst[:, :, None] + k))]
    for size in CHUNK_SIZES[1:]:
        pieces.append(compact(rest == size, src + whole * CHUNK_ROWS, dst + whole * CHUNK_ROWS))
    counts, srcs, dsts = zip(*pieces)
    i32 = lambda a: a.astype(jnp.int32).reshape(-1)
    chunks = (i32(jnp.stack(counts, axis=1)), i32(jnp.stack(srcs, axis=1)), i32(jnp.stack(dsts, axis=1)))
    return chunks, i32(fill), i32(cap - tot), i32(base), i32(cap // ROW_TILE), i32(n_used)


def _for_each_chunk(n, fn):
    for size in SEG_SIZES:
        done = n & ~(2 * size - 1)

        @pl.when((n & size) != 0)
        def _():
            fn(done, size)


def _segment_copies(cnt_ref, src_ref, dst_ref, tile, make_copy, start):
    for si, size in enumerate(CHUNK_SIZES):
        first = (tile * len(CHUNK_SIZES) + si) * N_EXPERTS

        def one(p, carry, si=si, size=size, first=first):
            copy = make_copy(pl.multiple_of(src_ref[first + p], SEG_ALIGN),
                             pl.multiple_of(dst_ref[first + p], SEG_ALIGN), size)
            if start:
                copy.start(priority=si % 2)
            else:
                copy.wait()
            return carry
        lax.fori_loop(0, cnt_ref[tile * len(CHUNK_SIZES) + si], one, 0)


SORT_BLOCK = 512


def _slot_matrix(slots, block, gates):
    row = (lax.broadcasted_iota(jnp.int32, (SORT_BLOCK, TOKEN_TILE), 0) + block * SORT_BLOCK).astype(F32)
    m = jnp.zeros((SORT_BLOCK, TOKEN_TILE), F32)
    for k in range(TOP_K):
        m = jnp.where(row == slots[k:k + 1, :], 1.0 if gates is None else gates[k:k + 1, :], m)
    return m.astype(BF16)


def _dispatch_kernel(seg_ref, src_ref, dst_ref, fill_ref, gap_ref, nu_ref, slot_ref, xn_ref, xs_ref, sorted_ref,
                     zero_ref, sem, seg_sem):
    tile = pl.program_id(0)
    n_tiles = pl.num_programs(0)
    buf = tile % 2

    @pl.when(tile == 0)
    def _():
        zero_ref[...] = jnp.zeros_like(zero_ref)

        def fill_copy(row, size):
            return pltpu.make_async_copy(zero_ref.at[pl.ds(0, size)],
                                         xs_ref.at[pl.ds(pl.multiple_of(row, SEG_ALIGN), size)], sem)

        def gaps(act):
            def per_expert(e, carry):
                _for_each_chunk(gap_ref[e], lambda done, size: act(fill_copy(fill_ref[e] + done, size)))
                return carry
            lax.fori_loop(0, N_EXPERTS, per_expert, 0)

        def tail(act):
            n_tail = xs_ref.shape[0] // ROW_TILE - nu_ref[0]
            lax.fori_loop(0, n_tail, lambda i, c: (act(fill_copy((nu_ref[0] + i) * ROW_TILE, ROW_TILE)), c)[1], 0)

        gaps(lambda c: c.start())
        tail(lambda c: c.start())
        gaps(lambda c: c.wait())
        tail(lambda c: c.wait())

    slots = slot_ref[...]
    xn = xn_ref[...]
    for rb in range(TILE_CAP // SORT_BLOCK):
        sorted_ref[buf, rb * SORT_BLOCK:(rb + 1) * SORT_BLOCK, :] = _dot(
            _slot_matrix(slots, rb, None), xn).astype(BF16)

    def copier(half):
        return lambda src, dst, size: pltpu.make_async_copy(
            sorted_ref.at[half, pl.ds(src, size)], xs_ref.at[pl.ds(dst, size)], seg_sem.at[half])

    _segment_copies(seg_ref, src_ref, dst_ref, tile, copier(buf), start=True)

    @pl.when(tile > 0)
    def _():
        _segment_copies(seg_ref, src_ref, dst_ref, tile - 1, copier(1 - buf), start=False)

    @pl.when(tile == n_tiles - 1)
    def _():
        _segment_copies(seg_ref, src_ref, dst_ref, tile, copier(buf), start=False)


def _dispatch(tables, slot_t, xn, n_rows_total):
    t = xn.shape[0]
    n_tiles = t // TOKEN_TILE
    return pl.pallas_call(
        _dispatch_kernel,
        grid_spec=pltpu.PrefetchScalarGridSpec(
            num_scalar_prefetch=len(tables),
            grid=(n_tiles,),
            in_specs=[
                pl.BlockSpec((SUBLANES, TOKEN_TILE), lambda i, *_: (0, i)),
                pl.BlockSpec((TOKEN_TILE, D_MODEL), lambda i, *_: (i, 0)),
            ],
            out_specs=pl.BlockSpec(memory_space=pl.ANY),
            scratch_shapes=[
                pltpu.VMEM((2, TILE_CAP, D_MODEL), BF16),
                pltpu.VMEM((ROW_TILE, D_MODEL), BF16),
                pltpu.SemaphoreType.DMA(()),
                pltpu.SemaphoreType.DMA((2,)),
            ],
        ),
        out_shape=jax.ShapeDtypeStruct((n_rows_total, D_MODEL), BF16),
        compiler_params=pltpu.CompilerParams(
            dimension_semantics=("arbitrary",), vmem_limit_bytes=VMEM_LIMIT_BYTES),
        name="dispatch",
    )(*tables, slot_t, xn)


X_SLOTS = 3
W_FETCH_STEPS = (0, 1, 3)


def _experts_kernel(base_ref, nt_ref, nu_ref, bg_ref, bu_ref, bd_ref, wg_hbm, wu_hbm, wd_hbm, xs_ref, zs_ref,
                    w_f32, wg_bf, wu_bf, wd_bf, xbuf, hbuf, zbuf, w_sem, in_sem, out_sem):
    e = pl.program_id(0)
    n_e = pl.num_programs(0)
    n = nt_ref[e]
    base = base_ref[e]
    w_slot = e % 2
    has_next = e + 1 < n_e

    def w_copy(j, expert, slot):
        w_hbm = (wg_hbm, wu_hbm, wd_hbm)[j]
        return pltpu.make_async_copy(w_hbm.at[expert], w_f32.at[slot, j], w_sem.at[slot])

    def x_copy(i):
        rows = pl.ds(pl.multiple_of(base + i * ROW_TILE, ROW_TILE), ROW_TILE)
        return pltpu.make_async_copy(xs_ref.at[rows], xbuf.at[i % X_SLOTS], in_sem.at[i % X_SLOTS])

    def z_copy(row, slot):
        rows = pl.ds(pl.multiple_of(row, ROW_TILE), ROW_TILE)
        return pltpu.make_async_copy(zbuf.at[slot], zs_ref.at[rows], out_sem.at[slot])

    def hidden(i):
        x = xbuf[i % X_SLOTS]
        a = jnp.minimum(_dot(x, wg_bf[...]) + bg_ref[...], SWIGLU_LIMIT)
        u = jnp.clip(_dot(x, wu_bf[...]) + bu_ref[...], -SWIGLU_LIMIT, SWIGLU_LIMIT)
        hbuf[i % 2] = ((u + 1.0) * a * jax.nn.sigmoid(SWIGLU_ALPHA * a)).astype(BF16)

    def project_down(i, slot):
        @pl.when(i >= 2)
        def _():
            z_copy(base, slot).wait()
        zbuf[slot] = (_dot(hbuf[slot], wd_bf[...]) + bd_ref[...]).astype(BF16)

    @pl.when(e == 0)
    def _():
        for j in range(3):
            w_copy(j, 0, 0).start()

    for i in range(X_SLOTS):
        @pl.when(i < n)
        def _():
            x_copy(i).start(priority=1)

    def fetch_next_weights(j):
        @pl.when(has_next)
        def _():
            w_copy(j, e + 1, 1 - w_slot).start()

    fetch_next_weights(0)
    for j in range(3):
        w_copy(j, e, w_slot).wait()

    @pl.when(n > 0)
    def _():
        wg_bf[...] = w_f32[w_slot, 0].astype(BF16)
        wu_bf[...] = w_f32[w_slot, 1].astype(BF16)
        wd_bf[...] = w_f32[w_slot, 2].astype(BF16)
        x_copy(0).wait()
        hidden(0)

        def step(i, carry):
            x_copy(i + 1).wait()

            @pl.when(i + X_SLOTS < n)
            def _():
                x_copy(i + X_SLOTS).start(priority=1)

            for j in (1, 2):
                @pl.when(i == W_FETCH_STEPS[j])
                def _():
                    fetch_next_weights(j)

            project_down(i, i % 2)
            hidden(i + 1)
            z_copy(base + i * ROW_TILE, i % 2).start()
            return carry

        lax.fori_loop(0, n - 1, step, 0)
        project_down(n - 1, (n - 1) % 2)
        z_copy(base + (n - 1) * ROW_TILE, (n - 1) % 2).start()

        @pl.when(n >= 2)
        def _():
            z_copy(base, n % 2).wait()
        z_copy(base, (n - 1) % 2).wait()

    for j in (1, 2):
        @pl.when(jnp.maximum(n - 1, 0) <= W_FETCH_STEPS[j])
        def _():
            fetch_next_weights(j)

    @pl.when(e == pl.num_programs(0) - 1)
    def _():
        zbuf[0] = jnp.zeros((ROW_TILE, D_MODEL), BF16)
        n_tail = zs_ref.shape[0] // ROW_TILE - nu_ref[0]
        tail = lambda act: lax.fori_loop(
            0, n_tail, lambda i, c: (act(z_copy((nu_ref[0] + i) * ROW_TILE, 0)), c)[1], 0)
        tail(lambda c: c.start())
        tail(lambda c: c.wait())


def _experts(base, n_tiles, n_used, xs, w_gate, b_gate, w_up, b_up, w_down, b_down):
    b_spec = pl.BlockSpec((None, 1, D_MODEL), lambda e, *_: (e, 0, 0))
    any_spec = pl.BlockSpec(memory_space=pl.ANY)
    return pl.pallas_call(
        _experts_kernel,
        grid_spec=pltpu.PrefetchScalarGridSpec(
            num_scalar_prefetch=3,
            grid=(N_EXPERTS,),
            in_specs=[b_spec, b_spec, b_spec, any_spec, any_spec, any_spec, any_spec],
            out_specs=any_spec,
            scratch_shapes=[pltpu.VMEM((2, 3, D_MODEL, D_MODEL), F32)] + [
                pltpu.VMEM((D_MODEL, D_MODEL), BF16) for _ in range(3)] + [
                pltpu.VMEM((X_SLOTS, ROW_TILE, D_MODEL), BF16)] + [
                pltpu.VMEM((2, ROW_TILE, D_MODEL), BF16) for _ in range(2)] + [
                pltpu.SemaphoreType.DMA((2,)), pltpu.SemaphoreType.DMA((X_SLOTS,)), pltpu.SemaphoreType.DMA((2,))],
        ),
        out_shape=jax.ShapeDtypeStruct(xs.shape, BF16),
        compiler_params=pltpu.CompilerParams(
            dimension_semantics=("arbitrary",), vmem_limit_bytes=VMEM_LIMIT_BYTES),
        name="experts",
    )(base, n_tiles, n_used, b_gate[:, None, :], b_up[:, None, :], b_down[:, None, :], w_gate, w_up, w_down, xs)


def _combine_kernel(n_first, seg_ref, src_ref, dst_ref, route_ref, x1_ref, fw_ref, zs_ref, ya_ref, yb_ref,
                    sorted_ref, sem):
    tile = pl.program_id(0)
    n_tiles = pl.num_programs(0)
    buf = tile % 2

    def fetch(t, half):
        sorted_ref[half] = jnp.zeros((TILE_CAP, D_MODEL), BF16)
        _segment_copies(seg_ref, src_ref, dst_ref, t, copier(half), start=True)

    def copier(half):
        return lambda src, dst, size: pltpu.make_async_copy(
            zs_ref.at[pl.ds(dst, size)], sorted_ref.at[half, pl.ds(src, size)], sem.at[half])

    @pl.when(tile == 0)
    def _():
        fetch(tile, buf)

    @pl.when(tile + 1 < n_tiles)
    def _():
        fetch(tile + 1, 1 - buf)

    _segment_copies(seg_ref, src_ref, dst_ref, tile, copier(buf), start=False)
    route = route_ref[...]
    slots, gates = route[0:TOP_K, :], route[TOP_K:2 * TOP_K, :]
    moe = jnp.zeros((TOKEN_TILE, D_MODEL), F32)
    for cb in range(TILE_CAP // SORT_BLOCK):
        moe = moe + _dot_tn(_slot_matrix(slots, cb, gates), sorted_ref[buf, cb * SORT_BLOCK:(cb + 1) * SORT_BLOCK, :])
    y = _rmsnorm(x1_ref[...] + moe, fw_ref[...])

    @pl.when(tile < n_first)
    def _():
        ya_ref[...] = y

    @pl.when(tile >= n_first)
    def _():
        yb_ref[...] = y


def _combine(tables, route_t, x1, final_w, zs, rows_first):
    seg, src, dst = tables
    t = x1.shape[0]
    assert rows_first % TOKEN_TILE == 0 and 0 < rows_first < t and TILE_CAP % SORT_BLOCK == 0
    n_tiles = t // TOKEN_TILE
    n_first = rows_first // TOKEN_TILE
    return pl.pallas_call(
        functools.partial(_combine_kernel, n_first),
        grid_spec=pltpu.PrefetchScalarGridSpec(
            num_scalar_prefetch=3,
            grid=(n_tiles,),
            in_specs=[
                pl.BlockSpec((SUBLANES, TOKEN_TILE), lambda i, *_: (0, i)),
                pl.BlockSpec((TOKEN_TILE, D_MODEL), lambda i, *_: (i, 0)),
                pl.BlockSpec((1, D_MODEL), lambda i, *_: (0, 0)),
                pl.BlockSpec(memory_space=pl.ANY),
            ],
            out_specs=[
                pl.BlockSpec((TOKEN_TILE, D_MODEL), lambda i, *_: (jnp.minimum(i, n_first - 1), 0)),
                pl.BlockSpec((TOKEN_TILE, D_MODEL), lambda i, *_: (jnp.maximum(i - n_first, 0), 0)),
            ],
            scratch_shapes=[pltpu.VMEM((2, TILE_CAP, D_MODEL), BF16), pltpu.SemaphoreType.DMA((2,))],
        ),
        out_shape=[jax.ShapeDtypeStruct((rows_first, D_MODEL), F32),
                   jax.ShapeDtypeStruct((t - rows_first, D_MODEL), F32)],
        compiler_params=pltpu.CompilerParams(
            dimension_semantics=("arbitrary",), vmem_limit_bytes=VMEM_LIMIT_BYTES),
        name="combine",
    )(seg, src, dst, route_t, x1, final_w, zs)


def _moe_and_final_norm(x1, rows_first, norm_ffn_w, router_w, router_b, w_gate, b_gate, w_up, b_up, w_down, b_down,
                        final_w):
    t = x1.shape[0]
    assert t % TOKEN_TILE == 0
    n_tiles = t // TOKEN_TILE
    n_rows_total = TOP_K * t + n_tiles * N_EXPERTS * (SEG_ALIGN - 1) + N_EXPERTS * (ROW_TILE - 1)
    n_rows_total = (n_rows_total + ROW_TILE - 1) // ROW_TILE * ROW_TILE
    xn, meta, cnt = _router(x1, norm_ffn_w.reshape(1, D_MODEL), router_w, router_b)
    cnt = cnt[:, 0, :N_EXPERTS].astype(jnp.int32)
    chunks, fill, gap, base, n_row_tiles, n_used = _segment_tables(cnt)
    route_t = jnp.concatenate([meta[:, 2 * TOP_K:3 * TOP_K], meta[:, TOP_K:2 * TOP_K]], axis=1).T
    xs = _dispatch((*chunks, fill, gap, n_used), route_t, xn, n_rows_total)
    zs = _experts(base, n_row_tiles, n_used, xs, w_gate, b_gate, w_up, b_up, w_down, b_down)
    return _combine(chunks, route_t, x1, final_w.reshape(1, D_MODEL), zs, rows_first)


def kernel(x_prompt, x_sample, state_conv, state_hgrn, lb_logits, norm_mix_w, w_in, conv_w, conv_b, gnorm_w,
           w_out, norm_ffn_w, router_w, router_b, w_gate, b_gate, w_up, b_up, w_down, b_down, final_norm_w):
    assert norm_mix_w.shape[0] == 1 and lb_logits.shape[0] == 2, "single-layer step"
    b, l, _ = x_prompt.shape
    nb, ls, _ = x_sample.shape
    lbl = lb_logits.astype(F32)
    nw = norm_mix_w[0].reshape(1, D_MODEL)
    w_in_bf = w_in[0].astype(BF16)
    w_out_bf = w_out[0].astype(BF16)
    cw, cb = conv_w[0], conv_b[0].reshape(1, D_CONV)
    gnw = gnorm_w[0].reshape(1, D_HGRN)
    rows_p, rows_s = b * l, nb * ls
    x1, conv_p, hgrn_p = _mix_prompt(x_prompt, lbl, nw, w_in_bf, cw, cb, gnw, w_out_bf, rows_p + rows_s)
    x1, conv_s, hgrn_s = _mix_sample(x_sample, state_conv[0], state_hgrn[0], lbl, nw, w_in_bf, cw, cb, gnw,
                                     w_out_bf, x1, rows_p)
    y_p, y_s = _moe_and_final_norm(x1, rows_p, norm_ffn_w[0], router_w[0], router_b[0], w_gate[0], b_gate[0],
                                   w_up[0], b_up[0], w_down[0], b_down[0], final_norm_w)
    conv_p = conv_p[:, SUBLANES - 2:, :]
    conv_s = conv_s[:, ls - 2:, :]
    return (y_p.reshape(b, l, D_MODEL), y_s.reshape(nb, ls, D_MODEL), conv_p[None], hgrn_p[None], conv_s[None],
            hgrn_s[None])
```

```python
import functools

import numpy as np
import jax
import jax.numpy as jnp
from jax import lax
from jax.experimental import pallas as pl
from jax.experimental.pallas import tpu as pltpu

F32 = jnp.float32
BF16 = jnp.bfloat16

D_MODEL = 1024
D_CONV = 512
D_HGRN = 512
N_HEADS = 4
D_HEAD = 128
D_PROJ = 3 * D_CONV + 4 * D_HGRN
N_EXPERTS = 32
TOP_K = 4
SWIGLU_LIMIT = 7.0
SWIGLU_ALPHA = 1.702
EPS = 1e-6
PAD_LOGIT = -1e30
PROMPT_CHUNK = 64

LANES = 128
SUBLANES = 8
BF16_ROWS = 16
VMEM_LIMIT_BYTES = 56 * 1024 * 1024

MIX_SLAB = 128
MIX_ROWS = 512
MIX_PART = 256
MIX_SEQS = 16
TOKEN_TILE = 512
ROUTER_PART = 256
SEG_ALIGN = BF16_ROWS
ROW_TILE = 512
SEG_SIZES = tuple(SEG_ALIGN << i for i in reversed(range(6)))
CHUNK_ROWS = 128
CHUNK_SIZES = tuple(range(CHUNK_ROWS, 0, -SEG_ALIGN))
TILE_CAP = TOP_K * TOKEN_TILE + N_EXPERTS * SEG_ALIGN
assert TILE_CAP // CHUNK_ROWS <= N_EXPERTS

NT_DIMS = (((1,), (1,)), ((), ()))
TN_DIMS = (((0,), (0,)), ((), ()))


def _dot(a, b):
    return jnp.dot(a, b, preferred_element_type=F32)


def _dot_nt(a, b):
    return lax.dot_general(a, b, NT_DIMS, preferred_element_type=F32)


def _dot_tn(a, b):
    return lax.dot_general(a, b, TN_DIMS, preferred_element_type=F32)


def _split2(x):
    hi = x.astype(BF16)
    lo = (x - hi.astype(F32)).astype(BF16)
    return hi, lo


def _rmsnorm(x, w):
    return x * lax.rsqrt(jnp.mean(x * x, axis=-1, keepdims=True) + EPS) * w


def _level_exponent_matrix(n, chunk, h):
    x = np.zeros((n, n), np.float32)
    for t in range(n):
        base = t - t % (2 * h)
        m = base + h - 1
        if t % (2 * h) >= h:
            x[t, m + 1:t + 1] = 1.0
        else:
            x[t, t + 1:m + 1] = 1.0
    return x


def _exponent_matrices(n, chunk, levels):
    t = np.arange(n)
    same_chunk = (t[:, None] // chunk) == (t[None, :] // chunk)
    tri = (same_chunk & (t[None, :] <= t[:, None])).astype(np.float32)
    suf = (same_chunk & (t[None, :] > t[:, None])).astype(np.float32)
    cmat = np.concatenate([tri, suf] + [_level_exponent_matrix(n, chunk, h) for h in levels], axis=0)
    return jnp.asarray(cmat, BF16)


def _level_masks(n, levels):
    t = np.arange(n)
    masks = []
    for h in levels:
        blk = (t[:, None] // (2 * h)) == (t[None, :] // (2 * h))
        masks.append((blk & ((t[:, None] % (2 * h)) >= h) & ((t[None, :] % (2 * h)) < h)).astype(np.float32))
    masks.append(np.eye(n, dtype=np.float32))
    return jnp.asarray(np.stack(masks), F32)


def _vpu_level_exponent(a, h):
    n = a.shape[0]
    pieces = []
    for j in range(n // (2 * h)):
        b = j * 2 * h
        ref = a[b + h - 1:b + h, :]
        pieces.append(ref - a[b:b + h, :])
        pieces.append(a[b + h:b + 2 * h, :] - ref)
    return jnp.concatenate(pieces, axis=0)


def _chunk_prefix_sums(x, chunk):
    n, width = x.shape
    x3 = x.reshape(n // SUBLANES, SUBLANES, width)
    pos = lax.broadcasted_iota(jnp.int32, (1, SUBLANES, 1), 1)
    d = 1
    while d < SUBLANES:
        x3 = x3 + jnp.where(pos >= d, pltpu.roll(x3, d, 1), 0.0)
        d *= 2
    x = x3.reshape(n, width)
    pieces = []
    for b in range(0, n, SUBLANES):
        blk = x[b:b + SUBLANES, :]
        if b % chunk:
            blk = blk + pieces[-1][SUBLANES - 1:SUBLANES, :]
        pieces.append(blk)
    return jnp.concatenate(pieces, axis=0)


def _small_level_exponent(a, logf, h):
    n = a.shape[0]
    pos = lax.broadcasted_iota(jnp.int32, (SUBLANES, 1), 0)
    if h == 1:
        odd = lax.broadcasted_iota(jnp.int32, (n, 1), 0) % 2 == 1
        return jnp.where(odd, logf, 0.0)
    pieces = []
    for b in range(0, n, SUBLANES):
        blk = a[b:b + SUBLANES, :]
        ref = blk[h - 1:h, :]
        for first in range(2 * h, SUBLANES, 2 * h):
            ref = jnp.where(pos >= first, blk[first + h - 1:first + h, :], ref)
        pieces.append(jnp.where(pos % (2 * h) >= h, blk - ref, ref - blk))
    return jnp.concatenate(pieces, axis=0)


def _forget_lower_bound(lbl):
    m = jnp.max(lbl, axis=0, keepdims=True)
    e = jnp.exp(lbl - m)
    return e[0:1, :] / jnp.sum(e, axis=0, keepdims=True)


PROJ_BLOCK = 512
N_PROJ_BLOCKS = D_PROJ // PROJ_BLOCK


class _ColumnBlocks:
    def __init__(self, h, w_in_ref):
        self.h, self.w, self.done = h, w_in_ref, []

    def next_block(self):
        c = len(self.done)
        if c < N_PROJ_BLOCKS:
            self.done.append(_dot(self.h, self.w[:, c * PROJ_BLOCK:(c + 1) * PROJ_BLOCK]))

    def block(self, c):
        while len(self.done) <= c:
            self.next_block()
        return self.done[c]


def _gates(ff, lb):
    e = jnp.exp(-jnp.abs(ff))
    r = 1.0 / (1.0 + e)
    pos = ff >= 0
    sig = jnp.where(pos, r, e * r)
    sig_neg = jnp.where(pos, e * r, r)
    logf = jnp.log(lb + (1.0 - lb) * sig)
    kk = (1.0 - lb) * sig_neg
    return logf, kk


def _intra_scores_times_v(q, kk, vi, exps, masks_ref):
    n = q.shape[0]
    sc = [jnp.zeros((n, n), F32) for _ in range(N_HEADS)]
    for l, ex in enumerate(exps):
        if ex is None:
            qh, kh = q.astype(BF16), kk.astype(BF16)
        else:
            w = jnp.exp(ex)
            qh, kh = (q * w).astype(BF16), (kk * w).astype(BF16)
        mask = masks_ref[l]
        for hd in range(N_HEADS):
            hs = slice(hd * D_HEAD, (hd + 1) * D_HEAD)
            sc[hd] = sc[hd] + mask * _dot_nt(qh[:, hs], kh[:, hs])
    vb = vi.astype(BF16)
    return [_dot(sc[hd].astype(BF16), vb[:, hd * D_HEAD:(hd + 1) * D_HEAD]) for hd in range(N_HEADS)]


def _head_out(o, g, gnw):
    parts = []
    for hd in range(N_HEADS):
        oh = o[:, hd * D_HEAD:(hd + 1) * D_HEAD]
        parts.append(oh * lax.rsqrt(jnp.mean(oh * oh, axis=-1, keepdims=True) + EPS))
    on = jnp.concatenate(parts, axis=1)
    return on * gnw * (g * jax.nn.sigmoid(g))


def _pad_rows_bf16(a):
    return jnp.concatenate([a, jnp.zeros_like(a)], axis=0).astype(BF16)


SAMPLE_LEVELS = (4, 2, 1)
PROMPT_WIDE_LEVELS = (32, 16, 8)
PROMPT_LEVELS = PROMPT_WIDE_LEVELS + SAMPLE_LEVELS
N_MIX_PROMPT_INPUTS = 9


def _mix_prompt_kernel(n_r, n_steps, *refs):
    step = pl.program_id(0)

    @pl.when(step < n_steps)
    def _():
        _mix_prompt_step(step % n_r, n_r, *refs)

    @pl.when(step >= n_steps)
    def _():
        x1_ref = refs[N_MIX_PROMPT_INPUTS]
        x1_ref[...] = jnp.zeros_like(x1_ref)


def _mix_prompt_step(r, n_r, x_ref, lbl_ref, nw_ref, win_ref, cw_ref, cb_ref, gnw_ref, wout_ref, masks_ref,
                     x1_ref, nconv_ref, nstate_ref, st_ref, tail_ref):
    rows = x_ref.shape[0]

    @pl.when(r == 0)
    def _():
        st_ref[...] = jnp.zeros_like(st_ref)
        tail_ref[...] = jnp.zeros_like(tail_ref)

    lb = _forget_lower_bound(lbl_ref[...])
    cw = cw_ref[...]
    n_chunks = MIX_SLAB // PROMPT_CHUNK
    rid = lax.broadcasted_iota(jnp.int32, (MIX_PART, 1), 0)
    t0 = tail_ref[SUBLANES - 2:SUBLANES - 1, :]
    t1 = tail_ref[SUBLANES - 1:SUBLANES, :]
    states = [st_ref[hd] for hd in range(N_HEADS)]

    n_parts = rows // MIX_PART
    part_rows = lambda p: slice(p * MIX_PART, (p + 1) * MIX_PART)
    project = lambda p: _ColumnBlocks(_rmsnorm(x_ref[part_rows(p), :], nw_ref[...]).astype(BF16), win_ref)
    nxt = project(0)
    for p in range(n_parts):
        pr = part_rows(p)
        x = x_ref[pr, :]
        bg, cg, vc, q, ff, vi, g = (nxt.block(c) for c in range(N_PROJ_BLOCKS))
        if p + 1 < n_parts:
            nxt = project(p + 1)
            nxt.block(N_PROJ_BLOCKS - 1)

        u = cg * vc
        u1 = jnp.where(rid == 0, t1, pltpu.roll(u, 1, 0))
        u2 = jnp.where(rid == 0, t0, jnp.where(rid == 1, t1, pltpu.roll(u, 2, 0)))
        y_conv = bg * (cw[0:1, :] * u2 + cw[1:2, :] * u1 + cw[2:3, :] * u + cb_ref[...])
        u_last = u[MIX_PART - SUBLANES:MIX_PART, :]
        t0, t1 = u_last[SUBLANES - 2:SUBLANES - 1, :], u_last[SUBLANES - 1:SUBLANES, :]

        logf, kk = _gates(ff, lb)
        o_parts = []
        for s in range(MIX_PART // MIX_SLAB):
            sl = slice(s * MIX_SLAB, (s + 1) * MIX_SLAB)
            qs, ks, vs = q[sl], kk[sl], vi[sl]
            a_pre = _chunk_prefix_sums(logf[sl], PROMPT_CHUNK)
            a_suf = jnp.concatenate(
                [a_pre[(c + 1) * PROMPT_CHUNK - 1:(c + 1) * PROMPT_CHUNK, :]
                 - a_pre[c * PROMPT_CHUNK:(c + 1) * PROMPT_CHUNK, :] for c in range(n_chunks)], axis=0)
            exps = [_vpu_level_exponent(a_pre, h) for h in PROMPT_WIDE_LEVELS]
            exps += [_small_level_exponent(a_pre, logf[sl], h) for h in SAMPLE_LEVELS]
            exps.append(None)
            intra = _intra_scores_times_v(qs, ks, vs, exps, masks_ref)
            ea = jnp.exp(a_pre)
            qa = (qs * ea).astype(BF16)
            kb = (ks * jnp.exp(a_suf)).astype(BF16)
            vb = vs.astype(BF16)
            for c in range(n_chunks):
                cr = slice(c * PROMPT_CHUNK, (c + 1) * PROMPT_CHUNK)
                last = (c + 1) * PROMPT_CHUNK - 1
                heads = []
                for hd in range(N_HEADS):
                    hs = slice(hd * D_HEAD, (hd + 1) * D_HEAD)
                    heads.append(_dot_nt(qa[cr, hs], states[hd].astype(BF16)) + intra[hd][cr, :])
                    states[hd] = states[hd] * ea[last:last + 1, hs] + _dot_tn(vb[cr, hs], kb[cr, hs])
                o_parts.append(jnp.concatenate(heads, axis=1))

        o = _head_out(jnp.concatenate(o_parts, axis=0), g, gnw_ref[...])
        mix_in = jnp.concatenate([y_conv, o], axis=1).astype(BF16)
        x1_ref[pr, :] = x + _dot(mix_in, wout_ref[...])

    tail_ref[...] = u_last
    for hd in range(N_HEADS):
        st_ref[hd] = states[hd]

    @pl.when(r == n_r - 1)
    def _():
        nconv_ref[...] = u_last
        for hd in range(N_HEADS):
            nstate_ref[hd] = states[hd].T


def _mix_prompt(x, lbl, nw, w_in, cw, cb, gnw, w_out, total_rows):
    b, l, _ = x.shape
    assert l % MIX_ROWS == 0 and MIX_ROWS % MIX_PART == 0 and MIX_PART % MIX_SLAB == 0 and l % PROMPT_CHUNK == 0
    assert total_rows % MIX_ROWS == 0
    n_r = l // MIX_ROWS
    n_steps = b * n_r
    seq = lambda s: jnp.minimum(s, n_steps - 1) // n_r
    masks = _level_masks(MIX_SLAB, PROMPT_LEVELS)
    const = lambda shape: pl.BlockSpec(shape, lambda s: (0,) * len(shape))
    return pl.pallas_call(
        functools.partial(_mix_prompt_kernel, n_r, n_steps),
        grid=(total_rows // MIX_ROWS,),
        in_specs=[
            pl.BlockSpec((None, MIX_ROWS, D_MODEL), lambda s: (seq(s), jnp.minimum(s, n_steps - 1) % n_r, 0)),
            const((2, D_HGRN)), const((1, D_MODEL)), const((D_MODEL, D_PROJ)), const((3, D_CONV)),
            const((1, D_CONV)), const((1, D_HGRN)), const((D_MODEL, D_MODEL)), const(masks.shape),
        ],
        out_specs=[
            pl.BlockSpec((MIX_ROWS, D_MODEL), lambda s: (s, 0)),
            pl.BlockSpec((None, SUBLANES, D_CONV), lambda s: (seq(s), 0, 0)),
            pl.BlockSpec((None, N_HEADS, D_HEAD, D_HEAD), lambda s: (seq(s), 0, 0, 0)),
        ],
        out_shape=[
            jax.ShapeDtypeStruct((total_rows, D_MODEL), F32),
            jax.ShapeDtypeStruct((b, SUBLANES, D_CONV), F32),
            jax.ShapeDtypeStruct((b, N_HEADS, D_HEAD, D_HEAD), F32),
        ],
        scratch_shapes=[
            pltpu.VMEM((N_HEADS, D_HEAD, D_HEAD), F32),
            pltpu.VMEM((SUBLANES, D_CONV), F32),
        ],
        compiler_params=pltpu.CompilerParams(
            dimension_semantics=("arbitrary",), vmem_limit_bytes=VMEM_LIMIT_BYTES),
        name="mix_prompt",
    )(x, lbl, nw, w_in, cw, cb, gnw, w_out, masks)


def _mix_sample_kernel(x_ref, cs_ref, hs_ref, lbl_ref, nw_ref, win_ref, cw_ref, cb_ref, gnw_ref, wout_ref,
                       cmat_ref, masks_ref, x1_in_ref, x1_ref, nconv_ref, nstate_ref, o_ref):
    del x1_in_ref
    nseq, length, _ = nconv_ref.shape
    rows = nseq * length
    x = x_ref[...]
    proj = _ColumnBlocks(_rmsnorm(x, nw_ref[...]).astype(BF16), win_ref)
    bg, cg, vc, q, ff, vi, g = (proj.block(c) for c in range(N_PROJ_BLOCKS))
    lb = _forget_lower_bound(lbl_ref[...])

    u = cg * vc
    cs = cs_ref[...]
    expand = lambda a: jnp.broadcast_to(a, (nseq, length, D_CONV)).reshape(rows, D_CONV)
    t0 = expand(cs[:, 0:1, :])
    t1 = expand(cs[:, 1:2, :])
    pos = lax.broadcasted_iota(jnp.int32, (rows, 1), 0) % length
    u1 = jnp.where(pos == 0, t1, pltpu.roll(u, 1, 0))
    u2 = jnp.where(pos == 0, t0, jnp.where(pos == 1, t1, pltpu.roll(u, 2, 0)))
    cw = cw_ref[...]
    y_conv = bg * (cw[0:1, :] * u2 + cw[1:2, :] * u1 + cw[2:3, :] * u + cb_ref[...])
    nconv_ref[...] = u.reshape(nseq, length, D_CONV)

    logf, kk = _gates(ff, lb)
    lf_hi, lf_lo = _split2(logf)
    ex = _dot(cmat_ref[...], lf_hi) + _dot(cmat_ref[...], lf_lo)
    a_pre = ex[0:rows]
    a_suf = ex[rows:2 * rows]
    exps = [ex[(2 + i) * rows:(3 + i) * rows] for i in range(len(SAMPLE_LEVELS))] + [None]
    intra = _intra_scores_times_v(q, kk, vi, exps, masks_ref)
    ea = jnp.exp(a_pre)
    qa = q * ea
    kb = kk * jnp.exp(a_suf)
    for s in range(nseq):
        cr = slice(s * length, (s + 1) * length)
        last = (s + 1) * length - 1
        for hd in range(N_HEADS):
            hs = slice(hd * D_HEAD, (hd + 1) * D_HEAD)
            st = hs_ref[s, hd].T
            inter = _dot_nt(_pad_rows_bf16(qa[cr, hs]), st.astype(BF16))[0:length, :]
            o_ref[cr, hs] = inter + intra[hd][cr, :]
            st_new = st * ea[last:last + 1, hs] + _dot_tn(_pad_rows_bf16(vi[cr, hs]), _pad_rows_bf16(kb[cr, hs]))
            nstate_ref[s, hd] = st_new.T

    o = _head_out(o_ref[...], g, gnw_ref[...])
    mix_in = jnp.concatenate([y_conv, o], axis=1).astype(BF16)
    x1_ref[...] = x + _dot(mix_in, wout_ref[...])


def _mix_sample(x, conv_state, hgrn_state, lbl, nw, w_in, cw, cb, gnw, w_out, x1_flat, row_offset):
    nb, length, _ = x.shape
    rows = MIX_SEQS * length
    assert nb % MIX_SEQS == 0 and length == SUBLANES and row_offset % rows == 0
    first_block = row_offset // rows
    cmat, masks = _exponent_matrices(rows, length, SAMPLE_LEVELS), _level_masks(rows, SAMPLE_LEVELS)
    const = lambda shape: pl.BlockSpec(shape, lambda i: (0,) * len(shape))
    return pl.pallas_call(
        _mix_sample_kernel,
        grid=(nb // MIX_SEQS,),
        in_specs=[
            pl.BlockSpec((rows, D_MODEL), lambda i: (i, 0)),
            pl.BlockSpec((MIX_SEQS, 2, D_CONV), lambda i: (i, 0, 0)),
            pl.BlockSpec((MIX_SEQS, N_HEADS, D_HEAD, D_HEAD), lambda i: (i, 0, 0, 0)),
            const((2, D_HGRN)), const((1, D_MODEL)), const((D_MODEL, D_PROJ)), const((3, D_CONV)),
            const((1, D_CONV)), const((1, D_HGRN)), const((D_MODEL, D_MODEL)), const(cmat.shape),
            const(masks.shape), pl.BlockSpec(memory_space=pl.ANY),
        ],
        out_specs=[
            pl.BlockSpec((rows, D_MODEL), lambda i: (first_block + i, 0)),
            pl.BlockSpec((MIX_SEQS, length, D_CONV), lambda i: (i, 0, 0)),
            pl.BlockSpec((MIX_SEQS, N_HEADS, D_HEAD, D_HEAD), lambda i: (i, 0, 0, 0)),
        ],
        out_shape=[
            jax.ShapeDtypeStruct(x1_flat.shape, F32),
            jax.ShapeDtypeStruct((nb, length, D_CONV), F32),
            jax.ShapeDtypeStruct((nb, N_HEADS, D_HEAD, D_HEAD), F32),
        ],
        scratch_shapes=[pltpu.VMEM((rows, D_HGRN), F32)],
        input_output_aliases={12: 0},
        compiler_params=pltpu.CompilerParams(
            dimension_semantics=("arbitrary",), vmem_limit_bytes=VMEM_LIMIT_BYTES),
        name="mix_sample",
    )(x.reshape(nb * length, D_MODEL), conv_state, hgrn_state, lbl, nw, w_in, cw, cb, gnw, w_out, cmat, masks,
      x1_flat)


def _router_kernel(x1_ref, nw_ref, rw_hi_ref, rw_lo_ref, rb_ref, ltri_ref, utri_ref, xn_ref, meta_ref, cnt_ref):
    n = ltri_ref.shape[0]
    parts = [slice(p * n, (p + 1) * n) for p in range(x1_ref.shape[0] // n)]
    lane = lax.broadcasted_iota(jnp.int32, (n, LANES), 1).astype(F32)
    logits = []
    for pr in parts:
        xn = _rmsnorm(x1_ref[pr, :], nw_ref[...])
        xn_ref[pr, :] = xn.astype(BF16)
        x_hi, x_lo = _split2(xn)
        logits.append(_dot(x_hi, rw_hi_ref[...]) + _dot(x_lo, rw_hi_ref[...]) + _dot(x_hi, rw_lo_ref[...])
                      + rb_ref[...])
    picks = []
    for work in logits:
        vals, ids = [], []
        for _ in range(TOP_K):
            m = jnp.max(work, axis=-1, keepdims=True)
            i = jnp.min(jnp.where(work == m, lane, float(LANES)), axis=-1, keepdims=True)
            vals.append(m)
            ids.append(i)
            work = jnp.where(lane == i, -jnp.inf, work)
        es = [jnp.exp(v - vals[0]) for v in vals]
        den = es[0] + es[1] + es[2] + es[3]
        onehots = [(lane == i) for i in ids]
        multi = jnp.zeros((n, LANES), F32)
        for oh in onehots:
            multi = multi + oh.astype(F32)
        picks.append((ids, [e / den for e in es], onehots, multi))

    befores, counts = [], jnp.zeros((1, LANES), F32)
    for _, _, _, multi in picks:
        befores.append(counts + _dot(ltri_ref[...], multi.astype(BF16)))
        counts = counts + jnp.sum(multi, axis=0, keepdims=True)
    seg = jnp.ceil(counts * (1.0 / SEG_ALIGN)) * SEG_ALIGN
    seg_rows = jnp.broadcast_to(seg, (BF16_ROWS, LANES)).astype(BF16)
    seg_off = _dot(seg_rows, utri_ref[...])[0:1, :]
    for pr, (ids, gates, onehots, _), before in zip(parts, picks, befores):
        slot_of = seg_off + before
        meta = jnp.zeros((n, LANES), F32)
        for k in range(TOP_K):
            slot = jnp.sum(jnp.where(onehots[k], slot_of, 0.0), axis=-1, keepdims=True)
            meta = jnp.where(lane == k, ids[k], meta)
            meta = jnp.where(lane == TOP_K + k, gates[k], meta)
            meta = jnp.where(lane == 2 * TOP_K + k, slot, meta)
        meta_ref[pr, :] = meta
    cnt_ref[...] = jnp.broadcast_to(counts, (SUBLANES, LANES))


def _router(x1, nw, router_w, router_b):
    t = x1.shape[0]
    n_tiles = t // TOKEN_TILE
    rw = jnp.zeros((D_MODEL, LANES), F32).at[:, :N_EXPERTS].set(router_w)
    rw_hi, rw_lo = _split2(rw)
    rb = jnp.full((1, LANES), PAD_LOGIT, F32).at[0, :N_EXPERTS].set(router_b)
    idx = np.arange(ROUTER_PART)
    ltri = jnp.asarray(idx[None, :] < idx[:, None], BF16)
    lid = np.arange(LANES)
    utri = jnp.asarray(lid[:, None] < lid[None, :], BF16)
    const = lambda shape: pl.BlockSpec(shape, lambda i: (0,) * len(shape))
    return pl.pallas_call(
        _router_kernel,
        grid=(n_tiles,),
        in_specs=[
            pl.BlockSpec((TOKEN_TILE, D_MODEL), lambda i: (i, 0)),
            const((1, D_MODEL)), const((D_MODEL, LANES)), const((D_MODEL, LANES)), const((1, LANES)),
            const((ROUTER_PART, ROUTER_PART)), const((LANES, LANES)),
        ],
        out_specs=[
            pl.BlockSpec((TOKEN_TILE, D_MODEL), lambda i: (i, 0)),
            pl.BlockSpec((TOKEN_TILE, LANES), lambda i: (i, 0)),
            pl.BlockSpec((None, SUBLANES, LANES), lambda i: (i, 0, 0)),
        ],
        out_shape=[
            jax.ShapeDtypeStruct((t, D_MODEL), BF16),
            jax.ShapeDtypeStruct((t, LANES), F32),
            jax.ShapeDtypeStruct((n_tiles, SUBLANES, LANES), F32),
        ],
        compiler_params=pltpu.CompilerParams(
            dimension_semantics=("arbitrary",), vmem_limit_bytes=VMEM_LIMIT_BYTES),
        name="router",
    )(x1, nw, rw_hi, rw_lo, rb, ltri, utri)


def _segment_tables(cnt):
    seg = (cnt + SEG_ALIGN - 1) // SEG_ALIGN * SEG_ALIGN
    src = jnp.cumsum(seg, axis=1) - seg
    tot = jnp.sum(seg, axis=0)
    cap = (tot + ROW_TILE - 1) // ROW_TILE * ROW_TILE
    base = jnp.cumsum(cap) - cap
    dst = base[None, :] + jnp.cumsum(seg, axis=0) - seg
    n_used = jnp.sum(cap) // ROW_TILE
    fill = base + tot
    n_tiles = seg.shape[0]
    lane = jnp.arange(N_EXPERTS)

    def compact(has, *values):
        place = has[:, :, None] & ((jnp.cumsum(has, axis=1) - 1)[:, :, None] == lane[None, None, :])
        return [jnp.sum(has, axis=1)] + [jnp.sum(jnp.where(place, v[:, :, None], 0), axis=1) for v in values]

    whole, rest = seg // CHUNK_ROWS, seg % CHUNK_ROWS
    k = jnp.arange(TOKEN_TILE // CHUNK_ROWS) * CHUNK_ROWS
    flat = lambda a: a.reshape(n_tiles, -1)
    pieces = [compact(flat(seg[:, :, None] >= k + CHUNK_ROWS), flat(src[:, :, None] + k), flat(dst[:, :, None] + k))]
    for size in CHUNK_SIZES[1:]:
        pieces.append(compact(rest == size, src + whole * CHUNK_ROWS, dst + whole * CHUNK_ROWS))
    counts, srcs, dsts = zip(*pieces)
    i32 = lambda a: a.astype(jnp.int32).reshape(-1)
    chunks = (i32(jnp.stack(counts, axis=1)), i32(jnp.stack(srcs, axis=1)), i32(jnp.stack(dsts, axis=1)))
    return chunks, i32(fill), i32(cap - tot), i32(base), i32(cap // ROW_TILE), i32(n_used)


def _for_each_chunk(n, fn):
    for size in SEG_SIZES:
        done = n & ~(2 * size - 1)

        @pl.when((n & size) != 0)
        def _():
            fn(done, size)


def _segment_copies(cnt_ref, src_ref, dst_ref, tile, make_copy, start):
    for si, size in enumerate(CHUNK_SIZES):
        first = (tile * len(CHUNK_SIZES) + si) * N_EXPERTS

        def one(p, carry, si=si, size=size, first=first):
            copy = make_copy(pl.multiple_of(src_ref[first + p], SEG_ALIGN),
                             pl.multiple_of(dst_ref[first + p], SEG_ALIGN), size)
            if start:
                copy.start(priority=si % 2)
            else:
                copy.wait()
            return carry
        lax.fori_loop(0, cnt_ref[tile * len(CHUNK_SIZES) + si], one, 0)


SORT_BLOCK = 512


def _slot_matrix(slots, block, gates):
    row = (lax.broadcasted_iota(jnp.int32, (SORT_BLOCK, TOKEN_TILE), 0) + block * SORT_BLOCK).astype(F32)
    m = jnp.zeros((SORT_BLOCK, TOKEN_TILE), F32)
    for k in range(TOP_K):
        m = jnp.where(row == slots[k:k + 1, :], 1.0 if gates is None else gates[k:k + 1, :], m)
    return m.astype(BF16)


def _dispatch_kernel(seg_ref, src_ref, dst_ref, fill_ref, gap_ref, nu_ref, slot_ref, xn_ref, xs_ref, sorted_ref,
                     zero_ref, sem, seg_sem):
    tile = pl.program_id(0)
    n_tiles = pl.num_programs(0)
    buf = tile % 2

    @pl.when(tile == 0)
    def _():
        zero_ref[...] = jnp.zeros_like(zero_ref)

        def fill_copy(row, size):
            return pltpu.make_async_copy(zero_ref.at[pl.ds(0, size)],
                                         xs_ref.at[pl.ds(pl.multiple_of(row, SEG_ALIGN), size)], sem)

        def gaps(act):
            def per_expert(e, carry):
                _for_each_chunk(gap_ref[e], lambda done, size: act(fill_copy(fill_ref[e] + done, size)))
                return carry
            lax.fori_loop(0, N_EXPERTS, per_expert, 0)

        def tail(act):
            n_tail = xs_ref.shape[0] // ROW_TILE - nu_ref[0]
            lax.fori_loop(0, n_tail, lambda i, c: (act(fill_copy((nu_ref[0] + i) * ROW_TILE, ROW_TILE)), c)[1], 0)

        gaps(lambda c: c.start())
        tail(lambda c: c.start())
        gaps(lambda c: c.wait())
        tail(lambda c: c.wait())

    slots = slot_ref[...]
    xn = xn_ref[...]
    for rb in range(TILE_CAP // SORT_BLOCK):
        sorted_ref[buf, rb * SORT_BLOCK:(rb + 1) * SORT_BLOCK, :] = _dot(
            _slot_matrix(slots, rb, None), xn).astype(BF16)

    def copier(half):
        return lambda src, dst, size: pltpu.make_async_copy(
            sorted_ref.at[half, pl.ds(src, size)], xs_ref.at[pl.ds(dst, size)], seg_sem.at[half])

    _segment_copies(seg_ref, src_ref, dst_ref, tile, copier(buf), start=True)

    @pl.when(tile > 0)
    def _():
        _segment_copies(seg_ref, src_ref, dst_ref, tile - 1, copier(1 - buf), start=False)

    @pl.when(tile == n_tiles - 1)
    def _():
        _segment_copies(seg_ref, src_ref, dst_ref, tile, copier(buf), start=False)


def _dispatch(tables, slot_t, xn, n_rows_total):
    t = xn.shape[0]
    n_tiles = t // TOKEN_TILE
    return pl.pallas_call(
        _dispatch_kernel,
        grid_spec=pltpu.PrefetchScalarGridSpec(
            num_scalar_prefetch=len(tables),
            grid=(n_tiles,),
            in_specs=[
                pl.BlockSpec((SUBLANES, TOKEN_TILE), lambda i, *_: (0, i)),
                pl.BlockSpec((TOKEN_TILE, D_MODEL), lambda i, *_: (i, 0)),
            ],
            out_specs=pl.BlockSpec(memory_space=pl.ANY),
            scratch_shapes=[
                pltpu.VMEM((2, TILE_CAP, D_MODEL), BF16),
                pltpu.VMEM((ROW_TILE, D_MODEL), BF16),
                pltpu.SemaphoreType.DMA(()),
                pltpu.SemaphoreType.DMA((2,)),
            ],
        ),
        out_shape=jax.ShapeDtypeStruct((n_rows_total, D_MODEL), BF16),
        compiler_params=pltpu.CompilerParams(
            dimension_semantics=("arbitrary",), vmem_limit_bytes=VMEM_LIMIT_BYTES),
        name="dispatch",
    )(*tables, slot_t, xn)


X_SLOTS = 3
W_FETCH_STEPS = (0, 1, 3)


def _experts_kernel(base_ref, nt_ref, nu_ref, bg_ref, bu_ref, bd_ref, wg_hbm, wu_hbm, wd_hbm, xs_ref, zs_ref,
                    w_f32, wg_bf, wu_bf, wd_bf, xbuf, hbuf, zbuf, w_sem, in_sem, out_sem):
    e = pl.program_id(0)
    n_e = pl.num_programs(0)
    n = nt_ref[e]
    base = base_ref[e]
    w_slot = e % 2
    has_next = e + 1 < n_e

    def w_copy(j, expert, slot):
        w_hbm = (wg_hbm, wu_hbm, wd_hbm)[j]
        return pltpu.make_async_copy(w_hbm.at[expert], w_f32.at[slot, j], w_sem.at[slot])

    def x_copy(i):
        rows = pl.ds(pl.multiple_of(base + i * ROW_TILE, ROW_TILE), ROW_TILE)
        return pltpu.make_async_copy(xs_ref.at[rows], xbuf.at[i % X_SLOTS], in_sem.at[i % X_SLOTS])

    def z_copy(row, slot):
        rows = pl.ds(pl.multiple_of(row, ROW_TILE), ROW_TILE)
        return pltpu.make_async_copy(zbuf.at[slot], zs_ref.at[rows], out_sem.at[slot])

    def hidden(i):
        x = xbuf[i % X_SLOTS]
        a = jnp.minimum(_dot(x, wg_bf[...]) + bg_ref[...], SWIGLU_LIMIT)
        u = jnp.clip(_dot(x, wu_bf[...]) + bu_ref[...], -SWIGLU_LIMIT, SWIGLU_LIMIT)
        hbuf[i % 2] = ((u + 1.0) * a * jax.nn.sigmoid(SWIGLU_ALPHA * a)).astype(BF16)

    def project_down(i, slot):
        @pl.when(i >= 2)
        def _():
            z_copy(base, slot).wait()
        zbuf[slot] = (_dot(hbuf[slot], wd_bf[...]) + bd_ref[...]).astype(BF16)

    @pl.when(e == 0)
    def _():
        for j in range(3):
            w_copy(j, 0, 0).start()

    for i in range(X_SLOTS):
        @pl.when(i < n)
        def _():
            x_copy(i).start(priority=1)

    def fetch_next_weights(j):
        @pl.when(has_next)
        def _():
            w_copy(j, e + 1, 1 - w_slot).start()

    fetch_next_weights(0)
    for j in range(3):
        w_copy(j, e, w_slot).wait()

    @pl.when(n > 0)
    def _():
        wg_bf[...] = w_f32[w_slot, 0].astype(BF16)
        wu_bf[...] = w_f32[w_slot, 1].astype(BF16)
        wd_bf[...] = w_f32[w_slot, 2].astype(BF16)
        x_copy(0).wait()
        hidden(0)

        def step(i, carry):
            x_copy(i + 1).wait()

            @pl.when(i + X_SLOTS < n)
            def _():
                x_copy(i + X_SLOTS).start(priority=1)

            for j in (1, 2):
                @pl.when(i == W_FETCH_STEPS[j])
                def _():
                    fetch_next_weights(j)

            project_down(i, i % 2)
            hidden(i + 1)
            z_copy(base + i * ROW_TILE, i % 2).start()
            return carry

        lax.fori_loop(0, n - 1, step, 0)
        project_down(n - 1, (n - 1) % 2)
        z_copy(base + (n - 1) * ROW_TILE, (n - 1) % 2).start()

        @pl.when(n >= 2)
        def _():
            z_copy(base, n % 2).wait()
        z_copy(base, (n - 1) % 2).wait()

    for j in (1, 2):
        @pl.when(jnp.maximum(n - 1, 0) <= W_FETCH_STEPS[j])
        def _():
            fetch_next_weights(j)

    @pl.when(e == pl.num_programs(0) - 1)
    def _():
        zbuf[0] = jnp.zeros((ROW_TILE, D_MODEL), BF16)
        n_tail = zs_ref.shape[0] // ROW_TILE - nu_ref[0]
        tail = lambda act: lax.fori_loop(
            0, n_tail, lambda i, c: (act(z_copy((nu_ref[0] + i) * ROW_TILE, 0)), c)[1], 0)
        tail(lambda c: c.start())
        tail(lambda c: c.wait())


def _experts(base, n_tiles, n_used, xs, w_gate, b_gate, w_up, b_up, w_down, b_down):
    b_spec = pl.BlockSpec((None, 1, D_MODEL), lambda e, *_: (e, 0, 0))
    any_spec = pl.BlockSpec(memory_space=pl.ANY)
    return pl.pallas_call(
        _experts_kernel,
        grid_spec=pltpu.PrefetchScalarGridSpec(
            num_scalar_prefetch=3,
            grid=(N_EXPERTS,),
            in_specs=[b_spec, b_spec, b_spec, any_spec, any_spec, any_spec, any_spec],
            out_specs=any_spec,
            scratch_shapes=[pltpu.VMEM((2, 3, D_MODEL, D_MODEL), F32)] + [
                pltpu.VMEM((D_MODEL, D_MODEL), BF16) for _ in range(3)] + [
                pltpu.VMEM((X_SLOTS, ROW_TILE, D_MODEL), BF16)] + [
                pltpu.VMEM((2, ROW_TILE, D_MODEL), BF16) for _ in range(2)] + [
                pltpu.SemaphoreType.DMA((2,)), pltpu.SemaphoreType.DMA((X_SLOTS,)), pltpu.SemaphoreType.DMA((2,))],
        ),
        out_shape=jax.ShapeDtypeStruct(xs.shape, BF16),
        compiler_params=pltpu.CompilerParams(
            dimension_semantics=("arbitrary",), vmem_limit_bytes=VMEM_LIMIT_BYTES),
        name="experts",
    )(base, n_tiles, n_used, b_gate[:, None, :], b_up[:, None, :], b_down[:, None, :], w_gate, w_up, w_down, xs)


def _combine_kernel(n_first, seg_ref, src_ref, dst_ref, route_ref, x1_ref, fw_ref, zs_ref, ya_ref, yb_ref,
                    sorted_ref, sem):
    tile = pl.program_id(0)
    n_tiles = pl.num_programs(0)
    buf = tile % 2

    def fetch(t, half):
        sorted_ref[half] = jnp.zeros((TILE_CAP, D_MODEL), BF16)
        _segment_copies(seg_ref, src_ref, dst_ref, t, copier(half), start=True)

    def copier(half):
        return lambda src, dst, size: pltpu.make_async_copy(
            zs_ref.at[pl.ds(dst, size)], sorted_ref.at[half, pl.ds(src, size)], sem.at[half])

    @pl.when(tile == 0)
    def _():
        fetch(tile, buf)

    @pl.when(tile + 1 < n_tiles)
    def _():
        fetch(tile + 1, 1 - buf)

    _segment_copies(seg_ref, src_ref, dst_ref, tile, copier(buf), start=False)
    route = route_ref[...]
    slots, gates = route[0:TOP_K, :], route[TOP_K:2 * TOP_K, :]
    moe = jnp.zeros((TOKEN_TILE, D_MODEL), F32)
    for cb in range(TILE_CAP // SORT_BLOCK):
        moe = moe + _dot_tn(_slot_matrix(slots, cb, gates), sorted_ref[buf, cb * SORT_BLOCK:(cb + 1) * SORT_BLOCK, :])
    y = _rmsnorm(x1_ref[...] + moe, fw_ref[...])

    @pl.when(tile < n_first)
    def _():
        ya_ref[...] = y

    @pl.when(tile >= n_first)
    def _():
        yb_ref[...] = y


def _combine(tables, route_t, x1, final_w, zs, rows_first):
    seg, src, dst = tables
    t = x1.shape[0]
    assert rows_first % TOKEN_TILE == 0 and 0 < rows_first < t and TILE_CAP % SORT_BLOCK == 0
    n_tiles = t // TOKEN_TILE
    n_first = rows_first // TOKEN_TILE
    return pl.pallas_call(
        functools.partial(_combine_kernel, n_first),
        grid_spec=pltpu.PrefetchScalarGridSpec(
            num_scalar_prefetch=3,
            grid=(n_tiles,),
            in_specs=[
                pl.BlockSpec((SUBLANES, TOKEN_TILE), lambda i, *_: (0, i)),
                pl.BlockSpec((TOKEN_TILE, D_MODEL), lambda i, *_: (i, 0)),
                pl.BlockSpec((1, D_MODEL), lambda i, *_: (0, 0)),
                pl.BlockSpec(memory_space=pl.ANY),
            ],
            out_specs=[
                pl.BlockSpec((TOKEN_TILE, D_MODEL), lambda i, *_: (jnp.minimum(i, n_first - 1), 0)),
                pl.BlockSpec((TOKEN_TILE, D_MODEL), lambda i, *_: (jnp.maximum(i - n_first, 0), 0)),
            ],
            scratch_shapes=[pltpu.VMEM((2, TILE_CAP, D_MODEL), BF16), pltpu.SemaphoreType.DMA((2,))],
        ),
        out_shape=[jax.ShapeDtypeStruct((rows_first, D_MODEL), F32),
                   jax.ShapeDtypeStruct((t - rows_first, D_MODEL), F32)],
        compiler_params=pltpu.CompilerParams(
            dimension_semantics=("arbitrary",), vmem_limit_bytes=VMEM_LIMIT_BYTES),
        name="combine",
    )(seg, src, dst, route_t, x1, final_w, zs)


def _moe_and_final_norm(x1, rows_first, norm_ffn_w, router_w, router_b, w_gate, b_gate, w_up, b_up, w_down, b_down,
                        final_w):
    t = x1.shape[0]
    assert t % TOKEN_TILE == 0
    n_tiles = t // TOKEN_TILE
    n_rows_total = TOP_K * t + n_tiles * N_EXPERTS * (SEG_ALIGN - 1) + N_EXPERTS * (ROW_TILE - 1)
    n_rows_total = (n_rows_total + ROW_TILE - 1) // ROW_TILE * ROW_TILE
    xn, meta, cnt = _router(x1, norm_ffn_w.reshape(1, D_MODEL), router_w, router_b)
    cnt = cnt[:, 0, :N_EXPERTS].astype(jnp.int32)
    chunks, fill, gap, base, n_row_tiles, n_used = _segment_tables(cnt)
    route_t = jnp.concatenate([meta[:, 2 * TOP_K:3 * TOP_K], meta[:, TOP_K:2 * TOP_K]], axis=1).T
    xs = _dispatch((*chunks, fill, gap, n_used), route_t, xn, n_rows_total)
    zs = _experts(base, n_row_tiles, n_used, xs, w_gate, b_gate, w_up, b_up, w_down, b_down)
    return _combine(chunks, route_t, x1, final_w.reshape(1, D_MODEL), zs, rows_first)


def kernel(x_prompt, x_sample, state_conv, state_hgrn, lb_logits, norm_mix_w, w_in, conv_w, conv_b, gnorm_w,
           w_out, norm_ffn_w, router_w, router_b, w_gate, b_gate, w_up, b_up, w_down, b_down, final_norm_w):
    assert norm_mix_w.shape[0] == 1 and lb_logits.shape[0] == 2, "single-layer step"
    b, l, _ = x_prompt.shape
    nb, ls, _ = x_sample.shape
    lbl = lb_logits.astype(F32)
    nw = norm_mix_w[0].reshape(1, D_MODEL)
    w_in_bf = w_in[0].astype(BF16)
    w_out_bf = w_out[0].astype(BF16)
    cw, cb = conv_w[0], conv_b[0].reshape(1, D_CONV)
    gnw = gnorm_w[0].reshape(1, D_HGRN)
    rows_p, rows_s = b * l, nb * ls
    x1, conv_p, hgrn_p = _mix_prompt(x_prompt, lbl, nw, w_in_bf, cw, cb, gnw, w_out_bf, rows_p + rows_s)
    x1, conv_s, hgrn_s = _mix_sample(x_sample, state_conv[0], state_hgrn[0], lbl, nw, w_in_bf, cw, cb, gnw,
                                     w_out_bf, x1, rows_p)
    y_p, y_s = _moe_and_final_norm(x1, rows_p, norm_ffn_w[0], router_w[0], router_b[0], w_gate[0], b_gate[0],
                                   w_up[0], b_up[0], w_down[0], b_down[0], final_norm_w)
    conv_p = conv_p[:, SUBLANES - 2:, :]
    conv_s = conv_s[:, ls - 2:, :]
    return (y_p.reshape(b, l, D_MODEL), y_s.reshape(nb, ls, D_MODEL), conv_p[None], hgrn_p[None], conv_s[None],
            hgrn_s[None])
```

```python
import functools

import numpy as np
import jax
import jax.numpy as jnp
from jax import lax
from jax.experimental import pallas as pl
from jax.experimental.pallas import tpu as pltpu

F32 = jnp.float32
BF16 = jnp.bfloat16

D_MODEL = 1024
D_CONV = 512
D_HGRN = 512
N_HEADS = 4
D_HEAD = 128
D_PROJ = 3 * D_CONV + 4 * D_HGRN
N_EXPERTS = 32
TOP_K = 4
SWIGLU_LIMIT = 7.0
SWIGLU_ALPHA = 1.702
EPS = 1e-6
PAD_LOGIT = -1e30
PROMPT_CHUNK = 64

LANES = 128
SUBLANES = 8
BF16_ROWS = 16
VMEM_LIMIT_BYTES = 56 * 1024 * 1024

MIX_SLAB = 128
MIX_ROWS = 512
MIX_PART = 256
MIX_SEQS = 16
TOKEN_TILE = 512
ROUTER_PART = 256
SEG_ALIGN = BF16_ROWS
ROW_TILE = 512
SEG_SIZES = tuple(SEG_ALIGN << i for i in reversed(range(6)))
CHUNK_ROWS = 128
CHUNK_SIZES = tuple(range(CHUNK_ROWS, 0, -SEG_ALIGN))
TILE_CAP = TOP_K * TOKEN_TILE + N_EXPERTS * SEG_ALIGN
assert TILE_CAP // CHUNK_ROWS <= N_EXPERTS

NT_DIMS = (((1,), (1,)), ((), ()))
TN_DIMS = (((0,), (0,)), ((), ()))


def _dot(a, b):
    return jnp.dot(a, b, preferred_element_type=F32)


def _dot_nt(a, b):
    return lax.dot_general(a, b, NT_DIMS, preferred_element_type=F32)


def _dot_tn(a, b):
    return lax.dot_general(a, b, TN_DIMS, preferred_element_type=F32)


def _split2(x):
    hi = x.astype(BF16)
    lo = (x - hi.astype(F32)).astype(BF16)
    return hi, lo


def _rmsnorm(x, w):
    return x * lax.rsqrt(jnp.mean(x * x, axis=-1, keepdims=True) + EPS) * w


def _level_exponent_matrix(n, chunk, h):
    x = np.zeros((n, n), np.float32)
    for t in range(n):
        base = t - t % (2 * h)
        m = base + h - 1
        if t % (2 * h) >= h:
            x[t, m + 1:t + 1] = 1.0
        else:
            x[t, t + 1:m + 1] = 1.0
    return x


def _exponent_matrices(n, chunk, levels):
    t = np.arange(n)
    same_chunk = (t[:, None] // chunk) == (t[None, :] // chunk)
    tri = (same_chunk & (t[None, :] <= t[:, None])).astype(np.float32)
    suf = (same_chunk & (t[None, :] > t[:, None])).astype(np.float32)
    cmat = np.concatenate([tri, suf] + [_level_exponent_matrix(n, chunk, h) for h in levels], axis=0)
    return jnp.asarray(cmat, BF16)


def _level_masks(n, levels):
    t = np.arange(n)
    masks = []
    for h in levels:
        blk = (t[:, None] // (2 * h)) == (t[None, :] // (2 * h))
        masks.append((blk & ((t[:, None] % (2 * h)) >= h) & ((t[None, :] % (2 * h)) < h)).astype(np.float32))
    masks.append(np.eye(n, dtype=np.float32))
    return jnp.asarray(np.stack(masks), F32)


def _vpu_level_exponent(a, h):
    n = a.shape[0]
    pieces = []
    for j in range(n // (2 * h)):
        b = j * 2 * h
        ref = a[b + h - 1:b + h, :]
        pieces.append(ref - a[b:b + h, :])
        pieces.append(a[b + h:b + 2 * h, :] - ref)
    return jnp.concatenate(pieces, axis=0)


def _chunk_prefix_sums(x, chunk):
    n, width = x.shape
    x3 = x.reshape(n // SUBLANES, SUBLANES, width)
    pos = lax.broadcasted_iota(jnp.int32, (1, SUBLANES, 1), 1)
    d = 1
    while d < SUBLANES:
        x3 = x3 + jnp.where(pos >= d, pltpu.roll(x3, d, 1), 0.0)
        d *= 2
    x = x3.reshape(n, width)
    pieces = []
    for b in range(0, n, SUBLANES):
        blk = x[b:b + SUBLANES, :]
        if b % chunk:
            blk = blk + pieces[-1][SUBLANES - 1:SUBLANES, :]
        pieces.append(blk)
    return jnp.concatenate(pieces, axis=0)


def _small_level_exponent(a, logf, h):
    n = a.shape[0]
    pos = lax.broadcasted_iota(jnp.int32, (SUBLANES, 1), 0)
    if h == 1:
        odd = lax.broadcasted_iota(jnp.int32, (n, 1), 0) % 2 == 1
        return jnp.where(odd, logf, 0.0)
    pieces = []
    for b in range(0, n, SUBLANES):
        blk = a[b:b + SUBLANES, :]
        ref = blk[h - 1:h, :]
        for first in range(2 * h, SUBLANES, 2 * h):
            ref = jnp.where(pos >= first, blk[first + h - 1:first + h, :], ref)
        pieces.append(jnp.where(pos % (2 * h) >= h, blk - ref, ref - blk))
    return jnp.concatenate(pieces, axis=0)


def _forget_lower_bound(lbl):
    m = jnp.max(lbl, axis=0, keepdims=True)
    e = jnp.exp(lbl - m)
    return e[0:1, :] / jnp.sum(e, axis=0, keepdims=True)


PROJ_BLOCK = 512
N_PROJ_BLOCKS = D_PROJ // PROJ_BLOCK


class _ColumnBlocks:
    def __init__(self, h, w_in_ref):
        self.h, self.w, self.done = h, w_in_ref, []

    def next_block(self):
        c = len(self.done)
        if c < N_PROJ_BLOCKS:
            self.done.append(_dot(self.h, self.w[:, c * PROJ_BLOCK:(c + 1) * PROJ_BLOCK]))

    def block(self, c):
        while len(self.done) <= c:
            self.next_block()
        return self.done[c]


def _gates(ff, lb):
    e = jnp.exp(-jnp.abs(ff))
    r = 1.0 / (1.0 + e)
    pos = ff >= 0
    sig = jnp.where(pos, r, e * r)
    sig_neg = jnp.where(pos, e * r, r)
    logf = jnp.log(lb + (1.0 - lb) * sig)
    kk = (1.0 - lb) * sig_neg
    return logf, kk


def _intra_scores_times_v(q, kk, vi, exps, masks_ref):
    n = q.shape[0]
    sc = [jnp.zeros((n, n), F32) for _ in range(N_HEADS)]
    for l, ex in enumerate(exps):
        if ex is None:
            qh, kh = q.astype(BF16), kk.astype(BF16)
        else:
            w = jnp.exp(ex)
            qh, kh = (q * w).astype(BF16), (kk * w).astype(BF16)
        mask = masks_ref[l]
        for hd in range(N_HEADS):
            hs = slice(hd * D_HEAD, (hd + 1) * D_HEAD)
            sc[hd] = sc[hd] + mask * _dot_nt(qh[:, hs], kh[:, hs])
    vb = vi.astype(BF16)
    return [_dot(sc[hd].astype(BF16), vb[:, hd * D_HEAD:(hd + 1) * D_HEAD]) for hd in range(N_HEADS)]


def _head_out(o, g, gnw):
    parts = []
    for hd in range(N_HEADS):
        oh = o[:, hd * D_HEAD:(hd + 1) * D_HEAD]
        parts.append(oh * lax.rsqrt(jnp.mean(oh * oh, axis=-1, keepdims=True) + EPS))
    on = jnp.concatenate(parts, axis=1)
    return on * gnw * (g * jax.nn.sigmoid(g))


def _pad_rows_bf16(a):
    return jnp.concatenate([a, jnp.zeros_like(a)], axis=0).astype(BF16)


SAMPLE_LEVELS = (4, 2, 1)
PROMPT_WIDE_LEVELS = (32, 16, 8)
PROMPT_LEVELS = PROMPT_WIDE_LEVELS + SAMPLE_LEVELS
N_MIX_PROMPT_INPUTS = 9


def _mix_prompt_kernel(n_r, n_steps, *refs):
    step = pl.program_id(0)

    @pl.when(step < n_steps)
    def _():
        _mix_prompt_step(step % n_r, n_r, *refs)

    @pl.when(step >= n_steps)
    def _():
        x1_ref = refs[N_MIX_PROMPT_INPUTS]
        x1_ref[...] = jnp.zeros_like(x1_ref)


def _mix_prompt_step(r, n_r, x_ref, lbl_ref, nw_ref, win_ref, cw_ref, cb_ref, gnw_ref, wout_ref, masks_ref,
                     x1_ref, nconv_ref, nstate_ref, st_ref, tail_ref):
    rows = x_ref.shape[0]

    @pl.when(r == 0)
    def _():
        st_ref[...] = jnp.zeros_like(st_ref)
        tail_ref[...] = jnp.zeros_like(tail_ref)

    lb = _forget_lower_bound(lbl_ref[...])
    cw = cw_ref[...]
    n_chunks = MIX_SLAB // PROMPT_CHUNK
    rid = lax.broadcasted_iota(jnp.int32, (MIX_PART, 1), 0)
    t0 = tail_ref[SUBLANES - 2:SUBLANES - 1, :]
    t1 = tail_ref[SUBLANES - 1:SUBLANES, :]
    states = [st_ref[hd] for hd in range(N_HEADS)]

    n_parts = rows // MIX_PART
    part_rows = lambda p: slice(p * MIX_PART, (p + 1) * MIX_PART)
    project = lambda p: _ColumnBlocks(_rmsnorm(x_ref[part_rows(p), :], nw_ref[...]).astype(BF16), win_ref)
    nxt = project(0)
    for p in range(n_parts):
        pr = part_rows(p)
        x = x_ref[pr, :]
        bg, cg, vc, q, ff, vi, g = (nxt.block(c) for c in range(N_PROJ_BLOCKS))
        if p + 1 < n_parts:
            nxt = project(p + 1)
            nxt.block(N_PROJ_BLOCKS - 1)

        u = cg * vc
        u1 = jnp.where(rid == 0, t1, pltpu.roll(u, 1, 0))
        u2 = jnp.where(rid == 0, t0, jnp.where(rid == 1, t1, pltpu.roll(u, 2, 0)))
        y_conv = bg * (cw[0:1, :] * u2 + cw[1:2, :] * u1 + cw[2:3, :] * u + cb_ref[...])
        u_last = u[MIX_PART - SUBLANES:MIX_PART, :]
        t0, t1 = u_last[SUBLANES - 2:SUBLANES - 1, :], u_last[SUBLANES - 1:SUBLANES, :]

        logf, kk = _gates(ff, lb)
        o_parts = []
        for s in range(MIX_PART // MIX_SLAB):
            sl = slice(s * MIX_SLAB, (s + 1) * MIX_SLAB)
            qs, ks, vs = q[sl], kk[sl], vi[sl]
            a_pre = _chunk_prefix_sums(logf[sl], PROMPT_CHUNK)
            a_suf = jnp.concatenate(
                [a_pre[(c + 1) * PROMPT_CHUNK - 1:(c + 1) * PROMPT_CHUNK, :]
                 - a_pre[c * PROMPT_CHUNK:(c + 1) * PROMPT_CHUNK, :] for c in range(n_chunks)], axis=0)
            exps = [_vpu_level_exponent(a_pre, h) for h in PROMPT_WIDE_LEVELS]
            exps += [_small_level_exponent(a_pre, logf[sl], h) for h in SAMPLE_LEVELS]
            exps.append(None)
            intra = _intra_scores_times_v(qs, ks, vs, exps, masks_ref)
            ea = jnp.exp(a_pre)
            qa = (qs * ea).astype(BF16)
            kb = (ks * jnp.exp(a_suf)).astype(BF16)
            vb = vs.astype(BF16)
            for c in range(n_chunks):
                cr = slice(c * PROMPT_CHUNK, (c + 1) * PROMPT_CHUNK)
                last = (c + 1) * PROMPT_CHUNK - 1
                heads = []
                for hd in range(N_HEADS):
                    hs = slice(hd * D_HEAD, (hd + 1) * D_HEAD)
                    heads.append(_dot_nt(qa[cr, hs], states[hd].astype(BF16)) + intra[hd][cr, :])
                    states[hd] = states[hd] * ea[last:last + 1, hs] + _dot_tn(vb[cr, hs], kb[cr, hs])
                o_parts.append(jnp.concatenate(heads, axis=1))

        o = _head_out(jnp.concatenate(o_parts, axis=0), g, gnw_ref[...])
        mix_in = jnp.concatenate([y_conv, o], axis=1).astype(BF16)
        x1_ref[pr, :] = x + _dot(mix_in, wout_ref[...])

    tail_ref[...] = u_last
    for hd in range(N_HEADS):
        st_ref[hd] = states[hd]

    @pl.when(r == n_r - 1)
    def _():
        nconv_ref[...] = u_last
        for hd in range(N_HEADS):
            nstate_ref[hd] = states[hd].T


def _mix_prompt(x, lbl, nw, w_in, cw, cb, gnw, w_out, total_rows):
    b, l, _ = x.shape
    assert l % MIX_ROWS == 0 and MIX_ROWS % MIX_PART == 0 and MIX_PART % MIX_SLAB == 0 and l % PROMPT_CHUNK == 0
    assert total_rows % MIX_ROWS == 0
    n_r = l // MIX_ROWS
    n_steps = b * n_r
    seq = lambda s: jnp.minimum(s, n_steps - 1) // n_r
    masks = _level_masks(MIX_SLAB, PROMPT_LEVELS)
    const = lambda shape: pl.BlockSpec(shape, lambda s: (0,) * len(shape))
    return pl.pallas_call(
        functools.partial(_mix_prompt_kernel, n_r, n_steps),
        grid=(total_rows // MIX_ROWS,),
        in_specs=[
            pl.BlockSpec((None, MIX_ROWS, D_MODEL), lambda s: (seq(s), jnp.minimum(s, n_steps - 1) % n_r, 0)),
            const((2, D_HGRN)), const((1, D_MODEL)), const((D_MODEL, D_PROJ)), const((3, D_CONV)),
            const((1, D_CONV)), const((1, D_HGRN)), const((D_MODEL, D_MODEL)), const(masks.shape),
        ],
        out_specs=[
            pl.BlockSpec((MIX_ROWS, D_MODEL), lambda s: (s, 0)),
            pl.BlockSpec((None, SUBLANES, D_CONV), lambda s: (seq(s), 0, 0)),
            pl.BlockSpec((None, N_HEADS, D_HEAD, D_HEAD), lambda s: (seq(s), 0, 0, 0)),
        ],
        out_shape=[
            jax.ShapeDtypeStruct((total_rows, D_MODEL), F32),
            jax.ShapeDtypeStruct((b, SUBLANES, D_CONV), F32),
            jax.ShapeDtypeStruct((b, N_HEADS, D_HEAD, D_HEAD), F32),
        ],
        scratch_shapes=[
            pltpu.VMEM((N_HEADS, D_HEAD, D_HEAD), F32),
            pltpu.VMEM((SUBLANES, D_CONV), F32),
        ],
        compiler_params=pltpu.CompilerParams(
            dimension_semantics=("arbitrary",), vmem_limit_bytes=VMEM_LIMIT_BYTES),
        name="mix_prompt",
    )(x, lbl, nw, w_in, cw, cb, gnw, w_out, masks)


def _mix_sample_kernel(x_ref, cs_ref, hs_ref, lbl_ref, nw_ref, win_ref, cw_ref, cb_ref, gnw_ref, wout_ref,
                       cmat_ref, masks_ref, x1_in_ref, x1_ref, nconv_ref, nstate_ref, o_ref):
    del x1_in_ref
    nseq, length, _ = nconv_ref.shape
    rows = nseq * length
    x = x_ref[...]
    proj = _ColumnBlocks(_rmsnorm(x, nw_ref[...]).astype(BF16), win_ref)
    bg, cg, vc, q, ff, vi, g = (proj.block(c) for c in range(N_PROJ_BLOCKS))
    lb = _forget_lower_bound(lbl_ref[...])

    u = cg * vc
    cs = cs_ref[...]
    expand = lambda a: jnp.broadcast_to(a, (nseq, length, D_CONV)).reshape(rows, D_CONV)
    t0 = expand(cs[:, 0:1, :])
    t1 = expand(cs[:, 1:2, :])
    pos = lax.broadcasted_iota(jnp.int32, (rows, 1), 0) % length
    u1 = jnp.where(pos == 0, t1, pltpu.roll(u, 1, 0))
    u2 = jnp.where(pos == 0, t0, jnp.where(pos == 1, t1, pltpu.roll(u, 2, 0)))
    cw = cw_ref[...]
    y_conv = bg * (cw[0:1, :] * u2 + cw[1:2, :] * u1 + cw[2:3, :] * u + cb_ref[...])
    nconv_ref[...] = u.reshape(nseq, length, D_CONV)

    logf, kk = _gates(ff, lb)
    lf_hi, lf_lo = _split2(logf)
    ex = _dot(cmat_ref[...], lf_hi) + _dot(cmat_ref[...], lf_lo)
    a_pre = ex[0:rows]
    a_suf = ex[rows:2 * rows]
    exps = [ex[(2 + i) * rows:(3 + i) * rows] for i in range(len(SAMPLE_LEVELS))] + [None]
    intra = _intra_scores_times_v(q, kk, vi, exps, masks_ref)
    ea = jnp.exp(a_pre)
    qa = q * ea
    kb = kk * jnp.exp(a_suf)
    for s in range(nseq):
        cr = slice(s * length, (s + 1) * length)
        last = (s + 1) * length - 1
        for hd in range(N_HEADS):
            hs = slice(hd * D_HEAD, (hd + 1) * D_HEAD)
            st = hs_ref[s, hd].T
            inter = _dot_nt(_pad_rows_bf16(qa[cr, hs]), st.astype(BF16))[0:length, :]
            o_ref[cr, hs] = inter + intra[hd][cr, :]
            st_new = st * ea[last:last + 1, hs] + _dot_tn(_pad_rows_bf16(vi[cr, hs]), _pad_rows_bf16(kb[cr, hs]))
            nstate_ref[s, hd] = st_new.T

    o = _head_out(o_ref[...], g, gnw_ref[...])
    mix_in = jnp.concatenate([y_conv, o], axis=1).astype(BF16)
    x1_ref[...] = x + _dot(mix_in, wout_ref[...])


def _mix_sample(x, conv_state, hgrn_state, lbl, nw, w_in, cw, cb, gnw, w_out, x1_flat, row_offset):
    nb, length, _ = x.shape
    rows = MIX_SEQS * length
    assert nb % MIX_SEQS == 0 and length == SUBLANES and row_offset % rows == 0
    first_block = row_offset // rows
    cmat, masks = _exponent_matrices(rows, length, SAMPLE_LEVELS), _level_masks(rows, SAMPLE_LEVELS)
    const = lambda shape: pl.BlockSpec(shape, lambda i: (0,) * len(shape))
    return pl.pallas_call(
        _mix_sample_kernel,
        grid=(nb // MIX_SEQS,),
        in_specs=[
            pl.BlockSpec((rows, D_MODEL), lambda i: (i, 0)),
            pl.BlockSpec((MIX_SEQS, 2, D_CONV), lambda i: (i, 0, 0)),
            pl.BlockSpec((MIX_SEQS, N_HEADS, D_HEAD, D_HEAD), lambda i: (i, 0, 0, 0)),
            const((2, D_HGRN)), const((1, D_MODEL)), const((D_MODEL, D_PROJ)), const((3, D_CONV)),
            const((1, D_CONV)), const((1, D_HGRN)), const((D_MODEL, D_MODEL)), const(cmat.shape),
            const(masks.shape), pl.BlockSpec(memory_space=pl.ANY),
        ],
        out_specs=[
            pl.BlockSpec((rows, D_MODEL), lambda i: (first_block + i, 0)),
            pl.BlockSpec((MIX_SEQS, length, D_CONV), lambda i: (i, 0, 0)),
            pl.BlockSpec((MIX_SEQS, N_HEADS, D_HEAD, D_HEAD), lambda i: (i, 0, 0, 0)),
        ],
        out_shape=[
            jax.ShapeDtypeStruct(x1_flat.shape, F32),
            jax.ShapeDtypeStruct((nb, length, D_CONV), F32),
            jax.ShapeDtypeStruct((nb, N_HEADS, D_HEAD, D_HEAD), F32),
        ],
        scratch_shapes=[pltpu.VMEM((rows, D_HGRN), F32)],
        input_output_aliases={12: 0},
        compiler_params=pltpu.CompilerParams(
            dimension_semantics=("arbitrary",), vmem_limit_bytes=VMEM_LIMIT_BYTES),
        name="mix_sample",
    )(x.reshape(nb * length, D_MODEL), conv_state, hgrn_state, lbl, nw, w_in, cw, cb, gnw, w_out, cmat, masks,
      x1_flat)


def _router_kernel(x1_ref, nw_ref, rw_hi_ref, rw_lo_ref, rb_ref, ltri_ref, utri_ref, xn_ref, meta_ref, cnt_ref):
    n = ltri_ref.shape[0]
    parts = [slice(p * n, (p + 1) * n) for p in range(x1_ref.shape[0] // n)]
    lane = lax.broadcasted_iota(jnp.int32, (n, LANES), 1).astype(F32)
    logits = []
    for pr in parts:
        xn = _rmsnorm(x1_ref[pr, :], nw_ref[...])
        xn_ref[pr, :] = xn.astype(BF16)
        x_hi, x_lo = _split2(xn)
        logits.append(_dot(x_hi, rw_hi_ref[...]) + _dot(x_lo, rw_hi_ref[...]) + _dot(x_hi, rw_lo_ref[...])
                      + rb_ref[...])
    picks = []
    for work in logits:
        vals, ids = [], []
        for _ in range(TOP_K):
            m = jnp.max(work, axis=-1, keepdims=True)
            i = jnp.min(jnp.where(work == m, lane, float(LANES)), axis=-1, keepdims=True)
            vals.append(m)
            ids.append(i)
            work = jnp.where(lane == i, -jnp.inf, work)
        es = [jnp.exp(v - vals[0]) for v in vals]
        den = es[0] + es[1] + es[2] + es[3]
        onehots = [(lane == i) for i in ids]
        multi = jnp.zeros((n, LANES), F32)
        for oh in onehots:
            multi = multi + oh.astype(F32)
        picks.append((ids, [e / den for e in es], onehots, multi))

    befores, counts = [], jnp.zeros((1, LANES), F32)
    for _, _, _, multi in picks:
        befores.append(counts + _dot(ltri_ref[...], multi.astype(BF16)))
        counts = counts + jnp.sum(multi, axis=0, keepdims=True)
    seg = jnp.ceil(counts * (1.0 / SEG_ALIGN)) * SEG_ALIGN
    seg_rows = jnp.broadcast_to(seg, (BF16_ROWS, LANES)).astype(BF16)
    seg_off = _dot(seg_rows, utri_ref[...])[0:1, :]
    for pr, (ids, gates, onehots, _), before in zip(parts, picks, befores):
        slot_of = seg_off + before
        meta = jnp.zeros((n, LANES), F32)
        for k in range(TOP_K):
            slot = jnp.sum(jnp.where(onehots[k], slot_of, 0.0), axis=-1, keepdims=True)
            meta = jnp.where(lane == k, ids[k], meta)
            meta = jnp.where(lane == TOP_K + k, gates[k], meta)
            meta = jnp.where(lane == 2 * TOP_K + k, slot, meta)
        meta_ref[pr, :] = meta
    cnt_ref[...] = jnp.broadcast_to(counts, (SUBLANES, LANES))


def _router(x1, nw, router_w, router_b):
    t = x1.shape[0]
    n_tiles = t // TOKEN_TILE
    rw = jnp.zeros((D_MODEL, LANES), F32).at[:, :N_EXPERTS].set(router_w)
    rw_hi, rw_lo = _split2(rw)
    rb = jnp.full((1, LANES), PAD_LOGIT, F32).at[0, :N_EXPERTS].set(router_b)
    idx = np.arange(ROUTER_PART)
    ltri = jnp.asarray(idx[None, :] < idx[:, None], BF16)
    lid = np.arange(LANES)
    utri = jnp.asarray(lid[:, None] < lid[None, :], BF16)
    const = lambda shape: pl.BlockSpec(shape, lambda i: (0,) * len(shape))
    return pl.pallas_call(
        _router_kernel,
        grid=(n_tiles,),
        in_specs=[
            pl.BlockSpec((TOKEN_TILE, D_MODEL), lambda i: (i, 0)),
            const((1, D_MODEL)), const((D_MODEL, LANES)), const((D_MODEL, LANES)), const((1, LANES)),
            const((ROUTER_PART, ROUTER_PART)), const((LANES, LANES)),
        ],
        out_specs=[
            pl.BlockSpec((TOKEN_TILE, D_MODEL), lambda i: (i, 0)),
            pl.BlockSpec((TOKEN_TILE, LANES), lambda i: (i, 0)),
            pl.BlockSpec((None, SUBLANES, LANES), lambda i: (i, 0, 0)),
        ],
        out_shape=[
            jax.ShapeDtypeStruct((t, D_MODEL), BF16),
            jax.ShapeDtypeStruct((t, LANES), F32),
            jax.ShapeDtypeStruct((n_tiles, SUBLANES, LANES), F32),
        ],
        compiler_params=pltpu.CompilerParams(
            dimension_semantics=("arbitrary",), vmem_limit_bytes=VMEM_LIMIT_BYTES),
        name="router",
    )(x1, nw, rw_hi, rw_lo, rb, ltri, utri)


def _segment_tables(cnt):
    seg = (cnt + SEG_ALIGN - 1) // SEG_ALIGN * SEG_ALIGN
    src = jnp.cumsum(seg, axis=1) - seg
    tot = jnp.sum(seg, axis=0)
    cap = (tot + ROW_TILE - 1) // ROW_TILE * ROW_TILE
    base = jnp.cumsum(cap) - cap
    dst = base[None, :] + jnp.cumsum(seg, axis=0) - seg
    n_used = jnp.sum(cap) // ROW_TILE
    fill = base + tot
    n_tiles = seg.shape[0]
    lane = jnp.arange(N_EXPERTS)

    def compact(has, *values):
        place = has[:, :, None] & ((jnp.cumsum(has, axis=1) - 1)[:, :, None] == lane[None, None, :])
        return [jnp.sum(has, axis=1)] + [jnp.sum(jnp.where(place, v[:, :, None], 0), axis=1) for v in values]

    whole, rest = seg // CHUNK_ROWS, seg % CHUNK_ROWS
    k = jnp.arange(TOKEN_TILE // CHUNK_ROWS) * CHUNK_ROWS
    flat = lambda a: a.reshape(n_tiles, -1)
    pieces = [compact(flat(seg[:, :, None] >= k + CHUNK_ROWS), flat(src[:, :, None] + k), flat(dst[:, :, None] + k))]
    for size in CHUNK_SIZES[1:]:
        pieces.append(compact(rest == size, src + whole * CHUNK_ROWS, dst + whole * CHUNK_ROWS))
    counts, srcs, dsts = zip(*pieces)
    i32 = lambda a: a.astype(jnp.int32).reshape(-1)
    chunks = (i32(jnp.stack(counts, axis=1)), i32(jnp.stack(srcs, axis=1)), i32(jnp.stack(dsts, axis=1)))
    return (chunks, i32(jnp.sum(seg, axis=1)), i32(fill), i32(cap - tot), i32(base), i32(cap // ROW_TILE),
            i32(n_used))


def _for_each_chunk(n, fn):
    for size in SEG_SIZES:
        done = n & ~(2 * size - 1)

        @pl.when((n & size) != 0)
        def _():
            fn(done, size)


def _segment_copies(cnt_ref, src_ref, dst_ref, tile, make_copy, start):
    for si, size in enumerate(CHUNK_SIZES):
        first = (tile * len(CHUNK_SIZES) + si) * N_EXPERTS

        def one(p, carry, si=si, size=size, first=first):
            copy = make_copy(pl.multiple_of(src_ref[first + p], SEG_ALIGN),
                             pl.multiple_of(dst_ref[first + p], SEG_ALIGN), size)
            if start:
                copy.start(priority=si % 2)
            else:
                copy.wait()
            return carry
        lax.fori_loop(0, cnt_ref[tile * len(CHUNK_SIZES) + si], one, 0)


SORT_BLOCK = 512


def _slot_matrix(slots, block, gates):
    row = (lax.broadcasted_iota(jnp.int32, (SORT_BLOCK, TOKEN_TILE), 0) + block * SORT_BLOCK).astype(F32)
    m = jnp.zeros((SORT_BLOCK, TOKEN_TILE), F32)
    for k in range(TOP_K):
        m = jnp.where(row == slots[k:k + 1, :], 1.0 if gates is None else gates[k:k + 1, :], m)
    return m.astype(BF16)


def _dispatch_kernel(seg_ref, src_ref, dst_ref, fill_ref, gap_ref, nu_ref, slot_ref, xn_ref, xs_ref, sorted_ref,
                     zero_ref, sem, seg_sem):
    tile = pl.program_id(0)
    n_tiles = pl.num_programs(0)
    buf = tile % 2

    @pl.when(tile == 0)
    def _():
        zero_ref[...] = jnp.zeros_like(zero_ref)

        def fill_copy(row, size):
            return pltpu.make_async_copy(zero_ref.at[pl.ds(0, size)],
                                         xs_ref.at[pl.ds(pl.multiple_of(row, SEG_ALIGN), size)], sem)

        def gaps(act):
            def per_expert(e, carry):
                _for_each_chunk(gap_ref[e], lambda done, size: act(fill_copy(fill_ref[e] + done, size)))
                return carry
            lax.fori_loop(0, N_EXPERTS, per_expert, 0)

        def tail(act):
            n_tail = xs_ref.shape[0] // ROW_TILE - nu_ref[0]
            lax.fori_loop(0, n_tail, lambda i, c: (act(fill_copy((nu_ref[0] + i) * ROW_TILE, ROW_TILE)), c)[1], 0)

        gaps(lambda c: c.start())
        tail(lambda c: c.start())
        gaps(lambda c: c.wait())
        tail(lambda c: c.wait())

    slots = slot_ref[...]
    xn = xn_ref[...]
    for rb in range(TILE_CAP // SORT_BLOCK):
        sorted_ref[buf, rb * SORT_BLOCK:(rb + 1) * SORT_BLOCK, :] = _dot(
            _slot_matrix(slots, rb, None), xn).astype(BF16)

    def copier(half):
        return lambda src, dst, size: pltpu.make_async_copy(
            sorted_ref.at[half, pl.ds(src, size)], xs_ref.at[pl.ds(dst, size)], seg_sem.at[half])

    _segment_copies(seg_ref, src_ref, dst_ref, tile, copier(buf), start=True)

    @pl.when(tile > 0)
    def _():
        _segment_copies(seg_ref, src_ref, dst_ref, tile - 1, copier(1 - buf), start=False)

    @pl.when(tile == n_tiles - 1)
    def _():
        _segment_copies(seg_ref, src_ref, dst_ref, tile, copier(buf), start=False)


def _dispatch(tables, slot_t, xn, n_rows_total):
    t = xn.shape[0]
    n_tiles = t // TOKEN_TILE
    return pl.pallas_call(
        _dispatch_kernel,
        grid_spec=pltpu.PrefetchScalarGridSpec(
            num_scalar_prefetch=len(tables),
            grid=(n_tiles,),
            in_specs=[
                pl.BlockSpec((SUBLANES, TOKEN_TILE), lambda i, *_: (0, i)),
                pl.BlockSpec((TOKEN_TILE, D_MODEL), lambda i, *_: (i, 0)),
            ],
            out_specs=pl.BlockSpec(memory_space=pl.ANY),
            scratch_shapes=[
                pltpu.VMEM((2, TILE_CAP, D_MODEL), BF16),
                pltpu.VMEM((ROW_TILE, D_MODEL), BF16),
                pltpu.SemaphoreType.DMA(()),
                pltpu.SemaphoreType.DMA((2,)),
            ],
        ),
        out_shape=jax.ShapeDtypeStruct((n_rows_total, D_MODEL), BF16),
        compiler_params=pltpu.CompilerParams(
            dimension_semantics=("arbitrary",), vmem_limit_bytes=VMEM_LIMIT_BYTES),
        name="dispatch",
    )(*tables, slot_t, xn)


X_SLOTS = 3
W_FETCH_STEPS = (0, 1, 3)


def _experts_kernel(base_ref, nt_ref, nu_ref, bg_ref, bu_ref, bd_ref, wg_hbm, wu_hbm, wd_hbm, xs_ref, zs_ref,
                    w_f32, wg_bf, wu_bf, wd_bf, xbuf, hbuf, zbuf, w_sem, in_sem, out_sem):
    e = pl.program_id(0)
    n_e = pl.num_programs(0)
    n = nt_ref[e]
    base = base_ref[e]
    w_slot = e % 2
    has_next = e + 1 < n_e

    def w_copy(j, expert, slot):
        w_hbm = (wg_hbm, wu_hbm, wd_hbm)[j]
        return pltpu.make_async_copy(w_hbm.at[expert], w_f32.at[slot, j], w_sem.at[slot])

    def x_copy(i):
        rows = pl.ds(pl.multiple_of(base + i * ROW_TILE, ROW_TILE), ROW_TILE)
        return pltpu.make_async_copy(xs_ref.at[rows], xbuf.at[i % X_SLOTS], in_sem.at[i % X_SLOTS])

    def z_copy(row, slot):
        rows = pl.ds(pl.multiple_of(row, ROW_TILE), ROW_TILE)
        return pltpu.make_async_copy(zbuf.at[slot], zs_ref.at[rows], out_sem.at[slot])

    def hidden(i):
        x = xbuf[i % X_SLOTS]
        a = jnp.minimum(_dot(x, wg_bf[...]) + bg_ref[...], SWIGLU_LIMIT)
        u = jnp.clip(_dot(x, wu_bf[...]) + bu_ref[...], -SWIGLU_LIMIT, SWIGLU_LIMIT)
        hbuf[i % 2] = ((u + 1.0) * a * jax.nn.sigmoid(SWIGLU_ALPHA * a)).astype(BF16)

    def project_down(i, slot):
        @pl.when(i >= 2)
        def _():
            z_copy(base, slot).wait()
        zbuf[slot] = (_dot(hbuf[slot], wd_bf[...]) + bd_ref[...]).astype(BF16)

    @pl.when(e == 0)
    def _():
        for j in range(3):
            w_copy(j, 0, 0).start()

    for i in range(X_SLOTS):
        @pl.when(i < n)
        def _():
            x_copy(i).start(priority=1)

    def fetch_next_weights(j):
        @pl.when(has_next)
        def _():
            w_copy(j, e + 1, 1 - w_slot).start()

    fetch_next_weights(0)
    for j in range(3):
        w_copy(j, e, w_slot).wait()

    @pl.when(n > 0)
    def _():
        wg_bf[...] = w_f32[w_slot, 0].astype(BF16)
        wu_bf[...] = w_f32[w_slot, 1].astype(BF16)
        wd_bf[...] = w_f32[w_slot, 2].astype(BF16)
        x_copy(0).wait()
        hidden(0)

        def step(i, carry):
            x_copy(i + 1).wait()

            @pl.when(i + X_SLOTS < n)
            def _():
                x_copy(i + X_SLOTS).start(priority=1)

            for j in (1, 2):
                @pl.when(i == W_FETCH_STEPS[j])
                def _():
                    fetch_next_weights(j)

            project_down(i, i % 2)
            hidden(i + 1)
            z_copy(base + i * ROW_TILE, i % 2).start()
            return carry

        lax.fori_loop(0, n - 1, step, 0)
        project_down(n - 1, (n - 1) % 2)
        z_copy(base + (n - 1) * ROW_TILE, (n - 1) % 2).start()

        @pl.when(n >= 2)
        def _():
            z_copy(base, n % 2).wait()
        z_copy(base, (n - 1) % 2).wait()

    for j in (1, 2):
        @pl.when(jnp.maximum(n - 1, 0) <= W_FETCH_STEPS[j])
        def _():
            fetch_next_weights(j)

    @pl.when(e == pl.num_programs(0) - 1)
    def _():
        zbuf[0] = jnp.zeros((ROW_TILE, D_MODEL), BF16)
        n_tail = zs_ref.shape[0] // ROW_TILE - nu_ref[0]
        tail = lambda act: lax.fori_loop(
            0, n_tail, lambda i, c: (act(z_copy((nu_ref[0] + i) * ROW_TILE, 0)), c)[1], 0)
        tail(lambda c: c.start())
        tail(lambda c: c.wait())


def _experts(base, n_tiles, n_used, xs, w_gate, b_gate, w_up, b_up, w_down, b_down):
    b_spec = pl.BlockSpec((None, 1, D_MODEL), lambda e, *_: (e, 0, 0))
    any_spec = pl.BlockSpec(memory_space=pl.ANY)
    return pl.pallas_call(
        _experts_kernel,
        grid_spec=pltpu.PrefetchScalarGridSpec(
            num_scalar_prefetch=3,
            grid=(N_EXPERTS,),
            in_specs=[b_spec, b_spec, b_spec, any_spec, any_spec, any_spec, any_spec],
            out_specs=any_spec,
            scratch_shapes=[pltpu.VMEM((2, 3, D_MODEL, D_MODEL), F32)] + [
                pltpu.VMEM((D_MODEL, D_MODEL), BF16) for _ in range(3)] + [
                pltpu.VMEM((X_SLOTS, ROW_TILE, D_MODEL), BF16)] + [
                pltpu.VMEM((2, ROW_TILE, D_MODEL), BF16) for _ in range(2)] + [
                pltpu.SemaphoreType.DMA((2,)), pltpu.SemaphoreType.DMA((X_SLOTS,)), pltpu.SemaphoreType.DMA((2,))],
        ),
        out_shape=jax.ShapeDtypeStruct(xs.shape, BF16),
        compiler_params=pltpu.CompilerParams(
            dimension_semantics=("arbitrary",), vmem_limit_bytes=VMEM_LIMIT_BYTES),
        name="experts",
    )(base, n_tiles, n_used, b_gate[:, None, :], b_up[:, None, :], b_down[:, None, :], w_gate, w_up, w_down, xs)


ZERO_BLOCK = 256


def _combine_kernel(n_first, seg_ref, src_ref, dst_ref, rows_ref, route_ref, x1_ref, fw_ref, zs_ref, ya_ref, yb_ref,
                    sorted_ref, sem):
    tile = pl.program_id(0)
    n_tiles = pl.num_programs(0)
    buf = tile % 2

    def fetch(t, half):
        def zero(b, carry):
            rows = pl.ds(pl.multiple_of(b * ZERO_BLOCK, ZERO_BLOCK), ZERO_BLOCK)
            sorted_ref[half, rows, :] = jnp.zeros((ZERO_BLOCK, D_MODEL), BF16)
            return carry
        lax.fori_loop(rows_ref[t] // ZERO_BLOCK, TILE_CAP // ZERO_BLOCK, zero, 0)
        _segment_copies(seg_ref, src_ref, dst_ref, t, copier(half), start=True)

    def copier(half):
        return lambda src, dst, size: pltpu.make_async_copy(
            zs_ref.at[pl.ds(dst, size)], sorted_ref.at[half, pl.ds(src, size)], sem.at[half])

    @pl.when(tile == 0)
    def _():
        fetch(tile, buf)

    @pl.when(tile + 1 < n_tiles)
    def _():
        fetch(tile + 1, 1 - buf)

    _segment_copies(seg_ref, src_ref, dst_ref, tile, copier(buf), start=False)
    route = route_ref[...]
    slots, gates = route[0:TOP_K, :], route[TOP_K:2 * TOP_K, :]
    moe = jnp.zeros((TOKEN_TILE, D_MODEL), F32)
    for cb in range(TILE_CAP // SORT_BLOCK):
        moe = moe + _dot_tn(_slot_matrix(slots, cb, gates), sorted_ref[buf, cb * SORT_BLOCK:(cb + 1) * SORT_BLOCK, :])
    y = _rmsnorm(x1_ref[...] + moe, fw_ref[...])

    @pl.when(tile < n_first)
    def _():
        ya_ref[...] = y

    @pl.when(tile >= n_first)
    def _():
        yb_ref[...] = y


def _combine(tables, route_t, x1, final_w, zs, rows_first):
    t = x1.shape[0]
    assert rows_first % TOKEN_TILE == 0 and 0 < rows_first < t
    assert TILE_CAP % SORT_BLOCK == 0 and TILE_CAP % ZERO_BLOCK == 0
    n_tiles = t // TOKEN_TILE
    n_first = rows_first // TOKEN_TILE
    return pl.pallas_call(
        functools.partial(_combine_kernel, n_first),
        grid_spec=pltpu.PrefetchScalarGridSpec(
            num_scalar_prefetch=len(tables),
            grid=(n_tiles,),
            in_specs=[
                pl.BlockSpec((SUBLANES, TOKEN_TILE), lambda i, *_: (0, i)),
                pl.BlockSpec((TOKEN_TILE, D_MODEL), lambda i, *_: (i, 0)),
                pl.BlockSpec((1, D_MODEL), lambda i, *_: (0, 0)),
                pl.BlockSpec(memory_space=pl.ANY),
            ],
            out_specs=[
                pl.BlockSpec((TOKEN_TILE, D_MODEL), lambda i, *_: (jnp.minimum(i, n_first - 1), 0)),
                pl.BlockSpec((TOKEN_TILE, D_MODEL), lambda i, *_: (jnp.maximum(i - n_first, 0), 0)),
            ],
            scratch_shapes=[pltpu.VMEM((2, TILE_CAP, D_MODEL), BF16), pltpu.SemaphoreType.DMA((2,))],
        ),
        out_shape=[jax.ShapeDtypeStruct((rows_first, D_MODEL), F32),
                   jax.ShapeDtypeStruct((t - rows_first, D_MODEL), F32)],
        compiler_params=pltpu.CompilerParams(
            dimension_semantics=("arbitrary",), vmem_limit_bytes=VMEM_LIMIT_BYTES),
        name="combine",
    )(*tables, route_t, x1, final_w, zs)


def _moe_and_final_norm(x1, rows_first, norm_ffn_w, router_w, router_b, w_gate, b_gate, w_up, b_up, w_down, b_down,
                        final_w):
    t = x1.shape[0]
    assert t % TOKEN_TILE == 0
    n_tiles = t // TOKEN_TILE
    n_rows_total = TOP_K * t + n_tiles * N_EXPERTS * (SEG_ALIGN - 1) + N_EXPERTS * (ROW_TILE - 1)
    n_rows_total = (n_rows_total + ROW_TILE - 1) // ROW_TILE * ROW_TILE
    xn, meta, cnt = _router(x1, norm_ffn_w.reshape(1, D_MODEL), router_w, router_b)
    cnt = cnt[:, 0, :N_EXPERTS].astype(jnp.int32)
    chunks, tile_rows, fill, gap, base, n_row_tiles, n_used = _segment_tables(cnt)
    route_t = jnp.concatenate([meta[:, 2 * TOP_K:3 * TOP_K], meta[:, TOP_K:2 * TOP_K]], axis=1).T
    xs = _dispatch((*chunks, fill, gap, n_used), route_t, xn, n_rows_total)
    zs = _experts(base, n_row_tiles, n_used, xs, w_gate, b_gate, w_up, b_up, w_down, b_down)
    return _combine((*chunks, tile_rows), route_t, x1, final_w.reshape(1, D_MODEL), zs, rows_first)


def kernel(x_prompt, x_sample, state_conv, state_hgrn, lb_logits, norm_mix_w, w_in, conv_w, conv_b, gnorm_w,
           w_out, norm_ffn_w, router_w, router_b, w_gate, b_gate, w_up, b_up, w_down, b_down, final_norm_w):
    assert norm_mix_w.shape[0] == 1 and lb_logits.shape[0] == 2, "single-layer step"
    b, l, _ = x_prompt.shape
    nb, ls, _ = x_sample.shape
    lbl = lb_logits.astype(F32)
    nw = norm_mix_w[0].reshape(1, D_MODEL)
    w_in_bf = w_in[0].astype(BF16)
    w_out_bf = w_out[0].astype(BF16)
    cw, cb = conv_w[0], conv_b[0].reshape(1, D_CONV)
    gnw = gnorm_w[0].reshape(1, D_HGRN)
    rows_p, rows_s = b * l, nb * ls
    x1, conv_p, hgrn_p = _mix_prompt(x_prompt, lbl, nw, w_in_bf, cw, cb, gnw, w_out_bf, rows_p + rows_s)
    x1, conv_s, hgrn_s = _mix_sample(x_sample, state_conv[0], state_hgrn[0], lbl, nw, w_in_bf, cw, cb, gnw,
                                     w_out_bf, x1, rows_p)
    y_p, y_s = _moe_and_final_norm(x1, rows_p, norm_ffn_w[0], router_w[0], router_b[0], w_gate[0], b_gate[0],
                                   w_up[0], b_up[0], w_down[0], b_down[0], final_norm_w)
    conv_p = conv_p[:, SUBLANES - 2:, :]
    conv_s = conv_s[:, ls - 2:, :]
    return (y_p.reshape(b, l, D_MODEL), y_s.reshape(nb, ls, D_MODEL), conv_p[None], hgrn_p[None], conv_s[None],
            hgrn_s[None])
```

```python
import functools

import numpy as np
import jax
import jax.numpy as jnp
from jax import lax
from jax.experimental import pallas as pl
from jax.experimental.pallas import tpu as pltpu

F32 = jnp.float32
BF16 = jnp.bfloat16

D_MODEL = 1024
D_CONV = 512
D_HGRN = 512
N_HEADS = 4
D_HEAD = 128
D_PROJ = 3 * D_CONV + 4 * D_HGRN
N_EXPERTS = 32
TOP_K = 4
SWIGLU_LIMIT = 7.0
SWIGLU_ALPHA = 1.702
EPS = 1e-6
PAD_LOGIT = -1e30
PROMPT_CHUNK = 64

LANES = 128
SUBLANES = 8
BF16_ROWS = 16
VMEM_LIMIT_BYTES = 56 * 1024 * 1024

MIX_SLAB = 128
MIX_ROWS = 512
MIX_PART = 256
MIX_SEQS = 16
TOKEN_TILE = 512
ROUTER_PART = 256
SEG_ALIGN = BF16_ROWS
ROW_TILE = 512
SEG_SIZES = tuple(SEG_ALIGN << i for i in reversed(range(6)))
CHUNK_ROWS = 128
CHUNK_SIZES = tuple(range(CHUNK_ROWS, 0, -SEG_ALIGN))
TILE_CAP = TOP_K * TOKEN_TILE + N_EXPERTS * SEG_ALIGN
assert TILE_CAP // CHUNK_ROWS <= N_EXPERTS

NT_DIMS = (((1,), (1,)), ((), ()))
TN_DIMS = (((0,), (0,)), ((), ()))


def _dot(a, b):
    return jnp.dot(a, b, preferred_element_type=F32)


def _dot_nt(a, b):
    return lax.dot_general(a, b, NT_DIMS, preferred_element_type=F32)


def _dot_tn(a, b):
    return lax.dot_general(a, b, TN_DIMS, preferred_element_type=F32)


def _split2(x):
    hi = x.astype(BF16)
    lo = (x - hi.astype(F32)).astype(BF16)
    return hi, lo


def _rmsnorm(x, w):
    return x * lax.rsqrt(jnp.mean(x * x, axis=-1, keepdims=True) + EPS) * w


def _level_exponent_matrix(n, chunk, h):
    x = np.zeros((n, n), np.float32)
    for t in range(n):
        base = t - t % (2 * h)
        m = base + h - 1
        if t % (2 * h) >= h:
            x[t, m + 1:t + 1] = 1.0
        else:
            x[t, t + 1:m + 1] = 1.0
    return x


def _exponent_matrices(n, chunk, levels):
    t = np.arange(n)
    same_chunk = (t[:, None] // chunk) == (t[None, :] // chunk)
    tri = (same_chunk & (t[None, :] <= t[:, None])).astype(np.float32)
    suf = (same_chunk & (t[None, :] > t[:, None])).astype(np.float32)
    cmat = np.concatenate([tri, suf] + [_level_exponent_matrix(n, chunk, h) for h in levels], axis=0)
    return jnp.asarray(cmat, BF16)


def _level_masks(n, levels):
    t = np.arange(n)
    masks = []
    for h in levels:
        blk = (t[:, None] // (2 * h)) == (t[None, :] // (2 * h))
        masks.append((blk & ((t[:, None] % (2 * h)) >= h) & ((t[None, :] % (2 * h)) < h)).astype(np.float32))
    masks.append(np.eye(n, dtype=np.float32))
    return jnp.asarray(np.stack(masks), F32)


def _vpu_level_exponent(a, h):
    n = a.shape[0]
    pieces = []
    for j in range(n // (2 * h)):
        b = j * 2 * h
        ref = a[b + h - 1:b + h, :]
        pieces.append(ref - a[b:b + h, :])
        pieces.append(a[b + h:b + 2 * h, :] - ref)
    return jnp.concatenate(pieces, axis=0)


def _chunk_prefix_sums(x, chunk):
    n, width = x.shape
    x3 = x.reshape(n // SUBLANES, SUBLANES, width)
    pos = lax.broadcasted_iota(jnp.int32, (1, SUBLANES, 1), 1)
    d = 1
    while d < SUBLANES:
        x3 = x3 + jnp.where(pos >= d, pltpu.roll(x3, d, 1), 0.0)
        d *= 2
    x = x3.reshape(n, width)
    pieces = []
    for b in range(0, n, SUBLANES):
        blk = x[b:b + SUBLANES, :]
        if b % chunk:
            blk = blk + pieces[-1][SUBLANES - 1:SUBLANES, :]
        pieces.append(blk)
    return jnp.concatenate(pieces, axis=0)


def _small_level_exponent(a, logf, h):
    n = a.shape[0]
    pos = lax.broadcasted_iota(jnp.int32, (SUBLANES, 1), 0)
    if h == 1:
        odd = lax.broadcasted_iota(jnp.int32, (n, 1), 0) % 2 == 1
        return jnp.where(odd, logf, 0.0)
    pieces = []
    for b in range(0, n, SUBLANES):
        blk = a[b:b + SUBLANES, :]
        ref = blk[h - 1:h, :]
        for first in range(2 * h, SUBLANES, 2 * h):
            ref = jnp.where(pos >= first, blk[first + h - 1:first + h, :], ref)
        pieces.append(jnp.where(pos % (2 * h) >= h, blk - ref, ref - blk))
    return jnp.concatenate(pieces, axis=0)


def _forget_lower_bound(lbl):
    m = jnp.max(lbl, axis=0, keepdims=True)
    e = jnp.exp(lbl - m)
    return e[0:1, :] / jnp.sum(e, axis=0, keepdims=True)


PROJ_BLOCK = 512
N_PROJ_BLOCKS = D_PROJ // PROJ_BLOCK


class _ColumnBlocks:
    def __init__(self, h, w_in_ref):
        self.h, self.w, self.done = h, w_in_ref, []

    def next_block(self):
        c = len(self.done)
        if c < N_PROJ_BLOCKS:
            self.done.append(_dot(self.h, self.w[:, c * PROJ_BLOCK:(c + 1) * PROJ_BLOCK]))

    def block(self, c):
        while len(self.done) <= c:
            self.next_block()
        return self.done[c]


def _gates(ff, lb):
    e = jnp.exp(-jnp.abs(ff))
    r = 1.0 / (1.0 + e)
    pos = ff >= 0
    sig = jnp.where(pos, r, e * r)
    sig_neg = jnp.where(pos, e * r, r)
    logf = jnp.log(lb + (1.0 - lb) * sig)
    kk = (1.0 - lb) * sig_neg
    return logf, kk


def _intra_scores_times_v(q, kk, vi, exps, masks_ref):
    n = q.shape[0]
    sc = [jnp.zeros((n, n), F32) for _ in range(N_HEADS)]
    for l, ex in enumerate(exps):
        if ex is None:
            qh, kh = q.astype(BF16), kk.astype(BF16)
        else:
            w = jnp.exp(ex)
            qh, kh = (q * w).astype(BF16), (kk * w).astype(BF16)
        mask = masks_ref[l]
        for hd in range(N_HEADS):
            hs = slice(hd * D_HEAD, (hd + 1) * D_HEAD)
            sc[hd] = sc[hd] + mask * _dot_nt(qh[:, hs], kh[:, hs])
    vb = vi.astype(BF16)
    return [_dot(sc[hd].astype(BF16), vb[:, hd * D_HEAD:(hd + 1) * D_HEAD]) for hd in range(N_HEADS)]


def _head_out(o, g, gnw):
    parts = []
    for hd in range(N_HEADS):
        oh = o[:, hd * D_HEAD:(hd + 1) * D_HEAD]
        parts.append(oh * lax.rsqrt(jnp.mean(oh * oh, axis=-1, keepdims=True) + EPS))
    on = jnp.concatenate(parts, axis=1)
    return on * gnw * (g * jax.nn.sigmoid(g))


def _pad_rows_bf16(a):
    return jnp.concatenate([a, jnp.zeros_like(a)], axis=0).astype(BF16)


SAMPLE_LEVELS = (4, 2, 1)
PROMPT_WIDE_LEVELS = (32, 16, 8)
PROMPT_LEVELS = PROMPT_WIDE_LEVELS + SAMPLE_LEVELS
N_MIX_PROMPT_INPUTS = 9


def _mix_prompt_kernel(n_r, n_steps, *refs):
    step = pl.program_id(0)

    @pl.when(step < n_steps)
    def _():
        _mix_prompt_step(step % n_r, n_r, *refs)

    @pl.when(step >= n_steps)
    def _():
        x1_ref = refs[N_MIX_PROMPT_INPUTS]
        x1_ref[...] = jnp.zeros_like(x1_ref)


def _mix_prompt_step(r, n_r, x_ref, lbl_ref, nw_ref, win_ref, cw_ref, cb_ref, gnw_ref, wout_ref, masks_ref,
                     x1_ref, nconv_ref, nstate_ref, st_ref, tail_ref):
    rows = x_ref.shape[0]

    @pl.when(r == 0)
    def _():
        st_ref[...] = jnp.zeros_like(st_ref)
        tail_ref[...] = jnp.zeros_like(tail_ref)

    lb = _forget_lower_bound(lbl_ref[...])
    cw = cw_ref[...]
    n_chunks = MIX_SLAB // PROMPT_CHUNK
    rid = lax.broadcasted_iota(jnp.int32, (MIX_PART, 1), 0)
    t0 = tail_ref[SUBLANES - 2:SUBLANES - 1, :]
    t1 = tail_ref[SUBLANES - 1:SUBLANES, :]
    states = [st_ref[hd] for hd in range(N_HEADS)]

    n_parts = rows // MIX_PART
    part_rows = lambda p: slice(p * MIX_PART, (p + 1) * MIX_PART)
    project = lambda p: _ColumnBlocks(_rmsnorm(x_ref[part_rows(p), :], nw_ref[...]).astype(BF16), win_ref)
    nxt = project(0)
    for p in range(n_parts):
        pr = part_rows(p)
        x = x_ref[pr, :]
        bg, cg, vc, q, ff, vi, g = (nxt.block(c) for c in range(N_PROJ_BLOCKS))
        if p + 1 < n_parts:
            nxt = project(p + 1)
            nxt.block(N_PROJ_BLOCKS - 1)

        u = cg * vc
        u1 = jnp.where(rid == 0, t1, pltpu.roll(u, 1, 0))
        u2 = jnp.where(rid == 0, t0, jnp.where(rid == 1, t1, pltpu.roll(u, 2, 0)))
        y_conv = bg * (cw[0:1, :] * u2 + cw[1:2, :] * u1 + cw[2:3, :] * u + cb_ref[...])
        u_last = u[MIX_PART - SUBLANES:MIX_PART, :]
        t0, t1 = u_last[SUBLANES - 2:SUBLANES - 1, :], u_last[SUBLANES - 1:SUBLANES, :]

        logf, kk = _gates(ff, lb)
        o_parts = []
        for s in range(MIX_PART // MIX_SLAB):
            sl = slice(s * MIX_SLAB, (s + 1) * MIX_SLAB)
            qs, ks, vs = q[sl], kk[sl], vi[sl]
            a_pre = _chunk_prefix_sums(logf[sl], PROMPT_CHUNK)
            a_suf = jnp.concatenate(
                [a_pre[(c + 1) * PROMPT_CHUNK - 1:(c + 1) * PROMPT_CHUNK, :]
                 - a_pre[c * PROMPT_CHUNK:(c + 1) * PROMPT_CHUNK, :] for c in range(n_chunks)], axis=0)
            exps = [_vpu_level_exponent(a_pre, h) for h in PROMPT_WIDE_LEVELS]
            exps += [_small_level_exponent(a_pre, logf[sl], h) for h in SAMPLE_LEVELS]
            exps.append(None)
            intra = _intra_scores_times_v(qs, ks, vs, exps, masks_ref)
            ea = jnp.exp(a_pre)
            qa = (qs * ea).astype(BF16)
            kb = (ks * jnp.exp(a_suf)).astype(BF16)
            vb = vs.astype(BF16)
            for c in range(n_chunks):
                cr = slice(c * PROMPT_CHUNK, (c + 1) * PROMPT_CHUNK)
                last = (c + 1) * PROMPT_CHUNK - 1
                heads = []
                for hd in range(N_HEADS):
                    hs = slice(hd * D_HEAD, (hd + 1) * D_HEAD)
                    heads.append(_dot_nt(qa[cr, hs], states[hd].astype(BF16)) + intra[hd][cr, :])
                    states[hd] = states[hd] * ea[last:last + 1, hs] + _dot_tn(vb[cr, hs], kb[cr, hs])
                o_parts.append(jnp.concatenate(heads, axis=1))

        o = _head_out(jnp.concatenate(o_parts, axis=0), g, gnw_ref[...])
        mix_in = jnp.concatenate([y_conv, o], axis=1).astype(BF16)
        x1_ref[pr, :] = x + _dot(mix_in, wout_ref[...])

    tail_ref[...] = u_last
    for hd in range(N_HEADS):
        st_ref[hd] = states[hd]

    @pl.when(r == n_r - 1)
    def _():
        nconv_ref[...] = u_last
        for hd in range(N_HEADS):
            nstate_ref[hd] = states[hd].T


def _mix_prompt(x, lbl, nw, w_in, cw, cb, gnw, w_out, total_rows):
    b, l, _ = x.shape
    assert l % MIX_ROWS == 0 and MIX_ROWS % MIX_PART == 0 and MIX_PART % MIX_SLAB == 0 and l % PROMPT_CHUNK == 0
    assert total_rows % MIX_ROWS == 0
    n_r = l // MIX_ROWS
    n_steps = b * n_r
    seq = lambda s: jnp.minimum(s, n_steps - 1) // n_r
    masks = _level_masks(MIX_SLAB, PROMPT_LEVELS)
    const = lambda shape: pl.BlockSpec(shape, lambda s: (0,) * len(shape))
    return pl.pallas_call(
        functools.partial(_mix_prompt_kernel, n_r, n_steps),
        grid=(total_rows // MIX_ROWS,),
        in_specs=[
            pl.BlockSpec((None, MIX_ROWS, D_MODEL), lambda s: (seq(s), jnp.minimum(s, n_steps - 1) % n_r, 0)),
            const((2, D_HGRN)), const((1, D_MODEL)), const((D_MODEL, D_PROJ)), const((3, D_CONV)),
            const((1, D_CONV)), const((1, D_HGRN)), const((D_MODEL, D_MODEL)), const(masks.shape),
        ],
        out_specs=[
            pl.BlockSpec((MIX_ROWS, D_MODEL), lambda s: (s, 0)),
            pl.BlockSpec((None, SUBLANES, D_CONV), lambda s: (seq(s), 0, 0)),
            pl.BlockSpec((None, N_HEADS, D_HEAD, D_HEAD), lambda s: (seq(s), 0, 0, 0)),
        ],
        out_shape=[
            jax.ShapeDtypeStruct((total_rows, D_MODEL), F32),
            jax.ShapeDtypeStruct((b, SUBLANES, D_CONV), F32),
            jax.ShapeDtypeStruct((b, N_HEADS, D_HEAD, D_HEAD), F32),
        ],
        scratch_shapes=[
            pltpu.VMEM((N_HEADS, D_HEAD, D_HEAD), F32),
            pltpu.VMEM((SUBLANES, D_CONV), F32),
        ],
        compiler_params=pltpu.CompilerParams(
            dimension_semantics=("arbitrary",), vmem_limit_bytes=VMEM_LIMIT_BYTES),
        name="mix_prompt",
    )(x, lbl, nw, w_in, cw, cb, gnw, w_out, masks)


def _mix_sample_kernel(x_ref, cs_ref, hs_ref, lbl_ref, nw_ref, win_ref, cw_ref, cb_ref, gnw_ref, wout_ref,
                       cmat_ref, masks_ref, x1_in_ref, x1_ref, nconv_ref, nstate_ref, o_ref):
    del x1_in_ref
    nseq, length, _ = nconv_ref.shape
    rows = nseq * length
    x = x_ref[...]
    proj = _ColumnBlocks(_rmsnorm(x, nw_ref[...]).astype(BF16), win_ref)
    bg, cg, vc, q, ff, vi, g = (proj.block(c) for c in range(N_PROJ_BLOCKS))
    lb = _forget_lower_bound(lbl_ref[...])

    u = cg * vc
    cs = cs_ref[...]
    expand = lambda a: jnp.broadcast_to(a, (nseq, length, D_CONV)).reshape(rows, D_CONV)
    t0 = expand(cs[:, 0:1, :])
    t1 = expand(cs[:, 1:2, :])
    pos = lax.broadcasted_iota(jnp.int32, (rows, 1), 0) % length
    u1 = jnp.where(pos == 0, t1, pltpu.roll(u, 1, 0))
    u2 = jnp.where(pos == 0, t0, jnp.where(pos == 1, t1, pltpu.roll(u, 2, 0)))
    cw = cw_ref[...]
    y_conv = bg * (cw[0:1, :] * u2 + cw[1:2, :] * u1 + cw[2:3, :] * u + cb_ref[...])
    nconv_ref[...] = u.reshape(nseq, length, D_CONV)

    logf, kk = _gates(ff, lb)
    lf_hi, lf_lo = _split2(logf)
    ex = _dot(cmat_ref[...], lf_hi) + _dot(cmat_ref[...], lf_lo)
    a_pre = ex[0:rows]
    a_suf = ex[rows:2 * rows]
    exps = [ex[(2 + i) * rows:(3 + i) * rows] for i in range(len(SAMPLE_LEVELS))] + [None]
    intra = _intra_scores_times_v(q, kk, vi, exps, masks_ref)
    ea = jnp.exp(a_pre)
    qa = q * ea
    kb = kk * jnp.exp(a_suf)
    for s in range(nseq):
        cr = slice(s * length, (s + 1) * length)
        last = (s + 1) * length - 1
        for hd in range(N_HEADS):
            hs = slice(hd * D_HEAD, (hd + 1) * D_HEAD)
            st = hs_ref[s, hd].T
            inter = _dot_nt(_pad_rows_bf16(qa[cr, hs]), st.astype(BF16))[0:length, :]
            o_ref[cr, hs] = inter + intra[hd][cr, :]
            st_new = st * ea[last:last + 1, hs] + _dot_tn(_pad_rows_bf16(vi[cr, hs]), _pad_rows_bf16(kb[cr, hs]))
            nstate_ref[s, hd] = st_new.T

    o = _head_out(o_ref[...], g, gnw_ref[...])
    mix_in = jnp.concatenate([y_conv, o], axis=1).astype(BF16)
    x1_ref[...] = x + _dot(mix_in, wout_ref[...])


def _mix_sample(x, conv_state, hgrn_state, lbl, nw, w_in, cw, cb, gnw, w_out, x1_flat, row_offset):
    nb, length, _ = x.shape
    rows = MIX_SEQS * length
    assert nb % MIX_SEQS == 0 and length == SUBLANES and row_offset % rows == 0
    first_block = row_offset // rows
    cmat, masks = _exponent_matrices(rows, length, SAMPLE_LEVELS), _level_masks(rows, SAMPLE_LEVELS)
    const = lambda shape: pl.BlockSpec(shape, lambda i: (0,) * len(shape))
    return pl.pallas_call(
        _mix_sample_kernel,
        grid=(nb // MIX_SEQS,),
        in_specs=[
            pl.BlockSpec((rows, D_MODEL), lambda i: (i, 0)),
            pl.BlockSpec((MIX_SEQS, 2, D_CONV), lambda i: (i, 0, 0)),
            pl.BlockSpec((MIX_SEQS, N_HEADS, D_HEAD, D_HEAD), lambda i: (i, 0, 0, 0)),
            const((2, D_HGRN)), const((1, D_MODEL)), const((D_MODEL, D_PROJ)), const((3, D_CONV)),
            const((1, D_CONV)), const((1, D_HGRN)), const((D_MODEL, D_MODEL)), const(cmat.shape),
            const(masks.shape), pl.BlockSpec(memory_space=pl.ANY),
        ],
        out_specs=[
            pl.BlockSpec((rows, D_MODEL), lambda i: (first_block + i, 0)),
            pl.BlockSpec((MIX_SEQS, length, D_CONV), lambda i: (i, 0, 0)),
            pl.BlockSpec((MIX_SEQS, N_HEADS, D_HEAD, D_HEAD), lambda i: (i, 0, 0, 0)),
        ],
        out_shape=[
            jax.ShapeDtypeStruct(x1_flat.shape, F32),
            jax.ShapeDtypeStruct((nb, length, D_CONV), F32),
            jax.ShapeDtypeStruct((nb, N_HEADS, D_HEAD, D_HEAD), F32),
        ],
        scratch_shapes=[pltpu.VMEM((rows, D_HGRN), F32)],
        input_output_aliases={12: 0},
        compiler_params=pltpu.CompilerParams(
            dimension_semantics=("arbitrary",), vmem_limit_bytes=VMEM_LIMIT_BYTES),
        name="mix_sample",
    )(x.reshape(nb * length, D_MODEL), conv_state, hgrn_state, lbl, nw, w_in, cw, cb, gnw, w_out, cmat, masks,
      x1_flat)


def _router_kernel(x1_ref, nw_ref, rw_hi_ref, rw_lo_ref, rb_ref, ltri_ref, utri_ref, xn_ref, meta_ref, cnt_ref):
    n = ltri_ref.shape[0]
    parts = [slice(p * n, (p + 1) * n) for p in range(x1_ref.shape[0] // n)]
    lane = lax.broadcasted_iota(jnp.int32, (n, LANES), 1).astype(F32)
    logits = []
    for pr in parts:
        xn = _rmsnorm(x1_ref[pr, :], nw_ref[...])
        xn_ref[pr, :] = xn.astype(BF16)
        x_hi, x_lo = _split2(xn)
        logits.append(_dot(x_hi, rw_hi_ref[...]) + _dot(x_lo, rw_hi_ref[...]) + _dot(x_hi, rw_lo_ref[...])
                      + rb_ref[...])
    picks = []
    for work in logits:
        vals, ids = [], []
        for _ in range(TOP_K):
            m = jnp.max(work, axis=-1, keepdims=True)
            i = jnp.min(jnp.where(work == m, lane, float(LANES)), axis=-1, keepdims=True)
            vals.append(m)
            ids.append(i)
            work = jnp.where(lane == i, -jnp.inf, work)
        es = [jnp.exp(v - vals[0]) for v in vals]
        den = es[0] + es[1] + es[2] + es[3]
        onehots = [(lane == i) for i in ids]
        multi = jnp.zeros((n, LANES), F32)
        for oh in onehots:
            multi = multi + oh.astype(F32)
        picks.append((ids, [e / den for e in es], onehots, multi))

    befores, counts = [], jnp.zeros((1, LANES), F32)
    for _, _, _, multi in picks:
        befores.append(counts + _dot(ltri_ref[...], multi.astype(BF16)))
        counts = counts + jnp.sum(multi, axis=0, keepdims=True)
    seg = jnp.ceil(counts * (1.0 / SEG_ALIGN)) * SEG_ALIGN
    seg_rows = jnp.broadcast_to(seg, (BF16_ROWS, LANES)).astype(BF16)
    seg_off = _dot(seg_rows, utri_ref[...])[0:1, :]
    for pr, (ids, gates, onehots, _), before in zip(parts, picks, befores):
        slot_of = seg_off + before
        meta = jnp.zeros((n, LANES), F32)
        for k in range(TOP_K):
            slot = jnp.sum(jnp.where(onehots[k], slot_of, 0.0), axis=-1, keepdims=True)
            meta = jnp.where(lane == k, ids[k], meta)
            meta = jnp.where(lane == TOP_K + k, gates[k], meta)
            meta = jnp.where(lane == 2 * TOP_K + k, slot, meta)
        meta_ref[pr, :] = meta
    cnt_ref[...] = jnp.broadcast_to(counts, (SUBLANES, LANES))


def _router(x1, nw, router_w, router_b):
    t = x1.shape[0]
    n_tiles = t // TOKEN_TILE
    rw = jnp.zeros((D_MODEL, LANES), F32).at[:, :N_EXPERTS].set(router_w)
    rw_hi, rw_lo = _split2(rw)
    rb = jnp.full((1, LANES), PAD_LOGIT, F32).at[0, :N_EXPERTS].set(router_b)
    idx = np.arange(ROUTER_PART)
    ltri = jnp.asarray(idx[None, :] < idx[:, None], BF16)
    lid = np.arange(LANES)
    utri = jnp.asarray(lid[:, None] < lid[None, :], BF16)
    const = lambda shape: pl.BlockSpec(shape, lambda i: (0,) * len(shape))
    return pl.pallas_call(
        _router_kernel,
        grid=(n_tiles,),
        in_specs=[
            pl.BlockSpec((TOKEN_TILE, D_MODEL), lambda i: (i, 0)),
            const((1, D_MODEL)), const((D_MODEL, LANES)), const((D_MODEL, LANES)), const((1, LANES)),
            const((ROUTER_PART, ROUTER_PART)), const((LANES, LANES)),
        ],
        out_specs=[
            pl.BlockSpec((TOKEN_TILE, D_MODEL), lambda i: (i, 0)),
            pl.BlockSpec((TOKEN_TILE, LANES), lambda i: (i, 0)),
            pl.BlockSpec((None, SUBLANES, LANES), lambda i: (i, 0, 0)),
        ],
        out_shape=[
            jax.ShapeDtypeStruct((t, D_MODEL), BF16),
            jax.ShapeDtypeStruct((t, LANES), F32),
            jax.ShapeDtypeStruct((n_tiles, SUBLANES, LANES), F32),
        ],
        compiler_params=pltpu.CompilerParams(
            dimension_semantics=("arbitrary",), vmem_limit_bytes=VMEM_LIMIT_BYTES),
        name="router",
    )(x1, nw, rw_hi, rw_lo, rb, ltri, utri)


def _segment_tables(cnt):
    seg = (cnt + SEG_ALIGN - 1) // SEG_ALIGN * SEG_ALIGN
    src = jnp.cumsum(seg, axis=1) - seg
    tot = jnp.sum(seg, axis=0)
    cap = (tot + ROW_TILE - 1) // ROW_TILE * ROW_TILE
    base = jnp.cumsum(cap) - cap
    dst = base[None, :] + jnp.cumsum(seg, axis=0) - seg
    n_used = jnp.sum(cap) // ROW_TILE
    fill = base + tot
    n_tiles = seg.shape[0]
    lane = jnp.arange(N_EXPERTS)

    def compact(has, *values):
        place = has[:, :, None] & ((jnp.cumsum(has, axis=1) - 1)[:, :, None] == lane[None, None, :])
        return [jnp.sum(has, axis=1)] + [jnp.sum(jnp.where(place, v[:, :, None], 0), axis=1) for v in values]

    whole, rest = seg // CHUNK_ROWS, seg % CHUNK_ROWS
    k = jnp.arange(TOKEN_TILE // CHUNK_ROWS) * CHUNK_ROWS
    flat = lambda a: a.reshape(n_tiles, -1)
    pieces = [compact(flat(seg[:, :, None] >= k + CHUNK_ROWS), flat(src[:, :, None] + k), flat(dst[:, :, None] + k))]
    for size in CHUNK_SIZES[1:]:
        pieces.append(compact(rest == size, src + whole * CHUNK_ROWS, dst + whole * CHUNK_ROWS))
    counts, srcs, dsts = zip(*pieces)
    i32 = lambda a: a.astype(jnp.int32).reshape(-1)
    chunks = (i32(jnp.stack(counts, axis=1)), i32(jnp.stack(srcs, axis=1)), i32(jnp.stack(dsts, axis=1)))
    return (chunks, i32(jnp.sum(seg, axis=1)), i32(fill), i32(cap - tot), i32(base), i32(cap // ROW_TILE),
            i32(n_used))


def _for_each_chunk(n, fn):
    for size in SEG_SIZES:
        done = n & ~(2 * size - 1)

        @pl.when((n & size) != 0)
        def _():
            fn(done, size)


def _segment_copies(cnt_ref, src_ref, dst_ref, tile, make_copy, start):
    for si, size in enumerate(CHUNK_SIZES):
        first = (tile * len(CHUNK_SIZES) + si) * N_EXPERTS

        def one(p, carry, si=si, size=size, first=first):
            copy = make_copy(pl.multiple_of(src_ref[first + p], SEG_ALIGN),
                             pl.multiple_of(dst_ref[first + p], SEG_ALIGN), size)
            if start:
                copy.start(priority=si % 2)
            else:
                copy.wait()
            return carry
        lax.fori_loop(0, cnt_ref[tile * len(CHUNK_SIZES) + si], one, 0)


SORT_BLOCK = 256


def _slot_matrix(slots, block, gates):
    row = (lax.broadcasted_iota(jnp.int32, (SORT_BLOCK, TOKEN_TILE), 0) + block * SORT_BLOCK).astype(F32)
    m = jnp.zeros((SORT_BLOCK, TOKEN_TILE), F32)
    for k in range(TOP_K):
        m = jnp.where(row == slots[k:k + 1, :], 1.0 if gates is None else gates[k:k + 1, :], m)
    return m.astype(BF16)


def _dispatch_kernel(seg_ref, src_ref, dst_ref, fill_ref, gap_ref, nu_ref, slot_ref, xn_ref, xs_ref, sorted_ref,
                     zero_ref, sem, seg_sem):
    tile = pl.program_id(0)
    n_tiles = pl.num_programs(0)
    buf = tile % 2

    @pl.when(tile == 0)
    def _():
        zero_ref[...] = jnp.zeros_like(zero_ref)

        def fill_copy(row, size):
            return pltpu.make_async_copy(zero_ref.at[pl.ds(0, size)],
                                         xs_ref.at[pl.ds(pl.multiple_of(row, SEG_ALIGN), size)], sem)

        def gaps(act):
            def per_expert(e, carry):
                _for_each_chunk(gap_ref[e], lambda done, size: act(fill_copy(fill_ref[e] + done, size)))
                return carry
            lax.fori_loop(0, N_EXPERTS, per_expert, 0)

        def tail(act):
            n_tail = xs_ref.shape[0] // ROW_TILE - nu_ref[0]
            lax.fori_loop(0, n_tail, lambda i, c: (act(fill_copy((nu_ref[0] + i) * ROW_TILE, ROW_TILE)), c)[1], 0)

        gaps(lambda c: c.start())
        tail(lambda c: c.start())
        gaps(lambda c: c.wait())
        tail(lambda c: c.wait())

    slots = slot_ref[...]
    xn = xn_ref[...]
    for rb in range(TILE_CAP // SORT_BLOCK):
        sorted_ref[buf, rb * SORT_BLOCK:(rb + 1) * SORT_BLOCK, :] = _dot(
            _slot_matrix(slots, rb, None), xn).astype(BF16)

    def copier(half):
        return lambda src, dst, size: pltpu.make_async_copy(
            sorted_ref.at[half, pl.ds(src, size)], xs_ref.at[pl.ds(dst, size)], seg_sem.at[half])

    _segment_copies(seg_ref, src_ref, dst_ref, tile, copier(buf), start=True)

    @pl.when(tile > 0)
    def _():
        _segment_copies(seg_ref, src_ref, dst_ref, tile - 1, copier(1 - buf), start=False)

    @pl.when(tile == n_tiles - 1)
    def _():
        _segment_copies(seg_ref, src_ref, dst_ref, tile, copier(buf), start=False)


def _dispatch(tables, slot_t, xn, n_rows_total):
    t = xn.shape[0]
    n_tiles = t // TOKEN_TILE
    return pl.pallas_call(
        _dispatch_kernel,
        grid_spec=pltpu.PrefetchScalarGridSpec(
            num_scalar_prefetch=len(tables),
            grid=(n_tiles,),
            in_specs=[
                pl.BlockSpec((SUBLANES, TOKEN_TILE), lambda i, *_: (0, i)),
                pl.BlockSpec((TOKEN_TILE, D_MODEL), lambda i, *_: (i, 0)),
            ],
            out_specs=pl.BlockSpec(memory_space=pl.ANY),
            scratch_shapes=[
                pltpu.VMEM((2, TILE_CAP, D_MODEL), BF16),
                pltpu.VMEM((ROW_TILE, D_MODEL), BF16),
                pltpu.SemaphoreType.DMA(()),
                pltpu.SemaphoreType.DMA((2,)),
            ],
        ),
        out_shape=jax.ShapeDtypeStruct((n_rows_total, D_MODEL), BF16),
        compiler_params=pltpu.CompilerParams(
            dimension_semantics=("arbitrary",), vmem_limit_bytes=VMEM_LIMIT_BYTES),
        name="dispatch",
    )(*tables, slot_t, xn)


X_SLOTS = 3
W_FETCH_STEPS = (0, 1, 3)


def _experts_kernel(base_ref, nt_ref, nu_ref, bg_ref, bu_ref, bd_ref, wg_hbm, wu_hbm, wd_hbm, xs_ref, zs_ref,
                    w_f32, wg_bf, wu_bf, wd_bf, xbuf, hbuf, zbuf, w_sem, in_sem, out_sem):
    e = pl.program_id(0)
    n_e = pl.num_programs(0)
    n = nt_ref[e]
    base = base_ref[e]
    w_slot = e % 2
    has_next = e + 1 < n_e

    def w_copy(j, expert, slot):
        w_hbm = (wg_hbm, wu_hbm, wd_hbm)[j]
        return pltpu.make_async_copy(w_hbm.at[expert], w_f32.at[slot, j], w_sem.at[slot])

    def x_copy(i):
        rows = pl.ds(pl.multiple_of(base + i * ROW_TILE, ROW_TILE), ROW_TILE)
        return pltpu.make_async_copy(xs_ref.at[rows], xbuf.at[i % X_SLOTS], in_sem.at[i % X_SLOTS])

    def z_copy(row, slot):
        rows = pl.ds(pl.multiple_of(row, ROW_TILE), ROW_TILE)
        return pltpu.make_async_copy(zbuf.at[slot], zs_ref.at[rows], out_sem.at[slot])

    def hidden(i):
        x = xbuf[i % X_SLOTS]
        a = jnp.minimum(_dot(x, wg_bf[...]) + bg_ref[...], SWIGLU_LIMIT)
        u = jnp.clip(_dot(x, wu_bf[...]) + bu_ref[...], -SWIGLU_LIMIT, SWIGLU_LIMIT)
        hbuf[i % 2] = ((u + 1.0) * a * jax.nn.sigmoid(SWIGLU_ALPHA * a)).astype(BF16)

    def project_down(i, slot):
        @pl.when(i >= 2)
        def _():
            z_copy(base, slot).wait()
        zbuf[slot] = (_dot(hbuf[slot], wd_bf[...]) + bd_ref[...]).astype(BF16)

    @pl.when(e == 0)
    def _():
        for j in range(3):
            w_copy(j, 0, 0).start()

    for i in range(X_SLOTS):
        @pl.when(i < n)
        def _():
            x_copy(i).start(priority=1)

    def fetch_next_weights(j):
        @pl.when(has_next)
        def _():
            w_copy(j, e + 1, 1 - w_slot).start()

    fetch_next_weights(0)
    for j in range(3):
        w_copy(j, e, w_slot).wait()

    @pl.when(n > 0)
    def _():
        wg_bf[...] = w_f32[w_slot, 0].astype(BF16)
        wu_bf[...] = w_f32[w_slot, 1].astype(BF16)
        wd_bf[...] = w_f32[w_slot, 2].astype(BF16)
        x_copy(0).wait()
        hidden(0)

        def step(i, carry):
            x_copy(i + 1).wait()

            @pl.when(i + X_SLOTS < n)
            def _():
                x_copy(i + X_SLOTS).start(priority=1)

            for j in (1, 2):
                @pl.when(i == W_FETCH_STEPS[j])
                def _():
                    fetch_next_weights(j)

            project_down(i, i % 2)
            hidden(i + 1)
            z_copy(base + i * ROW_TILE, i % 2).start()
            return carry

        lax.fori_loop(0, n - 1, step, 0)
        project_down(n - 1, (n - 1) % 2)
        z_copy(base + (n - 1) * ROW_TILE, (n - 1) % 2).start()

        @pl.when(n >= 2)
        def _():
            z_copy(base, n % 2).wait()
        z_copy(base, (n - 1) % 2).wait()

    for j in (1, 2):
        @pl.when(jnp.maximum(n - 1, 0) <= W_FETCH_STEPS[j])
        def _():
            fetch_next_weights(j)

    @pl.when(e == pl.num_programs(0) - 1)
    def _():
        zbuf[0] = jnp.zeros((ROW_TILE, D_MODEL), BF16)
        n_tail = zs_ref.shape[0] // ROW_TILE - nu_ref[0]
        tail = lambda act: lax.fori_loop(
            0, n_tail, lambda i, c: (act(z_copy((nu_ref[0] + i) * ROW_TILE, 0)), c)[1], 0)
        tail(lambda c: c.start())
        tail(lambda c: c.wait())


def _experts(base, n_tiles, n_used, xs, w_gate, b_gate, w_up, b_up, w_down, b_down):
    b_spec = pl.BlockSpec((None, 1, D_MODEL), lambda e, *_: (e, 0, 0))
    any_spec = pl.BlockSpec(memory_space=pl.ANY)
    return pl.pallas_call(
        _experts_kernel,
        grid_spec=pltpu.PrefetchScalarGridSpec(
            num_scalar_prefetch=3,
            grid=(N_EXPERTS,),
            in_specs=[b_spec, b_spec, b_spec, any_spec, any_spec, any_spec, any_spec],
            out_specs=any_spec,
            scratch_shapes=[pltpu.VMEM((2, 3, D_MODEL, D_MODEL), F32)] + [
                pltpu.VMEM((D_MODEL, D_MODEL), BF16) for _ in range(3)] + [
                pltpu.VMEM((X_SLOTS, ROW_TILE, D_MODEL), BF16)] + [
                pltpu.VMEM((2, ROW_TILE, D_MODEL), BF16) for _ in range(2)] + [
                pltpu.SemaphoreType.DMA((2,)), pltpu.SemaphoreType.DMA((X_SLOTS,)), pltpu.SemaphoreType.DMA((2,))],
        ),
        out_shape=jax.ShapeDtypeStruct(xs.shape, BF16),
        compiler_params=pltpu.CompilerParams(
            dimension_semantics=("arbitrary",), vmem_limit_bytes=VMEM_LIMIT_BYTES),
        name="experts",
    )(base, n_tiles, n_used, b_gate[:, None, :], b_up[:, None, :], b_down[:, None, :], w_gate, w_up, w_down, xs)


ZERO_BLOCK = 256


def _combine_kernel(n_first, seg_ref, src_ref, dst_ref, rows_ref, route_ref, x1_ref, fw_ref, zs_ref, ya_ref, yb_ref,
                    sorted_ref, sem):
    tile = pl.program_id(0)
    n_tiles = pl.num_programs(0)
    buf = tile % 2

    def fetch(t, half):
        def zero(b, carry):
            rows = pl.ds(pl.multiple_of(b * ZERO_BLOCK, ZERO_BLOCK), ZERO_BLOCK)
            sorted_ref[half, rows, :] = jnp.zeros((ZERO_BLOCK, D_MODEL), BF16)
            return carry
        lax.fori_loop(rows_ref[t] // ZERO_BLOCK, TILE_CAP // ZERO_BLOCK, zero, 0)
        _segment_copies(seg_ref, src_ref, dst_ref, t, copier(half), start=True)

    def copier(half):
        return lambda src, dst, size: pltpu.make_async_copy(
            zs_ref.at[pl.ds(dst, size)], sorted_ref.at[half, pl.ds(src, size)], sem.at[half])

    @pl.when(tile == 0)
    def _():
        fetch(tile, buf)

    @pl.when(tile + 1 < n_tiles)
    def _():
        fetch(tile + 1, 1 - buf)

    _segment_copies(seg_ref, src_ref, dst_ref, tile, copier(buf), start=False)
    route = route_ref[...]
    slots, gates = route[0:TOP_K, :], route[TOP_K:2 * TOP_K, :]
    moe = jnp.zeros((TOKEN_TILE, D_MODEL), F32)
    for cb in range(TILE_CAP // SORT_BLOCK):
        moe = moe + _dot_tn(_slot_matrix(slots, cb, gates), sorted_ref[buf, cb * SORT_BLOCK:(cb + 1) * SORT_BLOCK, :])
    y = _rmsnorm(x1_ref[...] + moe, fw_ref[...])

    @pl.when(tile < n_first)
    def _():
        ya_ref[...] = y

    @pl.when(tile >= n_first)
    def _():
        yb_ref[...] = y


def _combine(tables, route_t, x1, final_w, zs, rows_first):
    t = x1.shape[0]
    assert rows_first % TOKEN_TILE == 0 and 0 < rows_first < t
    assert TILE_CAP % SORT_BLOCK == 0 and TILE_CAP % ZERO_BLOCK == 0
    n_tiles = t // TOKEN_TILE
    n_first = rows_first // TOKEN_TILE
    return pl.pallas_call(
        functools.partial(_combine_kernel, n_first),
        grid_spec=pltpu.PrefetchScalarGridSpec(
            num_scalar_prefetch=len(tables),
            grid=(n_tiles,),
            in_specs=[
                pl.BlockSpec((SUBLANES, TOKEN_TILE), lambda i, *_: (0, i)),
                pl.BlockSpec((TOKEN_TILE, D_MODEL), lambda i, *_: (i, 0)),
                pl.BlockSpec((1, D_MODEL), lambda i, *_: (0, 0)),
                pl.BlockSpec(memory_space=pl.ANY),
            ],
            out_specs=[
                pl.BlockSpec((TOKEN_TILE, D_MODEL), lambda i, *_: (jnp.minimum(i, n_first - 1), 0)),
                pl.BlockSpec((TOKEN_TILE, D_MODEL), lambda i, *_: (jnp.maximum(i - n_first, 0), 0)),
            ],
            scratch_shapes=[pltpu.VMEM((2, TILE_CAP, D_MODEL), BF16), pltpu.SemaphoreType.DMA((2,))],
        ),
        out_shape=[jax.ShapeDtypeStruct((rows_first, D_MODEL), F32),
                   jax.ShapeDtypeStruct((t - rows_first, D_MODEL), F32)],
        compiler_params=pltpu.CompilerParams(
            dimension_semantics=("arbitrary",), vmem_limit_bytes=VMEM_LIMIT_BYTES),
        name="combine",
    )(*tables, route_t, x1, final_w, zs)


def _moe_and_final_norm(x1, rows_first, norm_ffn_w, router_w, router_b, w_gate, b_gate, w_up, b_up, w_down, b_down,
                        final_w):
    t = x1.shape[0]
    assert t % TOKEN_TILE == 0
    n_tiles = t // TOKEN_TILE
    n_rows_total = TOP_K * t + n_tiles * N_EXPERTS * (SEG_ALIGN - 1) + N_EXPERTS * (ROW_TILE - 1)
    n_rows_total = (n_rows_total + ROW_TILE - 1) // ROW_TILE * ROW_TILE
    xn, meta, cnt = _router(x1, norm_ffn_w.reshape(1, D_MODEL), router_w, router_b)
    cnt = cnt[:, 0, :N_EXPERTS].astype(jnp.int32)
    chunks, tile_rows, fill, gap, base, n_row_tiles, n_used = _segment_tables(cnt)
    route_t = jnp.concatenate([meta[:, 2 * TOP_K:3 * TOP_K], meta[:, TOP_K:2 * TOP_K]], axis=1).T
    xs = _dispatch((*chunks, fill, gap, n_used), route_t, xn, n_rows_total)
    zs = _experts(base, n_row_tiles, n_used, xs, w_gate, b_gate, w_up, b_up, w_down, b_down)
    return _combine((*chunks, tile_rows), route_t, x1, final_w.reshape(1, D_MODEL), zs, rows_first)


def kernel(x_prompt, x_sample, state_conv, state_hgrn, lb_logits, norm_mix_w, w_in, conv_w, conv_b, gnorm_w,
           w_out, norm_ffn_w, router_w, router_b, w_gate, b_gate, w_up, b_up, w_down, b_down, final_norm_w):
    assert norm_mix_w.shape[0] == 1 and lb_logits.shape[0] == 2, "single-layer step"
    b, l, _ = x_prompt.shape
    nb, ls, _ = x_sample.shape
    lbl = lb_logits.astype(F32)
    nw = norm_mix_w[0].reshape(1, D_MODEL)
    w_in_bf = w_in[0].astype(BF16)
    w_out_bf = w_out[0].astype(BF16)
    cw, cb = conv_w[0], conv_b[0].reshape(1, D_CONV)
    gnw = gnorm_w[0].reshape(1, D_HGRN)
    rows_p, rows_s = b * l, nb * ls
    x1, conv_p, hgrn_p = _mix_prompt(x_prompt, lbl, nw, w_in_bf, cw, cb, gnw, w_out_bf, rows_p + rows_s)
    x1, conv_s, hgrn_s = _mix_sample(x_sample, state_conv[0], state_hgrn[0], lbl, nw, w_in_bf, cw, cb, gnw,
                                     w_out_bf, x1, rows_p)
    y_p, y_s = _moe_and_final_norm(x1, rows_p, norm_ffn_w[0], router_w[0], router_b[0], w_gate[0], b_gate[0],
                                   w_up[0], b_up[0], w_down[0], b_down[0], final_norm_w)
    conv_p = conv_p[:, SUBLANES - 2:, :]
    conv_s = conv_s[:, ls - 2:, :]
    return (y_p.reshape(b, l, D_MODEL), y_s.reshape(nb, ls, D_MODEL), conv_p[None], hgrn_p[None], conv_s[None],
            hgrn_s[None])
```

```python
import functools

import numpy as np
import jax
import jax.numpy as jnp
from jax import lax
from jax.experimental import pallas as pl
from jax.experimental.pallas import tpu as pltpu

F32 = jnp.float32
BF16 = jnp.bfloat16

D_MODEL = 1024
D_CONV = 512
D_HGRN = 512
N_HEADS = 4
D_HEAD = 128
D_PROJ = 3 * D_CONV + 4 * D_HGRN
N_EXPERTS = 32
TOP_K = 4
SWIGLU_LIMIT = 7.0
SWIGLU_ALPHA = 1.702
EPS = 1e-6
PAD_LOGIT = -1e30
PROMPT_CHUNK = 64

LANES = 128
SUBLANES = 8
BF16_ROWS = 16
VMEM_LIMIT_BYTES = 56 * 1024 * 1024

MIX_SLAB = 128
MIX_ROWS = 512
MIX_PART = 256
MIX_SEQS = 16
TOKEN_TILE = 512
ROUTER_PART = 256
SEG_ALIGN = BF16_ROWS
ROW_TILE = 512
SEG_SIZES = tuple(SEG_ALIGN << i for i in reversed(range(6)))
CHUNK_ROWS = 128
CHUNK_SIZES = tuple(range(CHUNK_ROWS, 0, -SEG_ALIGN))
TILE_CAP = TOP_K * TOKEN_TILE + N_EXPERTS * SEG_ALIGN
assert TILE_CAP // CHUNK_ROWS <= N_EXPERTS

NT_DIMS = (((1,), (1,)), ((), ()))
TN_DIMS = (((0,), (0,)), ((), ()))


def _dot(a, b):
    return jnp.dot(a, b, preferred_element_type=F32)


def _dot_nt(a, b):
    return lax.dot_general(a, b, NT_DIMS, preferred_element_type=F32)


def _dot_tn(a, b):
    return lax.dot_general(a, b, TN_DIMS, preferred_element_type=F32)


def _split2(x):
    hi = x.astype(BF16)
    lo = (x - hi.astype(F32)).astype(BF16)
    return hi, lo


def _rmsnorm(x, w):
    return x * lax.rsqrt(jnp.mean(x * x, axis=-1, keepdims=True) + EPS) * w


def _level_exponent_matrix(n, chunk, h):
    x = np.zeros((n, n), np.float32)
    for t in range(n):
        base = t - t % (2 * h)
        m = base + h - 1
        if t % (2 * h) >= h:
            x[t, m + 1:t + 1] = 1.0
        else:
            x[t, t + 1:m + 1] = 1.0
    return x


def _exponent_matrices(n, chunk, levels):
    t = np.arange(n)
    same_chunk = (t[:, None] // chunk) == (t[None, :] // chunk)
    tri = (same_chunk & (t[None, :] <= t[:, None])).astype(np.float32)
    suf = (same_chunk & (t[None, :] > t[:, None])).astype(np.float32)
    cmat = np.concatenate([tri, suf] + [_level_exponent_matrix(n, chunk, h) for h in levels], axis=0)
    return jnp.asarray(cmat, BF16)


def _level_masks(n, levels):
    t = np.arange(n)
    masks = []
    for h in levels:
        blk = (t[:, None] // (2 * h)) == (t[None, :] // (2 * h))
        masks.append((blk & ((t[:, None] % (2 * h)) >= h) & ((t[None, :] % (2 * h)) < h)).astype(np.float32))
    masks.append(np.eye(n, dtype=np.float32))
    return jnp.asarray(np.stack(masks), F32)


def _vpu_level_exponent(a, h):
    n = a.shape[0]
    pieces = []
    for j in range(n // (2 * h)):
        b = j * 2 * h
        ref = a[b + h - 1:b + h, :]
        pieces.append(ref - a[b:b + h, :])
        pieces.append(a[b + h:b + 2 * h, :] - ref)
    return jnp.concatenate(pieces, axis=0)


def _chunk_prefix_sums(x, chunk):
    n, width = x.shape
    x3 = x.reshape(n // SUBLANES, SUBLANES, width)
    pos = lax.broadcasted_iota(jnp.int32, (1, SUBLANES, 1), 1)
    d = 1
    while d < SUBLANES:
        x3 = x3 + jnp.where(pos >= d, pltpu.roll(x3, d, 1), 0.0)
        d *= 2
    x = x3.reshape(n, width)
    pieces = []
    for b in range(0, n, SUBLANES):
        blk = x[b:b + SUBLANES, :]
        if b % chunk:
            blk = blk + pieces[-1][SUBLANES - 1:SUBLANES, :]
        pieces.append(blk)
    return jnp.concatenate(pieces, axis=0)


def _small_level_exponent(a, logf, h):
    n = a.shape[0]
    pos = lax.broadcasted_iota(jnp.int32, (SUBLANES, 1), 0)
    if h == 1:
        odd = lax.broadcasted_iota(jnp.int32, (n, 1), 0) % 2 == 1
        return jnp.where(odd, logf, 0.0)
    pieces = []
    for b in range(0, n, SUBLANES):
        blk = a[b:b + SUBLANES, :]
        ref = blk[h - 1:h, :]
        for first in range(2 * h, SUBLANES, 2 * h):
            ref = jnp.where(pos >= first, blk[first + h - 1:first + h, :], ref)
        pieces.append(jnp.where(pos % (2 * h) >= h, blk - ref, ref - blk))
    return jnp.concatenate(pieces, axis=0)


def _forget_lower_bound(lbl):
    m = jnp.max(lbl, axis=0, keepdims=True)
    e = jnp.exp(lbl - m)
    return e[0:1, :] / jnp.sum(e, axis=0, keepdims=True)


PROJ_BLOCK = 512
N_PROJ_BLOCKS = D_PROJ // PROJ_BLOCK


class _ColumnBlocks:
    def __init__(self, h, w_in_ref):
        self.h, self.w, self.done = h, w_in_ref, []

    def next_block(self):
        c = len(self.done)
        if c < N_PROJ_BLOCKS:
            self.done.append(_dot(self.h, self.w[:, c * PROJ_BLOCK:(c + 1) * PROJ_BLOCK]))

    def block(self, c):
        while len(self.done) <= c:
            self.next_block()
        return self.done[c]


def _gates(ff, lb):
    e = jnp.exp(-jnp.abs(ff))
    r = 1.0 / (1.0 + e)
    pos = ff >= 0
    sig = jnp.where(pos, r, e * r)
    sig_neg = jnp.where(pos, e * r, r)
    logf = jnp.log(lb + (1.0 - lb) * sig)
    kk = (1.0 - lb) * sig_neg
    return logf, kk


def _intra_scores_times_v(q, kk, vi, exps, masks_ref):
    n = q.shape[0]
    sc = [jnp.zeros((n, n), F32) for _ in range(N_HEADS)]
    for l, ex in enumerate(exps):
        if ex is None:
            qh, kh = q.astype(BF16), kk.astype(BF16)
        else:
            w = jnp.exp(ex)
            qh, kh = (q * w).astype(BF16), (kk * w).astype(BF16)
        mask = masks_ref[l]
        for hd in range(N_HEADS):
            hs = slice(hd * D_HEAD, (hd + 1) * D_HEAD)
            sc[hd] = sc[hd] + mask * _dot_nt(qh[:, hs], kh[:, hs])
    vb = vi.astype(BF16)
    return [_dot(sc[hd].astype(BF16), vb[:, hd * D_HEAD:(hd + 1) * D_HEAD]) for hd in range(N_HEADS)]


def _head_out(o, g, gnw):
    parts = []
    for hd in range(N_HEADS):
        oh = o[:, hd * D_HEAD:(hd + 1) * D_HEAD]
        parts.append(oh * lax.rsqrt(jnp.mean(oh * oh, axis=-1, keepdims=True) + EPS))
    on = jnp.concatenate(parts, axis=1)
    return on * gnw * (g * jax.nn.sigmoid(g))


def _pad_rows_bf16(a):
    return jnp.concatenate([a, jnp.zeros_like(a)], axis=0).astype(BF16)


SAMPLE_LEVELS = (4, 2, 1)
PROMPT_WIDE_LEVELS = (32, 16, 8)
PROMPT_LEVELS = PROMPT_WIDE_LEVELS + SAMPLE_LEVELS
N_MIX_PROMPT_INPUTS = 9


def _mix_prompt_kernel(n_r, n_steps, *refs):
    step = pl.program_id(0)

    @pl.when(step < n_steps)
    def _():
        _mix_prompt_step(step % n_r, n_r, *refs)

    @pl.when(step >= n_steps)
    def _():
        x1_ref = refs[N_MIX_PROMPT_INPUTS]
        x1_ref[...] = jnp.zeros_like(x1_ref)


def _mix_prompt_step(r, n_r, x_ref, lbl_ref, nw_ref, win_ref, cw_ref, cb_ref, gnw_ref, wout_ref, masks_ref,
                     x1_ref, nconv_ref, nstate_ref, st_ref, tail_ref):
    rows = x_ref.shape[0]

    @pl.when(r == 0)
    def _():
        st_ref[...] = jnp.zeros_like(st_ref)
        tail_ref[...] = jnp.zeros_like(tail_ref)

    lb = _forget_lower_bound(lbl_ref[...])
    cw = cw_ref[...]
    n_chunks = MIX_SLAB // PROMPT_CHUNK
    rid = lax.broadcasted_iota(jnp.int32, (MIX_PART, 1), 0)
    t0 = tail_ref[SUBLANES - 2:SUBLANES - 1, :]
    t1 = tail_ref[SUBLANES - 1:SUBLANES, :]
    states = [st_ref[hd] for hd in range(N_HEADS)]

    n_parts = rows // MIX_PART
    part_rows = lambda p: slice(p * MIX_PART, (p + 1) * MIX_PART)
    project = lambda p: _ColumnBlocks(_rmsnorm(x_ref[part_rows(p), :], nw_ref[...]).astype(BF16), win_ref)
    nxt = project(0)
    for p in range(n_parts):
        pr = part_rows(p)
        x = x_ref[pr, :]
        bg, cg, vc, q, ff, vi, g = (nxt.block(c) for c in range(N_PROJ_BLOCKS))
        if p + 1 < n_parts:
            nxt = project(p + 1)
            nxt.block(N_PROJ_BLOCKS - 1)

        u = cg * vc
        u1 = jnp.where(rid == 0, t1, pltpu.roll(u, 1, 0))
        u2 = jnp.where(rid == 0, t0, jnp.where(rid == 1, t1, pltpu.roll(u, 2, 0)))
        y_conv = bg * (cw[0:1, :] * u2 + cw[1:2, :] * u1 + cw[2:3, :] * u + cb_ref[...])
        u_last = u[MIX_PART - SUBLANES:MIX_PART, :]
        t0, t1 = u_last[SUBLANES - 2:SUBLANES - 1, :], u_last[SUBLANES - 1:SUBLANES, :]

        logf, kk = _gates(ff, lb)
        o_parts = []
        for s in range(MIX_PART // MIX_SLAB):
            sl = slice(s * MIX_SLAB, (s + 1) * MIX_SLAB)
            qs, ks, vs = q[sl], kk[sl], vi[sl]
            a_pre = _chunk_prefix_sums(logf[sl], PROMPT_CHUNK)
            a_suf = jnp.concatenate(
                [a_pre[(c + 1) * PROMPT_CHUNK - 1:(c + 1) * PROMPT_CHUNK, :]
                 - a_pre[c * PROMPT_CHUNK:(c + 1) * PROMPT_CHUNK, :] for c in range(n_chunks)], axis=0)
            exps = [_vpu_level_exponent(a_pre, h) for h in PROMPT_WIDE_LEVELS]
            exps += [_small_level_exponent(a_pre, logf[sl], h) for h in SAMPLE_LEVELS]
            exps.append(None)
            intra = _intra_scores_times_v(qs, ks, vs, exps, masks_ref)
            ea = jnp.exp(a_pre)
            qa = (qs * ea).astype(BF16)
            kb = (ks * jnp.exp(a_suf)).astype(BF16)
            vb = vs.astype(BF16)
            for c in range(n_chunks):
                cr = slice(c * PROMPT_CHUNK, (c + 1) * PROMPT_CHUNK)
                last = (c + 1) * PROMPT_CHUNK - 1
                heads = []
                for hd in range(N_HEADS):
                    hs = slice(hd * D_HEAD, (hd + 1) * D_HEAD)
                    heads.append(_dot_nt(qa[cr, hs], states[hd].astype(BF16)) + intra[hd][cr, :])
                    states[hd] = states[hd] * ea[last:last + 1, hs] + _dot_tn(vb[cr, hs], kb[cr, hs])
                o_parts.append(jnp.concatenate(heads, axis=1))

        o = _head_out(jnp.concatenate(o_parts, axis=0), g, gnw_ref[...])
        mix_in = jnp.concatenate([y_conv, o], axis=1).astype(BF16)
        x1_ref[pr, :] = x + _dot(mix_in, wout_ref[...])

    tail_ref[...] = u_last
    for hd in range(N_HEADS):
        st_ref[hd] = states[hd]

    @pl.when(r == n_r - 1)
    def _():
        nconv_ref[...] = u_last
        for hd in range(N_HEADS):
            nstate_ref[hd] = states[hd].T


def _mix_prompt(x, lbl, nw, w_in, cw, cb, gnw, w_out, total_rows):
    b, l, _ = x.shape
    assert l % MIX_ROWS == 0 and MIX_ROWS % MIX_PART == 0 and MIX_PART % MIX_SLAB == 0 and l % PROMPT_CHUNK == 0
    assert total_rows % MIX_ROWS == 0
    n_r = l // MIX_ROWS
    n_steps = b * n_r
    seq = lambda s: jnp.minimum(s, n_steps - 1) // n_r
    masks = _level_masks(MIX_SLAB, PROMPT_LEVELS)
    const = lambda shape: pl.BlockSpec(shape, lambda s: (0,) * len(shape))
    return pl.pallas_call(
        functools.partial(_mix_prompt_kernel, n_r, n_steps),
        grid=(total_rows // MIX_ROWS,),
        in_specs=[
            pl.BlockSpec((None, MIX_ROWS, D_MODEL), lambda s: (seq(s), jnp.minimum(s, n_steps - 1) % n_r, 0)),
            const((2, D_HGRN)), const((1, D_MODEL)), const((D_MODEL, D_PROJ)), const((3, D_CONV)),
            const((1, D_CONV)), const((1, D_HGRN)), const((D_MODEL, D_MODEL)), const(masks.shape),
        ],
        out_specs=[
            pl.BlockSpec((MIX_ROWS, D_MODEL), lambda s: (s, 0)),
            pl.BlockSpec((None, SUBLANES, D_CONV), lambda s: (seq(s), 0, 0)),
            pl.BlockSpec((None, N_HEADS, D_HEAD, D_HEAD), lambda s: (seq(s), 0, 0, 0)),
        ],
        out_shape=[
            jax.ShapeDtypeStruct((total_rows, D_MODEL), F32),
            jax.ShapeDtypeStruct((b, SUBLANES, D_CONV), F32),
            jax.ShapeDtypeStruct((b, N_HEADS, D_HEAD, D_HEAD), F32),
        ],
        scratch_shapes=[
            pltpu.VMEM((N_HEADS, D_HEAD, D_HEAD), F32),
            pltpu.VMEM((SUBLANES, D_CONV), F32),
        ],
        compiler_params=pltpu.CompilerParams(
            dimension_semantics=("arbitrary",), vmem_limit_bytes=VMEM_LIMIT_BYTES),
        name="mix_prompt",
    )(x, lbl, nw, w_in, cw, cb, gnw, w_out, masks)


def _mix_sample_kernel(x_ref, cs_ref, hs_ref, lbl_ref, nw_ref, win_ref, cw_ref, cb_ref, gnw_ref, wout_ref,
                       cmat_ref, masks_ref, x1_in_ref, x1_ref, nconv_ref, nstate_ref, o_ref):
    del x1_in_ref
    nseq, length, _ = nconv_ref.shape
    rows = nseq * length
    x = x_ref[...]
    proj = _ColumnBlocks(_rmsnorm(x, nw_ref[...]).astype(BF16), win_ref)
    bg, cg, vc, q, ff, vi, g = (proj.block(c) for c in range(N_PROJ_BLOCKS))
    lb = _forget_lower_bound(lbl_ref[...])

    u = cg * vc
    cs = cs_ref[...]
    expand = lambda a: jnp.broadcast_to(a, (nseq, length, D_CONV)).reshape(rows, D_CONV)
    t0 = expand(cs[:, 0:1, :])
    t1 = expand(cs[:, 1:2, :])
    pos = lax.broadcasted_iota(jnp.int32, (rows, 1), 0) % length
    u1 = jnp.where(pos == 0, t1, pltpu.roll(u, 1, 0))
    u2 = jnp.where(pos == 0, t0, jnp.where(pos == 1, t1, pltpu.roll(u, 2, 0)))
    cw = cw_ref[...]
    y_conv = bg * (cw[0:1, :] * u2 + cw[1:2, :] * u1 + cw[2:3, :] * u + cb_ref[...])
    nconv_ref[...] = u.reshape(nseq, length, D_CONV)

    logf, kk = _gates(ff, lb)
    lf_hi, lf_lo = _split2(logf)
    ex = _dot(cmat_ref[...], lf_hi) + _dot(cmat_ref[...], lf_lo)
    a_pre = ex[0:rows]
    a_suf = ex[rows:2 * rows]
    exps = [ex[(2 + i) * rows:(3 + i) * rows] for i in range(len(SAMPLE_LEVELS))] + [None]
    intra = _intra_scores_times_v(q, kk, vi, exps, masks_ref)
    ea = jnp.exp(a_pre)
    qa = q * ea
    kb = kk * jnp.exp(a_suf)
    for s in range(nseq):
        cr = slice(s * length, (s + 1) * length)
        last = (s + 1) * length - 1
        for hd in range(N_HEADS):
            hs = slice(hd * D_HEAD, (hd + 1) * D_HEAD)
            st = hs_ref[s, hd].T
            inter = _dot_nt(_pad_rows_bf16(qa[cr, hs]), st.astype(BF16))[0:length, :]
            o_ref[cr, hs] = inter + intra[hd][cr, :]
            st_new = st * ea[last:last + 1, hs] + _dot_tn(_pad_rows_bf16(vi[cr, hs]), _pad_rows_bf16(kb[cr, hs]))
            nstate_ref[s, hd] = st_new.T

    o = _head_out(o_ref[...], g, gnw_ref[...])
    mix_in = jnp.concatenate([y_conv, o], axis=1).astype(BF16)
    x1_ref[...] = x + _dot(mix_in, wout_ref[...])


def _mix_sample(x, conv_state, hgrn_state, lbl, nw, w_in, cw, cb, gnw, w_out, x1_flat, row_offset):
    nb, length, _ = x.shape
    rows = MIX_SEQS * length
    assert nb % MIX_SEQS == 0 and length == SUBLANES and row_offset % rows == 0
    first_block = row_offset // rows
    cmat, masks = _exponent_matrices(rows, length, SAMPLE_LEVELS), _level_masks(rows, SAMPLE_LEVELS)
    const = lambda shape: pl.BlockSpec(shape, lambda i: (0,) * len(shape))
    return pl.pallas_call(
        _mix_sample_kernel,
        grid=(nb // MIX_SEQS,),
        in_specs=[
            pl.BlockSpec((rows, D_MODEL), lambda i: (i, 0)),
            pl.BlockSpec((MIX_SEQS, 2, D_CONV), lambda i: (i, 0, 0)),
            pl.BlockSpec((MIX_SEQS, N_HEADS, D_HEAD, D_HEAD), lambda i: (i, 0, 0, 0)),
            const((2, D_HGRN)), const((1, D_MODEL)), const((D_MODEL, D_PROJ)), const((3, D_CONV)),
            const((1, D_CONV)), const((1, D_HGRN)), const((D_MODEL, D_MODEL)), const(cmat.shape),
            const(masks.shape), pl.BlockSpec(memory_space=pl.ANY),
        ],
        out_specs=[
            pl.BlockSpec((rows, D_MODEL), lambda i: (first_block + i, 0)),
            pl.BlockSpec((MIX_SEQS, length, D_CONV), lambda i: (i, 0, 0)),
            pl.BlockSpec((MIX_SEQS, N_HEADS, D_HEAD, D_HEAD), lambda i: (i, 0, 0, 0)),
        ],
        out_shape=[
            jax.ShapeDtypeStruct(x1_flat.shape, F32),
            jax.ShapeDtypeStruct((nb, length, D_CONV), F32),
            jax.ShapeDtypeStruct((nb, N_HEADS, D_HEAD, D_HEAD), F32),
        ],
        scratch_shapes=[pltpu.VMEM((rows, D_HGRN), F32)],
        input_output_aliases={12: 0},
        compiler_params=pltpu.CompilerParams(
            dimension_semantics=("arbitrary",), vmem_limit_bytes=VMEM_LIMIT_BYTES),
        name="mix_sample",
    )(x.reshape(nb * length, D_MODEL), conv_state, hgrn_state, lbl, nw, w_in, cw, cb, gnw, w_out, cmat, masks,
      x1_flat)


def _router_kernel(x1_ref, nw_ref, rw_hi_ref, rw_lo_ref, rb_ref, ltri_ref, utri_ref, xn_ref, meta_ref, cnt_ref):
    n = ltri_ref.shape[0]
    parts = [slice(p * n, (p + 1) * n) for p in range(x1_ref.shape[0] // n)]
    lane = lax.broadcasted_iota(jnp.int32, (n, LANES), 1).astype(F32)
    logits = []
    for pr in parts:
        xn = _rmsnorm(x1_ref[pr, :], nw_ref[...])
        xn_ref[pr, :] = xn.astype(BF16)
        x_hi, x_lo = _split2(xn)
        logits.append(_dot(x_hi, rw_hi_ref[...]) + _dot(x_lo, rw_hi_ref[...]) + _dot(x_hi, rw_lo_ref[...])
                      + rb_ref[...])
    picks = []
    for work in logits:
        vals, ids = [], []
        for _ in range(TOP_K):
            m = jnp.max(work, axis=-1, keepdims=True)
            i = jnp.min(jnp.where(work == m, lane, float(LANES)), axis=-1, keepdims=True)
            vals.append(m)
            ids.append(i)
            work = jnp.where(lane == i, -jnp.inf, work)
        es = [jnp.exp(v - vals[0]) for v in vals]
        den = es[0] + es[1] + es[2] + es[3]
        onehots = [(lane == i) for i in ids]
        multi = jnp.zeros((n, LANES), F32)
        for oh in onehots:
            multi = multi + oh.astype(F32)
        picks.append((ids, [e / den for e in es], onehots, multi))

    befores, counts = [], jnp.zeros((1, LANES), F32)
    for _, _, _, multi in picks:
        befores.append(counts + _dot(ltri_ref[...], multi.astype(BF16)))
        counts = counts + jnp.sum(multi, axis=0, keepdims=True)
    seg = jnp.ceil(counts * (1.0 / SEG_ALIGN)) * SEG_ALIGN
    seg_rows = jnp.broadcast_to(seg, (BF16_ROWS, LANES)).astype(BF16)
    seg_off = _dot(seg_rows, utri_ref[...])[0:1, :]
    for pr, (ids, gates, onehots, _), before in zip(parts, picks, befores):
        slot_of = seg_off + before
        meta = jnp.zeros((n, LANES), F32)
        for k in range(TOP_K):
            slot = jnp.sum(jnp.where(onehots[k], slot_of, 0.0), axis=-1, keepdims=True)
            meta = jnp.where(lane == k, ids[k], meta)
            meta = jnp.where(lane == TOP_K + k, gates[k], meta)
            meta = jnp.where(lane == 2 * TOP_K + k, slot, meta)
        meta_ref[pr, :] = meta
    cnt_ref[...] = jnp.broadcast_to(counts, (SUBLANES, LANES))


def _router(x1, nw, router_w, router_b):
    t = x1.shape[0]
    n_tiles = t // TOKEN_TILE
    rw = jnp.zeros((D_MODEL, LANES), F32).at[:, :N_EXPERTS].set(router_w)
    rw_hi, rw_lo = _split2(rw)
    rb = jnp.full((1, LANES), PAD_LOGIT, F32).at[0, :N_EXPERTS].set(router_b)
    idx = np.arange(ROUTER_PART)
    ltri = jnp.asarray(idx[None, :] < idx[:, None], BF16)
    lid = np.arange(LANES)
    utri = jnp.asarray(lid[:, None] < lid[None, :], BF16)
    const = lambda shape: pl.BlockSpec(shape, lambda i: (0,) * len(shape))
    return pl.pallas_call(
        _router_kernel,
        grid=(n_tiles,),
        in_specs=[
            pl.BlockSpec((TOKEN_TILE, D_MODEL), lambda i: (i, 0)),
            const((1, D_MODEL)), const((D_MODEL, LANES)), const((D_MODEL, LANES)), const((1, LANES)),
            const((ROUTER_PART, ROUTER_PART)), const((LANES, LANES)),
        ],
        out_specs=[
            pl.BlockSpec((TOKEN_TILE, D_MODEL), lambda i: (i, 0)),
            pl.BlockSpec((TOKEN_TILE, LANES), lambda i: (i, 0)),
            pl.BlockSpec((None, SUBLANES, LANES), lambda i: (i, 0, 0)),
        ],
        out_shape=[
            jax.ShapeDtypeStruct((t, D_MODEL), BF16),
            jax.ShapeDtypeStruct((t, LANES), F32),
            jax.ShapeDtypeStruct((n_tiles, SUBLANES, LANES), F32),
        ],
        compiler_params=pltpu.CompilerParams(
            dimension_semantics=("arbitrary",), vmem_limit_bytes=VMEM_LIMIT_BYTES),
        name="router",
    )(x1, nw, rw_hi, rw_lo, rb, ltri, utri)


def _segment_tables(cnt):
    seg = (cnt + SEG_ALIGN - 1) // SEG_ALIGN * SEG_ALIGN
    src = jnp.cumsum(seg, axis=1) - seg
    tot = jnp.sum(seg, axis=0)
    cap = (tot + ROW_TILE - 1) // ROW_TILE * ROW_TILE
    base = jnp.cumsum(cap) - cap
    dst = base[None, :] + jnp.cumsum(seg, axis=0) - seg
    n_used = jnp.sum(cap) // ROW_TILE
    fill = base + tot
    n_tiles = seg.shape[0]
    lane = jnp.arange(N_EXPERTS)

    def compact(has, *values):
        place = has[:, :, None] & ((jnp.cumsum(has, axis=1) - 1)[:, :, None] == lane[None, None, :])
        return [jnp.sum(has, axis=1)] + [jnp.sum(jnp.where(place, v[:, :, None], 0), axis=1) for v in values]

    whole, rest = seg // CHUNK_ROWS, seg % CHUNK_ROWS
    k = jnp.arange(TOKEN_TILE // CHUNK_ROWS) * CHUNK_ROWS
    flat = lambda a: a.reshape(n_tiles, -1)
    pieces = [compact(flat(seg[:, :, None] >= k + CHUNK_ROWS), flat(src[:, :, None] + k), flat(dst[:, :, None] + k))]
    for size in CHUNK_SIZES[1:]:
        pieces.append(compact(rest == size, src + whole * CHUNK_ROWS, dst + whole * CHUNK_ROWS))
    counts, srcs, dsts = zip(*pieces)
    i32 = lambda a: a.astype(jnp.int32).reshape(-1)
    chunks = (i32(jnp.stack(counts, axis=1)), i32(jnp.stack(srcs, axis=1)), i32(jnp.stack(dsts, axis=1)))
    return (chunks, i32(jnp.sum(seg, axis=1)), i32(fill), i32(cap - tot), i32(base), i32(cap // ROW_TILE),
            i32(n_used))


def _for_each_chunk(n, fn):
    for size in SEG_SIZES:
        done = n & ~(2 * size - 1)

        @pl.when((n & size) != 0)
        def _():
            fn(done, size)


def _segment_copies(cnt_ref, src_ref, dst_ref, tile, make_copy, start):
    for si, size in enumerate(CHUNK_SIZES):
        first = (tile * len(CHUNK_SIZES) + si) * N_EXPERTS

        def one(p, carry, si=si, size=size, first=first):
            copy = make_copy(pl.multiple_of(src_ref[first + p], SEG_ALIGN),
                             pl.multiple_of(dst_ref[first + p], SEG_ALIGN), size)
            if start:
                copy.start(priority=si % 2)
            else:
                copy.wait()
            return carry
        lax.fori_loop(0, cnt_ref[tile * len(CHUNK_SIZES) + si], one, 0)


SORT_BLOCK = 256


def _slot_matrix(slots, block, gates):
    row = (lax.broadcasted_iota(jnp.int32, (SORT_BLOCK, TOKEN_TILE), 0) + block * SORT_BLOCK).astype(F32)
    m = jnp.zeros((SORT_BLOCK, TOKEN_TILE), F32)
    for k in range(TOP_K):
        m = jnp.where(row == slots[k:k + 1, :], 1.0 if gates is None else gates[k:k + 1, :], m)
    return m.astype(BF16)


def _dispatch_kernel(seg_ref, src_ref, dst_ref, fill_ref, gap_ref, nu_ref, slot_ref, xn_ref, xs_ref, sorted_ref,
                     zero_ref, sem, seg_sem):
    tile = pl.program_id(0)
    n_tiles = pl.num_programs(0)
    buf = tile % 2

    @pl.when(tile == 0)
    def _():
        zero_ref[...] = jnp.zeros_like(zero_ref)

        def fill_copy(row, size):
            return pltpu.make_async_copy(zero_ref.at[pl.ds(0, size)],
                                         xs_ref.at[pl.ds(pl.multiple_of(row, SEG_ALIGN), size)], sem)

        def gaps(act):
            def per_expert(e, carry):
                _for_each_chunk(gap_ref[e], lambda done, size: act(fill_copy(fill_ref[e] + done, size)))
                return carry
            lax.fori_loop(0, N_EXPERTS, per_expert, 0)

        def tail(act):
            n_tail = xs_ref.shape[0] // ROW_TILE - nu_ref[0]
            lax.fori_loop(0, n_tail, lambda i, c: (act(fill_copy((nu_ref[0] + i) * ROW_TILE, ROW_TILE)), c)[1], 0)

        gaps(lambda c: c.start())
        tail(lambda c: c.start())
        gaps(lambda c: c.wait())
        tail(lambda c: c.wait())

    slots = slot_ref[...]
    xn = xn_ref[...]
    for rb in range(TILE_CAP // SORT_BLOCK):
        sorted_ref[buf, rb * SORT_BLOCK:(rb + 1) * SORT_BLOCK, :] = _dot(
            _slot_matrix(slots, rb, None), xn).astype(BF16)

    def copier(half):
        return lambda src, dst, size: pltpu.make_async_copy(
            sorted_ref.at[half, pl.ds(src, size)], xs_ref.at[pl.ds(dst, size)], seg_sem.at[half])

    _segment_copies(seg_ref, src_ref, dst_ref, tile, copier(buf), start=True)

    @pl.when(tile > 0)
    def _():
        _segment_copies(seg_ref, src_ref, dst_ref, tile - 1, copier(1 - buf), start=False)

    @pl.when(tile == n_tiles - 1)
    def _():
        _segment_copies(seg_ref, src_ref, dst_ref, tile, copier(buf), start=False)


def _dispatch(tables, slot_t, xn, n_rows_total):
    t = xn.shape[0]
    n_tiles = t // TOKEN_TILE
    return pl.pallas_call(
        _dispatch_kernel,
        grid_spec=pltpu.PrefetchScalarGridSpec(
            num_scalar_prefetch=len(tables),
            grid=(n_tiles,),
            in_specs=[
                pl.BlockSpec((SUBLANES, TOKEN_TILE), lambda i, *_: (0, i)),
                pl.BlockSpec((TOKEN_TILE, D_MODEL), lambda i, *_: (i, 0)),
            ],
            out_specs=pl.BlockSpec(memory_space=pl.ANY),
            scratch_shapes=[
                pltpu.VMEM((2, TILE_CAP, D_MODEL), BF16),
                pltpu.VMEM((ROW_TILE, D_MODEL), BF16),
                pltpu.SemaphoreType.DMA(()),
                pltpu.SemaphoreType.DMA((2,)),
            ],
        ),
        out_shape=jax.ShapeDtypeStruct((n_rows_total, D_MODEL), BF16),
        compiler_params=pltpu.CompilerParams(
            dimension_semantics=("arbitrary",), vmem_limit_bytes=VMEM_LIMIT_BYTES),
        name="dispatch",
    )(*tables, slot_t, xn)


X_SLOTS = 3
W_FETCH_STEPS = (0, 1, 2)


def _experts_kernel(base_ref, nt_ref, nu_ref, bg_ref, bu_ref, bd_ref, wg_hbm, wu_hbm, wd_hbm, xs_ref, zs_ref,
                    w_f32, wg_bf, wu_bf, wd_bf, xbuf, hbuf, zbuf, w_sem, in_sem, out_sem):
    e = pl.program_id(0)
    n_e = pl.num_programs(0)
    n = nt_ref[e]
    base = base_ref[e]
    w_slot = e % 2
    has_next = e + 1 < n_e

    def w_copy(j, expert, slot):
        w_hbm = (wg_hbm, wu_hbm, wd_hbm)[j]
        return pltpu.make_async_copy(w_hbm.at[expert], w_f32.at[slot, j], w_sem.at[slot])

    def x_copy(i):
        rows = pl.ds(pl.multiple_of(base + i * ROW_TILE, ROW_TILE), ROW_TILE)
        return pltpu.make_async_copy(xs_ref.at[rows], xbuf.at[i % X_SLOTS], in_sem.at[i % X_SLOTS])

    def z_copy(row, slot):
        rows = pl.ds(pl.multiple_of(row, ROW_TILE), ROW_TILE)
        return pltpu.make_async_copy(zbuf.at[slot], zs_ref.at[rows], out_sem.at[slot])

    def hidden(i):
        x = xbuf[i % X_SLOTS]
        a = jnp.minimum(_dot(x, wg_bf[...]) + bg_ref[...], SWIGLU_LIMIT)
        u = jnp.clip(_dot(x, wu_bf[...]) + bu_ref[...], -SWIGLU_LIMIT, SWIGLU_LIMIT)
        hbuf[i % 2] = ((u + 1.0) * a * jax.nn.sigmoid(SWIGLU_ALPHA * a)).astype(BF16)

    def project_down(i, slot):
        @pl.when(i >= 2)
        def _():
            z_copy(base, slot).wait()
        zbuf[slot] = (_dot(hbuf[slot], wd_bf[...]) + bd_ref[...]).astype(BF16)

    @pl.when(e == 0)
    def _():
        for j in range(3):
            w_copy(j, 0, 0).start()

    for i in range(X_SLOTS):
        @pl.when(i < n)
        def _():
            x_copy(i).start(priority=1)

    def fetch_next_weights(j):
        @pl.when(has_next)
        def _():
            w_copy(j, e + 1, 1 - w_slot).start()

    fetch_next_weights(0)
    for j in range(3):
        w_copy(j, e, w_slot).wait()

    @pl.when(n > 0)
    def _():
        wg_bf[...] = w_f32[w_slot, 0].astype(BF16)
        wu_bf[...] = w_f32[w_slot, 1].astype(BF16)
        wd_bf[...] = w_f32[w_slot, 2].astype(BF16)
        x_copy(0).wait()
        hidden(0)

        def step(i, carry):
            x_copy(i + 1).wait()

            @pl.when(i + X_SLOTS < n)
            def _():
                x_copy(i + X_SLOTS).start(priority=1)

            for j in (1, 2):
                @pl.when(i == W_FETCH_STEPS[j])
                def _():
                    fetch_next_weights(j)

            project_down(i, i % 2)
            hidden(i + 1)
            z_copy(base + i * ROW_TILE, i % 2).start()
            return carry

        lax.fori_loop(0, n - 1, step, 0)
        project_down(n - 1, (n - 1) % 2)
        z_copy(base + (n - 1) * ROW_TILE, (n - 1) % 2).start()

        @pl.when(n >= 2)
        def _():
            z_copy(base, n % 2).wait()
        z_copy(base, (n - 1) % 2).wait()

    for j in (1, 2):
        @pl.when(jnp.maximum(n - 1, 0) <= W_FETCH_STEPS[j])
        def _():
            fetch_next_weights(j)

    @pl.when(e == pl.num_programs(0) - 1)
    def _():
        zbuf[0] = jnp.zeros((ROW_TILE, D_MODEL), BF16)
        n_tail = zs_ref.shape[0] // ROW_TILE - nu_ref[0]
        tail = lambda act: lax.fori_loop(
            0, n_tail, lambda i, c: (act(z_copy((nu_ref[0] + i) * ROW_TILE, 0)), c)[1], 0)
        tail(lambda c: c.start())
        tail(lambda c: c.wait())


def _experts(base, n_tiles, n_used, xs, w_gate, b_gate, w_up, b_up, w_down, b_down):
    b_spec = pl.BlockSpec((None, 1, D_MODEL), lambda e, *_: (e, 0, 0))
    any_spec = pl.BlockSpec(memory_space=pl.ANY)
    return pl.pallas_call(
        _experts_kernel,
        grid_spec=pltpu.PrefetchScalarGridSpec(
            num_scalar_prefetch=3,
            grid=(N_EXPERTS,),
            in_specs=[b_spec, b_spec, b_spec, any_spec, any_spec, any_spec, any_spec],
            out_specs=any_spec,
            scratch_shapes=[pltpu.VMEM((2, 3, D_MODEL, D_MODEL), F32)] + [
                pltpu.VMEM((D_MODEL, D_MODEL), BF16) for _ in range(3)] + [
                pltpu.VMEM((X_SLOTS, ROW_TILE, D_MODEL), BF16)] + [
                pltpu.VMEM((2, ROW_TILE, D_MODEL), BF16) for _ in range(2)] + [
                pltpu.SemaphoreType.DMA((2,)), pltpu.SemaphoreType.DMA((X_SLOTS,)), pltpu.SemaphoreType.DMA((2,))],
        ),
        out_shape=jax.ShapeDtypeStruct(xs.shape, BF16),
        compiler_params=pltpu.CompilerParams(
            dimension_semantics=("arbitrary",), vmem_limit_bytes=VMEM_LIMIT_BYTES),
        name="experts",
    )(base, n_tiles, n_used, b_gate[:, None, :], b_up[:, None, :], b_down[:, None, :], w_gate, w_up, w_down, xs)


ZERO_BLOCK = 256


def _combine_kernel(n_first, seg_ref, src_ref, dst_ref, rows_ref, route_ref, x1_ref, fw_ref, zs_ref, ya_ref, yb_ref,
                    sorted_ref, sem):
    tile = pl.program_id(0)
    n_tiles = pl.num_programs(0)
    buf = tile % 2

    def fetch(t, half):
        def zero(b, carry):
            rows = pl.ds(pl.multiple_of(b * ZERO_BLOCK, ZERO_BLOCK), ZERO_BLOCK)
            sorted_ref[half, rows, :] = jnp.zeros((ZERO_BLOCK, D_MODEL), BF16)
            return carry
        lax.fori_loop(rows_ref[t] // ZERO_BLOCK, TILE_CAP // ZERO_BLOCK, zero, 0)
        _segment_copies(seg_ref, src_ref, dst_ref, t, copier(half), start=True)

    def copier(half):
        return lambda src, dst, size: pltpu.make_async_copy(
            zs_ref.at[pl.ds(dst, size)], sorted_ref.at[half, pl.ds(src, size)], sem.at[half])

    @pl.when(tile == 0)
    def _():
        fetch(tile, buf)

    @pl.when(tile + 1 < n_tiles)
    def _():
        fetch(tile + 1, 1 - buf)

    _segment_copies(seg_ref, src_ref, dst_ref, tile, copier(buf), start=False)
    route = route_ref[...]
    slots, gates = route[0:TOP_K, :], route[TOP_K:2 * TOP_K, :]
    moe = jnp.zeros((TOKEN_TILE, D_MODEL), F32)
    for cb in range(TILE_CAP // SORT_BLOCK):
        moe = moe + _dot_tn(_slot_matrix(slots, cb, gates), sorted_ref[buf, cb * SORT_BLOCK:(cb + 1) * SORT_BLOCK, :])
    y = _rmsnorm(x1_ref[...] + moe, fw_ref[...])

    @pl.when(tile < n_first)
    def _():
        ya_ref[...] = y

    @pl.when(tile >= n_first)
    def _():
        yb_ref[...] = y


def _combine(tables, route_t, x1, final_w, zs, rows_first):
    t = x1.shape[0]
    assert rows_first % TOKEN_TILE == 0 and 0 < rows_first < t
    assert TILE_CAP % SORT_BLOCK == 0 and TILE_CAP % ZERO_BLOCK == 0
    n_tiles = t // TOKEN_TILE
    n_first = rows_first // TOKEN_TILE
    return pl.pallas_call(
        functools.partial(_combine_kernel, n_first),
        grid_spec=pltpu.PrefetchScalarGridSpec(
            num_scalar_prefetch=len(tables),
            grid=(n_tiles,),
            in_specs=[
                pl.BlockSpec((SUBLANES, TOKEN_TILE), lambda i, *_: (0, i)),
                pl.BlockSpec((TOKEN_TILE, D_MODEL), lambda i, *_: (i, 0)),
                pl.BlockSpec((1, D_MODEL), lambda i, *_: (0, 0)),
                pl.BlockSpec(memory_space=pl.ANY),
            ],
            out_specs=[
                pl.BlockSpec((TOKEN_TILE, D_MODEL), lambda i, *_: (jnp.minimum(i, n_first - 1), 0)),
                pl.BlockSpec((TOKEN_TILE, D_MODEL), lambda i, *_: (jnp.maximum(i - n_first, 0), 0)),
            ],
            scratch_shapes=[pltpu.VMEM((2, TILE_CAP, D_MODEL), BF16), pltpu.SemaphoreType.DMA((2,))],
        ),
        out_shape=[jax.ShapeDtypeStruct((rows_first, D_MODEL), F32),
                   jax.ShapeDtypeStruct((t - rows_first, D_MODEL), F32)],
        compiler_params=pltpu.CompilerParams(
            dimension_semantics=("arbitrary",), vmem_limit_bytes=VMEM_LIMIT_BYTES),
        name="combine",
    )(*tables, route_t, x1, final_w, zs)


def _moe_and_final_norm(x1, rows_first, norm_ffn_w, router_w, router_b, w_gate, b_gate, w_up, b_up, w_down, b_down,
                        final_w):
    t = x1.shape[0]
    assert t % TOKEN_TILE == 0
    n_tiles = t // TOKEN_TILE
    n_rows_total = TOP_K * t + n_tiles * N_EXPERTS * (SEG_ALIGN - 1) + N_EXPERTS * (ROW_TILE - 1)
    n_rows_total = (n_rows_total + ROW_TILE - 1) // ROW_TILE * ROW_TILE
    xn, meta, cnt = _router(x1, norm_ffn_w.reshape(1, D_MODEL), router_w, router_b)
    cnt = cnt[:, 0, :N_EXPERTS].astype(jnp.int32)
    chunks, tile_rows, fill, gap, base, n_row_tiles, n_used = _segment_tables(cnt)
    route_t = jnp.concatenate([meta[:, 2 * TOP_K:3 * TOP_K], meta[:, TOP_K:2 * TOP_K]], axis=1).T
    xs = _dispatch((*chunks, fill, gap, n_used), route_t, xn, n_rows_total)
    zs = _experts(base, n_row_tiles, n_used, xs, w_gate, b_gate, w_up, b_up, w_down, b_down)
    return _combine((*chunks, tile_rows), route_t, x1, final_w.reshape(1, D_MODEL), zs, rows_first)


def kernel(x_prompt, x_sample, state_conv, state_hgrn, lb_logits, norm_mix_w, w_in, conv_w, conv_b, gnorm_w,
           w_out, norm_ffn_w, router_w, router_b, w_gate, b_gate, w_up, b_up, w_down, b_down, final_norm_w):
    assert norm_mix_w.shape[0] == 1 and lb_logits.shape[0] == 2, "single-layer step"
    b, l, _ = x_prompt.shape
    nb, ls, _ = x_sample.shape
    lbl = lb_logits.astype(F32)
    nw = norm_mix_w[0].reshape(1, D_MODEL)
    w_in_bf = w_in[0].astype(BF16)
    w_out_bf = w_out[0].astype(BF16)
    cw, cb = conv_w[0], conv_b[0].reshape(1, D_CONV)
    gnw = gnorm_w[0].reshape(1, D_HGRN)
    rows_p, rows_s = b * l, nb * ls
    x1, conv_p, hgrn_p = _mix_prompt(x_prompt, lbl, nw, w_in_bf, cw, cb, gnw, w_out_bf, rows_p + rows_s)
    x1, conv_s, hgrn_s = _mix_sample(x_sample, state_conv[0], state_hgrn[0], lbl, nw, w_in_bf, cw, cb, gnw,
                                     w_out_bf, x1, rows_p)
    y_p, y_s = _moe_and_final_norm(x1, rows_p, norm_ffn_w[0], router_w[0], router_b[0], w_gate[0], b_gate[0],
                                   w_up[0], b_up[0], w_down[0], b_down[0], final_norm_w)
    conv_p = conv_p[:, SUBLANES - 2:, :]
    conv_s = conv_s[:, ls - 2:, :]
    return (y_p.reshape(b, l, D_MODEL), y_s.reshape(nb, ls, D_MODEL), conv_p[None], hgrn_p[None], conv_s[None],
            hgrn_s[None])
```
